```python
import math
import jax, jax.numpy as jnp
from jax import lax
import numpy as np

D_MODEL = 2048
BATCH = 4
SEQ = 2048
DEPTH = 1

MLA_HEADS = 8
MLA_NOPE = 128
MLA_ROPE = 64
MLA_QK = MLA_NOPE + MLA_ROPE
MLA_V = 128
Q_LORA = 512
KV_LORA = 256
ROPE_THETA = 10000.0
Q_BLOCK = 128
ML_HEADS = 8
ML_DQK = 128
ML_DV = 128
ML_CONV = 4
ML_CHUNK = 64
F_BIAS_LO = 3.0
F_BIAS_HI = 6.0
MLA_W = MLA_HEADS * MLA_V
ML_W = ML_HEADS * ML_DV
N_GROUPS = 4
EXP_PER_GROUP = 8
N_EXPERTS = N_GROUPS * EXP_PER_GROUP
TOP_K = 2
D_FF_EXPERT = 512
EPS = 1e-6
IN_SIZES = (Q_LORA, KV_LORA, MLA_ROPE, 2 * ML_HEADS * ML_DQK, ML_HEADS * ML_DV, ML_W, ML_HEADS, ML_HEADS, D_MODEL, D_MODEL)
IN_COLS = Q_LORA + KV_LORA + MLA_ROPE + 2 * ML_HEADS * ML_DQK + ML_HEADS * ML_DV + ML_W + 2 * ML_HEADS + 2 * D_MODEL

kernel_name = "hybrid_mla_mlstm_hmoe_adaln"


def rmsnorm(x, g):
    xf = x.astype(jnp.float32)
    xf = xf * lax.rsqrt(jnp.mean(xf * xf, axis=-1, keepdims=True) + EPS)
    return (xf * g.astype(jnp.float32)).astype(x.dtype)


def rope_tables(positions):
    inv = ROPE_THETA ** (-jnp.arange(0, MLA_ROPE, 2, dtype=jnp.float32) / MLA_ROPE)
    ang = positions.astype(jnp.float32)[..., None] * inv
    return jnp.cos(ang), jnp.sin(ang)


def apply_rope(t, cos, sin):
    cos = cos[:, :, None, :].astype(t.dtype)
    sin = sin[:, :, None, :].astype(t.dtype)
    t1, t2 = jnp.split(t, 2, axis=-1)
    return jnp.concatenate([t1 * cos - t2 * sin, t1 * sin + t2 * cos], axis=-1)


def causal_conv(u, w, b):
    C = u.shape[-1]
    out = lax.conv_general_dilated(u, w[:, None, :].astype(u.dtype), window_strides=(1,), padding=((ML_CONV - 1, 0),), dimension_numbers=('NWC', 'WIO', 'NWC'), feature_group_count=C)
    return out + b.astype(u.dtype)


def mla_attention(cq, ckv, kpe, cos, sin, q_a_norm_g, w_uq, kv_a_norm_g, w_ukv, q_norm_g, k_norm_g):
    B, S, _ = cq.shape
    q = (rmsnorm(cq, q_a_norm_g) @ w_uq).reshape(B, S, MLA_HEADS, MLA_QK)
    kv = (rmsnorm(ckv, kv_a_norm_g) @ w_ukv).reshape(B, S, MLA_HEADS, MLA_NOPE + MLA_V)
    k_nope, v = kv[..., :MLA_NOPE], kv[..., MLA_NOPE:]
    k = jnp.concatenate([k_nope, jnp.broadcast_to(kpe[:, :, None, :], (B, S, MLA_HEADS, MLA_ROPE))], axis=-1)
    q = rmsnorm(q, q_norm_g)
    k = rmsnorm(k, k_norm_g)
    q = jnp.concatenate([q[..., :MLA_NOPE], apply_rope(q[..., MLA_NOPE:], cos, sin)], axis=-1)
    k = jnp.concatenate([k[..., :MLA_NOPE], apply_rope(k[..., MLA_NOPE:], cos, sin)], axis=-1)
    nb = S // Q_BLOCK
    q_blocks = q.reshape(B, nb, Q_BLOCK, MLA_HEADS, MLA_QK).transpose(1, 0, 2, 3, 4)
    key_pos = jnp.arange(S)
    scale = MLA_QK ** -0.5

    def attend(args):
        q_blk, blk = args
        s = jnp.einsum('bqhd,bkhd->bhqk', q_blk, k).astype(jnp.float32) * scale
        q_pos = blk * Q_BLOCK + jnp.arange(Q_BLOCK)
        s = jnp.where(key_pos[None, :] <= q_pos[:, None], s, -jnp.inf)
        p = jax.nn.softmax(s, axis=-1).astype(v.dtype)
        return jnp.einsum('bhqk,bkhd->bqhd', p, v)

    o = lax.map(attend, (q_blocks, jnp.arange(nb)))
    return o.transpose(1, 0, 2, 3, 4).reshape(B, S, MLA_W)


def mlstm_chunkwise(q, k, v, i_pre, f_pre):
    B, S, H, _ = q.shape
    nc = S // ML_CHUNK
    f32 = jnp.float32

    def chunks(t):
        return t.astype(f32).reshape(B, nc, ML_CHUNK, H, -1).transpose(1, 0, 3, 2, 4)

    def gate_chunks(t):
        return t.astype(f32).reshape(B, nc, ML_CHUNK, H).transpose(1, 0, 3, 2)

    qc = chunks(q)
    kc = chunks(k) * (ML_DQK ** -0.5)
    vc = chunks(v)
    log_i = gate_chunks(i_pre)
    b_cum = jnp.cumsum(gate_chunks(jax.nn.log_sigmoid(f_pre.astype(f32))), axis=-1)
    causal = jnp.tril(jnp.ones((ML_CHUNK, ML_CHUNK), dtype=bool))

    def step(carry, inp):
        C, n, m = carry
        q_, k_, v_, b_, li = inp
        log_w = jnp.where(causal, b_[..., :, None] - b_[..., None, :] + li[..., None, :], -jnp.inf)
        log_inter = b_ + m[..., None]
        m_t = jnp.maximum(jnp.max(log_w, axis=-1), log_inter)
        w = jnp.exp(log_w - m_t[..., None])
        a = jnp.exp(log_inter - m_t)
        s = jnp.einsum('bhtd,bhsd->bhts', q_, k_) * w
        num = jnp.einsum('bhts,bhse->bhte', s, v_) + a[..., None] * jnp.einsum('bhtd,bhde->bhte', q_, C)
        den = jnp.sum(s, axis=-1) + a * jnp.einsum('bhtd,bhd->bht', q_, n)
        h = num / jnp.maximum(jnp.abs(den), jnp.exp(-m_t))[..., None]
        b_last = b_[..., -1]
        log_g = b_last[..., None] - b_ + li
        m_new = jnp.maximum(b_last + m, jnp.max(log_g, axis=-1))
        g = jnp.exp(log_g - m_new[..., None])
        decay = jnp.exp(b_last + m - m_new)
        C = decay[..., None, None] * C + jnp.einsum('bhs,bhsd,bhse->bhde', g, k_, v_)
        n = decay[..., None] * n + jnp.einsum('bhs,bhsd->bhd', g, k_)
        return (C, n, m_new), h

    init = (jnp.zeros((B, H, ML_DQK, ML_DV), f32), jnp.zeros((B, H, ML_DQK), f32), jnp.zeros((B, H), f32))
    _, h = lax.scan(step, init, (qc, kc, vc, b_cum, log_i))
    return h.transpose(1, 0, 3, 2, 4).reshape(B, S, H, ML_DV).astype(v.dtype)


def hybrid_mixer(h, cos, sin, w_in, q_a_norm_g, w_uq, kv_a_norm_g, w_ukv, q_norm_g, k_norm_g, conv_w, conv_b, b_mlstm_gates, mlstm_norm_g, w_proj_a, w_proj_b, w_out):
    B, S, _ = h.shape
    proj = h @ w_in
    split_at = np.cumsum(IN_SIZES)[:-1].tolist()
    cq, ckv, kpe, ml_qk, ml_v, ml_o, ml_i, ml_f, g_a, g_b = jnp.split(proj, split_at, axis=-1)
    out_a = mla_attention(cq, ckv, kpe, cos, sin, q_a_norm_g, w_uq, kv_a_norm_g, w_ukv, q_norm_g, k_norm_g)
    qk = jax.nn.silu(causal_conv(ml_qk, conv_w, conv_b))
    ml_q, ml_k = jnp.split(qk, 2, axis=-1)
    ml_q = ml_q.reshape(B, S, ML_HEADS, ML_DQK)
    ml_k = ml_k.reshape(B, S, ML_HEADS, ML_DQK)
    ml_v = ml_v.reshape(B, S, ML_HEADS, ML_DV)
    i_pre = ml_i + b_mlstm_gates[0].astype(ml_i.dtype)
    f_pre = ml_f + b_mlstm_gates[1].astype(ml_f.dtype)
    hm = mlstm_chunkwise(ml_q, ml_k, ml_v, i_pre, f_pre)
    hm = rmsnorm(hm, mlstm_norm_g.reshape(ML_HEADS, ML_DV)).reshape(B, S, ML_W) * jax.nn.sigmoid(ml_o)
    mixed = jax.nn.sigmoid(g_a) * (out_a @ w_proj_a) + jax.nn.sigmoid(g_b) * (hm @ w_proj_b)
    return mixed @ w_out


def hier_moe(h, w_group, b_group, w_router, b_router, w_gate_e, w_up_e, w_down_e):
    B, S, D = h.shape
    t = h.reshape(B * S, D)
    T = t.shape[0]
    g_logits = (t @ w_group).astype(jnp.float32) + b_group.astype(jnp.float32)
    g_prob = jax.nn.softmax(g_logits, axis=-1)
    g_sel = jnp.argmax(g_logits, axis=-1)
    g_w = jnp.take_along_axis(g_prob, g_sel[:, None], axis=-1)[:, 0]
    e_logits = ((t @ w_router).astype(jnp.float32) + b_router.astype(jnp.float32)).reshape(T, N_GROUPS, EXP_PER_GROUP)
    e_in = jnp.take_along_axis(e_logits, g_sel[:, None, None], axis=1)[:, 0]
    top_p, top_i = lax.top_k(jax.nn.softmax(e_in, axis=-1), TOP_K)
    weights = g_w[:, None] * top_p / jnp.sum(top_p, axis=-1, keepdims=True)
    expert_idx = g_sel[:, None] * EXP_PER_GROUP + top_i
    comb = jnp.sum(jax.nn.one_hot(expert_idx, N_EXPERTS, dtype=jnp.float32) * weights[..., None], axis=1).astype(t.dtype)
    out = jnp.zeros_like(t)
    for gi in range(N_GROUPS):
        sl = slice(gi * EXP_PER_GROUP, (gi + 1) * EXP_PER_GROUP)
        hg = jnp.einsum('td,edf->tef', t, w_gate_e[sl])
        hu = jnp.einsum('td,edf->tef', t, w_up_e[sl])
        act = jax.nn.silu(hg) * hu * comb[:, sl, None]
        out = out + jnp.einsum('tef,efd->td', act, w_down_e[sl])
    return out.reshape(B, S, D)


def setup_inputs(seed: int = 0) -> dict:
    key = jax.random.key(seed)
    ks = jax.random.split(key, 32)
    L = DEPTH
    f32 = jnp.float32

    def nrm(k, shape, fan_in):
        return jax.random.normal(k, shape, f32) * (fan_in ** -0.5)

    def gain(k, shape):
        return 1.0 + 0.02 * jax.random.normal(k, shape, f32)

    def bias(k, shape, s=0.01):
        return s * jax.random.normal(k, shape, f32)

    f_bias = jnp.linspace(F_BIAS_LO, F_BIAS_HI, ML_HEADS, dtype=f32)
    b_mlstm_gates = jnp.stack([bias(ks[13], (L, ML_HEADS), 0.1), f_bias[None, :] + bias(ks[14], (L, ML_HEADS), 0.1)], axis=1)
    positions = jnp.arange(SEQ, dtype=jnp.int32)[None, :] + jax.random.randint(ks[2], (BATCH, 1), 0, 1024, dtype=jnp.int32)
    return {
        "x": jax.random.normal(ks[0], (BATCH, SEQ, D_MODEL), f32),
        "c": jax.random.normal(ks[1], (BATCH, D_MODEL), f32),
        "positions": positions,
        "w_ada": nrm(ks[3], (L, D_MODEL, 6 * D_MODEL), D_MODEL),
        "b_ada": bias(ks[4], (L, 6 * D_MODEL), 0.02),
        "norm_mix_g": gain(ks[5], (L, D_MODEL)),
        "w_in": nrm(ks[6], (L, D_MODEL, IN_COLS), D_MODEL),
        "q_a_norm_g": gain(ks[7], (L, Q_LORA)),
        "w_uq": nrm(ks[8], (L, Q_LORA, MLA_HEADS * MLA_QK), Q_LORA),
        "kv_a_norm_g": gain(ks[9], (L, KV_LORA)),
        "w_ukv": nrm(ks[10], (L, KV_LORA, MLA_HEADS * (MLA_NOPE + MLA_V)), KV_LORA),
        "q_norm_g": gain(ks[11], (L, MLA_QK)),
        "k_norm_g": gain(ks[12], (L, MLA_QK)),
        "conv_w": nrm(ks[15], (L, ML_CONV, 2 * ML_HEADS * ML_DQK), ML_CONV),
        "conv_b": bias(ks[16], (L, 2 * ML_HEADS * ML_DQK)),
        "b_mlstm_gates": b_mlstm_gates,
        "mlstm_norm_g": gain(ks[17], (L, ML_W)),
        "w_proj_a": nrm(ks[18], (L, MLA_W, D_MODEL), MLA_W),
        "w_proj_b": nrm(ks[19], (L, ML_W, D_MODEL), ML_W),
        "w_out": nrm(ks[20], (L, D_MODEL, D_MODEL), D_MODEL),
        "norm_ffn_g": gain(ks[21], (L, D_MODEL)),
        "w_group": nrm(ks[22], (L, D_MODEL, N_GROUPS), D_MODEL),
        "b_group": bias(ks[23], (L, N_GROUPS)),
        "w_router": nrm(ks[24], (L, D_MODEL, N_EXPERTS), D_MODEL),
        "b_router": bias(ks[25], (L, N_EXPERTS)),
        "w_gate_e": nrm(ks[26], (L, N_EXPERTS, D_MODEL, D_FF_EXPERT), D_MODEL),
        "w_up_e": nrm(ks[27], (L, N_EXPERTS, D_MODEL, D_FF_EXPERT), D_MODEL),
        "w_down_e": nrm(ks[28], (L, N_EXPERTS, D_FF_EXPERT, D_MODEL), D_FF_EXPERT),
    }


def reference(x, c, positions, w_ada, b_ada, norm_mix_g, w_in, q_a_norm_g, w_uq, kv_a_norm_g, w_ukv, q_norm_g, k_norm_g, conv_w, conv_b, b_mlstm_gates, mlstm_norm_g, w_proj_a, w_proj_b, w_out, norm_ffn_g, w_group, b_group, w_router, b_router, w_gate_e, w_up_e, w_down_e):
    cos, sin = rope_tables(positions)
    cond = jax.nn.silu(c)
    for l in range(DEPTH):
        mod = cond @ w_ada[l] + b_ada[l]
        sh_a, sc_a, gt_a, sh_m, sc_m, gt_m = [m[:, None, :] for m in jnp.split(mod, 6, axis=-1)]
        h = rmsnorm(x, norm_mix_g[l]) * (1.0 + sc_a) + sh_a
        x = x + gt_a * hybrid_mixer(h, cos, sin, w_in[l], q_a_norm_g[l], w_uq[l], kv_a_norm_g[l], w_ukv[l], q_norm_g[l], k_norm_g[l], conv_w[l], conv_b[l], b_mlstm_gates[l], mlstm_norm_g[l], w_proj_a[l], w_proj_b[l], w_out[l])
        h = rmsnorm(x, norm_ffn_g[l]) * (1.0 + sc_m) + sh_m
        x = x + gt_m * hier_moe(h, w_group[l], b_group[l], w_router[l], b_router[l], w_gate_e[l], w_up_e[l], w_down_e[l])
    return x
```

```python
import functools
import math

import jax
import jax.numpy as jnp
from jax import lax
from jax.experimental import pallas as pl
from jax.experimental.pallas import tpu as pltpu

F32 = jnp.float32
BF16 = jnp.bfloat16

LANES = 128
SUBLANES = 8

D_MODEL = 2048
MLA_HEADS = 8
MLA_NOPE = 128
MLA_ROPE = 64
MLA_QK = MLA_NOPE + MLA_ROPE
MLA_V = 128
Q_LORA = 512
KV_LORA = 256
ROPE_THETA = 10000.0
ML_HEADS = 8
ML_DQK = 128
ML_DV = 128
ML_CONV = 4
MLA_W = MLA_HEADS * MLA_V
ML_W = ML_HEADS * ML_DV
N_GROUPS = 4
EXP_PER_GROUP = 8
N_EXPERTS = N_GROUPS * EXP_PER_GROUP
D_FF_EXPERT = 512
EPS = 1e-6

QK_PAD = 2 * LANES

COL_QK = 0
COL_GA = 2048
COL_GB = 4096
COL_V = 6144
COL_O = 7168
COL_LAT = 8192
COL_GATES = 9088
IN_COLS_PAD = 9216
LAT_W = Q_LORA + KV_LORA + 2 * MLA_ROPE

MLSTM_CHUNK = 256
MOE_TILE = 256


def _cparams(sem, vmem_mb):
    return pltpu.CompilerParams(dimension_semantics=sem, vmem_limit_bytes=vmem_mb * 1024 * 1024)


def _dot(a, b):
    return jnp.dot(a, b, preferred_element_type=F32)


def _dot_nt(a, b):
    return lax.dot_general(a, b, (((1,), (1,)), ((), ())), preferred_element_type=F32)


def _dot_tn(a, b):
    return lax.dot_general(a, b, (((0,), (0,)), ((), ())), preferred_element_type=F32)


def _sigmoid(x):
    return 1.0 / (1.0 + jnp.exp(-x))


def _rms_scale(x, width):
    return lax.rsqrt(jnp.sum(x * x, axis=-1, keepdims=True) * (1.0 / width) + EPS)


def _adaln_kernel(c_ref, w_ref, b_ref, o_ref):
    c = c_ref[...]
    cond = (c * _sigmoid(c)).astype(BF16)
    o_ref[...] = _dot(cond, w_ref[...].astype(BF16)) + b_ref[...]


def _adaln(c_pad, w_ada, b_ada):
    rows, d = c_pad.shape
    n = w_ada.shape[1]
    tn = 1024
    return pl.pallas_call(
        _adaln_kernel,
        grid=(n // tn,),
        in_specs=[
            pl.BlockSpec((rows, d), lambda j: (0, 0)),
            pl.BlockSpec((d, tn), lambda j: (0, j)),
            pl.BlockSpec((1, tn), lambda j: (0, j)),
        ],
        out_specs=pl.BlockSpec((rows, tn), lambda j: (0, j)),
        out_shape=jax.ShapeDtypeStruct((rows, n), F32),
        compiler_params=_cparams(("arbitrary",), 40),
        name="adaln",
    )(c_pad, w_ada, b_ada.reshape(1, n))


def _inproj_kernel(x_ref, sc_ref, sh_ref, g_ref, w_ref, proj_ref, gates_ref, h_ref, *, gate_block, gate_off):
    j = pl.program_id(1)

    @pl.when(j == 0)
    def _():
        x = x_ref[...]
        h = x * _rms_scale(x, x.shape[-1]) * g_ref[...]
        h_ref[...] = (h * (1.0 + sc_ref[...]) + sh_ref[...]).astype(BF16)

    acc = _dot(h_ref[...], w_ref[...])
    proj_ref[...] = acc.astype(proj_ref.dtype)

    @pl.when(j == gate_block)
    def _():
        gates_ref[...] = acc[:, gate_off:gate_off + LANES]


def _inproj(x2, mod4, norm_g, w_in_p, seq):
    t, d = x2.shape
    n = w_in_p.shape[1]
    tm = min(1024, seq)
    tn = 1024
    bpr = seq // tm
    kern = functools.partial(_inproj_kernel, gate_block=COL_GATES // tn, gate_off=COL_GATES % tn)
    return pl.pallas_call(
        kern,
        grid=(t // tm, n // tn),
        in_specs=[
            pl.BlockSpec((tm, d), lambda i, j: (i, 0)),
            pl.BlockSpec((None, None, 1, d), lambda i, j: (i // bpr, 1, 0, 0)),
            pl.BlockSpec((None, None, 1, d), lambda i, j: (i // bpr, 0, 0, 0)),
            pl.BlockSpec((1, d), lambda i, j: (0, 0)),
            pl.BlockSpec((d, tn), lambda i, j: (0, j)),
        ],
        out_specs=[
            pl.BlockSpec((tm, tn), lambda i, j: (i, j)),
            pl.BlockSpec((tm, LANES), lambda i, j: (i, 0)),
        ],
        out_shape=[
            jax.ShapeDtypeStruct((t, n), BF16),
            jax.ShapeDtypeStruct((t, LANES), F32),
        ],
        scratch_shapes=[pltpu.VMEM((tm, d), BF16)],
        compiler_params=_cparams(("arbitrary", "arbitrary"), 48),
        name="inproj",
    )(x2, mod4, mod4, norm_g.reshape(1, d), w_in_p)


def _mla_prep_kernel(lat_ref, pos_ref, wuq_ref, wukv_ref, gqa_ref, gkva_ref, gq_ref, gk_ref, inv_ref,
                     q_ref, k_ref, v_ref):
    lat = lat_ref[...].astype(F32)
    cq = lat[:, :Q_LORA]
    ckv = lat[:, Q_LORA:Q_LORA + KV_LORA]
    kc = lat[:, Q_LORA + KV_LORA:LAT_W]
    cqn = (cq * _rms_scale(cq, Q_LORA) * gqa_ref[...]).astype(BF16)
    ckvn = (ckv * _rms_scale(ckv, KV_LORA) * gkva_ref[...]).astype(BF16)
    qraw = _dot(cqn, wuq_ref[...])
    kv = _dot(ckvn, wukv_ref[...])

    ang = pos_ref[...].astype(F32) * inv_ref[...]
    lane = lax.broadcasted_iota(jnp.int32, ang.shape, 1)
    lo = lane < MLA_ROPE
    cs = jnp.where(lo, jnp.cos(ang), jnp.sin(ang))

    gq = gq_ref[...]
    gk = gk_ref[...]
    gq_n, gq_r = gq[:, :LANES], gq[:, LANES:]
    gk_n, gk_r = gk[:, :LANES], gk[:, LANES:]

    def rope(chunk, g_r):
        a = chunk * (g_r * cs)
        return jnp.where(lo, a + pltpu.roll(a, MLA_ROPE, 1), 0.0)

    kpe_ss = jnp.sum(jnp.where(lo, kc * kc, 0.0), axis=-1, keepdims=True)
    k_rope = rope(kc, gk_r)
    scale = MLA_QK ** -0.5
    for h in range(MLA_HEADS):
        kn = kv[:, h * MLA_NOPE:(h + 1) * MLA_NOPE]
        sk = lax.rsqrt((jnp.sum(kn * kn, axis=-1, keepdims=True) + kpe_ss) * (1.0 / MLA_QK) + EPS)
        k_ref[:, h * QK_PAD:h * QK_PAD + LANES] = (kn * sk * gk_n).astype(BF16)
        k_ref[:, h * QK_PAD + LANES:(h + 1) * QK_PAD] = (k_rope * sk).astype(BF16)
        qn = qraw[:, h * QK_PAD:h * QK_PAD + LANES]
        qr = qraw[:, h * QK_PAD + LANES:(h + 1) * QK_PAD]
        ss = jnp.sum(qn * qn, axis=-1, keepdims=True) + jnp.sum(jnp.where(lo, qr * qr, 0.0), axis=-1, keepdims=True)
        sq = lax.rsqrt(ss * (1.0 / MLA_QK) + EPS) * scale
        q_ref[:, h * QK_PAD:h * QK_PAD + LANES] = (qn * sq * gq_n).astype(BF16)
        q_ref[:, h * QK_PAD + LANES:(h + 1) * QK_PAD] = (rope(qr, gq_r) * sq).astype(BF16)
    v_ref[...] = kv[:, MLA_HEADS * MLA_NOPE:].astype(BF16)


def _mla_prep(proj, pos2, wuq_p, wukv_p, gqa, gkva, gq, gk, inv_lanes, seq):
    t = proj.shape[0]
    tm = min(512, seq)
    hq = MLA_HEADS * QK_PAD
    lat_blk = COL_LAT // 1024
    const = lambda i: (0, 0)
    return pl.pallas_call(
        _mla_prep_kernel,
        grid=(t // tm,),
        in_specs=[
            pl.BlockSpec((tm, 1024), lambda i: (i, lat_blk)),
            pl.BlockSpec((tm, 1), lambda i: (i, 0)),
            pl.BlockSpec(wuq_p.shape, const),
            pl.BlockSpec(wukv_p.shape, const),
            pl.BlockSpec(gqa.shape, const),
            pl.BlockSpec(gkva.shape, const),
            pl.BlockSpec(gq.shape, const),
            pl.BlockSpec(gk.shape, const),
            pl.BlockSpec(inv_lanes.shape, const),
        ],
        out_specs=[
            pl.BlockSpec((tm, hq), lambda i: (i, 0)),
            pl.BlockSpec((tm, hq), lambda i: (i, 0)),
            pl.BlockSpec((tm, MLA_W), lambda i: (i, 0)),
        ],
        out_shape=[
            jax.ShapeDtypeStruct((t, hq), BF16),
            jax.ShapeDtypeStruct((t, hq), BF16),
            jax.ShapeDtypeStruct((t, MLA_W), BF16),
        ],
        compiler_params=_cparams(("arbitrary",), 48),
        name="mla_prep",
    )(proj, pos2, wuq_p, wukv_p, gqa, gkva, gq, gk, inv_lanes)


def _flash_kernel(q_ref, k_ref, v_ref, o_ref, m_ref, l_ref, acc_ref, *, tq, tk):
    seq = q_ref.shape[0]
    for qi in range(seq // tq):
        q = q_ref[qi * tq:(qi + 1) * tq, :]
        m_ref[...] = jnp.full(m_ref.shape, -jnp.inf, F32)
        l_ref[...] = jnp.zeros(l_ref.shape, F32)
        acc_ref[...] = jnp.zeros(acc_ref.shape, F32)

        def step(start, masked):
            kj = k_ref[pl.ds(start, tk), :]
            vj = v_ref[pl.ds(start, tk), :]
            s = _dot_nt(q, kj)
            if masked:
                row = lax.broadcasted_iota(jnp.int32, s.shape, 0) + qi * tq
                col = lax.broadcasted_iota(jnp.int32, s.shape, 1) + start
                s = jnp.where(col <= row, s, -jnp.inf)
            m_prev = m_ref[...]
            m_new = jnp.maximum(m_prev, jnp.max(s, axis=-1, keepdims=True))
            alpha = jnp.exp(m_prev - m_new)
            p = jnp.exp(s - m_new)
            l_ref[...] = alpha * l_ref[...] + jnp.sum(p, axis=-1, keepdims=True)
            acc_ref[...] = alpha * acc_ref[...] + _dot(p.astype(BF16), vj)
            m_ref[...] = m_new

        n_full = (qi * tq) // tk

        def body(j, carry):
            step(pl.multiple_of(j * tk, tk), False)
            return carry

        lax.fori_loop(0, n_full, body, 0)
        for d in range(tq // tk):
            step(qi * tq + d * tk, True)
        o_ref[qi * tq:(qi + 1) * tq, :] = (acc_ref[...] / l_ref[...]).astype(o_ref.dtype)


def _flash(q3, k3, v3):
    b, seq, _ = q3.shape
    tq = min(512, seq)
    tk = min(512, seq)
    kern = functools.partial(_flash_kernel, tq=tq, tk=tk)
    return pl.pallas_call(
        kern,
        grid=(b, MLA_HEADS),
        in_specs=[
            pl.BlockSpec((None, seq, QK_PAD), lambda i, h: (i, 0, h)),
            pl.BlockSpec((None, seq, QK_PAD), lambda i, h: (i, 0, h)),
            pl.BlockSpec((None, seq, MLA_V), lambda i, h: (i, 0, h)),
        ],
        out_specs=pl.BlockSpec((None, seq, MLA_V), lambda i, h: (i, 0, h)),
        out_shape=jax.ShapeDtypeStruct((b, seq, MLA_W), BF16),
        scratch_shapes=[
            pltpu.VMEM((tq, 1), F32),
            pltpu.VMEM((tq, 1), F32),
            pltpu.VMEM((tq, MLA_V), F32),
        ],
        compiler_params=_cparams(("arbitrary", "arbitrary"), 40),
        name="flash",
    )(q3, k3, v3)


def _conv_kernel(cur_ref, halo_ref, w_ref, b_ref, o_ref, buf_ref, *, k_scale):
    tm = cur_ref.shape[0]
    cols = cur_ref.shape[1]
    first = pl.program_id(1) == 0
    halo = halo_ref[...].astype(F32)
    buf_ref[0:SUBLANES, :] = jnp.where(first, 0.0, halo)
    buf_ref[SUBLANES:SUBLANES + tm, :] = cur_ref[...].astype(F32)
    cw = 512
    for c in range(cols // cw):
        sl = slice(c * cw, (c + 1) * cw)
        acc = jnp.zeros((tm, cw), F32) + b_ref[:, sl]
        for j in range(ML_CONV):
            off = SUBLANES - (ML_CONV - 1) + j
            acc = acc + buf_ref[off:off + tm, sl] * w_ref[j:j + 1, sl]
        y = acc * _sigmoid(acc)
        if c * cw >= cols // 2:
            y = y * k_scale
        o_ref[:, sl] = y.astype(o_ref.dtype)


def _conv_silu(proj3, conv_w, conv_b):
    b, seq, _ = proj3.shape
    cols = 2 * ML_HEADS * ML_DQK
    tm = min(512, seq)
    hb = tm // SUBLANES
    kern = functools.partial(_conv_kernel, k_scale=ML_DQK ** -0.5)
    return pl.pallas_call(
        kern,
        grid=(b, seq // tm),
        in_specs=[
            pl.BlockSpec((None, tm, cols), lambda i, s: (i, s, COL_QK // cols)),
            pl.BlockSpec((None, SUBLANES, cols), lambda i, s: (i, jnp.maximum(s * hb - 1, 0), COL_QK // cols)),
            pl.BlockSpec((ML_CONV, cols), lambda i, s: (0, 0)),
            pl.BlockSpec((1, cols), lambda i, s: (0, 0)),
        ],
        out_specs=pl.BlockSpec((None, tm, cols), lambda i, s: (i, s, 0)),
        out_shape=jax.ShapeDtypeStruct((b, seq, cols), BF16),
        scratch_shapes=[pltpu.VMEM((tm + SUBLANES, cols), F32)],
        compiler_params=_cparams(("arbitrary", "arbitrary"), 40),
        name="conv_silu",
    )(proj3, proj3, conv_w, conv_b.reshape(1, cols))


def _log_sigmoid(x):
    return -(jnp.maximum(-x, 0.0) + jnp.log1p(jnp.exp(-jnp.abs(x))))


def _mlstm_kernel(q_ref, k_ref, v_ref, o_ref, gates_ref, gbias_ref, ng_ref, out_ref, c_ref, m_ref):
    L = q_ref.shape[0]

    @pl.when(pl.program_id(1) == 0)
    def _():
        c_ref[...] = jnp.zeros(c_ref.shape, F32)
        m_ref[...] = jnp.zeros(m_ref.shape, F32)

    g = gates_ref[...] + gbias_ref[...]
    gt = g.T
    lf = _log_sigmoid(g)
    lft = _log_sigmoid(gt)
    r = lax.broadcasted_iota(jnp.int32, (L, L), 0)
    c = lax.broadcasted_iota(jnp.int32, (L, L), 1)
    causal = c <= r
    tril = causal.astype(F32)
    triu = (r <= c).astype(F32)
    hi = lax.Precision.HIGHEST
    bcol_all = jnp.dot(tril, lf, preferred_element_type=F32, precision=hi)
    brow_all = jnp.dot(lft, triu, preferred_element_type=F32, precision=hi)
    lane = lax.broadcasted_iota(jnp.int32, (L, ML_DV), 1)
    ones_col = jnp.where(lane == 0, 1.0, 0.0).astype(BF16)

    for h in range(ML_HEADS):
        hs = slice(h * ML_DQK, (h + 1) * ML_DQK)
        bcol = bcol_all[:, ML_HEADS + h:ML_HEADS + h + 1]
        brow = brow_all[ML_HEADS + h:ML_HEADS + h + 1, :]
        icol = g[:, h:h + 1]
        irow = gt[h:h + 1, :]
        m_prev = m_ref[h][:, :1]
        logw = jnp.where(causal, bcol - brow + irow, -jnp.inf)
        log_inter = bcol + m_prev
        m_t = jnp.maximum(jnp.max(logw, axis=-1, keepdims=True), log_inter)
        w = jnp.exp(logw - m_t)
        a = jnp.exp(log_inter - m_t)
        qh = q_ref[:, hs]
        kh = k_ref[:, hs]
        v_aug = jnp.concatenate([v_ref[:, hs], ones_col], axis=-1)
        s = _dot_nt(qh, kh) * w
        c_aug = c_ref[h]
        nd = _dot(s.astype(BF16), v_aug) + a * _dot(qh, c_aug.astype(BF16))
        num = nd[:, :ML_DV]
        den = nd[:, ML_DV:ML_DV + 1]
        hout = num / jnp.maximum(jnp.abs(den), jnp.exp(-m_t))
        hn = hout * _rms_scale(hout, ML_DV) * ng_ref[:, hs]
        gate = _sigmoid(o_ref[:, hs].astype(F32))
        out_ref[:, hs] = (hn * gate).astype(out_ref.dtype)

        blast = bcol[L - 1:L, :]
        logg = blast - bcol + icol
        m_new = jnp.maximum(blast + m_prev, jnp.max(logg, axis=0, keepdims=True))
        gcol = jnp.exp(logg - m_new)
        decay = jnp.exp(blast + m_prev - m_new)
        kg = (kh.astype(F32) * gcol).astype(BF16)
        c_ref[h] = decay * c_aug + _dot_tn(kg, v_aug)
        m_ref[h] = jnp.broadcast_to(m_new, m_ref.shape[1:])


def _mlstm(qk3, proj3, gates3, gbias, norm_g):
    b, seq, _ = qk3.shape
    L = min(MLSTM_CHUNK, seq)
    w = ML_W
    return pl.pallas_call(
        _mlstm_kernel,
        grid=(b, seq // L),
        in_specs=[
            pl.BlockSpec((None, L, w), lambda i, c: (i, c, 0)),
            pl.BlockSpec((None, L, w), lambda i, c: (i, c, 1)),
            pl.BlockSpec((None, L, w), lambda i, c: (i, c, COL_V // w)),
            pl.BlockSpec((None, L, w), lambda i, c: (i, c, COL_O // w)),
            pl.BlockSpec((None, L, LANES), lambda i, c: (i, c, 0)),
            pl.BlockSpec((1, LANES), lambda i, c: (0, 0)),
            pl.BlockSpec((1, w), lambda i, c: (0, 0)),
        ],
        out_specs=pl.BlockSpec((None, L, w), lambda i, c: (i, c, 0)),
        out_shape=jax.ShapeDtypeStruct((b, seq, w), BF16),
        scratch_shapes=[
            pltpu.VMEM((ML_HEADS, ML_DQK, 2 * ML_DV), F32),
            pltpu.VMEM((ML_HEADS, 1, LANES), F32),
        ],
        compiler_params=_cparams(("arbitrary", "arbitrary"), 40),
        name="mlstm",
    )(qk3, qk3, proj3, proj3, gates3, gbias, norm_g.reshape(1, w))


def _merge_kernel(a_ref, b_ref, ga_ref, gb_ref, wa_ref, wb_ref, o_ref):
    pa = _dot(a_ref[...], wa_ref[...])
    pb = _dot(b_ref[...], wb_ref[...])
    mixed = _sigmoid(ga_ref[...].astype(F32)) * pa + _sigmoid(gb_ref[...].astype(F32)) * pb
    o_ref[...] = mixed.astype(o_ref.dtype)


def _merge(out_a, hm, proj, wa, wb, seq):
    t = out_a.shape[0]
    d = wa.shape[1]
    tm = min(1024, seq)
    tn = 1024
    return pl.pallas_call(
        _merge_kernel,
        grid=(d // tn, t // tm),
        in_specs=[
            pl.BlockSpec((tm, MLA_W), lambda j, i: (i, 0)),
            pl.BlockSpec((tm, ML_W), lambda j, i: (i, 0)),
            pl.BlockSpec((tm, tn), lambda j, i: (i, COL_GA // tn + j)),
            pl.BlockSpec((tm, tn), lambda j, i: (i, COL_GB // tn + j)),
            pl.BlockSpec((MLA_W, tn), lambda j, i: (0, j)),
            pl.BlockSpec((ML_W, tn), lambda j, i: (0, j)),
        ],
        out_specs=pl.BlockSpec((tm, tn), lambda j, i: (i, j)),
        out_shape=jax.ShapeDtypeStruct((t, d), BF16),
        compiler_params=_cparams(("arbitrary", "arbitrary"), 40),
        name="merge",
    )(out_a, hm, proj, proj, wa, wb)


def _outproj_kernel(mix_ref, x_ref, w_ref, gt_ref, sc_ref, sh_ref, g_ref, wr_ref, br_ref, x1_ref, h2_ref, lg_ref):
    y = _dot(mix_ref[...], w_ref[...])
    x1 = x_ref[...] + gt_ref[...] * y
    x1_ref[...] = x1
    h2 = x1 * _rms_scale(x1, x1.shape[-1]) * g_ref[...]
    h2 = h2 * (1.0 + sc_ref[...]) + sh_ref[...]
    h2_ref[...] = h2
    lg_ref[...] = jnp.dot(h2, wr_ref[...], preferred_element_type=F32, precision=lax.Precision.HIGHEST) + br_ref[...]


def _outproj(mixed, x2, w_out, mod4, norm_g, w_route, b_route, seq):
    t, d = x2.shape
    tm = min(256, seq)
    bpr = seq // tm
    mod_spec = lambda k: pl.BlockSpec((None, None, 1, d), lambda i: (i // bpr, k, 0, 0))
    const = lambda i: (0, 0)
    return pl.pallas_call(
        _outproj_kernel,
        grid=(t // tm,),
        in_specs=[
            pl.BlockSpec((tm, d), lambda i: (i, 0)),
            pl.BlockSpec((tm, d), lambda i: (i, 0)),
            pl.BlockSpec((d, d), const),
            mod_spec(2),
            mod_spec(4),
            mod_spec(3),
            pl.BlockSpec((1, d), const),
            pl.BlockSpec((d, LANES), const),
            pl.BlockSpec((1, LANES), const),
        ],
        out_specs=[
            pl.BlockSpec((tm, d), lambda i: (i, 0)),
            pl.BlockSpec((tm, d), lambda i: (i, 0)),
            pl.BlockSpec((tm, LANES), lambda i: (i, 0)),
        ],
        out_shape=[
            jax.ShapeDtypeStruct((t, d), F32),
            jax.ShapeDtypeStruct((t, d), F32),
            jax.ShapeDtypeStruct((t, LANES), F32),
        ],
        compiler_params=_cparams(("arbitrary",), 56),
        name="outproj",
    )(mixed, x2, w_out, mod4, mod4, mod4, norm_g.reshape(1, d), w_route, b_route)


def _route_kernel(lg_ref, meta_ref, wts_ref, cnt_ref, carry_ref):
    tm = lg_ref.shape[0]

    @pl.when(pl.program_id(0) == 0)
    def _():
        carry_ref[...] = jnp.zeros(carry_ref.shape, F32)

    lg = lg_ref[...]
    lane = lax.broadcasted_iota(jnp.int32, lg.shape, 1)
    big = jnp.int32(LANES)
    ninf = -jnp.inf

    def first_argmax(vals):
        mx = jnp.max(vals, axis=-1, keepdims=True)
        idx = jnp.min(jnp.where(vals == mx, lane, big), axis=-1, keepdims=True)
        return mx, idx

    gl = jnp.where(lane < N_GROUPS, lg, ninf)
    gmax, gsel = first_argmax(gl)
    g_w = 1.0 / jnp.sum(jnp.exp(gl - gmax), axis=-1, keepdims=True)
    lo = N_GROUPS + gsel * EXP_PER_GROUP
    in_grp = (lane >= lo) & (lane < lo + EXP_PER_GROUP)
    el = jnp.where(in_grp, lg, ninf)
    e1, i1 = first_argmax(el)
    e2, i2 = first_argmax(jnp.where(lane == i1, ninf, el))
    p2 = jnp.exp(e2 - e1)
    w1 = g_w / (1.0 + p2)
    w2 = g_w * p2 / (1.0 + p2)

    oh1 = lane == i1
    oh2 = lane == i2
    oh = jnp.where(oh1 | oh2, 1.0, 0.0)
    r = lax.broadcasted_iota(jnp.int32, (tm, tm), 0)
    c = lax.broadcasted_iota(jnp.int32, (tm, tm), 1)
    strict = jnp.where(c < r, 1.0, 0.0).astype(BF16)
    before = _dot(strict, oh.astype(BF16)) + carry_ref[...]
    rank1 = jnp.sum(jnp.where(oh1, before, 0.0), axis=-1, keepdims=True).astype(jnp.int32)
    rank2 = jnp.sum(jnp.where(oh2, before, 0.0), axis=-1, keepdims=True).astype(jnp.int32)
    carry = carry_ref[...] + jnp.sum(oh, axis=0, keepdims=True)
    carry_ref[...] = carry
    cnt_ref[...] = jnp.broadcast_to(carry, cnt_ref.shape)

    ex1 = i1 - N_GROUPS
    ex2 = i2 - N_GROUPS
    meta = jnp.where(lane == 0, ex1, jnp.where(lane == 1, ex2, jnp.where(lane == 2, rank1, jnp.where(lane == 3, rank2, 0))))
    meta_ref[...] = meta
    wts_ref[...] = jnp.where(lane == 0, w1, jnp.where(lane == 1, w2, 0.0))


def _route(logits):
    t = logits.shape[0]
    tm = min(512, t)
    return pl.pallas_call(
        _route_kernel,
        grid=(t // tm,),
        in_specs=[pl.BlockSpec((tm, LANES), lambda i: (i, 0))],
        out_specs=[
            pl.BlockSpec((tm, LANES), lambda i: (i, 0)),
            pl.BlockSpec((tm, LANES), lambda i: (i, 0)),
            pl.BlockSpec((SUBLANES, LANES), lambda i: (0, 0)),
        ],
        out_shape=[
            jax.ShapeDtypeStruct((t, LANES), jnp.int32),
            jax.ShapeDtypeStruct((t, LANES), F32),
            jax.ShapeDtypeStruct((SUBLANES, LANES), F32),
        ],
        scratch_shapes=[pltpu.VMEM((1, LANES), F32)],
        compiler_params=_cparams(("arbitrary",), 32),
        name="route",
    )(logits)


def _row_gather_start(idx_ref, base, src_hbm, dst, sem, rows):
    def body(r, carry):
        tok = idx_ref[base + r]
        pltpu.make_async_copy(src_hbm.at[pl.ds(tok, 1)], dst.at[pl.ds(r, 1)], sem).start()
        return carry

    lax.fori_loop(0, rows, body, 0)


def _row_gather_wait(src_hbm, dst, sem, rows):
    pltpu.make_async_copy(src_hbm.at[pl.ds(0, rows)], dst, sem).wait()


def _moe_kernel(texp_ref, nt_ref, src_ref, h_hbm, wg_ref, wu_ref, wd_ref, y_ref, xbuf, sem):
    j = pl.program_id(0)
    nt = nt_ref[0]
    tile = xbuf.shape[1]

    @pl.when(j == 0)
    def _():
        _row_gather_start(src_ref, 0, h_hbm, xbuf.at[0], sem.at[0], tile)

    for slot in range(2):
        @pl.when((j + 1 < nt) & (lax.rem(j + 1, 2) == slot))
        def _():
            _row_gather_start(src_ref, (j + 1) * tile, h_hbm, xbuf.at[slot], sem.at[slot], tile)

    for slot in range(2):
        @pl.when((j < nt) & (lax.rem(j, 2) == slot))
        def _():
            _row_gather_wait(h_hbm, xbuf.at[slot], sem.at[slot], tile)
            x = xbuf[slot].astype(BF16)
            hg = _dot(x, wg_ref[...].astype(BF16))
            hu = _dot(x, wu_ref[...].astype(BF16))
            act = (hg * _sigmoid(hg) * hu).astype(BF16)
            y_ref[...] = _dot(act, wd_ref[...].astype(BF16))

    @pl.when(j >= nt)
    def _():
        y_ref[...] = jnp.zeros(y_ref.shape, y_ref.dtype)


def _moe(tile_expert, n_tiles, src, h2, w_gate_e, w_up_e, w_down_e):
    t, d = h2.shape
    f = w_gate_e.shape[-1]
    nt_max = tile_expert.shape[0]
    tile = MOE_TILE

    def w_idx(j, texp, nt, src_):
        return (texp[jnp.minimum(j, nt[0] - 1)], 0, 0)

    grid_spec = pltpu.PrefetchScalarGridSpec(
        num_scalar_prefetch=3,
        grid=(nt_max,),
        in_specs=[
            pl.BlockSpec(memory_space=pl.ANY),
            pl.BlockSpec((None, d, f), w_idx),
            pl.BlockSpec((None, d, f), w_idx),
            pl.BlockSpec((None, f, d), w_idx),
        ],
        out_specs=pl.BlockSpec((tile, d), lambda j, texp, nt, src_: (j, 0)),
        scratch_shapes=[
            pltpu.VMEM((2, tile, d), F32),
            pltpu.SemaphoreType.DMA((2,)),
        ],
    )
    return pl.pallas_call(
        _moe_kernel,
        grid_spec=grid_spec,
        out_shape=jax.ShapeDtypeStruct((nt_max * tile, d), F32),
        compiler_params=_cparams(("arbitrary",), 56),
        name="moe_experts",
    )(tile_expert, n_tiles, src, h2, w_gate_e, w_up_e, w_down_e)


def _combine_kernel(pos_ref, y_hbm, x1_ref, gt_ref, wts_ref, o_ref, ybuf, sem):
    i = pl.program_id(0)
    n = pl.num_programs(0)
    tm = x1_ref.shape[0]

    def start(blk, slot):
        for k in range(2):
            _row_gather_start(pos_ref, (2 * blk + k) * tm, y_hbm, ybuf.at[slot, k], sem.at[slot], tm)

    @pl.when(i == 0)
    def _():
        start(0, 0)

    for slot in range(2):
        @pl.when((i + 1 < n) & (lax.rem(i + 1, 2) == slot))
        def _():
            start(i + 1, slot)

    for slot in range(2):
        @pl.when(lax.rem(i, 2) == slot)
        def _():
            for k in range(2):
                _row_gather_wait(y_hbm, ybuf.at[slot, k], sem.at[slot], tm)
            w = wts_ref[...]
            moe = w[:, 0:1] * ybuf[slot, 0] + w[:, 1:2] * ybuf[slot, 1]
            o_ref[...] = x1_ref[...] + gt_ref[...] * moe


def _combine(pos, ys, x1, mod4, wts, seq):
    t, d = x1.shape
    tm = min(256, seq)
    bpr = seq // tm
    pos_blocks = pos.reshape(t // tm, tm, 2).transpose(0, 2, 1).reshape(-1)
    grid_spec = pltpu.PrefetchScalarGridSpec(
        num_scalar_prefetch=1,
        grid=(t // tm,),
        in_specs=[
            pl.BlockSpec(memory_space=pl.ANY),
            pl.BlockSpec((tm, d), lambda i, p: (i, 0)),
            pl.BlockSpec((None, None, 1, d), lambda i, p: (i // bpr, 5, 0, 0)),
            pl.BlockSpec((tm, LANES), lambda i, p: (i, 0)),
        ],
        out_specs=pl.BlockSpec((tm, d), lambda i, p: (i, 0)),
        scratch_shapes=[
            pltpu.VMEM((2, 2, tm, d), F32),
            pltpu.SemaphoreType.DMA((2,)),
        ],
    )
    return pl.pallas_call(
        _combine_kernel,
        grid_spec=grid_spec,
        out_shape=jax.ShapeDtypeStruct((t, d), F32),
        compiler_params=_cparams(("arbitrary",), 48),
        name="moe_combine",
    )(pos_blocks, ys, x1, mod4, wts)


def _swap_halves(w):
    half = w.shape[-1] // 2
    return jnp.concatenate([w[..., half:], w[..., :half]], axis=-1)


def _layout_w_in(w_in):
    d = w_in.shape[0]
    o = 0
    seg = {}
    for name, size in (("cq", Q_LORA), ("ckv", KV_LORA), ("kpe", MLA_ROPE), ("qk", 2 * ML_HEADS * ML_DQK),
                       ("v", ML_W), ("o", ML_W), ("i", ML_HEADS), ("f", ML_HEADS), ("ga", D_MODEL), ("gb", D_MODEL)):
        seg[name] = w_in[:, o:o + size]
        o += size
    zeros = lambda n: jnp.zeros((d, n), w_in.dtype)
    cols = [seg["qk"], seg["ga"], seg["gb"], seg["v"], seg["o"], seg["cq"], seg["ckv"], seg["kpe"],
            _swap_halves(seg["kpe"]), seg["i"], seg["f"], zeros(IN_COLS_PAD - COL_GATES - 2 * ML_HEADS)]
    return jnp.concatenate(cols, axis=1).astype(BF16)


def _layout_w_uq(w_uq):
    r = w_uq.shape[0]
    w = w_uq.reshape(r, MLA_HEADS, MLA_QK)
    rope = w[..., MLA_NOPE:]
    return jnp.concatenate([w[..., :MLA_NOPE], rope, _swap_halves(rope)], axis=-1).reshape(r, MLA_HEADS * QK_PAD).astype(BF16)


def _layout_w_ukv(w_ukv):
    r = w_ukv.shape[0]
    w = w_ukv.reshape(r, MLA_HEADS, MLA_NOPE + MLA_V)
    return jnp.concatenate([w[..., :MLA_NOPE].reshape(r, -1), w[..., MLA_NOPE:].reshape(r, -1)], axis=-1).astype(BF16)


def _rope_gain(g):
    g1 = g[MLA_NOPE:MLA_NOPE + MLA_ROPE // 2]
    g2 = g[MLA_NOPE + MLA_ROPE // 2:]
    return jnp.concatenate([g[:MLA_NOPE], g1, g2, -g2, g1]).reshape(1, QK_PAD)


def _layer(x2, cond_mod4, pos2, seq, w_in, q_a_norm_g, w_uq, kv_a_norm_g, w_ukv, q_norm_g, k_norm_g, conv_w, conv_b,
           b_mlstm_gates, mlstm_norm_g, w_proj_a, w_proj_b, w_out, norm_mix_g, norm_ffn_g, w_group, b_group,
           w_router, b_router, w_gate_e, w_up_e, w_down_e):
    t, d = x2.shape
    b = t // seq
    mod4 = cond_mod4

    proj, gates = _inproj(x2, mod4, norm_mix_g, _layout_w_in(w_in), seq)

    inv = ROPE_THETA ** (-jnp.arange(0, MLA_ROPE, 2, dtype=F32) / MLA_ROPE)
    inv_lanes = jnp.tile(inv, LANES // (MLA_ROPE // 2)).reshape(1, LANES)
    q, k, v = _mla_prep(proj, pos2, _layout_w_uq(w_uq), _layout_w_ukv(w_ukv), q_a_norm_g.reshape(1, -1),
                        kv_a_norm_g.reshape(1, -1), _rope_gain(q_norm_g), _rope_gain(k_norm_g), inv_lanes, seq)
    out_a = _flash(q.reshape(b, seq, -1), k.reshape(b, seq, -1), v.reshape(b, seq, -1)).reshape(t, MLA_W)

    proj3 = proj.reshape(b, seq, -1)
    qk3 = _conv_silu(proj3, conv_w, conv_b)
    gbias = jnp.zeros((1, LANES), F32).at[0, :2 * ML_HEADS].set(b_mlstm_gates.reshape(-1))
    hm = _mlstm(qk3, proj3, gates.reshape(b, seq, LANES), gbias, mlstm_norm_g).reshape(t, ML_W)

    mixed = _merge(out_a, hm, proj, w_proj_a.astype(BF16), w_proj_b.astype(BF16), seq)

    w_route = jnp.zeros((d, LANES), F32).at[:, :N_GROUPS].set(w_group).at[:, N_GROUPS:N_GROUPS + N_EXPERTS].set(w_router)
    b_route = jnp.zeros((1, LANES), F32).at[0, :N_GROUPS].set(b_group).at[0, N_GROUPS:N_GROUPS + N_EXPERTS].set(b_router)
    x1, h2, logits = _outproj(mixed, x2, w_out.astype(BF16), mod4, norm_ffn_g, w_route, b_route, seq)

    meta, wts, counts = _route(logits)

    tile = MOE_TILE
    cnt = counts[0, N_GROUPS:N_GROUPS + N_EXPERTS].astype(jnp.int32)
    padded = ((cnt + tile - 1) // tile) * tile
    ends = jnp.cumsum(padded)
    offs = ends - padded
    nt_max = (2 * t) // tile + N_EXPERTS
    pos = offs[meta[:, 0:2]] + meta[:, 2:4]
    tok = jnp.broadcast_to(jnp.arange(t, dtype=jnp.int32)[:, None], (t, 2))
    src = jnp.zeros((nt_max * tile,), jnp.int32).at[pos.reshape(-1)].set(tok.reshape(-1))
    tile_start = jnp.arange(nt_max, dtype=jnp.int32) * tile
    tile_expert = jnp.minimum(jnp.searchsorted(ends, tile_start, side="right"), N_EXPERTS - 1).astype(jnp.int32)
    n_tiles = (ends[-1] // tile).astype(jnp.int32).reshape(1)

    ys = _moe(tile_expert, n_tiles, src, h2, w_gate_e, w_up_e, w_down_e)

    return _combine(pos, ys, x1, mod4, wts, seq)


def kernel(x, c, positions, w_ada, b_ada, norm_mix_g, w_in, q_a_norm_g, w_uq, kv_a_norm_g, w_ukv, q_norm_g, k_norm_g, conv_w, conv_b, b_mlstm_gates, mlstm_norm_g, w_proj_a, w_proj_b, w_out, norm_ffn_g, w_group, b_group, w_router, b_router, w_gate_e, w_up_e, w_down_e):
    b, seq, d = x.shape
    depth = w_ada.shape[0]
    x2 = x.reshape(b * seq, d)
    pos2 = positions.reshape(b * seq, 1)
    c_pad = jnp.zeros((SUBLANES, d), F32).at[:b].set(c)
    for l in range(depth):
        mod = _adaln(c_pad, w_ada[l], b_ada[l])
        mod4 = mod[:b].reshape(b, 6, 1, d)
        x2 = _layer(x2, mod4, pos2, seq, w_in[l], q_a_norm_g[l], w_uq[l], kv_a_norm_g[l], w_ukv[l], q_norm_g[l],
                    k_norm_g[l], conv_w[l], conv_b[l], b_mlstm_gates[l], mlstm_norm_g[l], w_proj_a[l], w_proj_b[l],
                    w_out[l], norm_mix_g[l], norm_ffn_g[l], w_group[l], b_group[l], w_router[l], b_router[l],
                    w_gate_e[l], w_up_e[l], w_down_e[l])
    return x2.reshape(b, seq, d)
```

```python
import functools
import math

import jax
import jax.numpy as jnp
from jax import lax
from jax.experimental import pallas as pl
from jax.experimental.pallas import tpu as pltpu

F32 = jnp.float32
BF16 = jnp.bfloat16

LANES = 128
SUBLANES = 8

D_MODEL = 2048
MLA_HEADS = 8
MLA_NOPE = 128
MLA_ROPE = 64
MLA_QK = MLA_NOPE + MLA_ROPE
MLA_V = 128
Q_LORA = 512
KV_LORA = 256
ROPE_THETA = 10000.0
ML_HEADS = 8
ML_DQK = 128
ML_DV = 128
ML_CONV = 4
MLA_W = MLA_HEADS * MLA_V
ML_W = ML_HEADS * ML_DV
N_GROUPS = 4
EXP_PER_GROUP = 8
N_EXPERTS = N_GROUPS * EXP_PER_GROUP
D_FF_EXPERT = 512
EPS = 1e-6

QK_PAD = 2 * LANES

COL_QK = 0
COL_GA = 2048
COL_GB = 4096
COL_V = 6144
COL_O = 7168
COL_LAT = 8192
COL_GATES = 9088
IN_COLS_PAD = 9216
LAT_W = Q_LORA + KV_LORA + 2 * MLA_ROPE

MLSTM_CHUNK = 256
MOE_TILE = 256


def _cparams(sem, vmem_mb):
    return pltpu.CompilerParams(dimension_semantics=sem, vmem_limit_bytes=vmem_mb * 1024 * 1024)


def _dot(a, b):
    return jnp.dot(a, b, preferred_element_type=F32)


def _dot_nt(a, b):
    return lax.dot_general(a, b, (((1,), (1,)), ((), ())), preferred_element_type=F32)


def _dot_tn(a, b):
    return lax.dot_general(a, b, (((0,), (0,)), ((), ())), preferred_element_type=F32)


def _sigmoid(x):
    return 1.0 / (1.0 + jnp.exp(-x))


def _rms_scale(x, width):
    return lax.rsqrt(jnp.sum(x * x, axis=-1, keepdims=True) * (1.0 / width) + EPS)


def _adaln_kernel(c_ref, w_ref, b_ref, o_ref):
    c = c_ref[...]
    cond = (c * _sigmoid(c)).astype(BF16)
    o_ref[...] = _dot(cond, w_ref[...].astype(BF16)) + b_ref[...]


def _adaln(c_pad, w_ada, b_ada):
    rows, d = c_pad.shape
    n = w_ada.shape[1]
    tn = 1024
    return pl.pallas_call(
        _adaln_kernel,
        grid=(n // tn,),
        in_specs=[
            pl.BlockSpec((rows, d), lambda j: (0, 0)),
            pl.BlockSpec((d, tn), lambda j: (0, j)),
            pl.BlockSpec((1, tn), lambda j: (0, j)),
        ],
        out_specs=pl.BlockSpec((rows, tn), lambda j: (0, j)),
        out_shape=jax.ShapeDtypeStruct((rows, n), F32),
        compiler_params=_cparams(("arbitrary",), 40),
        name="adaln",
    )(c_pad, w_ada, b_ada.reshape(1, n))


def _inproj_kernel(x_ref, sc_ref, sh_ref, g_ref, w_ref, proj_ref, gates_ref, h_ref, *, gate_block, gate_off):
    j = pl.program_id(1)

    @pl.when(j == 0)
    def _():
        x = x_ref[...]
        h = x * _rms_scale(x, x.shape[-1]) * g_ref[...]
        h_ref[...] = (h * (1.0 + sc_ref[...]) + sh_ref[...]).astype(BF16)

    acc = _dot(h_ref[...], w_ref[...])
    proj_ref[...] = acc.astype(proj_ref.dtype)

    @pl.when(j == gate_block)
    def _():
        gates_ref[...] = acc[:, gate_off:gate_off + LANES]


def _inproj(x2, mod4, norm_g, w_in_p, seq):
    t, d = x2.shape
    n = w_in_p.shape[1]
    tm = min(1024, seq)
    tn = 1024
    bpr = seq // tm
    kern = functools.partial(_inproj_kernel, gate_block=COL_GATES // tn, gate_off=COL_GATES % tn)
    return pl.pallas_call(
        kern,
        grid=(t // tm, n // tn),
        in_specs=[
            pl.BlockSpec((tm, d), lambda i, j: (i, 0)),
            pl.BlockSpec((None, None, 1, d), lambda i, j: (i // bpr, 1, 0, 0)),
            pl.BlockSpec((None, None, 1, d), lambda i, j: (i // bpr, 0, 0, 0)),
            pl.BlockSpec((1, d), lambda i, j: (0, 0)),
            pl.BlockSpec((d, tn), lambda i, j: (0, j)),
        ],
        out_specs=[
            pl.BlockSpec((tm, tn), lambda i, j: (i, j)),
            pl.BlockSpec((tm, LANES), lambda i, j: (i, 0)),
        ],
        out_shape=[
            jax.ShapeDtypeStruct((t, n), BF16),
            jax.ShapeDtypeStruct((t, LANES), F32),
        ],
        scratch_shapes=[pltpu.VMEM((tm, d), BF16)],
        compiler_params=_cparams(("arbitrary", "arbitrary"), 48),
        name="inproj",
    )(x2, mod4, mod4, norm_g.reshape(1, d), w_in_p)


def _mla_prep_kernel(lat_ref, pos_ref, wuq_ref, wukv_ref, gqa_ref, gkva_ref, gq_ref, gk_ref, inv_ref,
                     qt_ref, k_ref, vt_ref):
    lat = lat_ref[...].astype(F32)
    cq = lat[:, :Q_LORA]
    ckv = lat[:, Q_LORA:Q_LORA + KV_LORA]
    kc = lat[:, Q_LORA + KV_LORA:LAT_W]
    cqn = (cq * _rms_scale(cq, Q_LORA) * gqa_ref[...]).astype(BF16)
    ckvn = (ckv * _rms_scale(ckv, KV_LORA) * gkva_ref[...]).astype(BF16)
    qraw = _dot(cqn, wuq_ref[...])
    kv = _dot(ckvn, wukv_ref[...])

    ang = pos_ref[...].astype(F32) * inv_ref[...]
    lane = lax.broadcasted_iota(jnp.int32, ang.shape, 1)
    lo = lane < MLA_ROPE
    cs = jnp.where(lo, jnp.cos(ang), jnp.sin(ang))

    gq = gq_ref[...]
    gk = gk_ref[...]
    gq_n, gq_r = gq[:, :LANES], gq[:, LANES:]
    gk_n, gk_r = gk[:, :LANES], gk[:, LANES:]

    def rope(chunk, g_r):
        a = chunk * (g_r * cs)
        return jnp.where(lo, a + pltpu.roll(a, MLA_ROPE, 1), 0.0)

    kpe_ss = jnp.sum(jnp.where(lo, kc * kc, 0.0), axis=-1, keepdims=True)
    k_rope = rope(kc, gk_r)
    scale = MLA_QK ** -0.5
    for h in range(MLA_HEADS):
        kn = kv[:, h * MLA_NOPE:(h + 1) * MLA_NOPE]
        sk = lax.rsqrt((jnp.sum(kn * kn, axis=-1, keepdims=True) + kpe_ss) * (1.0 / MLA_QK) + EPS)
        k_ref[:, h * QK_PAD:h * QK_PAD + LANES] = (kn * sk * gk_n).astype(BF16)
        k_ref[:, h * QK_PAD + LANES:(h + 1) * QK_PAD] = (k_rope * sk).astype(BF16)
        qn = qraw[:, h * QK_PAD:h * QK_PAD + LANES]
        qr = qraw[:, h * QK_PAD + LANES:(h + 1) * QK_PAD]
        ss = jnp.sum(qn * qn, axis=-1, keepdims=True) + jnp.sum(jnp.where(lo, qr * qr, 0.0), axis=-1, keepdims=True)
        sq = lax.rsqrt(ss * (1.0 / MLA_QK) + EPS) * scale
        qt_ref[h * QK_PAD:h * QK_PAD + LANES, :] = (qn * sq * gq_n).T.astype(BF16)
        qt_ref[h * QK_PAD + LANES:(h + 1) * QK_PAD, :] = (rope(qr, gq_r) * sq).T.astype(BF16)
        vh = kv[:, MLA_HEADS * MLA_NOPE + h * MLA_V:MLA_HEADS * MLA_NOPE + (h + 1) * MLA_V]
        vt_ref[h * MLA_V:(h + 1) * MLA_V, :] = vh.T.astype(BF16)


def _mla_prep(proj, pos2, wuq_p, wukv_p, gqa, gkva, gq, gk, inv_lanes, seq):
    t = proj.shape[0]
    tm = min(512, seq)
    hq = MLA_HEADS * QK_PAD
    lat_blk = COL_LAT // 1024
    const = lambda i: (0, 0)
    return pl.pallas_call(
        _mla_prep_kernel,
        grid=(t // tm,),
        in_specs=[
            pl.BlockSpec((tm, 1024), lambda i: (i, lat_blk)),
            pl.BlockSpec((tm, 1), lambda i: (i, 0)),
            pl.BlockSpec(wuq_p.shape, const),
            pl.BlockSpec(wukv_p.shape, const),
            pl.BlockSpec(gqa.shape, const),
            pl.BlockSpec(gkva.shape, const),
            pl.BlockSpec(gq.shape, const),
            pl.BlockSpec(gk.shape, const),
            pl.BlockSpec(inv_lanes.shape, const),
        ],
        out_specs=[
            pl.BlockSpec((hq, tm), lambda i: (0, i)),
            pl.BlockSpec((tm, hq), lambda i: (i, 0)),
            pl.BlockSpec((MLA_W, tm), lambda i: (0, i)),
        ],
        out_shape=[
            jax.ShapeDtypeStruct((hq, t), BF16),
            jax.ShapeDtypeStruct((t, hq), BF16),
            jax.ShapeDtypeStruct((MLA_W, t), BF16),
        ],
        compiler_params=_cparams(("arbitrary",), 48),
        name="mla_prep",
    )(proj, pos2, wuq_p, wukv_p, gqa, gkva, gq, gk, inv_lanes)


def _flash_kernel(qt_ref, k_ref, vt_ref, o_ref, *, tq, tk):
    seq = k_ref.shape[0]
    for qi in range(seq // tq):
        qt = qt_ref[:, qi * tq:(qi + 1) * tq]
        m = jnp.full((1, tq), -jnp.inf, F32)
        l = jnp.zeros((1, tq), F32)
        acc = jnp.zeros((MLA_V, tq), F32)
        n_blocks = ((qi + 1) * tq) // tk
        for j in range(n_blocks):
            st = _dot(k_ref[j * tk:(j + 1) * tk, :], qt)
            if (j + 1) * tk - 1 > qi * tq:
                key = lax.broadcasted_iota(jnp.int32, st.shape, 0) + j * tk
                qry = lax.broadcasted_iota(jnp.int32, st.shape, 1) + qi * tq
                st = jnp.where(key <= qry, st, -jnp.inf)
            m_new = jnp.maximum(m, jnp.max(st, axis=0, keepdims=True))
            alpha = jnp.exp(m - m_new)
            p = jnp.exp(st - m_new)
            l = alpha * l + jnp.sum(p, axis=0, keepdims=True)
            acc = alpha * acc + _dot(vt_ref[:, j * tk:(j + 1) * tk], p.astype(BF16))
            m = m_new
        o_ref[qi * tq:(qi + 1) * tq, :] = (acc / l).T.astype(o_ref.dtype)


def _flash(qt, k3, vt):
    b, seq, _ = k3.shape
    tq = min(512, seq)
    tk = min(512, seq)
    kern = functools.partial(_flash_kernel, tq=tq, tk=tk)
    return pl.pallas_call(
        kern,
        grid=(b, MLA_HEADS),
        in_specs=[
            pl.BlockSpec((QK_PAD, seq), lambda i, h: (h, i)),
            pl.BlockSpec((None, seq, QK_PAD), lambda i, h: (i, 0, h)),
            pl.BlockSpec((MLA_V, seq), lambda i, h: (h, i)),
        ],
        out_specs=pl.BlockSpec((None, seq, MLA_V), lambda i, h: (i, 0, h)),
        out_shape=jax.ShapeDtypeStruct((b, seq, MLA_W), BF16),
        compiler_params=_cparams(("arbitrary", "arbitrary"), 40),
        name="flash",
    )(qt, k3, vt)


def _conv_kernel(cur_ref, halo_ref, w_ref, b_ref, o_ref, buf_ref, *, k_scale):
    tm = cur_ref.shape[0]
    cols = cur_ref.shape[1]
    first = pl.program_id(1) == 0
    halo = halo_ref[...].astype(F32)
    buf_ref[0:SUBLANES, :] = jnp.where(first, 0.0, halo)
    buf_ref[SUBLANES:SUBLANES + tm, :] = cur_ref[...].astype(F32)
    cw = 512
    for c in range(cols // cw):
        sl = slice(c * cw, (c + 1) * cw)
        acc = jnp.zeros((tm, cw), F32) + b_ref[:, sl]
        for j in range(ML_CONV):
            off = SUBLANES - (ML_CONV - 1) + j
            acc = acc + buf_ref[off:off + tm, sl] * w_ref[j:j + 1, sl]
        y = acc * _sigmoid(acc)
        if c * cw >= cols // 2:
            y = y * k_scale
        o_ref[:, sl] = y.astype(o_ref.dtype)


def _conv_silu(proj3, conv_w, conv_b):
    b, seq, _ = proj3.shape
    cols = 2 * ML_HEADS * ML_DQK
    tm = min(512, seq)
    hb = tm // SUBLANES
    kern = functools.partial(_conv_kernel, k_scale=ML_DQK ** -0.5)
    return pl.pallas_call(
        kern,
        grid=(b, seq // tm),
        in_specs=[
            pl.BlockSpec((None, tm, cols), lambda i, s: (i, s, COL_QK // cols)),
            pl.BlockSpec((None, SUBLANES, cols), lambda i, s: (i, jnp.maximum(s * hb - 1, 0), COL_QK // cols)),
            pl.BlockSpec((ML_CONV, cols), lambda i, s: (0, 0)),
            pl.BlockSpec((1, cols), lambda i, s: (0, 0)),
        ],
        out_specs=pl.BlockSpec((None, tm, cols), lambda i, s: (i, s, 0)),
        out_shape=jax.ShapeDtypeStruct((b, seq, cols), BF16),
        scratch_shapes=[pltpu.VMEM((tm + SUBLANES, cols), F32)],
        compiler_params=_cparams(("arbitrary", "arbitrary"), 40),
        name="conv_silu",
    )(proj3, proj3, conv_w, conv_b.reshape(1, cols))


def _log_sigmoid(x):
    return -(jnp.maximum(-x, 0.0) + jnp.log1p(jnp.exp(-jnp.abs(x))))


def _mlstm_kernel(q_ref, k_ref, v_ref, o_ref, gates_ref, gbias_ref, ng_ref, out_ref, c_ref, m_ref):
    L = q_ref.shape[0]

    @pl.when(pl.program_id(1) == 0)
    def _():
        c_ref[...] = jnp.zeros(c_ref.shape, F32)
        m_ref[...] = jnp.zeros(m_ref.shape, F32)

    g = gates_ref[...] + gbias_ref[...]
    gt = g.T
    lf = _log_sigmoid(g)
    lft = _log_sigmoid(gt)
    r = lax.broadcasted_iota(jnp.int32, (L, L), 0)
    c = lax.broadcasted_iota(jnp.int32, (L, L), 1)
    causal = c <= r
    tril = causal.astype(F32)
    triu = (r <= c).astype(F32)
    hi = lax.Precision.HIGHEST
    bcol_all = jnp.dot(tril, lf, preferred_element_type=F32, precision=hi)
    brow_all = jnp.dot(lft, triu, preferred_element_type=F32, precision=hi)
    lane = lax.broadcasted_iota(jnp.int32, (L, ML_DV), 1)
    ones_col = jnp.where(lane == 0, 1.0, 0.0).astype(BF16)

    for h in range(ML_HEADS):
        hs = slice(h * ML_DQK, (h + 1) * ML_DQK)
        bcol = bcol_all[:, ML_HEADS + h:ML_HEADS + h + 1]
        brow = brow_all[ML_HEADS + h:ML_HEADS + h + 1, :]
        icol = g[:, h:h + 1]
        irow = gt[h:h + 1, :]
        m_prev = m_ref[h][:, :1]
        logw = jnp.where(causal, bcol - brow + irow, -jnp.inf)
        log_inter = bcol + m_prev
        m_t = jnp.maximum(jnp.max(logw, axis=-1, keepdims=True), log_inter)
        w = jnp.exp(logw - m_t)
        a = jnp.exp(log_inter - m_t)
        qh = q_ref[:, hs]
        kh = k_ref[:, hs]
        v_aug = jnp.concatenate([v_ref[:, hs], ones_col], axis=-1)
        s = _dot_nt(qh, kh) * w
        c_aug = c_ref[h]
        nd = _dot(s.astype(BF16), v_aug) + a * _dot(qh, c_aug.astype(BF16))
        num = nd[:, :ML_DV]
        den = nd[:, ML_DV:ML_DV + 1]
        hout = num / jnp.maximum(jnp.abs(den), jnp.exp(-m_t))
        hn = hout * _rms_scale(hout, ML_DV) * ng_ref[:, hs]
        gate = _sigmoid(o_ref[:, hs].astype(F32))
        out_ref[:, hs] = (hn * gate).astype(out_ref.dtype)

        blast = bcol[L - 1:L, :]
        logg = blast - bcol + icol
        m_new = jnp.maximum(blast + m_prev, jnp.max(logg, axis=0, keepdims=True))
        gcol = jnp.exp(logg - m_new)
        decay = jnp.exp(blast + m_prev - m_new)
        kg = (kh.astype(F32) * gcol).astype(BF16)
        c_ref[h] = decay * c_aug + _dot_tn(kg, v_aug)
        m_ref[h] = jnp.broadcast_to(m_new, m_ref.shape[1:])


def _mlstm(qk3, proj3, gates3, gbias, norm_g):
    b, seq, _ = qk3.shape
    L = min(MLSTM_CHUNK, seq)
    w = ML_W
    return pl.pallas_call(
        _mlstm_kernel,
        grid=(b, seq // L),
        in_specs=[
            pl.BlockSpec((None, L, w), lambda i, c: (i, c, 0)),
            pl.BlockSpec((None, L, w), lambda i, c: (i, c, 1)),
            pl.BlockSpec((None, L, w), lambda i, c: (i, c, COL_V // w)),
            pl.BlockSpec((None, L, w), lambda i, c: (i, c, COL_O // w)),
            pl.BlockSpec((None, L, LANES), lambda i, c: (i, c, 0)),
            pl.BlockSpec((1, LANES), lambda i, c: (0, 0)),
            pl.BlockSpec((1, w), lambda i, c: (0, 0)),
        ],
        out_specs=pl.BlockSpec((None, L, w), lambda i, c: (i, c, 0)),
        out_shape=jax.ShapeDtypeStruct((b, seq, w), BF16),
        scratch_shapes=[
            pltpu.VMEM((ML_HEADS, ML_DQK, 2 * ML_DV), F32),
            pltpu.VMEM((ML_HEADS, 1, LANES), F32),
        ],
        compiler_params=_cparams(("arbitrary", "arbitrary"), 40),
        name="mlstm",
    )(qk3, qk3, proj3, proj3, gates3, gbias, norm_g.reshape(1, w))


def _merge_kernel(a_ref, b_ref, ga_ref, gb_ref, wa_ref, wb_ref, o_ref):
    pa = _dot(a_ref[...], wa_ref[...])
    pb = _dot(b_ref[...], wb_ref[...])
    mixed = _sigmoid(ga_ref[...].astype(F32)) * pa + _sigmoid(gb_ref[...].astype(F32)) * pb
    o_ref[...] = mixed.astype(o_ref.dtype)


def _merge(out_a, hm, proj, wa, wb, seq):
    t = out_a.shape[0]
    d = wa.shape[1]
    tm = min(1024, seq)
    tn = 1024
    return pl.pallas_call(
        _merge_kernel,
        grid=(d // tn, t // tm),
        in_specs=[
            pl.BlockSpec((tm, MLA_W), lambda j, i: (i, 0)),
            pl.BlockSpec((tm, ML_W), lambda j, i: (i, 0)),
            pl.BlockSpec((tm, tn), lambda j, i: (i, COL_GA // tn + j)),
            pl.BlockSpec((tm, tn), lambda j, i: (i, COL_GB // tn + j)),
            pl.BlockSpec((MLA_W, tn), lambda j, i: (0, j)),
            pl.BlockSpec((ML_W, tn), lambda j, i: (0, j)),
        ],
        out_specs=pl.BlockSpec((tm, tn), lambda j, i: (i, j)),
        out_shape=jax.ShapeDtypeStruct((t, d), BF16),
        compiler_params=_cparams(("arbitrary", "arbitrary"), 40),
        name="merge",
    )(out_a, hm, proj, proj, wa, wb)


def _outproj_kernel(mix_ref, x_ref, w_ref, gt_ref, sc_ref, sh_ref, g_ref, wr_ref, br_ref, x1_ref, h2_ref, lg_ref):
    y = _dot(mix_ref[...], w_ref[...])
    x1 = x_ref[...] + gt_ref[...] * y
    x1_ref[...] = x1
    h2 = x1 * _rms_scale(x1, x1.shape[-1]) * g_ref[...]
    h2 = h2 * (1.0 + sc_ref[...]) + sh_ref[...]
    h2_ref[...] = h2
    h_hi = h2.astype(BF16)
    h_lo = (h2 - h_hi.astype(F32)).astype(BF16)
    r = _dot(h_hi, wr_ref[...]) + _dot(h_lo, wr_ref[...])
    lg_ref[...] = r[:, :LANES] + r[:, LANES:] + br_ref[...]


def _outproj(mixed, x2, w_out, mod4, norm_g, w_route2, b_route, seq):
    t, d = x2.shape
    tm = min(512, seq)
    bpr = seq // tm
    mod_spec = lambda k: pl.BlockSpec((None, None, 1, d), lambda i: (i // bpr, k, 0, 0))
    const = lambda i: (0, 0)
    return pl.pallas_call(
        _outproj_kernel,
        grid=(t // tm,),
        in_specs=[
            pl.BlockSpec((tm, d), lambda i: (i, 0)),
            pl.BlockSpec((tm, d), lambda i: (i, 0)),
            pl.BlockSpec((d, d), const, pipeline_mode=pl.Buffered(1)),
            mod_spec(2),
            mod_spec(4),
            mod_spec(3),
            pl.BlockSpec((1, d), const),
            pl.BlockSpec((d, 2 * LANES), const, pipeline_mode=pl.Buffered(1)),
            pl.BlockSpec((1, LANES), const),
        ],
        out_specs=[
            pl.BlockSpec((tm, d), lambda i: (i, 0)),
            pl.BlockSpec((tm, d), lambda i: (i, 0)),
            pl.BlockSpec((tm, LANES), lambda i: (i, 0)),
        ],
        out_shape=[
            jax.ShapeDtypeStruct((t, d), F32),
            jax.ShapeDtypeStruct((t, d), F32),
            jax.ShapeDtypeStruct((t, LANES), F32),
        ],
        compiler_params=_cparams(("arbitrary",), 56),
        name="outproj",
    )(mixed, x2, w_out, mod4, mod4, mod4, norm_g.reshape(1, d), w_route2, b_route)


def _route_kernel(lg_ref, meta_ref, wts_ref, cnt_ref, carry_ref):
    tm = lg_ref.shape[0]

    @pl.when(pl.program_id(0) == 0)
    def _():
        carry_ref[...] = jnp.zeros(carry_ref.shape, F32)

    lg = lg_ref[...]
    lane = lax.broadcasted_iota(jnp.int32, lg.shape, 1)
    big = jnp.int32(LANES)
    ninf = -jnp.inf

    def first_argmax(vals):
        mx = jnp.max(vals, axis=-1, keepdims=True)
        idx = jnp.min(jnp.where(vals == mx, lane, big), axis=-1, keepdims=True)
        return mx, idx

    gl = jnp.where(lane < N_GROUPS, lg, ninf)
    gmax, gsel = first_argmax(gl)
    g_w = 1.0 / jnp.sum(jnp.exp(gl - gmax), axis=-1, keepdims=True)
    lo = N_GROUPS + gsel * EXP_PER_GROUP
    in_grp = (lane >= lo) & (lane < lo + EXP_PER_GROUP)
    el = jnp.where(in_grp, lg, ninf)
    e1, i1 = first_argmax(el)
    e2, i2 = first_argmax(jnp.where(lane == i1, ninf, el))
    p2 = jnp.exp(e2 - e1)
    w1 = g_w / (1.0 + p2)
    w2 = g_w * p2 / (1.0 + p2)

    oh1 = lane == i1
    oh2 = lane == i2
    oh = jnp.where(oh1 | oh2, 1.0, 0.0)
    r = lax.broadcasted_iota(jnp.int32, (tm, tm), 0)
    c = lax.broadcasted_iota(jnp.int32, (tm, tm), 1)
    strict = jnp.where(c < r, 1.0, 0.0).astype(BF16)
    before = _dot(strict, oh.astype(BF16)) + carry_ref[...]
    rank1 = jnp.sum(jnp.where(oh1, before, 0.0), axis=-1, keepdims=True).astype(jnp.int32)
    rank2 = jnp.sum(jnp.where(oh2, before, 0.0), axis=-1, keepdims=True).astype(jnp.int32)
    carry = carry_ref[...] + jnp.sum(oh, axis=0, keepdims=True)
    carry_ref[...] = carry
    cnt_ref[...] = jnp.broadcast_to(carry, cnt_ref.shape)

    ex1 = i1 - N_GROUPS
    ex2 = i2 - N_GROUPS
    meta = jnp.where(lane == 0, ex1, jnp.where(lane == 1, ex2, jnp.where(lane == 2, rank1, jnp.where(lane == 3, rank2, 0))))
    meta_ref[...] = meta
    wts_ref[...] = jnp.where(lane == 0, w1, jnp.where(lane == 1, w2, 0.0))


def _route(logits):
    t = logits.shape[0]
    tm = min(512, t)
    return pl.pallas_call(
        _route_kernel,
        grid=(t // tm,),
        in_specs=[pl.BlockSpec((tm, LANES), lambda i: (i, 0))],
        out_specs=[
            pl.BlockSpec((tm, LANES), lambda i: (i, 0)),
            pl.BlockSpec((tm, LANES), lambda i: (i, 0)),
            pl.BlockSpec((SUBLANES, LANES), lambda i: (0, 0)),
        ],
        out_shape=[
            jax.ShapeDtypeStruct((t, LANES), jnp.int32),
            jax.ShapeDtypeStruct((t, LANES), F32),
            jax.ShapeDtypeStruct((SUBLANES, LANES), F32),
        ],
        scratch_shapes=[pltpu.VMEM((1, LANES), F32)],
        compiler_params=_cparams(("arbitrary",), 32),
        name="route",
    )(logits)


def _row_gather_start(idx_ref, base, src_hbm, dst, sem, rows):
    for r in range(rows):
        tok = idx_ref[base + r]
        pltpu.make_async_copy(src_hbm.at[pl.ds(tok, 1)], dst.at[pl.ds(r, 1)], sem).start()


def _row_gather_wait(src_hbm, dst, sem, rows):
    pltpu.make_async_copy(src_hbm.at[pl.ds(0, rows)], dst, sem).wait()


def _moe_kernel(texp_ref, nt_ref, src_ref, h_hbm, wg_ref, wu_ref, wd_ref, y_ref, xbuf, sem):
    j = pl.program_id(0)
    nt = nt_ref[0]
    tile = xbuf.shape[1]
    slot = lax.rem(j, 2)

    @pl.when(j == 0)
    def _():
        _row_gather_start(src_ref, 0, h_hbm, xbuf.at[0], sem.at[0], tile)

        @pl.when(nt > 1)
        def _():
            _row_gather_start(src_ref, tile, h_hbm, xbuf.at[1], sem.at[1], tile)

    @pl.when(j < nt)
    def _():
        _row_gather_wait(h_hbm, xbuf.at[slot], sem.at[slot], tile)
        x = xbuf[slot].astype(BF16)
        hg = _dot(x, wg_ref[...].astype(BF16))
        hu = _dot(x, wu_ref[...].astype(BF16))
        act = (hg * _sigmoid(hg) * hu).astype(BF16)
        y_ref[...] = _dot(act, wd_ref[...].astype(BF16))

    @pl.when(j + 2 < nt)
    def _():
        _row_gather_start(src_ref, (j + 2) * tile, h_hbm, xbuf.at[slot], sem.at[slot], tile)

    @pl.when(j >= nt)
    def _():
        y_ref[...] = jnp.zeros(y_ref.shape, y_ref.dtype)


def _moe(tile_expert, n_tiles, src, h2, w_gate_e, w_up_e, w_down_e):
    t, d = h2.shape
    f = w_gate_e.shape[-1]
    nt_max = tile_expert.shape[0]
    tile = MOE_TILE

    def w_idx(j, texp, nt, src_):
        return (texp[jnp.minimum(j, nt[0] - 1)], 0, 0)

    grid_spec = pltpu.PrefetchScalarGridSpec(
        num_scalar_prefetch=3,
        grid=(nt_max,),
        in_specs=[
            pl.BlockSpec(memory_space=pl.ANY),
            pl.BlockSpec((None, d, f), w_idx),
            pl.BlockSpec((None, d, f), w_idx),
            pl.BlockSpec((None, f, d), w_idx),
        ],
        out_specs=pl.BlockSpec((tile, d), lambda j, texp, nt, src_: (j, 0)),
        scratch_shapes=[
            pltpu.VMEM((2, tile, d), F32),
            pltpu.SemaphoreType.DMA((2,)),
        ],
    )
    return pl.pallas_call(
        _moe_kernel,
        grid_spec=grid_spec,
        out_shape=jax.ShapeDtypeStruct((nt_max * tile, d), F32),
        compiler_params=_cparams(("arbitrary",), 56),
        name="moe_experts",
    )(tile_expert, n_tiles, src, h2, w_gate_e, w_up_e, w_down_e)


def _combine_kernel(pos_ref, y_hbm, x1_ref, gt_ref, wts_ref, o_ref, ybuf, sem):
    i = pl.program_id(0)
    n = pl.num_programs(0)
    tm = x1_ref.shape[0]
    slot = lax.rem(i, 2)

    def start(blk, s):
        for k in range(2):
            _row_gather_start(pos_ref, (2 * blk + k) * tm, y_hbm, ybuf.at[s, k], sem.at[s], tm)

    @pl.when(i == 0)
    def _():
        start(0, 0)

        @pl.when(n > 1)
        def _():
            start(1, 1)

    for k in range(2):
        _row_gather_wait(y_hbm, ybuf.at[slot, k], sem.at[slot], tm)
    w = wts_ref[...]
    moe = w[:, 0:1] * ybuf[slot, 0] + w[:, 1:2] * ybuf[slot, 1]
    o_ref[...] = x1_ref[...] + gt_ref[...] * moe

    @pl.when(i + 2 < n)
    def _():
        start(i + 2, slot)


def _combine(pos, ys, x1, mod4, wts, seq):
    t, d = x1.shape
    tm = min(256, seq)
    bpr = seq // tm
    pos_blocks = pos.reshape(t // tm, tm, 2).transpose(0, 2, 1).reshape(-1)
    grid_spec = pltpu.PrefetchScalarGridSpec(
        num_scalar_prefetch=1,
        grid=(t // tm,),
        in_specs=[
            pl.BlockSpec(memory_space=pl.ANY),
            pl.BlockSpec((tm, d), lambda i, p: (i, 0)),
            pl.BlockSpec((None, None, 1, d), lambda i, p: (i // bpr, 5, 0, 0)),
            pl.BlockSpec((tm, LANES), lambda i, p: (i, 0)),
        ],
        out_specs=pl.BlockSpec((tm, d), lambda i, p: (i, 0)),
        scratch_shapes=[
            pltpu.VMEM((2, 2, tm, d), F32),
            pltpu.SemaphoreType.DMA((2,)),
        ],
    )
    return pl.pallas_call(
        _combine_kernel,
        grid_spec=grid_spec,
        out_shape=jax.ShapeDtypeStruct((t, d), F32),
        compiler_params=_cparams(("arbitrary",), 48),
        name="moe_combine",
    )(pos_blocks, ys, x1, mod4, wts)


def _swap_halves(w):
    half = w.shape[-1] // 2
    return jnp.concatenate([w[..., half:], w[..., :half]], axis=-1)


SRC_CQ = 0
SRC_KPE = Q_LORA + KV_LORA
SRC_QK = SRC_KPE + MLA_ROPE
SRC_V = SRC_QK + 2 * ML_HEADS * ML_DQK
SRC_O = SRC_V + ML_W
SRC_I = SRC_O + ML_W
SRC_GA = SRC_I + 2 * ML_HEADS
SRC_GB = SRC_GA + D_MODEL
IN_COLS = SRC_GB + D_MODEL


def _layout_w_in_kernel(w_ref, o_ref):
    def seg(dst, src, n):
        o_ref[:, dst:dst + n] = w_ref[:, src:src + n].astype(BF16)

    seg(COL_QK, SRC_QK, 2 * ML_HEADS * ML_DQK)
    seg(COL_GA, SRC_GA, D_MODEL)
    seg(COL_GB, SRC_GB, D_MODEL)
    seg(COL_V, SRC_V, ML_W)
    seg(COL_O, SRC_O, ML_W)
    seg(COL_LAT, SRC_CQ, Q_LORA + KV_LORA + MLA_ROPE)
    half = MLA_ROPE // 2
    kpe_swapped = COL_LAT + SRC_KPE + MLA_ROPE
    seg(kpe_swapped, SRC_KPE + half, half)
    seg(kpe_swapped + half, SRC_KPE, half)
    seg(COL_GATES, SRC_I, 2 * ML_HEADS)
    pad = LANES - 2 * ML_HEADS
    o_ref[:, COL_GATES + 2 * ML_HEADS:COL_GATES + LANES] = jnp.zeros((w_ref.shape[0], pad), BF16)


def _layout_w_in(w_in):
    d, n = w_in.shape
    tk = 256
    return pl.pallas_call(
        _layout_w_in_kernel,
        grid=(d // tk,),
        in_specs=[pl.BlockSpec((tk, n), lambda i: (i, 0))],
        out_specs=pl.BlockSpec((tk, IN_COLS_PAD), lambda i: (i, 0)),
        out_shape=jax.ShapeDtypeStruct((d, IN_COLS_PAD), BF16),
        compiler_params=_cparams(("arbitrary",), 48),
        name="layout_w_in",
    )(w_in)


def _layout_w_uq(w_uq):
    r = w_uq.shape[0]
    w = w_uq.reshape(r, MLA_HEADS, MLA_QK)
    rope = w[..., MLA_NOPE:]
    return jnp.concatenate([w[..., :MLA_NOPE], rope, _swap_halves(rope)], axis=-1).reshape(r, MLA_HEADS * QK_PAD).astype(BF16)


def _layout_w_ukv(w_ukv):
    r = w_ukv.shape[0]
    w = w_ukv.reshape(r, MLA_HEADS, MLA_NOPE + MLA_V)
    return jnp.concatenate([w[..., :MLA_NOPE].reshape(r, -1), w[..., MLA_NOPE:].reshape(r, -1)], axis=-1).astype(BF16)


def _rope_gain(g):
    g1 = g[MLA_NOPE:MLA_NOPE + MLA_ROPE // 2]
    g2 = g[MLA_NOPE + MLA_ROPE // 2:]
    return jnp.concatenate([g[:MLA_NOPE], g1, g2, -g2, g1]).reshape(1, QK_PAD)


def _layer(x2, cond_mod4, pos2, seq, w_in, q_a_norm_g, w_uq, kv_a_norm_g, w_ukv, q_norm_g, k_norm_g, conv_w, conv_b,
           b_mlstm_gates, mlstm_norm_g, w_proj_a, w_proj_b, w_out, norm_mix_g, norm_ffn_g, w_group, b_group,
           w_router, b_router, w_gate_e, w_up_e, w_down_e):
    t, d = x2.shape
    b = t // seq
    mod4 = cond_mod4

    proj, gates = _inproj(x2, mod4, norm_mix_g, _layout_w_in(w_in), seq)

    inv = ROPE_THETA ** (-jnp.arange(0, MLA_ROPE, 2, dtype=F32) / MLA_ROPE)
    inv_lanes = jnp.tile(inv, LANES // (MLA_ROPE // 2)).reshape(1, LANES)
    qt, k, vt = _mla_prep(proj, pos2, _layout_w_uq(w_uq), _layout_w_ukv(w_ukv), q_a_norm_g.reshape(1, -1),
                        kv_a_norm_g.reshape(1, -1), _rope_gain(q_norm_g), _rope_gain(k_norm_g), inv_lanes, seq)
    out_a = _flash(qt, k.reshape(b, seq, -1), vt).reshape(t, MLA_W)

    proj3 = proj.reshape(b, seq, -1)
    qk3 = _conv_silu(proj3, conv_w, conv_b)
    gbias = jnp.zeros((1, LANES), F32).at[0, :2 * ML_HEADS].set(b_mlstm_gates.reshape(-1))
    hm = _mlstm(qk3, proj3, gates.reshape(b, seq, LANES), gbias, mlstm_norm_g).reshape(t, ML_W)

    mixed = _merge(out_a, hm, proj, w_proj_a.astype(BF16), w_proj_b.astype(BF16), seq)

    w_route = jnp.zeros((d, LANES), F32).at[:, :N_GROUPS].set(w_group).at[:, N_GROUPS:N_GROUPS + N_EXPERTS].set(w_router)
    b_route = jnp.zeros((1, LANES), F32).at[0, :N_GROUPS].set(b_group).at[0, N_GROUPS:N_GROUPS + N_EXPERTS].set(b_router)
    w_route_hi = w_route.astype(BF16)
    w_route2 = jnp.concatenate([w_route_hi, (w_route - w_route_hi.astype(F32)).astype(BF16)], axis=1)
    x1, h2, logits = _outproj(mixed, x2, w_out.astype(BF16), mod4, norm_ffn_g, w_route2, b_route, seq)

    meta, wts, counts = _route(logits)

    tile = MOE_TILE
    cnt = counts[0, N_GROUPS:N_GROUPS + N_EXPERTS].astype(jnp.int32)
    padded = ((cnt + tile - 1) // tile) * tile
    ends = jnp.cumsum(padded)
    offs = ends - padded
    nt_max = (2 * t) // tile + N_EXPERTS
    pos = offs[meta[:, 0:2]] + meta[:, 2:4]
    tok = jnp.broadcast_to(jnp.arange(t, dtype=jnp.int32)[:, None], (t, 2))
    src = jnp.zeros((nt_max * tile,), jnp.int32).at[pos.reshape(-1)].set(tok.reshape(-1))
    tile_start = jnp.arange(nt_max, dtype=jnp.int32) * tile
    tile_expert = jnp.minimum(jnp.sum((ends[None, :] <= tile_start[:, None]).astype(jnp.int32), axis=1), N_EXPERTS - 1)
    n_tiles = (ends[-1] // tile).astype(jnp.int32).reshape(1)

    ys = _moe(tile_expert, n_tiles, src, h2, w_gate_e, w_up_e, w_down_e)

    return _combine(pos, ys, x1, mod4, wts, seq)


def kernel(x, c, positions, w_ada, b_ada, norm_mix_g, w_in, q_a_norm_g, w_uq, kv_a_norm_g, w_ukv, q_norm_g, k_norm_g, conv_w, conv_b, b_mlstm_gates, mlstm_norm_g, w_proj_a, w_proj_b, w_out, norm_ffn_g, w_group, b_group, w_router, b_router, w_gate_e, w_up_e, w_down_e):
    b, seq, d = x.shape
    depth = w_ada.shape[0]
    x2 = x.reshape(b * seq, d)
    pos2 = positions.reshape(b * seq, 1)
    c_pad = jnp.zeros((SUBLANES, d), F32).at[:b].set(c)
    for l in range(depth):
        mod = _adaln(c_pad, w_ada[l], b_ada[l])
        mod4 = mod[:b].reshape(b, 6, 1, d)
        x2 = _layer(x2, mod4, pos2, seq, w_in[l], q_a_norm_g[l], w_uq[l], kv_a_norm_g[l], w_ukv[l], q_norm_g[l],
                    k_norm_g[l], conv_w[l], conv_b[l], b_mlstm_gates[l], mlstm_norm_g[l], w_proj_a[l], w_proj_b[l],
                    w_out[l], norm_mix_g[l], norm_ffn_g[l], w_group[l], b_group[l], w_router[l], b_router[l],
                    w_gate_e[l], w_up_e[l], w_down_e[l])
    return x2.reshape(b, seq, d)
```

```python
import functools
import math

import jax
import jax.numpy as jnp
from jax import lax
from jax.experimental import pallas as pl
from jax.experimental.pallas import tpu as pltpu

F32 = jnp.float32
BF16 = jnp.bfloat16

LANES = 128
SUBLANES = 8

D_MODEL = 2048
MLA_HEADS = 8
MLA_NOPE = 128
MLA_ROPE = 64
MLA_QK = MLA_NOPE + MLA_ROPE
MLA_V = 128
Q_LORA = 512
KV_LORA = 256
ROPE_THETA = 10000.0
ML_HEADS = 8
ML_DQK = 128
ML_DV = 128
ML_CONV = 4
MLA_W = MLA_HEADS * MLA_V
ML_W = ML_HEADS * ML_DV
N_GROUPS = 4
EXP_PER_GROUP = 8
N_EXPERTS = N_GROUPS * EXP_PER_GROUP
D_FF_EXPERT = 512
EPS = 1e-6

QK_PAD = 2 * LANES

SRC_CQ = 0
SRC_KPE = Q_LORA + KV_LORA
SRC_QK = SRC_KPE + MLA_ROPE
SRC_V = SRC_QK + 2 * ML_HEADS * ML_DQK
SRC_O = SRC_V + ML_W
SRC_I = SRC_O + ML_W
SRC_GA = SRC_I + 2 * ML_HEADS
SRC_GB = SRC_GA + D_MODEL
GATE_LANE = SRC_I % LANES

IN_BLOCK = 1024
IN_BLOCK_SRC = (SRC_QK, SRC_QK + IN_BLOCK, SRC_GA, SRC_GA + IN_BLOCK, SRC_GB, SRC_GB + IN_BLOCK, SRC_V, SRC_O, SRC_CQ)
COL_QK = 0
COL_GA = 2048
COL_GB = 4096
COL_V = 6144
COL_O = 7168
COL_LAT = 8192
LAT_W = Q_LORA + KV_LORA + MLA_ROPE

MLSTM_CHUNK = 256
MOE_TILE = 256


def _cparams(sem, vmem_mb):
    return pltpu.CompilerParams(dimension_semantics=sem, vmem_limit_bytes=vmem_mb * 1024 * 1024)


def _dot(a, b):
    return jnp.dot(a, b, preferred_element_type=F32)


def _dot_nt(a, b):
    return lax.dot_general(a, b, (((1,), (1,)), ((), ())), preferred_element_type=F32)


def _dot_tn(a, b):
    return lax.dot_general(a, b, (((0,), (0,)), ((), ())), preferred_element_type=F32)


def _sigmoid(x):
    return 1.0 / (1.0 + jnp.exp(-x))


def _rms_scale(x, width):
    return lax.rsqrt(jnp.sum(x * x, axis=-1, keepdims=True) * (1.0 / width) + EPS)


def _adaln_kernel(c_ref, w_ref, b_ref, o_ref):
    c = c_ref[...]
    cond = (c * _sigmoid(c)).astype(BF16)
    o_ref[...] = _dot(cond, w_ref[...].astype(BF16)) + b_ref[...]


def _adaln(c_pad, w_ada, b_ada):
    rows, d = c_pad.shape
    n = w_ada.shape[1]
    tn = 1024
    return pl.pallas_call(
        _adaln_kernel,
        grid=(n // tn,),
        in_specs=[
            pl.BlockSpec((rows, d), lambda j: (0, 0)),
            pl.BlockSpec((d, tn), lambda j: (0, j)),
            pl.BlockSpec((1, tn), lambda j: (0, j)),
        ],
        out_specs=pl.BlockSpec((rows, tn), lambda j: (0, j)),
        out_shape=jax.ShapeDtypeStruct((rows, n), F32),
        compiler_params=_cparams(("arbitrary",), 40),
        name="adaln",
    )(c_pad, w_ada, b_ada.reshape(1, n))


def _inproj_kernel(off_ref, x_ref, sc_ref, sh_ref, g_ref, w_ref, wgate_ref, proj_ref, gates_ref, h_ref):
    j = pl.program_id(1)

    @pl.when(j == 0)
    def _():
        x = x_ref[...]
        h = x * _rms_scale(x, x.shape[-1]) * g_ref[...]
        h = (h * (1.0 + sc_ref[...]) + sh_ref[...]).astype(BF16)
        h_ref[...] = h
        gates_ref[...] = _dot_nt(h, wgate_ref[...].astype(BF16))

    proj_ref[...] = _dot_nt(h_ref[...], w_ref[...].astype(BF16)).astype(proj_ref.dtype)


def _inproj(x2, mod4, norm_g, w_in_t, seq):
    t, d = x2.shape
    tm = min(1024, seq)
    tn = IN_BLOCK
    bpr = seq // tm
    assert all(o % SUBLANES == 0 for o in IN_BLOCK_SRC)
    offs = jnp.asarray([o // SUBLANES for o in IN_BLOCK_SRC], jnp.int32)
    nblk = len(IN_BLOCK_SRC)
    gate_tile = SRC_I // LANES
    grid_spec = pltpu.PrefetchScalarGridSpec(
        num_scalar_prefetch=1,
        grid=(t // tm, nblk),
        in_specs=[
            pl.BlockSpec((tm, d), lambda i, j, o: (i, 0)),
            pl.BlockSpec((None, None, 1, d), lambda i, j, o: (i // bpr, 1, 0, 0)),
            pl.BlockSpec((None, None, 1, d), lambda i, j, o: (i // bpr, 0, 0, 0)),
            pl.BlockSpec((1, d), lambda i, j, o: (0, 0)),
            pl.BlockSpec((pl.Element(tn), pl.Element(d)), lambda i, j, o: (o[j] * SUBLANES, 0)),
            pl.BlockSpec((LANES, d), lambda i, j, o: (gate_tile, 0)),
        ],
        out_specs=[
            pl.BlockSpec((tm, tn), lambda i, j, o: (i, j)),
            pl.BlockSpec((tm, LANES), lambda i, j, o: (i, 0)),
        ],
        scratch_shapes=[pltpu.VMEM((tm, d), BF16)],
    )
    return pl.pallas_call(
        _inproj_kernel,
        grid_spec=grid_spec,
        out_shape=[
            jax.ShapeDtypeStruct((t, nblk * tn), BF16),
            jax.ShapeDtypeStruct((t, LANES), F32),
        ],
        compiler_params=_cparams(("arbitrary", "arbitrary"), 56),
        name="inproj",
    )(offs, x2, mod4, mod4, norm_g.reshape(1, d), w_in_t, w_in_t)


def _mla_prep_kernel(lat_ref, pos_ref, wuq_ref, wukv_ref, gqa_ref, gkva_ref, gq_ref, gk_ref, inv_ref,
                     qt_ref, k_ref, vt_ref):
    lat = lat_ref[...].astype(F32)
    cq = lat[:, :Q_LORA]
    ckv = lat[:, Q_LORA:Q_LORA + KV_LORA]
    kc = lat[:, SRC_KPE:SRC_KPE + LANES]
    cqn = (cq * _rms_scale(cq, Q_LORA) * gqa_ref[...]).astype(BF16)
    ckvn = (ckv * _rms_scale(ckv, KV_LORA) * gkva_ref[...]).astype(BF16)
    qraw = _dot(cqn, wuq_ref[...])
    kv = _dot(ckvn, wukv_ref[...])

    ang = pos_ref[...].astype(F32) * inv_ref[...]
    lane = lax.broadcasted_iota(jnp.int32, ang.shape, 1)
    lo = lane < MLA_ROPE
    cs = jnp.where(lo, jnp.cos(ang), jnp.sin(ang))

    quarter = MLA_ROPE // 2
    want = jnp.where(lane < 3 * quarter, lane - quarter, lane - 3 * quarter)
    came = pltpu.roll(lane, quarter, 1)
    swapped = jnp.where(came == want, pltpu.roll(kc, quarter, 1), pltpu.roll(kc, 3 * quarter, 1))
    kc = jnp.where(lo, kc, swapped)

    gq = gq_ref[...]
    gk = gk_ref[...]
    gq_n, gq_r = gq[:, :LANES], gq[:, LANES:]
    gk_n, gk_r = gk[:, :LANES], gk[:, LANES:]

    def rope(chunk, g_r):
        a = chunk * (g_r * cs)
        return jnp.where(lo, a + pltpu.roll(a, MLA_ROPE, 1), 0.0)

    kpe_ss = jnp.sum(jnp.where(lo, kc * kc, 0.0), axis=-1, keepdims=True)
    k_rope = rope(kc, gk_r)
    scale = MLA_QK ** -0.5
    for h in range(MLA_HEADS):
        kn = kv[:, h * MLA_NOPE:(h + 1) * MLA_NOPE]
        sk = lax.rsqrt((jnp.sum(kn * kn, axis=-1, keepdims=True) + kpe_ss) * (1.0 / MLA_QK) + EPS)
        k_ref[:, h * QK_PAD:h * QK_PAD + LANES] = (kn * sk * gk_n).astype(BF16)
        k_ref[:, h * QK_PAD + LANES:(h + 1) * QK_PAD] = (k_rope * sk).astype(BF16)
        qn = qraw[:, h * QK_PAD:h * QK_PAD + LANES]
        qr = qraw[:, h * QK_PAD + LANES:(h + 1) * QK_PAD]
        ss = jnp.sum(qn * qn, axis=-1, keepdims=True) + jnp.sum(jnp.where(lo, qr * qr, 0.0), axis=-1, keepdims=True)
        sq = lax.rsqrt(ss * (1.0 / MLA_QK) + EPS) * scale
        qt_ref[h * QK_PAD:h * QK_PAD + LANES, :] = (qn * sq * gq_n).T.astype(BF16)
        qt_ref[h * QK_PAD + LANES:(h + 1) * QK_PAD, :] = (rope(qr, gq_r) * sq).T.astype(BF16)
        vh = kv[:, MLA_HEADS * MLA_NOPE + h * MLA_V:MLA_HEADS * MLA_NOPE + (h + 1) * MLA_V]
        vt_ref[h * MLA_V:(h + 1) * MLA_V, :] = vh.T.astype(BF16)


def _mla_prep(proj, pos2, wuq_p, wukv_p, gqa, gkva, gq, gk, inv_lanes, seq):
    t = proj.shape[0]
    tm = min(512, seq)
    hq = MLA_HEADS * QK_PAD
    lat_blk = COL_LAT // 1024
    const = lambda i: (0, 0)
    return pl.pallas_call(
        _mla_prep_kernel,
        grid=(t // tm,),
        in_specs=[
            pl.BlockSpec((tm, 1024), lambda i: (i, lat_blk)),
            pl.BlockSpec((tm, 1), lambda i: (i, 0)),
            pl.BlockSpec(wuq_p.shape, const),
            pl.BlockSpec(wukv_p.shape, const),
            pl.BlockSpec(gqa.shape, const),
            pl.BlockSpec(gkva.shape, const),
            pl.BlockSpec(gq.shape, const),
            pl.BlockSpec(gk.shape, const),
            pl.BlockSpec(inv_lanes.shape, const),
        ],
        out_specs=[
            pl.BlockSpec((hq, tm), lambda i: (0, i)),
            pl.BlockSpec((tm, hq), lambda i: (i, 0)),
            pl.BlockSpec((MLA_W, tm), lambda i: (0, i)),
        ],
        out_shape=[
            jax.ShapeDtypeStruct((hq, t), BF16),
            jax.ShapeDtypeStruct((t, hq), BF16),
            jax.ShapeDtypeStruct((MLA_W, t), BF16),
        ],
        compiler_params=_cparams(("arbitrary",), 48),
        name="mla_prep",
    )(proj, pos2, wuq_p, wukv_p, gqa, gkva, gq, gk, inv_lanes)


def _flash_kernel(qt_ref, k_ref, vt_ref, o_ref, *, tq, tk):
    seq = k_ref.shape[0]
    for qi in range(seq // tq):
        qt = qt_ref[:, qi * tq:(qi + 1) * tq]
        m = jnp.full((1, tq), -jnp.inf, F32)
        l = jnp.zeros((1, tq), F32)
        acc = jnp.zeros((MLA_V, tq), F32)
        n_blocks = ((qi + 1) * tq) // tk
        for j in range(n_blocks):
            st = _dot(k_ref[j * tk:(j + 1) * tk, :], qt)
            if (j + 1) * tk - 1 > qi * tq:
                key = lax.broadcasted_iota(jnp.int32, st.shape, 0) + j * tk
                qry = lax.broadcasted_iota(jnp.int32, st.shape, 1) + qi * tq
                st = jnp.where(key <= qry, st, -jnp.inf)
            m_new = jnp.maximum(m, jnp.max(st, axis=0, keepdims=True))
            alpha = jnp.exp(m - m_new)
            p = jnp.exp(st - m_new)
            l = alpha * l + jnp.sum(p, axis=0, keepdims=True)
            acc = alpha * acc + _dot(vt_ref[:, j * tk:(j + 1) * tk], p.astype(BF16))
            m = m_new
        o_ref[qi * tq:(qi + 1) * tq, :] = (acc / l).T.astype(o_ref.dtype)


def _flash(qt, k3, vt):
    b, seq, _ = k3.shape
    tq = min(512, seq)
    tk = min(512, seq)
    kern = functools.partial(_flash_kernel, tq=tq, tk=tk)
    return pl.pallas_call(
        kern,
        grid=(b, MLA_HEADS),
        in_specs=[
            pl.BlockSpec((QK_PAD, seq), lambda i, h: (h, i)),
            pl.BlockSpec((None, seq, QK_PAD), lambda i, h: (i, 0, h)),
            pl.BlockSpec((MLA_V, seq), lambda i, h: (h, i)),
        ],
        out_specs=pl.BlockSpec((None, seq, MLA_V), lambda i, h: (i, 0, h)),
        out_shape=jax.ShapeDtypeStruct((b, seq, MLA_W), BF16),
        compiler_params=_cparams(("arbitrary", "arbitrary"), 40),
        name="flash",
    )(qt, k3, vt)


def _conv_kernel(cur_ref, halo_ref, w_ref, b_ref, o_ref, buf_ref, *, k_scale):
    tm = cur_ref.shape[0]
    cols = cur_ref.shape[1]
    first = pl.program_id(1) == 0
    halo = halo_ref[...].astype(F32)
    buf_ref[0:SUBLANES, :] = jnp.where(first, 0.0, halo)
    buf_ref[SUBLANES:SUBLANES + tm, :] = cur_ref[...].astype(F32)
    cw = 512
    for c in range(cols // cw):
        sl = slice(c * cw, (c + 1) * cw)
        acc = jnp.zeros((tm, cw), F32) + b_ref[:, sl]
        for j in range(ML_CONV):
            off = SUBLANES - (ML_CONV - 1) + j
            acc = acc + buf_ref[off:off + tm, sl] * w_ref[j:j + 1, sl]
        y = acc * _sigmoid(acc)
        if c * cw >= cols // 2:
            y = y * k_scale
        o_ref[:, sl] = y.astype(o_ref.dtype)


def _conv_silu(proj3, conv_w, conv_b):
    b, seq, _ = proj3.shape
    cols = 2 * ML_HEADS * ML_DQK
    tm = min(512, seq)
    hb = tm // SUBLANES
    kern = functools.partial(_conv_kernel, k_scale=ML_DQK ** -0.5)
    return pl.pallas_call(
        kern,
        grid=(b, seq // tm),
        in_specs=[
            pl.BlockSpec((None, tm, cols), lambda i, s: (i, s, COL_QK // cols)),
            pl.BlockSpec((None, SUBLANES, cols), lambda i, s: (i, jnp.maximum(s * hb - 1, 0), COL_QK // cols)),
            pl.BlockSpec((ML_CONV, cols), lambda i, s: (0, 0)),
            pl.BlockSpec((1, cols), lambda i, s: (0, 0)),
        ],
        out_specs=pl.BlockSpec((None, tm, cols), lambda i, s: (i, s, 0)),
        out_shape=jax.ShapeDtypeStruct((b, seq, cols), BF16),
        scratch_shapes=[pltpu.VMEM((tm + SUBLANES, cols), F32)],
        compiler_params=_cparams(("arbitrary", "arbitrary"), 40),
        name="conv_silu",
    )(proj3, proj3, conv_w, conv_b.reshape(1, cols))


def _log_sigmoid(x):
    return -(jnp.maximum(-x, 0.0) + jnp.log1p(jnp.exp(-jnp.abs(x))))


def _mlstm_kernel(q_ref, k_ref, v_ref, o_ref, gates_ref, gbias_ref, ng_ref, out_ref, c_ref, m_ref):
    L = q_ref.shape[0]

    @pl.when(pl.program_id(1) == 0)
    def _():
        c_ref[...] = jnp.zeros(c_ref.shape, F32)
        m_ref[...] = jnp.zeros(m_ref.shape, F32)

    g = gates_ref[...] + gbias_ref[...]
    gt = g.T
    lf = _log_sigmoid(g)
    lft = _log_sigmoid(gt)
    r = lax.broadcasted_iota(jnp.int32, (L, L), 0)
    c = lax.broadcasted_iota(jnp.int32, (L, L), 1)
    causal = c <= r
    tril = causal.astype(F32)
    triu = (r <= c).astype(F32)
    hi = lax.Precision.HIGHEST
    bcol_all = jnp.dot(tril, lf, preferred_element_type=F32, precision=hi)
    brow_all = jnp.dot(lft, triu, preferred_element_type=F32, precision=hi)
    lane = lax.broadcasted_iota(jnp.int32, (L, ML_DV), 1)
    ones_col = jnp.where(lane == 0, 1.0, 0.0).astype(BF16)

    for h in range(ML_HEADS):
        hs = slice(h * ML_DQK, (h + 1) * ML_DQK)
        li, lf_ = GATE_LANE + h, GATE_LANE + ML_HEADS + h
        bcol = bcol_all[:, lf_:lf_ + 1]
        brow = brow_all[lf_:lf_ + 1, :]
        icol = g[:, li:li + 1]
        irow = gt[li:li + 1, :]
        m_prev = m_ref[h][:, :1]
        logw = jnp.where(causal, bcol - brow + irow, -jnp.inf)
        log_inter = bcol + m_prev
        m_t = jnp.maximum(jnp.max(logw, axis=-1, keepdims=True), log_inter)
        w = jnp.exp(logw - m_t)
        a = jnp.exp(log_inter - m_t)
        qh = q_ref[:, hs]
        kh = k_ref[:, hs]
        v_aug = jnp.concatenate([v_ref[:, hs], ones_col], axis=-1)
        s = _dot_nt(qh, kh) * w
        c_aug = c_ref[h]
        nd = _dot(s.astype(BF16), v_aug) + a * _dot(qh, c_aug.astype(BF16))
        num = nd[:, :ML_DV]
        den = nd[:, ML_DV:ML_DV + 1]
        hout = num / jnp.maximum(jnp.abs(den), jnp.exp(-m_t))
        hn = hout * _rms_scale(hout, ML_DV) * ng_ref[:, hs]
        gate = _sigmoid(o_ref[:, hs].astype(F32))
        out_ref[:, hs] = (hn * gate).astype(out_ref.dtype)

        blast = bcol[L - 1:L, :]
        logg = blast - bcol + icol
        m_new = jnp.maximum(blast + m_prev, jnp.max(logg, axis=0, keepdims=True))
        gcol = jnp.exp(logg - m_new)
        decay = jnp.exp(blast + m_prev - m_new)
        kg = (kh.astype(F32) * gcol).astype(BF16)
        c_ref[h] = decay * c_aug + _dot_tn(kg, v_aug)
        m_ref[h] = jnp.broadcast_to(m_new, m_ref.shape[1:])


def _mlstm(qk3, proj3, gates3, gbias, norm_g):
    b, seq, _ = qk3.shape
    L = min(MLSTM_CHUNK, seq)
    w = ML_W
    return pl.pallas_call(
        _mlstm_kernel,
        grid=(b, seq // L),
        in_specs=[
            pl.BlockSpec((None, L, w), lambda i, c: (i, c, 0)),
            pl.BlockSpec((None, L, w), lambda i, c: (i, c, 1)),
            pl.BlockSpec((None, L, w), lambda i, c: (i, c, COL_V // w)),
            pl.BlockSpec((None, L, w), lambda i, c: (i, c, COL_O // w)),
            pl.BlockSpec((None, L, LANES), lambda i, c: (i, c, 0)),
            pl.BlockSpec((1, LANES), lambda i, c: (0, 0)),
            pl.BlockSpec((1, w), lambda i, c: (0, 0)),
        ],
        out_specs=pl.BlockSpec((None, L, w), lambda i, c: (i, c, 0)),
        out_shape=jax.ShapeDtypeStruct((b, seq, w), BF16),
        scratch_shapes=[
            pltpu.VMEM((ML_HEADS, ML_DQK, 2 * ML_DV), F32),
            pltpu.VMEM((ML_HEADS, 1, LANES), F32),
        ],
        compiler_params=_cparams(("arbitrary", "arbitrary"), 40),
        name="mlstm",
    )(qk3, qk3, proj3, proj3, gates3, gbias, norm_g.reshape(1, w))


def _merge_kernel(a_ref, b_ref, ga_ref, gb_ref, wa_ref, wb_ref, o_ref):
    pa = _dot(a_ref[...], wa_ref[...])
    pb = _dot(b_ref[...], wb_ref[...])
    mixed = _sigmoid(ga_ref[...].astype(F32)) * pa + _sigmoid(gb_ref[...].astype(F32)) * pb
    o_ref[...] = mixed.astype(o_ref.dtype)


def _merge(out_a, hm, proj, wa, wb, seq):
    t = out_a.shape[0]
    d = wa.shape[1]
    tm = min(1024, seq)
    tn = 1024
    return pl.pallas_call(
        _merge_kernel,
        grid=(d // tn, t // tm),
        in_specs=[
            pl.BlockSpec((tm, MLA_W), lambda j, i: (i, 0)),
            pl.BlockSpec((tm, ML_W), lambda j, i: (i, 0)),
            pl.BlockSpec((tm, tn), lambda j, i: (i, COL_GA // tn + j)),
            pl.BlockSpec((tm, tn), lambda j, i: (i, COL_GB // tn + j)),
            pl.BlockSpec((MLA_W, tn), lambda j, i: (0, j)),
            pl.BlockSpec((ML_W, tn), lambda j, i: (0, j)),
        ],
        out_specs=pl.BlockSpec((tm, tn), lambda j, i: (i, j)),
        out_shape=jax.ShapeDtypeStruct((t, d), BF16),
        compiler_params=_cparams(("arbitrary", "arbitrary"), 40),
        name="merge",
    )(out_a, hm, proj, proj, wa, wb)


def _outproj_kernel(mix_ref, x_ref, w_ref, gt_ref, sc_ref, sh_ref, g_ref, wr_ref, br_ref, x1_ref, h2_ref, lg_ref):
    y = _dot(mix_ref[...], w_ref[...])
    x1 = x_ref[...] + gt_ref[...] * y
    x1_ref[...] = x1
    h2 = x1 * _rms_scale(x1, x1.shape[-1]) * g_ref[...]
    h2 = h2 * (1.0 + sc_ref[...]) + sh_ref[...]
    h2_ref[...] = h2
    h_hi = h2.astype(BF16)
    h_lo = (h2 - h_hi.astype(F32)).astype(BF16)
    r = _dot(h_hi, wr_ref[...]) + _dot(h_lo, wr_ref[...])
    lg_ref[...] = r[:, :LANES] + r[:, LANES:] + br_ref[...]


def _outproj(mixed, x2, w_out, mod4, norm_g, w_route2, b_route, seq):
    t, d = x2.shape
    tm = min(512, seq)
    bpr = seq // tm
    mod_spec = lambda k: pl.BlockSpec((None, None, 1, d), lambda i: (i // bpr, k, 0, 0))
    const = lambda i: (0, 0)
    return pl.pallas_call(
        _outproj_kernel,
        grid=(t // tm,),
        in_specs=[
            pl.BlockSpec((tm, d), lambda i: (i, 0)),
            pl.BlockSpec((tm, d), lambda i: (i, 0)),
            pl.BlockSpec((d, d), const, pipeline_mode=pl.Buffered(1)),
            mod_spec(2),
            mod_spec(4),
            mod_spec(3),
            pl.BlockSpec((1, d), const),
            pl.BlockSpec((d, 2 * LANES), const, pipeline_mode=pl.Buffered(1)),
            pl.BlockSpec((1, LANES), const),
        ],
        out_specs=[
            pl.BlockSpec((tm, d), lambda i: (i, 0)),
            pl.BlockSpec((tm, d), lambda i: (i, 0)),
            pl.BlockSpec((tm, LANES), lambda i: (i, 0)),
        ],
        out_shape=[
            jax.ShapeDtypeStruct((t, d), F32),
            jax.ShapeDtypeStruct((t, d), F32),
            jax.ShapeDtypeStruct((t, LANES), F32),
        ],
        compiler_params=_cparams(("arbitrary",), 56),
        name="outproj",
    )(mixed, x2, w_out, mod4, mod4, mod4, norm_g.reshape(1, d), w_route2, b_route)


def _route_kernel(lg_ref, pos_ref, wts_ref, cnt_ref, carry_ref, offs_ref, meta_s, wts_s):
    phase = pl.program_id(0)
    i = pl.program_id(1)
    tm = lg_ref.shape[0]
    lane = lax.broadcasted_iota(jnp.int32, (tm, LANES), 1)

    @pl.when((phase == 0) & (i == 0))
    def _():
        carry_ref[...] = jnp.zeros(carry_ref.shape, F32)

    @pl.when(phase == 0)
    def _():
        lg = lg_ref[...]
        big = jnp.int32(LANES)
        ninf = -jnp.inf

        def first_argmax(vals):
            mx = jnp.max(vals, axis=-1, keepdims=True)
            idx = jnp.min(jnp.where(vals == mx, lane, big), axis=-1, keepdims=True)
            return mx, idx

        gl = jnp.where(lane < N_GROUPS, lg, ninf)
        gmax, gsel = first_argmax(gl)
        g_w = 1.0 / jnp.sum(jnp.exp(gl - gmax), axis=-1, keepdims=True)
        lo = N_GROUPS + gsel * EXP_PER_GROUP
        in_grp = (lane >= lo) & (lane < lo + EXP_PER_GROUP)
        el = jnp.where(in_grp, lg, ninf)
        e1, i1 = first_argmax(el)
        e2, i2 = first_argmax(jnp.where(lane == i1, ninf, el))
        p2 = jnp.exp(e2 - e1)
        w1 = g_w / (1.0 + p2)
        w2 = g_w * p2 / (1.0 + p2)

        oh1 = lane == i1
        oh2 = lane == i2
        oh = jnp.where(oh1 | oh2, 1.0, 0.0)
        r = lax.broadcasted_iota(jnp.int32, (tm, tm), 0)
        c = lax.broadcasted_iota(jnp.int32, (tm, tm), 1)
        strict = jnp.where(c < r, 1.0, 0.0).astype(BF16)
        before = _dot(strict, oh.astype(BF16)) + carry_ref[...]
        rank1 = jnp.sum(jnp.where(oh1, before, 0.0), axis=-1, keepdims=True).astype(jnp.int32)
        rank2 = jnp.sum(jnp.where(oh2, before, 0.0), axis=-1, keepdims=True).astype(jnp.int32)
        carry_ref[...] = carry_ref[...] + jnp.sum(oh, axis=0, keepdims=True)
        meta_s[i] = jnp.where(lane == 0, i1, jnp.where(lane == 1, i2, jnp.where(lane == 2, rank1, jnp.where(lane == 3, rank2, 0))))
        wts_s[i] = jnp.where(lane == 0, w1, jnp.where(lane == 1, w2, 0.0))

    @pl.when((phase == 1) & (i == 0))
    def _():
        cnt = carry_ref[...]
        cnt_ref[...] = jnp.broadcast_to(cnt, cnt_ref.shape)
        padded = jnp.ceil(cnt * (1.0 / MOE_TILE)) * MOE_TILE
        r = lax.broadcasted_iota(jnp.int32, (LANES, LANES), 0)
        c = lax.broadcasted_iota(jnp.int32, (LANES, LANES), 1)
        upper = jnp.where(r < c, 1.0, 0.0).astype(BF16)
        padded8 = jnp.broadcast_to(padded, (SUBLANES, LANES)).astype(BF16)
        offs_ref[...] = _dot(padded8, upper)[:1, :]

    @pl.when(phase == 1)
    def _():
        meta = meta_s[i]
        offs = offs_ref[...]
        off1 = jnp.sum(jnp.where(lane == meta[:, 0:1], offs, 0.0), axis=-1, keepdims=True).astype(jnp.int32)
        off2 = jnp.sum(jnp.where(lane == meta[:, 1:2], offs, 0.0), axis=-1, keepdims=True).astype(jnp.int32)
        pos1 = off1 + meta[:, 2:3]
        pos2 = off2 + meta[:, 3:4]
        pos_ref[...] = jnp.where(lane == 0, pos1, jnp.where(lane == 1, pos2, 0))
        wts_ref[...] = wts_s[i]


def _route(logits):
    t = logits.shape[0]
    tm = min(512, t)
    nb = t // tm
    return pl.pallas_call(
        _route_kernel,
        grid=(2, nb),
        in_specs=[pl.BlockSpec((tm, LANES), lambda p, i: (i * (1 - p), 0))],
        out_specs=[
            pl.BlockSpec((tm, LANES), lambda p, i: (i * p, 0)),
            pl.BlockSpec((tm, LANES), lambda p, i: (i * p, 0)),
            pl.BlockSpec((SUBLANES, LANES), lambda p, i: (0, 0)),
        ],
        out_shape=[
            jax.ShapeDtypeStruct((t, LANES), jnp.int32),
            jax.ShapeDtypeStruct((t, LANES), F32),
            jax.ShapeDtypeStruct((SUBLANES, LANES), F32),
        ],
        scratch_shapes=[
            pltpu.VMEM((1, LANES), F32),
            pltpu.VMEM((1, LANES), F32),
            pltpu.VMEM((nb, tm, LANES), jnp.int32),
            pltpu.VMEM((nb, tm, LANES), F32),
        ],
        compiler_params=_cparams(("arbitrary", "arbitrary"), 32),
        name="route",
    )(logits)


def _row_gather_start(idx_ref, base, src_hbm, dst, sem, rows):
    for r in range(rows):
        tok = idx_ref[base + r]
        pltpu.make_async_copy(src_hbm.at[pl.ds(tok, 1)], dst.at[pl.ds(r, 1)], sem).start()


def _row_gather_wait(src_hbm, dst, sem, rows):
    pltpu.make_async_copy(src_hbm.at[pl.ds(0, rows)], dst, sem).wait()


def _moe_kernel(plan_ref, src_ref, h_hbm, wg_hbm, wu_hbm, wd_hbm, y_ref,
                xbuf, xsem, stg_g, stg_u, stg_d, wsem, wb_g, wb_u, wb_d):
    j = pl.program_id(0)
    nt = plan_ref[3, 0]
    tile = xbuf.shape[1]
    slot = lax.rem(j, 2)

    def w_copies(e, s):
        return (pltpu.make_async_copy(wg_hbm.at[e], stg_g.at[s], wsem.at[s]),
                pltpu.make_async_copy(wu_hbm.at[e], stg_u.at[s], wsem.at[s]),
                pltpu.make_async_copy(wd_hbm.at[e], stg_d.at[s], wsem.at[s]))

    @pl.when(j == 0)
    def _():
        for cp in w_copies(plan_ref[3, 1], 0):
            cp.start()
        _row_gather_start(src_ref, 0, h_hbm, xbuf.at[0], xsem.at[0], tile)

        @pl.when(plan_ref[3, 2] >= 0)
        def _():
            for cp in w_copies(plan_ref[3, 2], 1):
                cp.start()

        @pl.when(nt > 1)
        def _():
            _row_gather_start(src_ref, tile, h_hbm, xbuf.at[1], xsem.at[1], tile)

    @pl.when((j < nt) & (plan_ref[0, j] == 1))
    def _():
        s = plan_ref[1, j]
        for cp in w_copies(0, s):
            cp.wait()
        wb_g[...] = stg_g[s].astype(BF16)
        wb_u[...] = stg_u[s].astype(BF16)
        wb_d[...] = stg_d[s].astype(BF16)

        @pl.when(plan_ref[2, j] >= 0)
        def _():
            for cp in w_copies(plan_ref[2, j], s):
                cp.start()

    @pl.when(j < nt)
    def _():
        _row_gather_wait(h_hbm, xbuf.at[slot], xsem.at[slot], tile)
        x = xbuf[slot].astype(BF16)
        hg = _dot(x, wb_g[...])
        hu = _dot(x, wb_u[...])
        act = (hg * _sigmoid(hg) * hu).astype(BF16)
        y_ref[...] = _dot(act, wb_d[...])

    @pl.when(j + 2 < nt)
    def _():
        _row_gather_start(src_ref, (j + 2) * tile, h_hbm, xbuf.at[slot], xsem.at[slot], tile)

    @pl.when(j >= nt)
    def _():
        y_ref[...] = jnp.zeros(y_ref.shape, y_ref.dtype)


def _moe(plan, src, h2, w_gate_e, w_up_e, w_down_e):
    t, d = h2.shape
    f = w_gate_e.shape[-1]
    nt_max = plan.shape[1]
    tile = MOE_TILE
    any_spec = pl.BlockSpec(memory_space=pl.ANY)
    grid_spec = pltpu.PrefetchScalarGridSpec(
        num_scalar_prefetch=2,
        grid=(nt_max,),
        in_specs=[any_spec, any_spec, any_spec, any_spec],
        out_specs=pl.BlockSpec((tile, d), lambda j, plan_, src_: (j, 0)),
        scratch_shapes=[
            pltpu.VMEM((2, tile, d), F32),
            pltpu.SemaphoreType.DMA((2,)),
            pltpu.VMEM((2, d, f), F32),
            pltpu.VMEM((2, d, f), F32),
            pltpu.VMEM((2, f, d), F32),
            pltpu.SemaphoreType.DMA((2,)),
            pltpu.VMEM((d, f), BF16),
            pltpu.VMEM((d, f), BF16),
            pltpu.VMEM((f, d), BF16),
        ],
    )
    return pl.pallas_call(
        _moe_kernel,
        grid_spec=grid_spec,
        out_shape=jax.ShapeDtypeStruct((nt_max * tile, d), F32),
        compiler_params=_cparams(("arbitrary",), 56),
        name="moe_experts",
    )(plan, src, h2, w_gate_e, w_up_e, w_down_e)


def _combine_kernel(pos_ref, y_hbm, x1_ref, gt_ref, wts_ref, o_ref, ybuf, sem):
    i = pl.program_id(0)
    n = pl.num_programs(0)
    tm = x1_ref.shape[0]
    slot = lax.rem(i, 2)

    def start(blk, s):
        for k in range(2):
            _row_gather_start(pos_ref, (2 * blk + k) * tm, y_hbm, ybuf.at[s, k], sem.at[s], tm)

    @pl.when(i == 0)
    def _():
        start(0, 0)

        @pl.when(n > 1)
        def _():
            start(1, 1)

    for k in range(2):
        _row_gather_wait(y_hbm, ybuf.at[slot, k], sem.at[slot], tm)
    w = wts_ref[...]
    moe = w[:, 0:1] * ybuf[slot, 0] + w[:, 1:2] * ybuf[slot, 1]
    o_ref[...] = x1_ref[...] + gt_ref[...] * moe

    @pl.when(i + 2 < n)
    def _():
        start(i + 2, slot)


def _combine(pos, ys, x1, mod4, wts, seq):
    t, d = x1.shape
    tm = min(256, seq)
    bpr = seq // tm
    pos_blocks = pos.reshape(t // tm, tm, 2).transpose(0, 2, 1).reshape(-1)
    grid_spec = pltpu.PrefetchScalarGridSpec(
        num_scalar_prefetch=1,
        grid=(t // tm,),
        in_specs=[
            pl.BlockSpec(memory_space=pl.ANY),
            pl.BlockSpec((tm, d), lambda i, p: (i, 0)),
            pl.BlockSpec((None, None, 1, d), lambda i, p: (i // bpr, 5, 0, 0)),
            pl.BlockSpec((tm, LANES), lambda i, p: (i, 0)),
        ],
        out_specs=pl.BlockSpec((tm, d), lambda i, p: (i, 0)),
        scratch_shapes=[
            pltpu.VMEM((2, 2, tm, d), F32),
            pltpu.SemaphoreType.DMA((2,)),
        ],
    )
    return pl.pallas_call(
        _combine_kernel,
        grid_spec=grid_spec,
        out_shape=jax.ShapeDtypeStruct((t, d), F32),
        compiler_params=_cparams(("arbitrary",), 48),
        name="moe_combine",
    )(pos_blocks, ys, x1, mod4, wts)


def _swap_halves(w):
    half = w.shape[-1] // 2
    return jnp.concatenate([w[..., half:], w[..., :half]], axis=-1)


def _layout_w_uq(w_uq):
    r = w_uq.shape[0]
    w = w_uq.reshape(r, MLA_HEADS, MLA_QK)
    rope = w[..., MLA_NOPE:]
    return jnp.concatenate([w[..., :MLA_NOPE], rope, _swap_halves(rope)], axis=-1).reshape(r, MLA_HEADS * QK_PAD).astype(BF16)


def _layout_w_ukv(w_ukv):
    r = w_ukv.shape[0]
    w = w_ukv.reshape(r, MLA_HEADS, MLA_NOPE + MLA_V)
    return jnp.concatenate([w[..., :MLA_NOPE].reshape(r, -1), w[..., MLA_NOPE:].reshape(r, -1)], axis=-1).astype(BF16)


def _rope_gain(g):
    g1 = g[MLA_NOPE:MLA_NOPE + MLA_ROPE // 2]
    g2 = g[MLA_NOPE + MLA_ROPE // 2:]
    return jnp.concatenate([g[:MLA_NOPE], g1, g2, -g2, g1]).reshape(1, QK_PAD)


def _layer(x2, cond_mod4, pos2, seq, w_in, q_a_norm_g, w_uq, kv_a_norm_g, w_ukv, q_norm_g, k_norm_g, conv_w, conv_b,
           b_mlstm_gates, mlstm_norm_g, w_proj_a, w_proj_b, w_out, norm_mix_g, norm_ffn_g, w_group, b_group,
           w_router, b_router, w_gate_e, w_up_e, w_down_e):
    t, d = x2.shape
    b = t // seq
    mod4 = cond_mod4

    proj, gates = _inproj(x2, mod4, norm_mix_g, w_in.T, seq)

    inv = ROPE_THETA ** (-jnp.arange(0, MLA_ROPE, 2, dtype=F32) / MLA_ROPE)
    inv_lanes = jnp.tile(inv, LANES // (MLA_ROPE // 2)).reshape(1, LANES)
    qt, k, vt = _mla_prep(proj, pos2, _layout_w_uq(w_uq), _layout_w_ukv(w_ukv), q_a_norm_g.reshape(1, -1),
                        kv_a_norm_g.reshape(1, -1), _rope_gain(q_norm_g), _rope_gain(k_norm_g), inv_lanes, seq)
    out_a = _flash(qt, k.reshape(b, seq, -1), vt).reshape(t, MLA_W)

    proj3 = proj.reshape(b, seq, -1)
    qk3 = _conv_silu(proj3, conv_w, conv_b)
    gbias = jnp.zeros((1, LANES), F32).at[0, GATE_LANE:GATE_LANE + 2 * ML_HEADS].set(b_mlstm_gates.reshape(-1))
    hm = _mlstm(qk3, proj3, gates.reshape(b, seq, LANES), gbias, mlstm_norm_g).reshape(t, ML_W)

    mixed = _merge(out_a, hm, proj, w_proj_a.astype(BF16), w_proj_b.astype(BF16), seq)

    w_route = jnp.zeros((d, LANES), F32).at[:, :N_GROUPS].set(w_group).at[:, N_GROUPS:N_GROUPS + N_EXPERTS].set(w_router)
    b_route = jnp.zeros((1, LANES), F32).at[0, :N_GROUPS].set(b_group).at[0, N_GROUPS:N_GROUPS + N_EXPERTS].set(b_router)
    w_route_hi = w_route.astype(BF16)
    w_route2 = jnp.concatenate([w_route_hi, (w_route - w_route_hi.astype(F32)).astype(BF16)], axis=1)
    x1, h2, logits = _outproj(mixed, x2, w_out.astype(BF16), mod4, norm_ffn_g, w_route2, b_route, seq)

    posm, wts, counts = _route(logits)
    pos = posm[:, 0:2]

    tile = MOE_TILE
    i32 = jnp.int32
    cnt = counts[0, N_GROUPS:N_GROUPS + N_EXPERTS].astype(i32)
    padded = ((cnt + tile - 1) // tile) * tile
    ends = jnp.cumsum(padded)
    offs = ends - padded
    nt_max = (2 * t) // tile + N_EXPERTS
    n_tiles = ends[-1] // tile
    tile_idx = jnp.arange(nt_max, dtype=i32)
    tile_start = tile_idx * tile
    live = tile_idx < n_tiles
    texp = jnp.minimum(jnp.sum((ends[None, :] <= tile_start[:, None]).astype(i32), axis=1), N_EXPERTS - 1)
    active = cnt > 0
    order = jnp.cumsum(active.astype(i32)) - 1
    n_active = jnp.sum(active.astype(i32))
    experts = jnp.arange(N_EXPERTS, dtype=i32)
    by_order = jnp.sum(jnp.where(active[None, :] & (order[None, :] == experts[:, None]), experts[None, :], 0), axis=1)
    t_order = order[texp]
    first = (live & (tile_start == offs[texp])).astype(i32)
    ahead = t_order + 2
    prefetch = jnp.where(ahead < n_active, by_order[jnp.minimum(ahead, N_EXPERTS - 1)], -1)
    head = jnp.zeros((nt_max,), i32).at[0].set(n_tiles).at[1].set(by_order[0]).at[2].set(
        jnp.where(n_active > 1, by_order[1], -1))
    plan = jnp.stack([first, t_order % 2, prefetch, head]).astype(i32)

    tok = jnp.broadcast_to(jnp.arange(t, dtype=i32)[:, None], (t, 2))
    src = jnp.zeros((nt_max * tile,), i32).at[pos.reshape(-1)].set(tok.reshape(-1))

    ys = _moe(plan, src, h2, w_gate_e, w_up_e, w_down_e)

    return _combine(pos, ys, x1, mod4, wts, seq)


def kernel(x, c, positions, w_ada, b_ada, norm_mix_g, w_in, q_a_norm_g, w_uq, kv_a_norm_g, w_ukv, q_norm_g, k_norm_g, conv_w, conv_b, b_mlstm_gates, mlstm_norm_g, w_proj_a, w_proj_b, w_out, norm_ffn_g, w_group, b_group, w_router, b_router, w_gate_e, w_up_e, w_down_e):
    b, seq, d = x.shape
    depth = w_ada.shape[0]
    x2 = x.reshape(b * seq, d)
    pos2 = positions.reshape(b * seq, 1)
    c_pad = jnp.zeros((SUBLANES, d), F32).at[:b].set(c)
    for l in range(depth):
        mod = _adaln(c_pad, w_ada[l], b_ada[l])
        mod4 = mod[:b].reshape(b, 6, 1, d)
        x2 = _layer(x2, mod4, pos2, seq, w_in[l], q_a_norm_g[l], w_uq[l], kv_a_norm_g[l], w_ukv[l], q_norm_g[l],
                    k_norm_g[l], conv_w[l], conv_b[l], b_mlstm_gates[l], mlstm_norm_g[l], w_proj_a[l], w_proj_b[l],
                    w_out[l], norm_mix_g[l], norm_ffn_g[l], w_group[l], b_group[l], w_router[l], b_router[l],
                    w_gate_e[l], w_up_e[l], w_down_e[l])
    return x2.reshape(b, seq, d)
```

```python
import functools
import math

import jax
import jax.numpy as jnp
from jax import lax
from jax.experimental import pallas as pl
from jax.experimental.pallas import tpu as pltpu

F32 = jnp.float32
BF16 = jnp.bfloat16

LANES = 128
SUBLANES = 8

D_MODEL = 2048
MLA_HEADS = 8
MLA_NOPE = 128
MLA_ROPE = 64
MLA_QK = MLA_NOPE + MLA_ROPE
MLA_V = 128
Q_LORA = 512
KV_LORA = 256
ROPE_THETA = 10000.0
ML_HEADS = 8
ML_DQK = 128
ML_DV = 128
ML_CONV = 4
MLA_W = MLA_HEADS * MLA_V
ML_W = ML_HEADS * ML_DV
N_GROUPS = 4
EXP_PER_GROUP = 8
N_EXPERTS = N_GROUPS * EXP_PER_GROUP
D_FF_EXPERT = 512
EPS = 1e-6

QK_PAD = 2 * LANES

SRC_CQ = 0
SRC_KPE = Q_LORA + KV_LORA
SRC_QK = SRC_KPE + MLA_ROPE
SRC_V = SRC_QK + 2 * ML_HEADS * ML_DQK
SRC_O = SRC_V + ML_W
SRC_I = SRC_O + ML_W
SRC_GA = SRC_I + 2 * ML_HEADS
SRC_GB = SRC_GA + D_MODEL
GATE_LANE = SRC_I % LANES

IN_BLOCK = 1024
IN_BLOCK_SRC = (SRC_QK, SRC_QK + IN_BLOCK, SRC_GA, SRC_GA + IN_BLOCK, SRC_GB, SRC_GB + IN_BLOCK, SRC_V, SRC_O, SRC_CQ)
COL_QK = 0
COL_GA = 2048
COL_GB = 4096
COL_V = 6144
COL_O = 7168
COL_LAT = 8192
LAT_W = Q_LORA + KV_LORA + MLA_ROPE

MLSTM_CHUNK = 256
MOE_TILE = 256


def _cparams(sem, vmem_mb):
    return pltpu.CompilerParams(dimension_semantics=sem, vmem_limit_bytes=vmem_mb * 1024 * 1024)


def _dot(a, b):
    return jnp.dot(a, b, preferred_element_type=F32)


def _dot_nt(a, b):
    return lax.dot_general(a, b, (((1,), (1,)), ((), ())), preferred_element_type=F32)


def _dot_tn(a, b):
    return lax.dot_general(a, b, (((0,), (0,)), ((), ())), preferred_element_type=F32)


def _sigmoid(x):
    return 1.0 / (1.0 + jnp.exp(-x))


def _rms_scale(x, width):
    return lax.rsqrt(jnp.sum(x * x, axis=-1, keepdims=True) * (1.0 / width) + EPS)


def _adaln_kernel(c_ref, w_ref, b_ref, o_ref):
    c = c_ref[...]
    cond = (c * _sigmoid(c)).astype(BF16)
    o_ref[...] = _dot(cond, w_ref[...].astype(BF16)) + b_ref[...]


def _adaln(c_pad, w_ada, b_ada):
    rows, d = c_pad.shape
    n = w_ada.shape[1]
    tn = 1024
    return pl.pallas_call(
        _adaln_kernel,
        grid=(n // tn,),
        in_specs=[
            pl.BlockSpec((rows, d), lambda j: (0, 0)),
            pl.BlockSpec((d, tn), lambda j: (0, j)),
            pl.BlockSpec((1, tn), lambda j: (0, j)),
        ],
        out_specs=pl.BlockSpec((rows, tn), lambda j: (0, j)),
        out_shape=jax.ShapeDtypeStruct((rows, n), F32),
        compiler_params=_cparams(("arbitrary",), 40),
        name="adaln",
    )(c_pad, w_ada, b_ada.reshape(1, n))


def _inproj_kernel(off_ref, x_ref, sc_ref, sh_ref, g_ref, w_ref, wgate_ref, proj_ref, gates_ref, h_ref):
    j = pl.program_id(1)

    @pl.when(j == 0)
    def _():
        x = x_ref[...]
        h = x * _rms_scale(x, x.shape[-1]) * g_ref[...]
        h = (h * (1.0 + sc_ref[...]) + sh_ref[...]).astype(BF16)
        h_ref[...] = h
        gates_ref[...] = _dot_nt(h, wgate_ref[...].astype(BF16))

    proj_ref[...] = _dot_nt(h_ref[...], w_ref[...].astype(BF16)).astype(proj_ref.dtype)


def _inproj(x2, mod4, norm_g, w_in_t, seq):
    t, d = x2.shape
    tm = min(1024, seq)
    tn = IN_BLOCK
    bpr = seq // tm
    assert all(o % SUBLANES == 0 for o in IN_BLOCK_SRC)
    offs = jnp.asarray([o // SUBLANES for o in IN_BLOCK_SRC], jnp.int32)
    nblk = len(IN_BLOCK_SRC)
    gate_tile = SRC_I // LANES
    grid_spec = pltpu.PrefetchScalarGridSpec(
        num_scalar_prefetch=1,
        grid=(t // tm, nblk),
        in_specs=[
            pl.BlockSpec((tm, d), lambda i, j, o: (i, 0)),
            pl.BlockSpec((None, None, 1, d), lambda i, j, o: (i // bpr, 1, 0, 0)),
            pl.BlockSpec((None, None, 1, d), lambda i, j, o: (i // bpr, 0, 0, 0)),
            pl.BlockSpec((1, d), lambda i, j, o: (0, 0)),
            pl.BlockSpec((pl.Element(tn), pl.Element(d)), lambda i, j, o: (o[j] * SUBLANES, 0)),
            pl.BlockSpec((LANES, d), lambda i, j, o: (gate_tile, 0)),
        ],
        out_specs=[
            pl.BlockSpec((tm, tn), lambda i, j, o: (i, j)),
            pl.BlockSpec((tm, LANES), lambda i, j, o: (i, 0)),
        ],
        scratch_shapes=[pltpu.VMEM((tm, d), BF16)],
    )
    return pl.pallas_call(
        _inproj_kernel,
        grid_spec=grid_spec,
        out_shape=[
            jax.ShapeDtypeStruct((t, nblk * tn), BF16),
            jax.ShapeDtypeStruct((t, LANES), F32),
        ],
        compiler_params=_cparams(("arbitrary", "arbitrary"), 56),
        name="inproj",
    )(offs, x2, mod4, mod4, norm_g.reshape(1, d), w_in_t, w_in_t)


def _mla_prep_kernel(lat_ref, pos_ref, wuq_ref, wukv_ref, gqa_ref, gkva_ref, gq_ref, gk_ref, inv_ref,
                     qt_ref, k_ref, vt_ref):
    lat = lat_ref[...].astype(F32)
    cq = lat[:, :Q_LORA]
    ckv = lat[:, Q_LORA:Q_LORA + KV_LORA]
    kc = lat[:, SRC_KPE:SRC_KPE + LANES]
    cqn = (cq * _rms_scale(cq, Q_LORA) * gqa_ref[...]).astype(BF16)
    ckvn = (ckv * _rms_scale(ckv, KV_LORA) * gkva_ref[...]).astype(BF16)
    qraw = _dot(cqn, wuq_ref[...])
    kv = _dot(ckvn, wukv_ref[...])

    ang = pos_ref[...].astype(F32) * inv_ref[...]
    lane = lax.broadcasted_iota(jnp.int32, ang.shape, 1)
    lo = lane < MLA_ROPE
    cs = jnp.where(lo, jnp.cos(ang), jnp.sin(ang))

    quarter = MLA_ROPE // 2
    want = jnp.where(lane < 3 * quarter, lane - quarter, lane - 3 * quarter)
    came = pltpu.roll(lane, quarter, 1)
    swapped = jnp.where(came == want, pltpu.roll(kc, quarter, 1), pltpu.roll(kc, 3 * quarter, 1))
    kc = jnp.where(lo, kc, swapped)

    gq = gq_ref[...]
    gk = gk_ref[...]
    gq_n, gq_r = gq[:, :LANES], gq[:, LANES:]
    gk_n, gk_r = gk[:, :LANES], gk[:, LANES:]

    def rope(chunk, g_r):
        a = chunk * (g_r * cs)
        return jnp.where(lo, a + pltpu.roll(a, MLA_ROPE, 1), 0.0)

    kpe_ss = jnp.sum(jnp.where(lo, kc * kc, 0.0), axis=-1, keepdims=True)
    k_rope = rope(kc, gk_r)
    scale = MLA_QK ** -0.5
    for h in range(MLA_HEADS):
        kn = kv[:, h * MLA_NOPE:(h + 1) * MLA_NOPE]
        sk = lax.rsqrt((jnp.sum(kn * kn, axis=-1, keepdims=True) + kpe_ss) * (1.0 / MLA_QK) + EPS)
        k_ref[:, h * QK_PAD:h * QK_PAD + LANES] = (kn * sk * gk_n).astype(BF16)
        k_ref[:, h * QK_PAD + LANES:(h + 1) * QK_PAD] = (k_rope * sk).astype(BF16)
        qn = qraw[:, h * QK_PAD:h * QK_PAD + LANES]
        qr = qraw[:, h * QK_PAD + LANES:(h + 1) * QK_PAD]
        ss = jnp.sum(qn * qn, axis=-1, keepdims=True) + jnp.sum(jnp.where(lo, qr * qr, 0.0), axis=-1, keepdims=True)
        sq = lax.rsqrt(ss * (1.0 / MLA_QK) + EPS) * scale
        qt_ref[h * QK_PAD:h * QK_PAD + LANES, :] = (qn * sq * gq_n).T.astype(BF16)
        qt_ref[h * QK_PAD + LANES:(h + 1) * QK_PAD, :] = (rope(qr, gq_r) * sq).T.astype(BF16)
        vh = kv[:, MLA_HEADS * MLA_NOPE + h * MLA_V:MLA_HEADS * MLA_NOPE + (h + 1) * MLA_V]
        vt_ref[h * MLA_V:(h + 1) * MLA_V, :] = vh.T.astype(BF16)


def _mla_prep(proj, pos2, wuq_p, wukv_p, gqa, gkva, gq, gk, inv_lanes, seq):
    t = proj.shape[0]
    tm = min(512, seq)
    hq = MLA_HEADS * QK_PAD
    lat_blk = COL_LAT // 1024
    const = lambda i: (0, 0)
    return pl.pallas_call(
        _mla_prep_kernel,
        grid=(t // tm,),
        in_specs=[
            pl.BlockSpec((tm, 1024), lambda i: (i, lat_blk)),
            pl.BlockSpec((tm, 1), lambda i: (i, 0)),
            pl.BlockSpec(wuq_p.shape, const),
            pl.BlockSpec(wukv_p.shape, const),
            pl.BlockSpec(gqa.shape, const),
            pl.BlockSpec(gkva.shape, const),
            pl.BlockSpec(gq.shape, const),
            pl.BlockSpec(gk.shape, const),
            pl.BlockSpec(inv_lanes.shape, const),
        ],
        out_specs=[
            pl.BlockSpec((hq, tm), lambda i: (0, i)),
            pl.BlockSpec((tm, hq), lambda i: (i, 0)),
            pl.BlockSpec((MLA_W, tm), lambda i: (0, i)),
        ],
        out_shape=[
            jax.ShapeDtypeStruct((hq, t), BF16),
            jax.ShapeDtypeStruct((t, hq), BF16),
            jax.ShapeDtypeStruct((MLA_W, t), BF16),
        ],
        compiler_params=_cparams(("arbitrary",), 48),
        name="mla_prep",
    )(proj, pos2, wuq_p, wukv_p, gqa, gkva, gq, gk, inv_lanes)


def _flash_kernel(qt_ref, k_ref, vt_ref, o_ref, *, tq, tk):
    seq = k_ref.shape[0]
    for qi in range(seq // tq):
        qt = qt_ref[:, qi * tq:(qi + 1) * tq]
        m = jnp.full((1, tq), -jnp.inf, F32)
        l = jnp.zeros((1, tq), F32)
        acc = jnp.zeros((MLA_V, tq), F32)
        n_blocks = ((qi + 1) * tq) // tk
        for j in range(n_blocks):
            st = _dot(k_ref[j * tk:(j + 1) * tk, :], qt)
            if (j + 1) * tk - 1 > qi * tq:
                key = lax.broadcasted_iota(jnp.int32, st.shape, 0) + j * tk
                qry = lax.broadcasted_iota(jnp.int32, st.shape, 1) + qi * tq
                st = jnp.where(key <= qry, st, -jnp.inf)
            m_new = jnp.maximum(m, jnp.max(st, axis=0, keepdims=True))
            alpha = jnp.exp(m - m_new)
            p = jnp.exp(st - m_new)
            l = alpha * l + jnp.sum(p, axis=0, keepdims=True)
            acc = alpha * acc + _dot(vt_ref[:, j * tk:(j + 1) * tk], p.astype(BF16))
            m = m_new
        o_ref[qi * tq:(qi + 1) * tq, :] = (acc / l).T.astype(o_ref.dtype)


def _flash(qt, k3, vt):
    b, seq, _ = k3.shape
    tq = min(512, seq)
    tk = min(512, seq)
    kern = functools.partial(_flash_kernel, tq=tq, tk=tk)
    return pl.pallas_call(
        kern,
        grid=(b, MLA_HEADS),
        in_specs=[
            pl.BlockSpec((QK_PAD, seq), lambda i, h: (h, i)),
            pl.BlockSpec((None, seq, QK_PAD), lambda i, h: (i, 0, h)),
            pl.BlockSpec((MLA_V, seq), lambda i, h: (h, i)),
        ],
        out_specs=pl.BlockSpec((None, seq, MLA_V), lambda i, h: (i, 0, h)),
        out_shape=jax.ShapeDtypeStruct((b, seq, MLA_W), BF16),
        compiler_params=_cparams(("arbitrary", "arbitrary"), 40),
        name="flash",
    )(qt, k3, vt)


def _conv_kernel(cur_ref, halo_ref, w_ref, b_ref, qt_ref, k_ref, buf_ref, *, k_scale):
    tm = cur_ref.shape[0]
    cols = cur_ref.shape[1]
    half = cols // 2
    first = pl.program_id(1) == 0
    halo = halo_ref[...].astype(F32)
    buf_ref[0:SUBLANES, :] = jnp.where(first, 0.0, halo)
    buf_ref[SUBLANES:SUBLANES + tm, :] = cur_ref[...].astype(F32)
    cw = 512
    for c in range(cols // cw):
        sl = slice(c * cw, (c + 1) * cw)
        acc = jnp.zeros((tm, cw), F32) + b_ref[:, sl]
        for j in range(ML_CONV):
            off = SUBLANES - (ML_CONV - 1) + j
            acc = acc + buf_ref[off:off + tm, sl] * w_ref[j:j + 1, sl]
        y = acc * _sigmoid(acc)
        if c * cw < half:
            qt_ref[sl, :] = y.T.astype(qt_ref.dtype)
        else:
            k_ref[:, c * cw - half:(c + 1) * cw - half] = (y * k_scale).astype(k_ref.dtype)


def _conv_silu(proj3, conv_w, conv_b):
    b, seq, _ = proj3.shape
    cols = 2 * ML_HEADS * ML_DQK
    half = cols // 2
    tm = min(512, seq)
    hb = tm // SUBLANES
    kern = functools.partial(_conv_kernel, k_scale=ML_DQK ** -0.5)
    return pl.pallas_call(
        kern,
        grid=(b, seq // tm),
        in_specs=[
            pl.BlockSpec((None, tm, cols), lambda i, s: (i, s, COL_QK // cols)),
            pl.BlockSpec((None, SUBLANES, cols), lambda i, s: (i, jnp.maximum(s * hb - 1, 0), COL_QK // cols)),
            pl.BlockSpec((ML_CONV, cols), lambda i, s: (0, 0)),
            pl.BlockSpec((1, cols), lambda i, s: (0, 0)),
        ],
        out_specs=[
            pl.BlockSpec((None, half, tm), lambda i, s: (i, 0, s)),
            pl.BlockSpec((None, tm, half), lambda i, s: (i, s, 0)),
        ],
        out_shape=[
            jax.ShapeDtypeStruct((b, half, seq), BF16),
            jax.ShapeDtypeStruct((b, seq, half), BF16),
        ],
        scratch_shapes=[pltpu.VMEM((tm + SUBLANES, cols), F32)],
        compiler_params=_cparams(("arbitrary", "arbitrary"), 40),
        name="conv_silu",
    )(proj3, proj3, conv_w, conv_b.reshape(1, cols))


def _log_sigmoid(x):
    return -(jnp.maximum(-x, 0.0) + jnp.log1p(jnp.exp(-jnp.abs(x))))


def _mlstm_kernel(qt_ref, k_ref, v_ref, o_ref, gates_ref, gbias_ref, ng_ref, out_ref, ct_ref, m_ref):
    L = k_ref.shape[0]

    @pl.when(pl.program_id(1) == 0)
    def _():
        ct_ref[...] = jnp.zeros(ct_ref.shape, F32)
        m_ref[...] = jnp.zeros(m_ref.shape, F32)

    g = gates_ref[...] + gbias_ref[...]
    gt = g.T
    lf = _log_sigmoid(g)
    lft = _log_sigmoid(gt)
    r = lax.broadcasted_iota(jnp.int32, (L, L), 0)
    c = lax.broadcasted_iota(jnp.int32, (L, L), 1)
    src_le_qry = r <= c
    tril = (c <= r).astype(F32)
    triu = src_le_qry.astype(F32)
    hi = lax.Precision.HIGHEST
    bcol_all = jnp.dot(tril, lf, preferred_element_type=F32, precision=hi)
    brow_all = jnp.dot(lft, triu, preferred_element_type=F32, precision=hi)
    row = lax.broadcasted_iota(jnp.int32, (ML_DV, L), 0)
    ones_row = jnp.where(row == 0, 1.0, 0.0)

    for h in range(ML_HEADS):
        hs = slice(h * ML_DQK, (h + 1) * ML_DQK)
        li, lf_ = GATE_LANE + h, GATE_LANE + ML_HEADS + h
        b_row = brow_all[lf_:lf_ + 1, :]
        i_row = gt[li:li + 1, :]
        u_col = g[:, li:li + 1] - bcol_all[:, lf_:lf_ + 1]
        m_prev = m_ref[h][:, :1]
        logw_t = jnp.where(src_le_qry, b_row + u_col, -jnp.inf)
        log_inter = b_row + m_prev
        m_t = jnp.maximum(jnp.max(logw_t, axis=0, keepdims=True), log_inter)
        w_t = jnp.exp(logw_t - m_t)
        a = jnp.exp(log_inter - m_t)
        qt = qt_ref[hs, :]
        kh = k_ref[:, hs]
        vt_aug = jnp.concatenate([v_ref[:, hs].astype(F32).T, ones_row], axis=0)
        s_t = _dot(kh, qt) * w_t
        ct = ct_ref[h]
        nd = _dot(vt_aug.astype(BF16), s_t.astype(BF16)) + a * _dot(ct.astype(BF16), qt)
        num = nd[:ML_DV, :]
        den = nd[ML_DV:ML_DV + 1, :]
        hout_t = num * (1.0 / jnp.maximum(jnp.abs(den), jnp.exp(-m_t)))
        hn_t = hout_t * lax.rsqrt(jnp.sum(hout_t * hout_t, axis=0, keepdims=True) * (1.0 / ML_DV) + EPS)
        gate = _sigmoid(o_ref[:, hs].astype(F32))
        out_ref[:, hs] = (hn_t.T * ng_ref[:, hs] * gate).astype(out_ref.dtype)

        b_last = b_row[:, L - 1:L]
        logg = b_last - b_row + i_row
        m_new = jnp.maximum(b_last + m_prev, jnp.max(logg, axis=-1, keepdims=True))
        g_row = jnp.exp(logg - m_new)
        decay = jnp.exp(b_last + m_prev - m_new)
        ct_ref[h] = decay * ct + _dot((vt_aug * g_row).astype(BF16), kh)
        m_ref[h] = jnp.broadcast_to(m_new, m_ref.shape[1:])


def _mlstm(qt3, k3, proj3, gates3, gbias, norm_g):
    b, seq, w = k3.shape
    L = min(MLSTM_CHUNK, seq)
    return pl.pallas_call(
        _mlstm_kernel,
        grid=(b, seq // L),
        in_specs=[
            pl.BlockSpec((None, w, L), lambda i, c: (i, 0, c)),
            pl.BlockSpec((None, L, w), lambda i, c: (i, c, 0)),
            pl.BlockSpec((None, L, w), lambda i, c: (i, c, COL_V // w)),
            pl.BlockSpec((None, L, w), lambda i, c: (i, c, COL_O // w)),
            pl.BlockSpec((None, L, LANES), lambda i, c: (i, c, 0)),
            pl.BlockSpec((1, LANES), lambda i, c: (0, 0)),
            pl.BlockSpec((1, w), lambda i, c: (0, 0)),
        ],
        out_specs=pl.BlockSpec((None, L, w), lambda i, c: (i, c, 0)),
        out_shape=jax.ShapeDtypeStruct((b, seq, w), BF16),
        scratch_shapes=[
            pltpu.VMEM((ML_HEADS, 2 * ML_DV, ML_DQK), F32),
            pltpu.VMEM((ML_HEADS, 1, LANES), F32),
        ],
        compiler_params=_cparams(("arbitrary", "arbitrary"), 40),
        name="mlstm",
    )(qt3, k3, proj3, proj3, gates3, gbias, norm_g.reshape(1, w))


def _merge_kernel(a_ref, b_ref, ga_ref, gb_ref, wa_ref, wb_ref, o_ref):
    pa = _dot(a_ref[...], wa_ref[...])
    pb = _dot(b_ref[...], wb_ref[...])
    mixed = _sigmoid(ga_ref[...].astype(F32)) * pa + _sigmoid(gb_ref[...].astype(F32)) * pb
    o_ref[...] = mixed.astype(o_ref.dtype)


def _merge(out_a, hm, proj, wa, wb, seq):
    t = out_a.shape[0]
    d = wa.shape[1]
    tm = min(1024, seq)
    tn = 1024
    return pl.pallas_call(
        _merge_kernel,
        grid=(d // tn, t // tm),
        in_specs=[
            pl.BlockSpec((tm, MLA_W), lambda j, i: (i, 0)),
            pl.BlockSpec((tm, ML_W), lambda j, i: (i, 0)),
            pl.BlockSpec((tm, tn), lambda j, i: (i, COL_GA // tn + j)),
            pl.BlockSpec((tm, tn), lambda j, i: (i, COL_GB // tn + j)),
            pl.BlockSpec((MLA_W, tn), lambda j, i: (0, j)),
            pl.BlockSpec((ML_W, tn), lambda j, i: (0, j)),
        ],
        out_specs=pl.BlockSpec((tm, tn), lambda j, i: (i, j)),
        out_shape=jax.ShapeDtypeStruct((t, d), BF16),
        compiler_params=_cparams(("arbitrary", "arbitrary"), 40),
        name="merge",
    )(out_a, hm, proj, proj, wa, wb)


def _outproj_kernel(mix_ref, x_ref, w_ref, gt_ref, sc_ref, sh_ref, g_ref, wr_ref, br_ref, x1_ref, h2_ref, lg_ref):
    y = _dot(mix_ref[...], w_ref[...])
    x1 = x_ref[...] + gt_ref[...] * y
    x1_ref[...] = x1
    h2 = x1 * _rms_scale(x1, x1.shape[-1]) * g_ref[...]
    h2 = h2 * (1.0 + sc_ref[...]) + sh_ref[...]
    h2_ref[...] = h2
    h_hi = h2.astype(BF16)
    h_lo = (h2 - h_hi.astype(F32)).astype(BF16)
    r = _dot(h_hi, wr_ref[...]) + _dot(h_lo, wr_ref[...])
    lg_ref[...] = r[:, :LANES] + r[:, LANES:] + br_ref[...]


def _outproj(mixed, x2, w_out, mod4, norm_g, w_route2, b_route, seq):
    t, d = x2.shape
    tm = min(512, seq)
    bpr = seq // tm
    mod_spec = lambda k: pl.BlockSpec((None, None, 1, d), lambda i: (i // bpr, k, 0, 0))
    const = lambda i: (0, 0)
    return pl.pallas_call(
        _outproj_kernel,
        grid=(t // tm,),
        in_specs=[
            pl.BlockSpec((tm, d), lambda i: (i, 0)),
            pl.BlockSpec((tm, d), lambda i: (i, 0)),
            pl.BlockSpec((d, d), const, pipeline_mode=pl.Buffered(1)),
            mod_spec(2),
            mod_spec(4),
            mod_spec(3),
            pl.BlockSpec((1, d), const),
            pl.BlockSpec((d, 2 * LANES), const, pipeline_mode=pl.Buffered(1)),
            pl.BlockSpec((1, LANES), const),
        ],
        out_specs=[
            pl.BlockSpec((tm, d), lambda i: (i, 0)),
            pl.BlockSpec((tm, d), lambda i: (i, 0)),
            pl.BlockSpec((tm, LANES), lambda i: (i, 0)),
        ],
        out_shape=[
            jax.ShapeDtypeStruct((t, d), F32),
            jax.ShapeDtypeStruct((t, d), F32),
            jax.ShapeDtypeStruct((t, LANES), F32),
        ],
        compiler_params=_cparams(("arbitrary",), 56),
        name="outproj",
    )(mixed, x2, w_out, mod4, mod4, mod4, norm_g.reshape(1, d), w_route2, b_route)


def _route_kernel(lg_ref, pos_ref, wts_ref, cnt_ref, carry_ref, offs_ref, meta_s, wts_s):
    phase = pl.program_id(0)
    i = pl.program_id(1)
    tm = lg_ref.shape[0]
    lane = lax.broadcasted_iota(jnp.int32, (tm, LANES), 1)

    @pl.when((phase == 0) & (i == 0))
    def _():
        carry_ref[...] = jnp.zeros(carry_ref.shape, F32)

    @pl.when(phase == 0)
    def _():
        lg = lg_ref[...]
        big = jnp.int32(LANES)
        ninf = -jnp.inf

        def first_argmax(vals):
            mx = jnp.max(vals, axis=-1, keepdims=True)
            idx = jnp.min(jnp.where(vals == mx, lane, big), axis=-1, keepdims=True)
            return mx, idx

        gl = jnp.where(lane < N_GROUPS, lg, ninf)
        gmax, gsel = first_argmax(gl)
        g_w = 1.0 / jnp.sum(jnp.exp(gl - gmax), axis=-1, keepdims=True)
        lo = N_GROUPS + gsel * EXP_PER_GROUP
        in_grp = (lane >= lo) & (lane < lo + EXP_PER_GROUP)
        el = jnp.where(in_grp, lg, ninf)
        e1, i1 = first_argmax(el)
        e2, i2 = first_argmax(jnp.where(lane == i1, ninf, el))
        p2 = jnp.exp(e2 - e1)
        w1 = g_w / (1.0 + p2)
        w2 = g_w * p2 / (1.0 + p2)

        oh1 = lane == i1
        oh2 = lane == i2
        oh = jnp.where(oh1 | oh2, 1.0, 0.0)
        r = lax.broadcasted_iota(jnp.int32, (tm, tm), 0)
        c = lax.broadcasted_iota(jnp.int32, (tm, tm), 1)
        strict = jnp.where(c < r, 1.0, 0.0).astype(BF16)
        before = _dot(strict, oh.astype(BF16)) + carry_ref[...]
        rank1 = jnp.sum(jnp.where(oh1, before, 0.0), axis=-1, keepdims=True).astype(jnp.int32)
        rank2 = jnp.sum(jnp.where(oh2, before, 0.0), axis=-1, keepdims=True).astype(jnp.int32)
        carry_ref[...] = carry_ref[...] + jnp.sum(oh, axis=0, keepdims=True)
        meta_s[i] = jnp.where(lane == 0, i1, jnp.where(lane == 1, i2, jnp.where(lane == 2, rank1, jnp.where(lane == 3, rank2, 0))))
        wts_s[i] = jnp.where(lane == 0, w1, jnp.where(lane == 1, w2, 0.0))

    @pl.when((phase == 1) & (i == 0))
    def _():
        cnt = carry_ref[...]
        cnt_ref[...] = jnp.broadcast_to(cnt, cnt_ref.shape)
        padded = jnp.ceil(cnt * (1.0 / MOE_TILE)) * MOE_TILE
        r = lax.broadcasted_iota(jnp.int32, (LANES, LANES), 0)
        c = lax.broadcasted_iota(jnp.int32, (LANES, LANES), 1)
        upper = jnp.where(r < c, 1.0, 0.0).astype(BF16)
        padded8 = jnp.broadcast_to(padded, (SUBLANES, LANES)).astype(BF16)
        offs_ref[...] = _dot(padded8, upper)[:1, :]

    @pl.when(phase == 1)
    def _():
        meta = meta_s[i]
        offs = offs_ref[...]
        off1 = jnp.sum(jnp.where(lane == meta[:, 0:1], offs, 0.0), axis=-1, keepdims=True).astype(jnp.int32)
        off2 = jnp.sum(jnp.where(lane == meta[:, 1:2], offs, 0.0), axis=-1, keepdims=True).astype(jnp.int32)
        pos1 = off1 + meta[:, 2:3]
        pos2 = off2 + meta[:, 3:4]
        pos_ref[...] = jnp.where(lane == 0, pos1, jnp.where(lane == 1, pos2, 0))
        wts_ref[...] = wts_s[i]


def _route(logits):
    t = logits.shape[0]
    tm = min(512, t)
    nb = t // tm
    return pl.pallas_call(
        _route_kernel,
        grid=(2, nb),
        in_specs=[pl.BlockSpec((tm, LANES), lambda p, i: (i * (1 - p), 0))],
        out_specs=[
            pl.BlockSpec((tm, LANES), lambda p, i: (i * p, 0)),
            pl.BlockSpec((tm, LANES), lambda p, i: (i * p, 0)),
            pl.BlockSpec((SUBLANES, LANES), lambda p, i: (0, 0)),
        ],
        out_shape=[
            jax.ShapeDtypeStruct((t, LANES), jnp.int32),
            jax.ShapeDtypeStruct((t, LANES), F32),
            jax.ShapeDtypeStruct((SUBLANES, LANES), F32),
        ],
        scratch_shapes=[
            pltpu.VMEM((1, LANES), F32),
            pltpu.VMEM((1, LANES), F32),
            pltpu.VMEM((nb, tm, LANES), jnp.int32),
            pltpu.VMEM((nb, tm, LANES), F32),
        ],
        compiler_params=_cparams(("arbitrary", "arbitrary"), 32),
        name="route",
    )(logits)


def _row_gather_start(idx_at, src_hbm, dst, sem, rows):
    for r in range(rows):
        pltpu.make_async_copy(src_hbm.at[pl.ds(idx_at(r), 1)], dst.at[pl.ds(r, 1)], sem).start()


def _row_gather_wait(src_hbm, dst, sem, rows):
    pltpu.make_async_copy(src_hbm.at[pl.ds(0, rows)], dst, sem).wait()


def _moe_kernel(plan_ref, src_ref, h_hbm, wg_hbm, wu_hbm, wd_hbm, y_ref,
                xbuf, xsem, stg_g, stg_u, stg_d, wsem, wb_g, wb_u, wb_d):
    j = pl.program_id(0)
    nt = plan_ref[3, 0]
    tile = xbuf.shape[1]
    nslot = xbuf.shape[0]
    slot = lax.rem(j, nslot)

    def w_copies(e, s):
        return (pltpu.make_async_copy(wg_hbm.at[e], stg_g.at[s], wsem.at[s]),
                pltpu.make_async_copy(wu_hbm.at[e], stg_u.at[s], wsem.at[s]),
                pltpu.make_async_copy(wd_hbm.at[e], stg_d.at[s], wsem.at[s]))

    def gather(tile_idx, s):
        base = tile_idx * tile
        _row_gather_start(lambda r: src_ref[base + r], h_hbm, xbuf.at[s], xsem.at[s], tile)

    @pl.when(j == 0)
    def _():
        for cp in w_copies(plan_ref[3, 1], 0):
            cp.start()
        gather(0, 0)

        @pl.when(plan_ref[3, 2] >= 0)
        def _():
            for cp in w_copies(plan_ref[3, 2], 1):
                cp.start()

        @pl.when(nt > 1)
        def _():
            gather(1, 1)

    @pl.when(j + 2 < nt)
    def _():
        gather(j + 2, lax.rem(j + 2, nslot))

    @pl.when((j < nt) & (plan_ref[0, j] == 1))
    def _():
        s = plan_ref[1, j]
        for cp in w_copies(0, s):
            cp.wait()
        wb_g[...] = stg_g[s].astype(BF16)
        wb_u[...] = stg_u[s].astype(BF16)
        wb_d[...] = stg_d[s].astype(BF16)

        @pl.when(plan_ref[2, j] >= 0)
        def _():
            for cp in w_copies(plan_ref[2, j], s):
                cp.start()

    @pl.when(j < nt)
    def _():
        _row_gather_wait(h_hbm, xbuf.at[slot], xsem.at[slot], tile)
        x = xbuf[slot].astype(BF16)
        hg = _dot(x, wb_g[...])
        hu = _dot(x, wb_u[...])
        act = (hg * _sigmoid(hg) * hu).astype(BF16)
        y_ref[...] = _dot(act, wb_d[...])

    @pl.when(j >= nt)
    def _():
        y_ref[...] = jnp.zeros(y_ref.shape, y_ref.dtype)


def _moe(plan, src, h2, w_gate_e, w_up_e, w_down_e):
    t, d = h2.shape
    f = w_gate_e.shape[-1]
    nt_max = plan.shape[1]
    tile = MOE_TILE
    any_spec = pl.BlockSpec(memory_space=pl.ANY)
    grid_spec = pltpu.PrefetchScalarGridSpec(
        num_scalar_prefetch=2,
        grid=(nt_max,),
        in_specs=[any_spec, any_spec, any_spec, any_spec],
        out_specs=pl.BlockSpec((tile, d), lambda j, plan_, src_: (j, 0)),
        scratch_shapes=[
            pltpu.VMEM((3, tile, d), F32),
            pltpu.SemaphoreType.DMA((3,)),
            pltpu.VMEM((2, d, f), F32),
            pltpu.VMEM((2, d, f), F32),
            pltpu.VMEM((2, f, d), F32),
            pltpu.SemaphoreType.DMA((2,)),
            pltpu.VMEM((d, f), BF16),
            pltpu.VMEM((d, f), BF16),
            pltpu.VMEM((f, d), BF16),
        ],
    )
    return pl.pallas_call(
        _moe_kernel,
        grid_spec=grid_spec,
        out_shape=jax.ShapeDtypeStruct((nt_max * tile, d), F32),
        compiler_params=_cparams(("arbitrary",), 56),
        name="moe_experts",
    )(plan, src, h2, w_gate_e, w_up_e, w_down_e)


def _combine_kernel(pos_ref, y_hbm, x1_ref, gt_ref, wts_ref, o_ref, ybuf, sem):
    i = pl.program_id(0)
    n = pl.num_programs(0)
    tm = x1_ref.shape[0]
    slot = lax.rem(i, 2)

    def start(blk, s):
        for k in range(2):
            base = (2 * blk + k) * tm
            _row_gather_start(lambda r, base=base: pos_ref[base + r], y_hbm, ybuf.at[s, k], sem.at[s], tm)

    @pl.when(i == 0)
    def _():
        start(0, 0)

        @pl.when(n > 1)
        def _():
            start(1, 1)

    for k in range(2):
        _row_gather_wait(y_hbm, ybuf.at[slot, k], sem.at[slot], tm)
    w = wts_ref[...]
    moe = w[:, 0:1] * ybuf[slot, 0] + w[:, 1:2] * ybuf[slot, 1]
    o_ref[...] = x1_ref[...] + gt_ref[...] * moe

    @pl.when(i + 2 < n)
    def _():
        start(i + 2, slot)


def _combine(pos, ys, x1, mod4, wts, seq):
    t, d = x1.shape
    tm = min(256, seq)
    bpr = seq // tm
    pos_blocks = pos.reshape(t // tm, tm, 2).transpose(0, 2, 1).reshape(-1)
    grid_spec = pltpu.PrefetchScalarGridSpec(
        num_scalar_prefetch=1,
        grid=(t // tm,),
        in_specs=[
            pl.BlockSpec(memory_space=pl.ANY),
            pl.BlockSpec((tm, d), lambda i, p: (i, 0)),
            pl.BlockSpec((None, None, 1, d), lambda i, p: (i // bpr, 5, 0, 0)),
            pl.BlockSpec((tm, LANES), lambda i, p: (i, 0)),
        ],
        out_specs=pl.BlockSpec((tm, d), lambda i, p: (i, 0)),
        scratch_shapes=[
            pltpu.VMEM((2, 2, tm, d), F32),
            pltpu.SemaphoreType.DMA((2,)),
        ],
    )
    return pl.pallas_call(
        _combine_kernel,
        grid_spec=grid_spec,
        out_shape=jax.ShapeDtypeStruct((t, d), F32),
        compiler_params=_cparams(("arbitrary",), 48),
        name="moe_combine",
    )(pos_blocks, ys, x1, mod4, wts)


def _swap_halves(w):
    half = w.shape[-1] // 2
    return jnp.concatenate([w[..., half:], w[..., :half]], axis=-1)


def _layout_w_uq(w_uq):
    r = w_uq.shape[0]
    w = w_uq.reshape(r, MLA_HEADS, MLA_QK)
    rope = w[..., MLA_NOPE:]
    return jnp.concatenate([w[..., :MLA_NOPE], rope, _swap_halves(rope)], axis=-1).reshape(r, MLA_HEADS * QK_PAD).astype(BF16)


def _layout_w_ukv(w_ukv):
    r = w_ukv.shape[0]
    w = w_ukv.reshape(r, MLA_HEADS, MLA_NOPE + MLA_V)
    return jnp.concatenate([w[..., :MLA_NOPE].reshape(r, -1), w[..., MLA_NOPE:].reshape(r, -1)], axis=-1).astype(BF16)


def _rope_gain(g):
    g1 = g[MLA_NOPE:MLA_NOPE + MLA_ROPE // 2]
    g2 = g[MLA_NOPE + MLA_ROPE // 2:]
    return jnp.concatenate([g[:MLA_NOPE], g1, g2, -g2, g1]).reshape(1, QK_PAD)


def _layer(x2, cond_mod4, pos2, seq, w_in, q_a_norm_g, w_uq, kv_a_norm_g, w_ukv, q_norm_g, k_norm_g, conv_w, conv_b,
           b_mlstm_gates, mlstm_norm_g, w_proj_a, w_proj_b, w_out, norm_mix_g, norm_ffn_g, w_group, b_group,
           w_router, b_router, w_gate_e, w_up_e, w_down_e):
    t, d = x2.shape
    b = t // seq
    mod4 = cond_mod4

    proj, gates = _inproj(x2, mod4, norm_mix_g, w_in.T, seq)

    inv = ROPE_THETA ** (-jnp.arange(0, MLA_ROPE, 2, dtype=F32) / MLA_ROPE)
    inv_lanes = jnp.tile(inv, LANES // (MLA_ROPE // 2)).reshape(1, LANES)
    qt, k, vt = _mla_prep(proj, pos2, _layout_w_uq(w_uq), _layout_w_ukv(w_ukv), q_a_norm_g.reshape(1, -1),
                        kv_a_norm_g.reshape(1, -1), _rope_gain(q_norm_g), _rope_gain(k_norm_g), inv_lanes, seq)
    out_a = _flash(qt, k.reshape(b, seq, -1), vt).reshape(t, MLA_W)

    proj3 = proj.reshape(b, seq, -1)
    ml_qt, ml_k = _conv_silu(proj3, conv_w, conv_b)
    gbias = jnp.zeros((1, LANES), F32).at[0, GATE_LANE:GATE_LANE + 2 * ML_HEADS].set(b_mlstm_gates.reshape(-1))
    hm = _mlstm(ml_qt, ml_k, proj3, gates.reshape(b, seq, LANES), gbias, mlstm_norm_g).reshape(t, ML_W)

    mixed = _merge(out_a, hm, proj, w_proj_a.astype(BF16), w_proj_b.astype(BF16), seq)

    w_route = jnp.zeros((d, LANES), F32).at[:, :N_GROUPS].set(w_group).at[:, N_GROUPS:N_GROUPS + N_EXPERTS].set(w_router)
    b_route = jnp.zeros((1, LANES), F32).at[0, :N_GROUPS].set(b_group).at[0, N_GROUPS:N_GROUPS + N_EXPERTS].set(b_router)
    w_route_hi = w_route.astype(BF16)
    w_route2 = jnp.concatenate([w_route_hi, (w_route - w_route_hi.astype(F32)).astype(BF16)], axis=1)
    x1, h2, logits = _outproj(mixed, x2, w_out.astype(BF16), mod4, norm_ffn_g, w_route2, b_route, seq)

    posm, wts, counts = _route(logits)
    pos = posm[:, 0:2]

    tile = MOE_TILE
    i32 = jnp.int32
    cnt = counts[0, N_GROUPS:N_GROUPS + N_EXPERTS].astype(i32)
    padded = ((cnt + tile - 1) // tile) * tile
    ends = jnp.cumsum(padded)
    offs = ends - padded
    nt_max = (2 * t) // tile + N_EXPERTS
    n_tiles = ends[-1] // tile
    tile_idx = jnp.arange(nt_max, dtype=i32)
    tile_start = tile_idx * tile
    live = tile_idx < n_tiles
    texp = jnp.minimum(jnp.sum((ends[None, :] <= tile_start[:, None]).astype(i32), axis=1), N_EXPERTS - 1)
    active = cnt > 0
    order = jnp.cumsum(active.astype(i32)) - 1
    n_active = jnp.sum(active.astype(i32))
    experts = jnp.arange(N_EXPERTS, dtype=i32)
    by_order = jnp.sum(jnp.where(active[None, :] & (order[None, :] == experts[:, None]), experts[None, :], 0), axis=1)
    t_order = order[texp]
    first = (live & (tile_start == offs[texp])).astype(i32)
    ahead = t_order + 2
    prefetch = jnp.where(ahead < n_active, by_order[jnp.minimum(ahead, N_EXPERTS - 1)], -1)
    head = jnp.zeros((nt_max,), i32).at[0].set(n_tiles).at[1].set(by_order[0]).at[2].set(
        jnp.where(n_active > 1, by_order[1], -1))
    plan = jnp.stack([first, t_order % 2, prefetch, head]).astype(i32)

    tok = jnp.broadcast_to(jnp.arange(t, dtype=i32)[:, None], (t, 2))
    src = jnp.zeros((nt_max * tile,), i32).at[pos.reshape(-1)].set(tok.reshape(-1))

    ys = _moe(plan, src, h2, w_gate_e, w_up_e, w_down_e)

    return _combine(pos, ys, x1, mod4, wts, seq)


def kernel(x, c, positions, w_ada, b_ada, norm_mix_g, w_in, q_a_norm_g, w_uq, kv_a_norm_g, w_ukv, q_norm_g, k_norm_g, conv_w, conv_b, b_mlstm_gates, mlstm_norm_g, w_proj_a, w_proj_b, w_out, norm_ffn_g, w_group, b_group, w_router, b_router, w_gate_e, w_up_e, w_down_e):
    b, seq, d = x.shape
    depth = w_ada.shape[0]
    x2 = x.reshape(b * seq, d)
    pos2 = positions.reshape(b * seq, 1)
    c_pad = jnp.zeros((SUBLANES, d), F32).at[:b].set(c)
    for l in range(depth):
        mod = _adaln(c_pad, w_ada[l], b_ada[l])
        mod4 = mod[:b].reshape(b, 6, 1, d)
        x2 = _layer(x2, mod4, pos2, seq, w_in[l], q_a_norm_g[l], w_uq[l], kv_a_norm_g[l], w_ukv[l], q_norm_g[l],
                    k_norm_g[l], conv_w[l], conv_b[l], b_mlstm_gates[l], mlstm_norm_g[l], w_proj_a[l], w_proj_b[l],
                    w_out[l], norm_mix_g[l], norm_ffn_g[l], w_group[l], b_group[l], w_router[l], b_router[l],
                    w_gate_e[l], w_up_e[l], w_down_e[l])
    return x2.reshape(b, seq, d)
```

```python
import functools
import math

import jax
import jax.numpy as jnp
from jax import lax
from jax.experimental import pallas as pl
from jax.experimental.pallas import tpu as pltpu

F32 = jnp.float32
BF16 = jnp.bfloat16

LANES = 128
SUBLANES = 8

D_MODEL = 2048
MLA_HEADS = 8
MLA_NOPE = 128
MLA_ROPE = 64
MLA_QK = MLA_NOPE + MLA_ROPE
MLA_V = 128
Q_LORA = 512
KV_LORA = 256
ROPE_THETA = 10000.0
ML_HEADS = 8
ML_DQK = 128
ML_DV = 128
ML_CONV = 4
MLA_W = MLA_HEADS * MLA_V
ML_W = ML_HEADS * ML_DV
N_GROUPS = 4
EXP_PER_GROUP = 8
N_EXPERTS = N_GROUPS * EXP_PER_GROUP
D_FF_EXPERT = 512
EPS = 1e-6

QK_PAD = 2 * LANES

SRC_CQ = 0
SRC_KPE = Q_LORA + KV_LORA
SRC_QK = SRC_KPE + MLA_ROPE
SRC_V = SRC_QK + 2 * ML_HEADS * ML_DQK
SRC_O = SRC_V + ML_W
SRC_I = SRC_O + ML_W
SRC_GA = SRC_I + 2 * ML_HEADS
SRC_GB = SRC_GA + D_MODEL
GATE_LANE = SRC_I % LANES

IN_BLOCK = 1024
IN_BLOCK_SRC = (SRC_QK, SRC_QK + IN_BLOCK, SRC_GA, SRC_GA + IN_BLOCK, SRC_GB, SRC_GB + IN_BLOCK, SRC_V, SRC_O, SRC_CQ)
COL_QK = 0
COL_GA = 2048
COL_GB = 4096
COL_V = 6144
COL_O = 7168
COL_LAT = 8192
LAT_W = Q_LORA + KV_LORA + MLA_ROPE

MLSTM_CHUNK = 256
MOE_TILE = 256


def _cparams(sem, vmem_mb):
    return pltpu.CompilerParams(dimension_semantics=sem, vmem_limit_bytes=vmem_mb * 1024 * 1024)


def _dot(a, b):
    return jnp.dot(a, b, preferred_element_type=F32)


def _dot_nt(a, b):
    return lax.dot_general(a, b, (((1,), (1,)), ((), ())), preferred_element_type=F32)


def _dot_tn(a, b):
    return lax.dot_general(a, b, (((0,), (0,)), ((), ())), preferred_element_type=F32)


def _sigmoid(x):
    return 1.0 / (1.0 + jnp.exp(-x))


def _rms_scale(x, width):
    return lax.rsqrt(jnp.sum(x * x, axis=-1, keepdims=True) * (1.0 / width) + EPS)


def _adaln_kernel(c_ref, w_ref, b_ref, o_ref):
    c = c_ref[...]
    cond = (c * _sigmoid(c)).astype(BF16)
    o_ref[...] = _dot(cond, w_ref[...].astype(BF16)) + b_ref[...]


def _adaln(c_pad, w_ada, b_ada):
    rows, d = c_pad.shape
    n = w_ada.shape[1]
    tn = 1024
    return pl.pallas_call(
        _adaln_kernel,
        grid=(n // tn,),
        in_specs=[
            pl.BlockSpec((rows, d), lambda j: (0, 0)),
            pl.BlockSpec((d, tn), lambda j: (0, j)),
            pl.BlockSpec((1, tn), lambda j: (0, j)),
        ],
        out_specs=pl.BlockSpec((rows, tn), lambda j: (0, j)),
        out_shape=jax.ShapeDtypeStruct((rows, n), F32),
        compiler_params=_cparams(("arbitrary",), 40),
        name="adaln",
    )(c_pad, w_ada, b_ada.reshape(1, n))


def _inproj_kernel(off_ref, x_ref, sc_ref, sh_ref, g_ref, w_ref, wgate_ref, proj_ref, gates_ref, h_ref):
    j = pl.program_id(1)

    @pl.when(j == 0)
    def _():
        x = x_ref[...]
        h = x * _rms_scale(x, x.shape[-1]) * g_ref[...]
        h = (h * (1.0 + sc_ref[...]) + sh_ref[...]).astype(BF16)
        h_ref[...] = h
        gates_ref[...] = _dot_nt(h, wgate_ref[...].astype(BF16))

    proj_ref[...] = _dot_nt(h_ref[...], w_ref[...].astype(BF16)).astype(proj_ref.dtype)


def _inproj(x2, mod4, norm_g, w_in_t, seq):
    t, d = x2.shape
    tm = min(1024, seq)
    tn = IN_BLOCK
    bpr = seq // tm
    assert all(o % SUBLANES == 0 for o in IN_BLOCK_SRC)
    offs = jnp.asarray([o // SUBLANES for o in IN_BLOCK_SRC], jnp.int32)
    nblk = len(IN_BLOCK_SRC)
    gate_tile = SRC_I // LANES
    grid_spec = pltpu.PrefetchScalarGridSpec(
        num_scalar_prefetch=1,
        grid=(t // tm, nblk),
        in_specs=[
            pl.BlockSpec((tm, d), lambda i, j, o: (i, 0)),
            pl.BlockSpec((None, None, 1, d), lambda i, j, o: (i // bpr, 1, 0, 0)),
            pl.BlockSpec((None, None, 1, d), lambda i, j, o: (i // bpr, 0, 0, 0)),
            pl.BlockSpec((1, d), lambda i, j, o: (0, 0)),
            pl.BlockSpec((pl.Element(tn), pl.Element(d)), lambda i, j, o: (o[j] * SUBLANES, 0)),
            pl.BlockSpec((LANES, d), lambda i, j, o: (gate_tile, 0)),
        ],
        out_specs=[
            pl.BlockSpec((tm, tn), lambda i, j, o: (i, j)),
            pl.BlockSpec((tm, LANES), lambda i, j, o: (i, 0)),
        ],
        scratch_shapes=[pltpu.VMEM((tm, d), BF16)],
    )
    return pl.pallas_call(
        _inproj_kernel,
        grid_spec=grid_spec,
        out_shape=[
            jax.ShapeDtypeStruct((t, nblk * tn), BF16),
            jax.ShapeDtypeStruct((t, LANES), F32),
        ],
        compiler_params=_cparams(("arbitrary", "arbitrary"), 56),
        name="inproj",
    )(offs, x2, mod4, mod4, norm_g.reshape(1, d), w_in_t, w_in_t)


def _mla_prep_kernel(lat_ref, pos_ref, wuq_ref, wukv_ref, gqa_ref, gkva_ref, gq_ref, gk_ref, inv_ref,
                     qt_ref, k_ref, vt_ref):
    lat = lat_ref[...].astype(F32)
    cq = lat[:, :Q_LORA]
    ckv = lat[:, Q_LORA:Q_LORA + KV_LORA]
    kc = lat[:, SRC_KPE:SRC_KPE + LANES]
    cqn = (cq * _rms_scale(cq, Q_LORA) * gqa_ref[...]).astype(BF16)
    ckvn = (ckv * _rms_scale(ckv, KV_LORA) * gkva_ref[...]).astype(BF16)
    qraw = _dot(cqn, wuq_ref[...])
    kv = _dot(ckvn, wukv_ref[...])

    ang = pos_ref[...].astype(F32) * inv_ref[...]
    lane = lax.broadcasted_iota(jnp.int32, ang.shape, 1)
    lo = lane < MLA_ROPE
    cs = jnp.where(lo, jnp.cos(ang), jnp.sin(ang))

    quarter = MLA_ROPE // 2
    want = jnp.where(lane < 3 * quarter, lane - quarter, lane - 3 * quarter)
    came = pltpu.roll(lane, quarter, 1)
    swapped = jnp.where(came == want, pltpu.roll(kc, quarter, 1), pltpu.roll(kc, 3 * quarter, 1))
    kc = jnp.where(lo, kc, swapped)

    gq = gq_ref[...]
    gk = gk_ref[...]
    gq_n, gq_r = gq[:, :LANES], gq[:, LANES:]
    gk_n, gk_r = gk[:, :LANES], gk[:, LANES:]

    def rope(chunk, g_r):
        a = chunk * (g_r * cs)
        return jnp.where(lo, a + pltpu.roll(a, MLA_ROPE, 1), 0.0)

    kpe_ss = jnp.sum(jnp.where(lo, kc * kc, 0.0), axis=-1, keepdims=True)
    k_rope = rope(kc, gk_r)
    scale = MLA_QK ** -0.5
    for h in range(MLA_HEADS):
        kn = kv[:, h * MLA_NOPE:(h + 1) * MLA_NOPE]
        sk = lax.rsqrt((jnp.sum(kn * kn, axis=-1, keepdims=True) + kpe_ss) * (1.0 / MLA_QK) + EPS)
        k_ref[:, h * QK_PAD:h * QK_PAD + LANES] = (kn * sk * gk_n).astype(BF16)
        k_ref[:, h * QK_PAD + LANES:(h + 1) * QK_PAD] = (k_rope * sk).astype(BF16)
        qn = qraw[:, h * QK_PAD:h * QK_PAD + LANES]
        qr = qraw[:, h * QK_PAD + LANES:(h + 1) * QK_PAD]
        ss = jnp.sum(qn * qn, axis=-1, keepdims=True) + jnp.sum(jnp.where(lo, qr * qr, 0.0), axis=-1, keepdims=True)
        sq = lax.rsqrt(ss * (1.0 / MLA_QK) + EPS) * scale
        qt_ref[h * QK_PAD:h * QK_PAD + LANES, :] = (qn * sq * gq_n).T.astype(BF16)
        qt_ref[h * QK_PAD + LANES:(h + 1) * QK_PAD, :] = (rope(qr, gq_r) * sq).T.astype(BF16)
        vh = kv[:, MLA_HEADS * MLA_NOPE + h * MLA_V:MLA_HEADS * MLA_NOPE + (h + 1) * MLA_V]
        vt_ref[h * MLA_V:(h + 1) * MLA_V, :] = vh.T.astype(BF16)


def _mla_prep(proj, pos2, wuq_p, wukv_p, gqa, gkva, gq, gk, inv_lanes, seq):
    t = proj.shape[0]
    tm = min(512, seq)
    hq = MLA_HEADS * QK_PAD
    lat_blk = COL_LAT // 1024
    const = lambda i: (0, 0)
    return pl.pallas_call(
        _mla_prep_kernel,
        grid=(t // tm,),
        in_specs=[
            pl.BlockSpec((tm, 1024), lambda i: (i, lat_blk)),
            pl.BlockSpec((tm, 1), lambda i: (i, 0)),
            pl.BlockSpec(wuq_p.shape, const),
            pl.BlockSpec(wukv_p.shape, const),
            pl.BlockSpec(gqa.shape, const),
            pl.BlockSpec(gkva.shape, const),
            pl.BlockSpec(gq.shape, const),
            pl.BlockSpec(gk.shape, const),
            pl.BlockSpec(inv_lanes.shape, const),
        ],
        out_specs=[
            pl.BlockSpec((hq, tm), lambda i: (0, i)),
            pl.BlockSpec((tm, hq), lambda i: (i, 0)),
            pl.BlockSpec((MLA_W, tm), lambda i: (0, i)),
        ],
        out_shape=[
            jax.ShapeDtypeStruct((hq, t), BF16),
            jax.ShapeDtypeStruct((t, hq), BF16),
            jax.ShapeDtypeStruct((MLA_W, t), BF16),
        ],
        compiler_params=_cparams(("arbitrary",), 48),
        name="mla_prep",
    )(proj, pos2, wuq_p, wukv_p, gqa, gkva, gq, gk, inv_lanes)


def _flash_kernel(qt_ref, k_ref, vt_ref, o_ref, *, tq, tk):
    seq = k_ref.shape[0]
    for qi in range(seq // tq):
        qt = qt_ref[:, qi * tq:(qi + 1) * tq]
        m = jnp.full((1, tq), -jnp.inf, F32)
        l = jnp.zeros((1, tq), F32)
        acc = jnp.zeros((MLA_V, tq), F32)
        n_blocks = ((qi + 1) * tq) // tk
        for j in range(n_blocks):
            st = _dot(k_ref[j * tk:(j + 1) * tk, :], qt)
            if (j + 1) * tk - 1 > qi * tq:
                key = lax.broadcasted_iota(jnp.int32, st.shape, 0) + j * tk
                qry = lax.broadcasted_iota(jnp.int32, st.shape, 1) + qi * tq
                st = jnp.where(key <= qry, st, -jnp.inf)
            m_new = jnp.maximum(m, jnp.max(st, axis=0, keepdims=True))
            alpha = jnp.exp(m - m_new)
            p = jnp.exp(st - m_new)
            l = alpha * l + jnp.sum(p, axis=0, keepdims=True)
            acc = alpha * acc + _dot(vt_ref[:, j * tk:(j + 1) * tk], p.astype(BF16))
            m = m_new
        o_ref[qi * tq:(qi + 1) * tq, :] = (acc / l).T.astype(o_ref.dtype)


def _flash(qt, k3, vt):
    b, seq, _ = k3.shape
    tq = min(512, seq)
    tk = min(512, seq)
    kern = functools.partial(_flash_kernel, tq=tq, tk=tk)
    return pl.pallas_call(
        kern,
        grid=(b, MLA_HEADS),
        in_specs=[
            pl.BlockSpec((QK_PAD, seq), lambda i, h: (h, i)),
            pl.BlockSpec((None, seq, QK_PAD), lambda i, h: (i, 0, h)),
            pl.BlockSpec((MLA_V, seq), lambda i, h: (h, i)),
        ],
        out_specs=pl.BlockSpec((None, seq, MLA_V), lambda i, h: (i, 0, h)),
        out_shape=jax.ShapeDtypeStruct((b, seq, MLA_W), BF16),
        compiler_params=_cparams(("arbitrary", "arbitrary"), 40),
        name="flash",
    )(qt, k3, vt)


def _conv_kernel(cur_ref, halo_ref, w_ref, b_ref, qt_ref, k_ref, buf_ref, *, k_scale):
    tm = cur_ref.shape[0]
    cols = cur_ref.shape[1]
    half = cols // 2
    first = pl.program_id(1) == 0
    halo = halo_ref[...].astype(F32)
    buf_ref[0:SUBLANES, :] = jnp.where(first, 0.0, halo)
    buf_ref[SUBLANES:SUBLANES + tm, :] = cur_ref[...].astype(F32)
    cw = 512
    for c in range(cols // cw):
        sl = slice(c * cw, (c + 1) * cw)
        acc = jnp.zeros((tm, cw), F32) + b_ref[:, sl]
        for j in range(ML_CONV):
            off = SUBLANES - (ML_CONV - 1) + j
            acc = acc + buf_ref[off:off + tm, sl] * w_ref[j:j + 1, sl]
        y = acc * _sigmoid(acc)
        if c * cw < half:
            qt_ref[sl, :] = y.T.astype(qt_ref.dtype)
        else:
            k_ref[:, c * cw - half:(c + 1) * cw - half] = (y * k_scale).astype(k_ref.dtype)


def _conv_silu(proj3, conv_w, conv_b):
    b, seq, _ = proj3.shape
    cols = 2 * ML_HEADS * ML_DQK
    half = cols // 2
    tm = min(512, seq)
    hb = tm // SUBLANES
    kern = functools.partial(_conv_kernel, k_scale=ML_DQK ** -0.5)
    return pl.pallas_call(
        kern,
        grid=(b, seq // tm),
        in_specs=[
            pl.BlockSpec((None, tm, cols), lambda i, s: (i, s, COL_QK // cols)),
            pl.BlockSpec((None, SUBLANES, cols), lambda i, s: (i, jnp.maximum(s * hb - 1, 0), COL_QK // cols)),
            pl.BlockSpec((ML_CONV, cols), lambda i, s: (0, 0)),
            pl.BlockSpec((1, cols), lambda i, s: (0, 0)),
        ],
        out_specs=[
            pl.BlockSpec((None, half, tm), lambda i, s: (i, 0, s)),
            pl.BlockSpec((None, tm, half), lambda i, s: (i, s, 0)),
        ],
        out_shape=[
            jax.ShapeDtypeStruct((b, half, seq), BF16),
            jax.ShapeDtypeStruct((b, seq, half), BF16),
        ],
        scratch_shapes=[pltpu.VMEM((tm + SUBLANES, cols), F32)],
        compiler_params=_cparams(("arbitrary", "arbitrary"), 40),
        name="conv_silu",
    )(proj3, proj3, conv_w, conv_b.reshape(1, cols))


def _log_sigmoid(x):
    return -(jnp.maximum(-x, 0.0) + jnp.log1p(jnp.exp(-jnp.abs(x))))


def _mlstm_kernel(qt_ref, k_ref, v_ref, o_ref, gates_ref, gbias_ref, ng_ref, out_ref, ct_ref, m_ref):
    L = k_ref.shape[0]

    @pl.when(pl.program_id(1) == 0)
    def _():
        ct_ref[...] = jnp.zeros(ct_ref.shape, F32)
        m_ref[...] = jnp.zeros(m_ref.shape, F32)

    g = gates_ref[...] + gbias_ref[...]
    gt = g.T
    lf = _log_sigmoid(g)
    lft = _log_sigmoid(gt)
    r = lax.broadcasted_iota(jnp.int32, (L, L), 0)
    c = lax.broadcasted_iota(jnp.int32, (L, L), 1)
    src_le_qry = r <= c
    tril = (c <= r).astype(F32)
    triu = src_le_qry.astype(F32)
    hi = lax.Precision.HIGHEST
    bcol_all = jnp.dot(tril, lf, preferred_element_type=F32, precision=hi)
    brow_all = jnp.dot(lft, triu, preferred_element_type=F32, precision=hi)
    row = lax.broadcasted_iota(jnp.int32, (ML_DV, L), 0)
    ones_row = jnp.where(row == 0, 1.0, 0.0)

    for h in range(ML_HEADS):
        hs = slice(h * ML_DQK, (h + 1) * ML_DQK)
        li, lf_ = GATE_LANE + h, GATE_LANE + ML_HEADS + h
        b_row = brow_all[lf_:lf_ + 1, :]
        i_row = gt[li:li + 1, :]
        u_col = g[:, li:li + 1] - bcol_all[:, lf_:lf_ + 1]
        m_prev = m_ref[h][:, :1]
        logw_t = jnp.where(src_le_qry, b_row + u_col, -jnp.inf)
        log_inter = b_row + m_prev
        m_t = jnp.maximum(jnp.max(logw_t, axis=0, keepdims=True), log_inter)
        w_t = jnp.exp(logw_t - m_t)
        a = jnp.exp(log_inter - m_t)
        qt = qt_ref[hs, :]
        kh = k_ref[:, hs]
        vt_aug = jnp.concatenate([v_ref[:, hs].astype(F32).T, ones_row], axis=0)
        s_t = _dot(kh, qt) * w_t
        ct = ct_ref[h]
        nd = _dot(vt_aug.astype(BF16), s_t.astype(BF16)) + a * _dot(ct.astype(BF16), qt)
        num = nd[:ML_DV, :]
        den = nd[ML_DV:ML_DV + 1, :]
        hout_t = num * (1.0 / jnp.maximum(jnp.abs(den), jnp.exp(-m_t)))
        hn_t = hout_t * lax.rsqrt(jnp.sum(hout_t * hout_t, axis=0, keepdims=True) * (1.0 / ML_DV) + EPS)
        gate = _sigmoid(o_ref[:, hs].astype(F32))
        out_ref[:, hs] = (hn_t.T * ng_ref[:, hs] * gate).astype(out_ref.dtype)

        b_last = b_row[:, L - 1:L]
        logg = b_last - b_row + i_row
        m_new = jnp.maximum(b_last + m_prev, jnp.max(logg, axis=-1, keepdims=True))
        g_row = jnp.exp(logg - m_new)
        decay = jnp.exp(b_last + m_prev - m_new)
        ct_ref[h] = decay * ct + _dot((vt_aug * g_row).astype(BF16), kh)
        m_ref[h] = jnp.broadcast_to(m_new, m_ref.shape[1:])


def _mlstm(qt3, k3, proj3, gates3, gbias, norm_g):
    b, seq, w = k3.shape
    L = min(MLSTM_CHUNK, seq)
    return pl.pallas_call(
        _mlstm_kernel,
        grid=(b, seq // L),
        in_specs=[
            pl.BlockSpec((None, w, L), lambda i, c: (i, 0, c)),
            pl.BlockSpec((None, L, w), lambda i, c: (i, c, 0)),
            pl.BlockSpec((None, L, w), lambda i, c: (i, c, COL_V // w)),
            pl.BlockSpec((None, L, w), lambda i, c: (i, c, COL_O // w)),
            pl.BlockSpec((None, L, LANES), lambda i, c: (i, c, 0)),
            pl.BlockSpec((1, LANES), lambda i, c: (0, 0)),
            pl.BlockSpec((1, w), lambda i, c: (0, 0)),
        ],
        out_specs=pl.BlockSpec((None, L, w), lambda i, c: (i, c, 0)),
        out_shape=jax.ShapeDtypeStruct((b, seq, w), BF16),
        scratch_shapes=[
            pltpu.VMEM((ML_HEADS, 2 * ML_DV, ML_DQK), F32),
            pltpu.VMEM((ML_HEADS, 1, LANES), F32),
        ],
        compiler_params=_cparams(("arbitrary", "arbitrary"), 40),
        name="mlstm",
    )(qt3, k3, proj3, proj3, gates3, gbias, norm_g.reshape(1, w))


def _merge_kernel(a_ref, b_ref, ga_ref, gb_ref, wa_ref, wb_ref, o_ref):
    pa = _dot(a_ref[...], wa_ref[...])
    pb = _dot(b_ref[...], wb_ref[...])
    mixed = _sigmoid(ga_ref[...].astype(F32)) * pa + _sigmoid(gb_ref[...].astype(F32)) * pb
    o_ref[...] = mixed.astype(o_ref.dtype)


def _merge(out_a, hm, proj, wa, wb, seq):
    t = out_a.shape[0]
    d = wa.shape[1]
    tm = min(1024, seq)
    tn = 1024
    return pl.pallas_call(
        _merge_kernel,
        grid=(d // tn, t // tm),
        in_specs=[
            pl.BlockSpec((tm, MLA_W), lambda j, i: (i, 0)),
            pl.BlockSpec((tm, ML_W), lambda j, i: (i, 0)),
            pl.BlockSpec((tm, tn), lambda j, i: (i, COL_GA // tn + j)),
            pl.BlockSpec((tm, tn), lambda j, i: (i, COL_GB // tn + j)),
            pl.BlockSpec((MLA_W, tn), lambda j, i: (0, j)),
            pl.BlockSpec((ML_W, tn), lambda j, i: (0, j)),
        ],
        out_specs=pl.BlockSpec((tm, tn), lambda j, i: (i, j)),
        out_shape=jax.ShapeDtypeStruct((t, d), BF16),
        compiler_params=_cparams(("arbitrary", "arbitrary"), 40),
        name="merge",
    )(out_a, hm, proj, proj, wa, wb)


def _outproj_kernel(mix_ref, x_ref, w_ref, gt_ref, sc_ref, sh_ref, g_ref, wr_ref, br_ref, x1_ref, h2_ref, lg_ref):
    y = _dot(mix_ref[...], w_ref[...])
    x1 = x_ref[...] + gt_ref[...] * y
    x1_ref[...] = x1
    h2 = x1 * _rms_scale(x1, x1.shape[-1]) * g_ref[...]
    h2 = h2 * (1.0 + sc_ref[...]) + sh_ref[...]
    h2_ref[...] = h2
    h_hi = h2.astype(BF16)
    h_lo = (h2 - h_hi.astype(F32)).astype(BF16)
    r = _dot(h_hi, wr_ref[...]) + _dot(h_lo, wr_ref[...])
    lg_ref[...] = r[:, :LANES] + r[:, LANES:] + br_ref[...]


def _outproj(mixed, x2, w_out, mod4, norm_g, w_route2, b_route, seq):
    t, d = x2.shape
    tm = min(512, seq)
    bpr = seq // tm
    mod_spec = lambda k: pl.BlockSpec((None, None, 1, d), lambda i: (i // bpr, k, 0, 0))
    const = lambda i: (0, 0)
    return pl.pallas_call(
        _outproj_kernel,
        grid=(t // tm,),
        in_specs=[
            pl.BlockSpec((tm, d), lambda i: (i, 0)),
            pl.BlockSpec((tm, d), lambda i: (i, 0)),
            pl.BlockSpec((d, d), const, pipeline_mode=pl.Buffered(1)),
            mod_spec(2),
            mod_spec(4),
            mod_spec(3),
            pl.BlockSpec((1, d), const),
            pl.BlockSpec((d, 2 * LANES), const, pipeline_mode=pl.Buffered(1)),
            pl.BlockSpec((1, LANES), const),
        ],
        out_specs=[
            pl.BlockSpec((tm, d), lambda i: (i, 0)),
            pl.BlockSpec((tm, d), lambda i: (i, 0)),
            pl.BlockSpec((tm, LANES), lambda i: (i, 0)),
        ],
        out_shape=[
            jax.ShapeDtypeStruct((t, d), F32),
            jax.ShapeDtypeStruct((t, d), F32),
            jax.ShapeDtypeStruct((t, LANES), F32),
        ],
        compiler_params=_cparams(("arbitrary",), 56),
        name="outproj",
    )(mixed, x2, w_out, mod4, mod4, mod4, norm_g.reshape(1, d), w_route2, b_route)


def _route_kernel(lg_ref, pos_ref, wts_ref, cnt_ref, carry_ref, offs_ref, meta_s, wts_s):
    phase = pl.program_id(0)
    i = pl.program_id(1)
    tm = lg_ref.shape[0]
    lane = lax.broadcasted_iota(jnp.int32, (tm, LANES), 1)

    @pl.when((phase == 0) & (i == 0))
    def _():
        carry_ref[...] = jnp.zeros(carry_ref.shape, F32)

    @pl.when(phase == 0)
    def _():
        lg = lg_ref[...]
        big = jnp.int32(LANES)
        ninf = -jnp.inf

        def first_argmax(vals):
            mx = jnp.max(vals, axis=-1, keepdims=True)
            idx = jnp.min(jnp.where(vals == mx, lane, big), axis=-1, keepdims=True)
            return mx, idx

        gl = jnp.where(lane < N_GROUPS, lg, ninf)
        gmax, gsel = first_argmax(gl)
        g_w = 1.0 / jnp.sum(jnp.exp(gl - gmax), axis=-1, keepdims=True)
        lo = N_GROUPS + gsel * EXP_PER_GROUP
        in_grp = (lane >= lo) & (lane < lo + EXP_PER_GROUP)
        el = jnp.where(in_grp, lg, ninf)
        e1, i1 = first_argmax(el)
        e2, i2 = first_argmax(jnp.where(lane == i1, ninf, el))
        p2 = jnp.exp(e2 - e1)
        w1 = g_w / (1.0 + p2)
        w2 = g_w * p2 / (1.0 + p2)

        oh1 = lane == i1
        oh2 = lane == i2
        oh = jnp.where(oh1 | oh2, 1.0, 0.0)
        r = lax.broadcasted_iota(jnp.int32, (tm, tm), 0)
        c = lax.broadcasted_iota(jnp.int32, (tm, tm), 1)
        strict = jnp.where(c < r, 1.0, 0.0).astype(BF16)
        before = _dot(strict, oh.astype(BF16)) + carry_ref[...]
        rank1 = jnp.sum(jnp.where(oh1, before, 0.0), axis=-1, keepdims=True).astype(jnp.int32)
        rank2 = jnp.sum(jnp.where(oh2, before, 0.0), axis=-1, keepdims=True).astype(jnp.int32)
        carry_ref[...] = carry_ref[...] + jnp.sum(oh, axis=0, keepdims=True)
        meta_s[i] = jnp.where(lane == 0, i1, jnp.where(lane == 1, i2, jnp.where(lane == 2, rank1, jnp.where(lane == 3, rank2, 0))))
        wts_s[i] = jnp.where(lane == 0, w1, jnp.where(lane == 1, w2, 0.0))

    @pl.when((phase == 1) & (i == 0))
    def _():
        cnt = carry_ref[...]
        cnt_ref[...] = jnp.broadcast_to(cnt, cnt_ref.shape)
        padded = jnp.ceil(cnt * (1.0 / MOE_TILE)) * MOE_TILE
        r = lax.broadcasted_iota(jnp.int32, (LANES, LANES), 0)
        c = lax.broadcasted_iota(jnp.int32, (LANES, LANES), 1)
        upper = jnp.where(r < c, 1.0, 0.0).astype(BF16)
        padded8 = jnp.broadcast_to(padded, (SUBLANES, LANES)).astype(BF16)
        offs_ref[...] = _dot(padded8, upper)[:1, :]

    @pl.when(phase == 1)
    def _():
        meta = meta_s[i]
        offs = offs_ref[...]
        off1 = jnp.sum(jnp.where(lane == meta[:, 0:1], offs, 0.0), axis=-1, keepdims=True).astype(jnp.int32)
        off2 = jnp.sum(jnp.where(lane == meta[:, 1:2], offs, 0.0), axis=-1, keepdims=True).astype(jnp.int32)
        pos1 = off1 + meta[:, 2:3]
        pos2 = off2 + meta[:, 3:4]
        pos_ref[...] = jnp.where(lane == 0, pos1, jnp.where(lane == 1, pos2, 0))
        wts_ref[...] = wts_s[i]


def _route(logits):
    t = logits.shape[0]
    tm = min(512, t)
    nb = t // tm
    return pl.pallas_call(
        _route_kernel,
        grid=(2, nb),
        in_specs=[pl.BlockSpec((tm, LANES), lambda p, i: (i * (1 - p), 0))],
        out_specs=[
            pl.BlockSpec((tm, LANES), lambda p, i: (i * p, 0)),
            pl.BlockSpec((tm, LANES), lambda p, i: (i * p, 0)),
            pl.BlockSpec((SUBLANES, LANES), lambda p, i: (0, 0)),
        ],
        out_shape=[
            jax.ShapeDtypeStruct((t, LANES), jnp.int32),
            jax.ShapeDtypeStruct((t, LANES), F32),
            jax.ShapeDtypeStruct((SUBLANES, LANES), F32),
        ],
        scratch_shapes=[
            pltpu.VMEM((1, LANES), F32),
            pltpu.VMEM((1, LANES), F32),
            pltpu.VMEM((nb, tm, LANES), jnp.int32),
            pltpu.VMEM((nb, tm, LANES), F32),
        ],
        compiler_params=_cparams(("arbitrary", "arbitrary"), 32),
        name="route",
    )(logits)


def _row_gather_start(idx_at, src_hbm, dst, sem, rows):
    for r in range(rows):
        pltpu.make_async_copy(src_hbm.at[pl.ds(idx_at(r), 1)], dst.at[pl.ds(r, 1)], sem).start()


def _row_gather_wait(src_hbm, dst, sem, rows):
    pltpu.make_async_copy(src_hbm.at[pl.ds(0, rows)], dst, sem).wait()


def _dispatch_kernel(pos_ref, zrow_ref, h_ref, xs_hbm, zbuf, sems):
    i = pl.program_id(0)
    tm = h_ref.shape[0]
    tile = zbuf.shape[0]

    def zero_copy(z):
        return pltpu.make_async_copy(zbuf, xs_hbm.at[pl.ds(pl.multiple_of(zrow_ref[z], tile), tile)], sems.at[0])

    @pl.when(i == 0)
    def _():
        zbuf[...] = jnp.zeros(zbuf.shape, zbuf.dtype)
        for z in range(zrow_ref.shape[0]):
            @pl.when(zrow_ref[z] >= 0)
            def _():
                zero_copy(z).start()
        for z in range(zrow_ref.shape[0]):
            @pl.when(zrow_ref[z] >= 0)
            def _():
                zero_copy(z).wait()

    for k in range(2):
        base = (2 * i + k) * tm
        for r in range(tm):
            pltpu.make_async_copy(h_ref.at[pl.ds(r, 1)], xs_hbm.at[pl.ds(pos_ref[base + r], 1)], sems.at[1]).start()
    for k in range(2):
        pltpu.make_async_copy(h_ref, xs_hbm.at[pl.ds(0, tm)], sems.at[1]).wait()


def _dispatch(pos, zrow, h2, n_rows, seq):
    t, d = h2.shape
    tm = min(512, seq)
    pos_blocks = pos.reshape(t // tm, tm, 2).transpose(0, 2, 1).reshape(-1)
    grid_spec = pltpu.PrefetchScalarGridSpec(
        num_scalar_prefetch=2,
        grid=(t // tm,),
        in_specs=[pl.BlockSpec((tm, d), lambda i, p, z: (i, 0))],
        out_specs=pl.BlockSpec(memory_space=pl.ANY),
        scratch_shapes=[
            pltpu.VMEM((MOE_TILE, d), F32),
            pltpu.SemaphoreType.DMA((2,)),
        ],
    )
    return pl.pallas_call(
        _dispatch_kernel,
        grid_spec=grid_spec,
        out_shape=jax.ShapeDtypeStruct((n_rows, d), F32),
        compiler_params=_cparams(("arbitrary",), 40),
        name="moe_dispatch",
    )(pos_blocks, zrow, h2)


def _moe_kernel(plan_ref, x_ref, wg_hbm, wu_hbm, wd_hbm, y_ref, stg_g, stg_u, stg_d, wsem, wb_g, wb_u, wb_d):
    j = pl.program_id(0)
    nt = plan_ref[3, 0]

    def w_copies(e, s):
        return (pltpu.make_async_copy(wg_hbm.at[e], stg_g.at[s], wsem.at[s]),
                pltpu.make_async_copy(wu_hbm.at[e], stg_u.at[s], wsem.at[s]),
                pltpu.make_async_copy(wd_hbm.at[e], stg_d.at[s], wsem.at[s]))

    @pl.when(j == 0)
    def _():
        for cp in w_copies(plan_ref[3, 1], 0):
            cp.start()

        @pl.when(plan_ref[3, 2] >= 0)
        def _():
            for cp in w_copies(plan_ref[3, 2], 1):
                cp.start()

    @pl.when((j < nt) & (plan_ref[0, j] == 1))
    def _():
        s = plan_ref[1, j]
        for cp in w_copies(0, s):
            cp.wait()
        wb_g[...] = stg_g[s].astype(BF16)
        wb_u[...] = stg_u[s].astype(BF16)
        wb_d[...] = stg_d[s].astype(BF16)

        @pl.when(plan_ref[2, j] >= 0)
        def _():
            for cp in w_copies(plan_ref[2, j], s):
                cp.start()

    @pl.when(j < nt)
    def _():
        x = x_ref[...].astype(BF16)
        hg = _dot(x, wb_g[...])
        hu = _dot(x, wb_u[...])
        act = (hg * _sigmoid(hg) * hu).astype(BF16)
        y_ref[...] = _dot(act, wb_d[...])

    @pl.when(j >= nt)
    def _():
        y_ref[...] = jnp.zeros(y_ref.shape, y_ref.dtype)


def _moe(plan, xs, w_gate_e, w_up_e, w_down_e):
    d = xs.shape[1]
    f = w_gate_e.shape[-1]
    nt_max = plan.shape[1]
    tile = MOE_TILE
    any_spec = pl.BlockSpec(memory_space=pl.ANY)
    grid_spec = pltpu.PrefetchScalarGridSpec(
        num_scalar_prefetch=1,
        grid=(nt_max,),
        in_specs=[
            pl.BlockSpec((tile, d), lambda j, plan_: (jnp.minimum(j, plan_[3, 0] - 1), 0)),
            any_spec, any_spec, any_spec,
        ],
        out_specs=pl.BlockSpec((tile, d), lambda j, plan_: (j, 0)),
        scratch_shapes=[
            pltpu.VMEM((2, d, f), F32),
            pltpu.VMEM((2, d, f), F32),
            pltpu.VMEM((2, f, d), F32),
            pltpu.SemaphoreType.DMA((2,)),
            pltpu.VMEM((d, f), BF16),
            pltpu.VMEM((d, f), BF16),
            pltpu.VMEM((f, d), BF16),
        ],
    )
    return pl.pallas_call(
        _moe_kernel,
        grid_spec=grid_spec,
        out_shape=jax.ShapeDtypeStruct((nt_max * tile, d), F32),
        compiler_params=_cparams(("arbitrary",), 56),
        name="moe_experts",
    )(plan, xs, w_gate_e, w_up_e, w_down_e)


def _combine_kernel(pos_ref, y_hbm, x1_ref, gt_ref, wts_ref, o_ref, ybuf, sem):
    i = pl.program_id(0)
    n = pl.num_programs(0)
    tm = x1_ref.shape[0]
    slot = lax.rem(i, 2)

    def start(blk, s):
        for k in range(2):
            base = (2 * blk + k) * tm
            _row_gather_start(lambda r, base=base: pos_ref[base + r], y_hbm, ybuf.at[s, k], sem.at[s], tm)

    @pl.when(i == 0)
    def _():
        start(0, 0)

        @pl.when(n > 1)
        def _():
            start(1, 1)

    for k in range(2):
        _row_gather_wait(y_hbm, ybuf.at[slot, k], sem.at[slot], tm)
    w = wts_ref[...]
    moe = w[:, 0:1] * ybuf[slot, 0] + w[:, 1:2] * ybuf[slot, 1]
    o_ref[...] = x1_ref[...] + gt_ref[...] * moe

    @pl.when(i + 2 < n)
    def _():
        start(i + 2, slot)


def _combine(pos, ys, x1, mod4, wts, seq):
    t, d = x1.shape
    tm = min(256, seq)
    bpr = seq // tm
    pos_blocks = pos.reshape(t // tm, tm, 2).transpose(0, 2, 1).reshape(-1)
    grid_spec = pltpu.PrefetchScalarGridSpec(
        num_scalar_prefetch=1,
        grid=(t // tm,),
        in_specs=[
            pl.BlockSpec(memory_space=pl.ANY),
            pl.BlockSpec((tm, d), lambda i, p: (i, 0)),
            pl.BlockSpec((None, None, 1, d), lambda i, p: (i // bpr, 5, 0, 0)),
            pl.BlockSpec((tm, LANES), lambda i, p: (i, 0)),
        ],
        out_specs=pl.BlockSpec((tm, d), lambda i, p: (i, 0)),
        scratch_shapes=[
            pltpu.VMEM((2, 2, tm, d), F32),
            pltpu.SemaphoreType.DMA((2,)),
        ],
    )
    return pl.pallas_call(
        _combine_kernel,
        grid_spec=grid_spec,
        out_shape=jax.ShapeDtypeStruct((t, d), F32),
        compiler_params=_cparams(("arbitrary",), 48),
        name="moe_combine",
    )(pos_blocks, ys, x1, mod4, wts)


def _swap_halves(w):
    half = w.shape[-1] // 2
    return jnp.concatenate([w[..., half:], w[..., :half]], axis=-1)


def _layout_w_uq(w_uq):
    r = w_uq.shape[0]
    w = w_uq.reshape(r, MLA_HEADS, MLA_QK)
    rope = w[..., MLA_NOPE:]
    return jnp.concatenate([w[..., :MLA_NOPE], rope, _swap_halves(rope)], axis=-1).reshape(r, MLA_HEADS * QK_PAD).astype(BF16)


def _layout_w_ukv(w_ukv):
    r = w_ukv.shape[0]
    w = w_ukv.reshape(r, MLA_HEADS, MLA_NOPE + MLA_V)
    return jnp.concatenate([w[..., :MLA_NOPE].reshape(r, -1), w[..., MLA_NOPE:].reshape(r, -1)], axis=-1).astype(BF16)


def _rope_gain(g):
    g1 = g[MLA_NOPE:MLA_NOPE + MLA_ROPE // 2]
    g2 = g[MLA_NOPE + MLA_ROPE // 2:]
    return jnp.concatenate([g[:MLA_NOPE], g1, g2, -g2, g1]).reshape(1, QK_PAD)


def _layer(x2, cond_mod4, pos2, seq, w_in, q_a_norm_g, w_uq, kv_a_norm_g, w_ukv, q_norm_g, k_norm_g, conv_w, conv_b,
           b_mlstm_gates, mlstm_norm_g, w_proj_a, w_proj_b, w_out, norm_mix_g, norm_ffn_g, w_group, b_group,
           w_router, b_router, w_gate_e, w_up_e, w_down_e):
    t, d = x2.shape
    b = t // seq
    mod4 = cond_mod4

    proj, gates = _inproj(x2, mod4, norm_mix_g, w_in.T, seq)

    inv = ROPE_THETA ** (-jnp.arange(0, MLA_ROPE, 2, dtype=F32) / MLA_ROPE)
    inv_lanes = jnp.tile(inv, LANES // (MLA_ROPE // 2)).reshape(1, LANES)
    qt, k, vt = _mla_prep(proj, pos2, _layout_w_uq(w_uq), _layout_w_ukv(w_ukv), q_a_norm_g.reshape(1, -1),
                        kv_a_norm_g.reshape(1, -1), _rope_gain(q_norm_g), _rope_gain(k_norm_g), inv_lanes, seq)
    out_a = _flash(qt, k.reshape(b, seq, -1), vt).reshape(t, MLA_W)

    proj3 = proj.reshape(b, seq, -1)
    ml_qt, ml_k = _conv_silu(proj3, conv_w, conv_b)
    gbias = jnp.zeros((1, LANES), F32).at[0, GATE_LANE:GATE_LANE + 2 * ML_HEADS].set(b_mlstm_gates.reshape(-1))
    hm = _mlstm(ml_qt, ml_k, proj3, gates.reshape(b, seq, LANES), gbias, mlstm_norm_g).reshape(t, ML_W)

    mixed = _merge(out_a, hm, proj, w_proj_a.astype(BF16), w_proj_b.astype(BF16), seq)

    w_route = jnp.zeros((d, LANES), F32).at[:, :N_GROUPS].set(w_group).at[:, N_GROUPS:N_GROUPS + N_EXPERTS].set(w_router)
    b_route = jnp.zeros((1, LANES), F32).at[0, :N_GROUPS].set(b_group).at[0, N_GROUPS:N_GROUPS + N_EXPERTS].set(b_router)
    w_route_hi = w_route.astype(BF16)
    w_route2 = jnp.concatenate([w_route_hi, (w_route - w_route_hi.astype(F32)).astype(BF16)], axis=1)
    x1, h2, logits = _outproj(mixed, x2, w_out.astype(BF16), mod4, norm_ffn_g, w_route2, b_route, seq)

    posm, wts, counts = _route(logits)
    pos = posm[:, 0:2]

    tile = MOE_TILE
    i32 = jnp.int32
    cnt = counts[0, N_GROUPS:N_GROUPS + N_EXPERTS].astype(i32)
    padded = ((cnt + tile - 1) // tile) * tile
    ends = jnp.cumsum(padded)
    offs = ends - padded
    nt_max = (2 * t) // tile + N_EXPERTS
    n_tiles = ends[-1] // tile
    tile_idx = jnp.arange(nt_max, dtype=i32)
    tile_start = tile_idx * tile
    live = tile_idx < n_tiles
    texp = jnp.minimum(jnp.sum((ends[None, :] <= tile_start[:, None]).astype(i32), axis=1), N_EXPERTS - 1)
    active = cnt > 0
    order = jnp.cumsum(active.astype(i32)) - 1
    n_active = jnp.sum(active.astype(i32))
    experts = jnp.arange(N_EXPERTS, dtype=i32)
    by_order = jnp.sum(jnp.where(active[None, :] & (order[None, :] == experts[:, None]), experts[None, :], 0), axis=1)
    t_order = order[texp]
    first = (live & (tile_start == offs[texp])).astype(i32)
    ahead = t_order + 2
    prefetch = jnp.where(ahead < n_active, by_order[jnp.minimum(ahead, N_EXPERTS - 1)], -1)
    head = jnp.zeros((nt_max,), i32).at[0].set(n_tiles).at[1].set(by_order[0]).at[2].set(
        jnp.where(n_active > 1, by_order[1], -1))
    plan = jnp.stack([first, t_order % 2, prefetch, head]).astype(i32)

    last_tile = jnp.where(padded > cnt, ends - tile, -1)
    spare_idx = n_tiles + jnp.arange(N_EXPERTS, dtype=i32)
    spare_tile = jnp.where(spare_idx < nt_max, spare_idx * tile, -1)
    zrow = jnp.concatenate([last_tile, spare_tile]).astype(i32)

    xs = _dispatch(pos, zrow, h2, nt_max * tile, seq)
    ys = _moe(plan, xs, w_gate_e, w_up_e, w_down_e)

    return _combine(pos, ys, x1, mod4, wts, seq)


def kernel(x, c, positions, w_ada, b_ada, norm_mix_g, w_in, q_a_norm_g, w_uq, kv_a_norm_g, w_ukv, q_norm_g, k_norm_g, conv_w, conv_b, b_mlstm_gates, mlstm_norm_g, w_proj_a, w_proj_b, w_out, norm_ffn_g, w_group, b_group, w_router, b_router, w_gate_e, w_up_e, w_down_e):
    b, seq, d = x.shape
    depth = w_ada.shape[0]
    x2 = x.reshape(b * seq, d)
    pos2 = positions.reshape(b * seq, 1)
    c_pad = jnp.zeros((SUBLANES, d), F32).at[:b].set(c)
    for l in range(depth):
        mod = _adaln(c_pad, w_ada[l], b_ada[l])
        mod4 = mod[:b].reshape(b, 6, 1, d)
        x2 = _layer(x2, mod4, pos2, seq, w_in[l], q_a_norm_g[l], w_uq[l], kv_a_norm_g[l], w_ukv[l], q_norm_g[l],
                    k_norm_g[l], conv_w[l], conv_b[l], b_mlstm_gates[l], mlstm_norm_g[l], w_proj_a[l], w_proj_b[l],
                    w_out[l], norm_mix_g[l], norm_ffn_g[l], w_group[l], b_group[l], w_router[l], b_router[l],
                    w_gate_e[l], w_up_e[l], w_down_e[l])
    return x2.reshape(b, seq, d)
```

```python
import functools
import math

import jax
import jax.numpy as jnp
from jax import lax
from jax.experimental import pallas as pl
from jax.experimental.pallas import tpu as pltpu

F32 = jnp.float32
BF16 = jnp.bfloat16

LANES = 128
SUBLANES = 8

D_MODEL = 2048
MLA_HEADS = 8
MLA_NOPE = 128
MLA_ROPE = 64
MLA_QK = MLA_NOPE + MLA_ROPE
MLA_V = 128
Q_LORA = 512
KV_LORA = 256
ROPE_THETA = 10000.0
ML_HEADS = 8
ML_DQK = 128
ML_DV = 128
ML_CONV = 4
MLA_W = MLA_HEADS * MLA_V
ML_W = ML_HEADS * ML_DV
N_GROUPS = 4
EXP_PER_GROUP = 8
N_EXPERTS = N_GROUPS * EXP_PER_GROUP
D_FF_EXPERT = 512
EPS = 1e-6

QK_PAD = 2 * LANES

SRC_CQ = 0
SRC_KPE = Q_LORA + KV_LORA
SRC_QK = SRC_KPE + MLA_ROPE
SRC_V = SRC_QK + 2 * ML_HEADS * ML_DQK
SRC_O = SRC_V + ML_W
SRC_I = SRC_O + ML_W
SRC_GA = SRC_I + 2 * ML_HEADS
SRC_GB = SRC_GA + D_MODEL
GATE_LANE = SRC_I % LANES

IN_BLOCK = 1024
IN_BLOCK_SRC = (SRC_QK, SRC_QK + IN_BLOCK, SRC_GA, SRC_GA + IN_BLOCK, SRC_GB, SRC_GB + IN_BLOCK, SRC_V, SRC_O, SRC_CQ)
COL_QK = 0
COL_GA = 2048
COL_GB = 4096
COL_V = 6144
COL_O = 7168
COL_LAT = 8192
LAT_W = Q_LORA + KV_LORA + MLA_ROPE

MLSTM_CHUNK = 256
MOE_TILE = 256
FLASH_HEADS_PER_STEP = 1
OUTPROJ_ROW_GROUP = 256
MERGE_ROW_GROUP = 256


def _cparams(sem, vmem_mb):
    return pltpu.CompilerParams(dimension_semantics=sem, vmem_limit_bytes=vmem_mb * 1024 * 1024)


def _dot(a, b):
    return jnp.dot(a, b, preferred_element_type=F32)


def _dot_nt(a, b):
    return lax.dot_general(a, b, (((1,), (1,)), ((), ())), preferred_element_type=F32)


def _dot_tn(a, b):
    return lax.dot_general(a, b, (((0,), (0,)), ((), ())), preferred_element_type=F32)


def _sigmoid(x):
    return 1.0 / (1.0 + jnp.exp(-x))


def _rms_scale(x, width):
    return lax.rsqrt(jnp.sum(x * x, axis=-1, keepdims=True) * (1.0 / width) + EPS)


def _adaln_kernel(c_ref, w_ref, b_ref, o_ref):
    c = c_ref[...]
    cond = (c * _sigmoid(c)).astype(BF16)
    o_ref[...] = _dot(cond, w_ref[...].astype(BF16)) + b_ref[...]


def _adaln(c_pad, w_ada, b_ada):
    rows, d = c_pad.shape
    n = w_ada.shape[1]
    tn = 1024
    return pl.pallas_call(
        _adaln_kernel,
        grid=(n // tn,),
        in_specs=[
            pl.BlockSpec((rows, d), lambda j: (0, 0)),
            pl.BlockSpec((d, tn), lambda j: (0, j)),
            pl.BlockSpec((1, tn), lambda j: (0, j)),
        ],
        out_specs=pl.BlockSpec((rows, tn), lambda j: (0, j)),
        out_shape=jax.ShapeDtypeStruct((rows, n), F32),
        compiler_params=_cparams(("arbitrary",), 40),
        name="adaln",
    )(c_pad, w_ada, b_ada.reshape(1, n))


def _inproj_kernel(off_ref, x_ref, sc_ref, sh_ref, g_ref, w_ref, wgate_ref, proj_ref, gates_ref, h_ref):
    j = pl.program_id(1)

    @pl.when(j == 0)
    def _():
        x = x_ref[...]
        h = x * _rms_scale(x, x.shape[-1]) * g_ref[...]
        h = (h * (1.0 + sc_ref[...]) + sh_ref[...]).astype(BF16)
        h_ref[...] = h
        gates_ref[...] = _dot_nt(h, wgate_ref[...].astype(BF16))

    proj_ref[...] = _dot_nt(h_ref[...], w_ref[...].astype(BF16)).astype(proj_ref.dtype)


def _inproj(x2, mod4, norm_g, w_in_t, seq):
    t, d = x2.shape
    tm = min(1024, seq)
    tn = IN_BLOCK
    bpr = seq // tm
    assert all(o % SUBLANES == 0 for o in IN_BLOCK_SRC)
    offs = jnp.asarray([o // SUBLANES for o in IN_BLOCK_SRC], jnp.int32)
    nblk = len(IN_BLOCK_SRC)
    gate_tile = SRC_I // LANES
    grid_spec = pltpu.PrefetchScalarGridSpec(
        num_scalar_prefetch=1,
        grid=(t // tm, nblk),
        in_specs=[
            pl.BlockSpec((tm, d), lambda i, j, o: (i, 0)),
            pl.BlockSpec((None, None, 1, d), lambda i, j, o: (i // bpr, 1, 0, 0)),
            pl.BlockSpec((None, None, 1, d), lambda i, j, o: (i // bpr, 0, 0, 0)),
            pl.BlockSpec((1, d), lambda i, j, o: (0, 0)),
            pl.BlockSpec((pl.Element(tn), pl.Element(d)), lambda i, j, o: (o[j] * SUBLANES, 0)),
            pl.BlockSpec((LANES, d), lambda i, j, o: (gate_tile, 0)),
        ],
        out_specs=[
            pl.BlockSpec((tm, tn), lambda i, j, o: (i, j)),
            pl.BlockSpec((tm, LANES), lambda i, j, o: (i, 0)),
        ],
        scratch_shapes=[pltpu.VMEM((tm, d), BF16)],
    )
    return pl.pallas_call(
        _inproj_kernel,
        grid_spec=grid_spec,
        out_shape=[
            jax.ShapeDtypeStruct((t, nblk * tn), BF16),
            jax.ShapeDtypeStruct((t, LANES), F32),
        ],
        compiler_params=_cparams(("arbitrary", "arbitrary"), 56),
        name="inproj",
    )(offs, x2, mod4, mod4, norm_g.reshape(1, d), w_in_t, w_in_t)


def _mla_prep_kernel(lat_ref, pos_ref, wuq_ref, wukv_ref, gqa_ref, gkva_ref, gq_ref, gk_ref, inv_ref,
                     qt_ref, k_ref, vt_ref):
    lat = lat_ref[...].astype(F32)
    cq = lat[:, :Q_LORA]
    ckv = lat[:, Q_LORA:Q_LORA + KV_LORA]
    kc = lat[:, SRC_KPE:SRC_KPE + LANES]
    cqn = (cq * _rms_scale(cq, Q_LORA) * gqa_ref[...]).astype(BF16)
    ckvn = (ckv * _rms_scale(ckv, KV_LORA) * gkva_ref[...]).astype(BF16)
    qraw = _dot(cqn, wuq_ref[...])
    kv = _dot(ckvn, wukv_ref[...])

    ang = pos_ref[...].astype(F32) * inv_ref[...]
    lane = lax.broadcasted_iota(jnp.int32, ang.shape, 1)
    lo = lane < MLA_ROPE
    cs = jnp.cos(ang - jnp.where(lo, 0.0, 0.5 * math.pi))

    quarter = MLA_ROPE // 2
    want = jnp.where(lane < 3 * quarter, lane - quarter, lane - 3 * quarter)
    came = pltpu.roll(lane, quarter, 1)
    swapped = jnp.where(came == want, pltpu.roll(kc, quarter, 1), pltpu.roll(kc, 3 * quarter, 1))
    kc = jnp.where(lo, kc, swapped)

    gq = gq_ref[...]
    gk = gk_ref[...]
    gq_n, gq_r = gq[:, :LANES], gq[:, LANES:]
    gk_n, gk_r = gk[:, :LANES], gk[:, LANES:]

    def rope(chunk, g_r):
        a = chunk * (g_r * cs)
        return jnp.where(lo, a + pltpu.roll(a, MLA_ROPE, 1), 0.0)

    kpe_ss = jnp.sum(jnp.where(lo, kc * kc, 0.0), axis=-1, keepdims=True)
    k_rope = rope(kc, gk_r)
    scale = MLA_QK ** -0.5 * math.log2(math.e)
    for h in range(MLA_HEADS):
        kn = kv[:, h * MLA_NOPE:(h + 1) * MLA_NOPE]
        sk = lax.rsqrt((jnp.sum(kn * kn, axis=-1, keepdims=True) + kpe_ss) * (1.0 / MLA_QK) + EPS)
        k_ref[:, h * QK_PAD:h * QK_PAD + LANES] = (kn * sk * gk_n).astype(BF16)
        k_ref[:, h * QK_PAD + LANES:(h + 1) * QK_PAD] = (k_rope * sk).astype(BF16)
        qn = qraw[:, h * QK_PAD:h * QK_PAD + LANES]
        qr = qraw[:, h * QK_PAD + LANES:(h + 1) * QK_PAD]
        ss = jnp.sum(qn * qn, axis=-1, keepdims=True) + jnp.sum(jnp.where(lo, qr * qr, 0.0), axis=-1, keepdims=True)
        sq = lax.rsqrt(ss * (1.0 / MLA_QK) + EPS) * scale
        qt_ref[h * QK_PAD:h * QK_PAD + LANES, :] = (qn * sq * gq_n).T.astype(BF16)
        qt_ref[h * QK_PAD + LANES:(h + 1) * QK_PAD, :] = (rope(qr, gq_r) * sq).T.astype(BF16)
        vh = kv[:, MLA_HEADS * MLA_NOPE + h * MLA_V:MLA_HEADS * MLA_NOPE + (h + 1) * MLA_V]
        vt_ref[h * MLA_V:(h + 1) * MLA_V, :] = vh.T.astype(BF16)


def _mla_prep(proj, pos2, wuq_p, wukv_p, gqa, gkva, gq, gk, inv_lanes, seq):
    t = proj.shape[0]
    tm = min(512, seq)
    hq = MLA_HEADS * QK_PAD
    lat_blk = COL_LAT // 1024
    const = lambda i: (0, 0)
    return pl.pallas_call(
        _mla_prep_kernel,
        grid=(t // tm,),
        in_specs=[
            pl.BlockSpec((tm, 1024), lambda i: (i, lat_blk)),
            pl.BlockSpec((tm, 1), lambda i: (i, 0)),
            pl.BlockSpec(wuq_p.shape, const),
            pl.BlockSpec(wukv_p.shape, const),
            pl.BlockSpec(gqa.shape, const),
            pl.BlockSpec(gkva.shape, const),
            pl.BlockSpec(gq.shape, const),
            pl.BlockSpec(gk.shape, const),
            pl.BlockSpec(inv_lanes.shape, const),
        ],
        out_specs=[
            pl.BlockSpec((hq, tm), lambda i: (0, i)),
            pl.BlockSpec((tm, hq), lambda i: (i, 0)),
            pl.BlockSpec((MLA_W, tm), lambda i: (0, i)),
        ],
        out_shape=[
            jax.ShapeDtypeStruct((hq, t), BF16),
            jax.ShapeDtypeStruct((t, hq), BF16),
            jax.ShapeDtypeStruct((MLA_W, t), BF16),
        ],
        compiler_params=_cparams(("arbitrary",), 48),
        name="mla_prep",
    )(proj, pos2, wuq_p, wukv_p, gqa, gkva, gq, gk, inv_lanes)


def _flash_kernel(qt_ref, k_ref, vt_ref, o_ref, *, tq, tk):
    seq = k_ref.shape[0]
    heads = k_ref.shape[1] // QK_PAD

    def scores(h, k0, q0):
        kj = k_ref[k0:k0 + tk, h * QK_PAD:(h + 1) * QK_PAD]
        return _dot(kj, qt_ref[h * QK_PAD:(h + 1) * QK_PAD, q0:q0 + tq])

    def update(h, state, st, k0, q0):
        m, l, acc = state
        if k0 + tk - 1 > q0:
            key = lax.broadcasted_iota(jnp.int32, st.shape, 0) + k0
            qry = lax.broadcasted_iota(jnp.int32, st.shape, 1) + q0
            st = jnp.where(key <= qry, st, -jnp.inf)
        m_new = jnp.maximum(m, jnp.max(st, axis=0, keepdims=True))
        alpha = jnp.exp2(m - m_new)
        p = jnp.exp2(st - m_new)
        l = alpha * l + jnp.sum(p, axis=0, keepdims=True)
        acc = alpha * acc + _dot(vt_ref[h * MLA_V:(h + 1) * MLA_V, k0:k0 + tk], p.astype(BF16))
        return m_new, l, acc

    steps = [(h, qi * tq, j * tk) for qi in range(seq // tq) for h in range(heads) for j in range(qi * tq // tk + 1)]
    st_next = scores(steps[0][0], steps[0][2], steps[0][1])
    state = None
    for n, (h, q0, k0) in enumerate(steps):
        st = st_next
        if n + 1 < len(steps):
            hn, qn, kn = steps[n + 1]
            st_next = scores(hn, kn, qn)
        if k0 == 0:
            state = (jnp.full((1, tq), -jnp.inf, F32), jnp.zeros((1, tq), F32), jnp.zeros((MLA_V, tq), F32))
        state = update(h, state, st, k0, q0)
        if k0 + tk >= q0 + tq:
            _, l, acc = state
            o_ref[q0:q0 + tq, h * MLA_V:(h + 1) * MLA_V] = (acc / l).T.astype(o_ref.dtype)


def _flash(qt, k3, vt):
    b, seq, _ = k3.shape
    tq = min(512, seq)
    tk = min(512, seq)
    hp = FLASH_HEADS_PER_STEP
    kern = functools.partial(_flash_kernel, tq=tq, tk=tk)
    return pl.pallas_call(
        kern,
        grid=(b, MLA_HEADS // hp),
        in_specs=[
            pl.BlockSpec((hp * QK_PAD, seq), lambda i, h: (h, i)),
            pl.BlockSpec((None, seq, hp * QK_PAD), lambda i, h: (i, 0, h)),
            pl.BlockSpec((hp * MLA_V, seq), lambda i, h: (h, i)),
        ],
        out_specs=pl.BlockSpec((None, seq, hp * MLA_V), lambda i, h: (i, 0, h)),
        out_shape=jax.ShapeDtypeStruct((b, seq, MLA_W), BF16),
        compiler_params=_cparams(("arbitrary", "arbitrary"), 40),
        name="flash",
    )(qt, k3, vt)


def _conv_kernel(cur_ref, halo_ref, w_ref, b_ref, qt_ref, k_ref, buf_ref, *, k_scale):
    tm = cur_ref.shape[0]
    cols = cur_ref.shape[1]
    half = cols // 2
    first = pl.program_id(1) == 0
    halo = halo_ref[...].astype(F32)
    buf_ref[0:SUBLANES, :] = jnp.where(first, 0.0, halo)
    buf_ref[SUBLANES:SUBLANES + tm, :] = cur_ref[...].astype(F32)
    cw = 512
    for c in range(cols // cw):
        sl = slice(c * cw, (c + 1) * cw)
        acc = jnp.zeros((tm, cw), F32) + b_ref[:, sl]
        for j in range(ML_CONV):
            off = SUBLANES - (ML_CONV - 1) + j
            acc = acc + buf_ref[off:off + tm, sl] * w_ref[j:j + 1, sl]
        y = acc * _sigmoid(acc)
        if c * cw < half:
            qt_ref[sl, :] = y.T.astype(qt_ref.dtype)
        else:
            k_ref[:, c * cw - half:(c + 1) * cw - half] = (y * k_scale).astype(k_ref.dtype)


def _conv_silu(proj3, conv_w, conv_b):
    b, seq, _ = proj3.shape
    cols = 2 * ML_HEADS * ML_DQK
    half = cols // 2
    tm = min(512, seq)
    hb = tm // SUBLANES
    kern = functools.partial(_conv_kernel, k_scale=ML_DQK ** -0.5)
    return pl.pallas_call(
        kern,
        grid=(b, seq // tm),
        in_specs=[
            pl.BlockSpec((None, tm, cols), lambda i, s: (i, s, COL_QK // cols)),
            pl.BlockSpec((None, SUBLANES, cols), lambda i, s: (i, jnp.maximum(s * hb - 1, 0), COL_QK // cols)),
            pl.BlockSpec((ML_CONV, cols), lambda i, s: (0, 0)),
            pl.BlockSpec((1, cols), lambda i, s: (0, 0)),
        ],
        out_specs=[
            pl.BlockSpec((None, half, tm), lambda i, s: (i, 0, s)),
            pl.BlockSpec((None, tm, half), lambda i, s: (i, s, 0)),
        ],
        out_shape=[
            jax.ShapeDtypeStruct((b, half, seq), BF16),
            jax.ShapeDtypeStruct((b, seq, half), BF16),
        ],
        scratch_shapes=[pltpu.VMEM((tm + SUBLANES, cols), F32)],
        compiler_params=_cparams(("arbitrary", "arbitrary"), 40),
        name="conv_silu",
    )(proj3, proj3, conv_w, conv_b.reshape(1, cols))


def _log_sigmoid(x):
    return -(jnp.maximum(-x, 0.0) + jnp.log1p(jnp.exp(-jnp.abs(x))))


def _mlstm_kernel(qt_ref, k_ref, v_ref, o_ref, gates_ref, gbias_ref, ng_ref, out_ref, ct_ref, m_ref):
    L = k_ref.shape[0]

    @pl.when(pl.program_id(1) == 0)
    def _():
        ct_ref[...] = jnp.zeros(ct_ref.shape, F32)
        m_ref[...] = jnp.zeros(m_ref.shape, F32)

    g = gates_ref[...] + gbias_ref[...]
    gt = g.T
    lf = _log_sigmoid(g)
    lft = _log_sigmoid(gt)
    r = lax.broadcasted_iota(jnp.int32, (L, L), 0)
    c = lax.broadcasted_iota(jnp.int32, (L, L), 1)
    src_le_qry = r <= c
    tril = (c <= r).astype(F32)
    triu = src_le_qry.astype(F32)
    hi = lax.Precision.HIGHEST
    bcol_all = jnp.dot(tril, lf, preferred_element_type=F32, precision=hi)
    brow_all = jnp.dot(lft, triu, preferred_element_type=F32, precision=hi)
    row = lax.broadcasted_iota(jnp.int32, (ML_DV, L), 0)
    ones_row = jnp.where(row == 0, 1.0, 0.0)

    def lead_matmuls(h):
        hs = slice(h * ML_DQK, (h + 1) * ML_DQK)
        qt = qt_ref[hs, :]
        return _dot(k_ref[:, hs], qt), _dot(ct_ref[h].astype(BF16), qt)

    lead_next = lead_matmuls(0)
    for h in range(ML_HEADS):
        hs = slice(h * ML_DQK, (h + 1) * ML_DQK)
        qk_t, cq_t = lead_next
        if h + 1 < ML_HEADS:
            lead_next = lead_matmuls(h + 1)
        li, lf_ = GATE_LANE + h, GATE_LANE + ML_HEADS + h
        b_row = brow_all[lf_:lf_ + 1, :]
        i_row = gt[li:li + 1, :]
        u_col = g[:, li:li + 1] - bcol_all[:, lf_:lf_ + 1]
        m_prev = m_ref[h][:, :1]
        logw_t = jnp.where(src_le_qry, b_row + u_col, -jnp.inf)
        log_inter = b_row + m_prev
        m_t = jnp.maximum(jnp.max(logw_t, axis=0, keepdims=True), log_inter)
        w_t = jnp.exp(logw_t - m_t)
        a = jnp.exp(log_inter - m_t)
        kh = k_ref[:, hs]
        vt_aug = jnp.concatenate([v_ref[:, hs].astype(F32).T, ones_row], axis=0)
        s_t = qk_t * w_t
        ct = ct_ref[h]
        nd = _dot(vt_aug.astype(BF16), s_t.astype(BF16)) + a * cq_t
        num = nd[:ML_DV, :]
        den = nd[ML_DV:ML_DV + 1, :]
        hout_t = num * (1.0 / jnp.maximum(jnp.abs(den), jnp.exp(-m_t)))
        hn_t = hout_t * lax.rsqrt(jnp.sum(hout_t * hout_t, axis=0, keepdims=True) * (1.0 / ML_DV) + EPS)
        gate = _sigmoid(o_ref[:, hs].astype(F32))
        out_ref[:, hs] = (hn_t.T * ng_ref[:, hs] * gate).astype(out_ref.dtype)

        b_last = b_row[:, L - 1:L]
        logg = b_last - b_row + i_row
        m_new = jnp.maximum(b_last + m_prev, jnp.max(logg, axis=-1, keepdims=True))
        g_row = jnp.exp(logg - m_new)
        decay = jnp.exp(b_last + m_prev - m_new)
        ct_ref[h] = decay * ct + _dot((vt_aug * g_row).astype(BF16), kh)
        m_ref[h] = jnp.broadcast_to(m_new, m_ref.shape[1:])


def _mlstm(qt3, k3, proj3, gates3, gbias, norm_g):
    b, seq, w = k3.shape
    L = min(MLSTM_CHUNK, seq)
    return pl.pallas_call(
        _mlstm_kernel,
        grid=(b, seq // L),
        in_specs=[
            pl.BlockSpec((None, w, L), lambda i, c: (i, 0, c)),
            pl.BlockSpec((None, L, w), lambda i, c: (i, c, 0)),
            pl.BlockSpec((None, L, w), lambda i, c: (i, c, COL_V // w)),
            pl.BlockSpec((None, L, w), lambda i, c: (i, c, COL_O // w)),
            pl.BlockSpec((None, L, LANES), lambda i, c: (i, c, 0)),
            pl.BlockSpec((1, LANES), lambda i, c: (0, 0)),
            pl.BlockSpec((1, w), lambda i, c: (0, 0)),
        ],
        out_specs=pl.BlockSpec((None, L, w), lambda i, c: (i, c, 0)),
        out_shape=jax.ShapeDtypeStruct((b, seq, w), BF16),
        scratch_shapes=[
            pltpu.VMEM((ML_HEADS, 2 * ML_DV, ML_DQK), F32),
            pltpu.VMEM((ML_HEADS, 1, LANES), F32),
        ],
        compiler_params=_cparams(("arbitrary", "arbitrary"), 40),
        name="mlstm",
    )(qt3, k3, proj3, proj3, gates3, gbias, norm_g.reshape(1, w))


def _merge_kernel(a_ref, b_ref, ga_ref, gb_ref, wa_ref, wb_ref, o_ref):
    tm = a_ref.shape[0]
    rows = min(MERGE_ROW_GROUP, tm)
    groups = [slice(r0, r0 + rows) for r0 in range(0, tm, rows)]

    def proj(rs):
        return _dot(a_ref[rs, :], wa_ref[...]), _dot(b_ref[rs, :], wb_ref[...])

    nxt = proj(groups[0])
    for n, rs in enumerate(groups):
        pa, pb = nxt
        if n + 1 < len(groups):
            nxt = proj(groups[n + 1])
        mixed = _sigmoid(ga_ref[rs, :].astype(F32)) * pa + _sigmoid(gb_ref[rs, :].astype(F32)) * pb
        o_ref[rs, :] = mixed.astype(o_ref.dtype)


def _merge(out_a, hm, proj, wa, wb, seq):
    t = out_a.shape[0]
    d = wa.shape[1]
    tm = min(1024, seq)
    tn = 1024
    return pl.pallas_call(
        _merge_kernel,
        grid=(d // tn, t // tm),
        in_specs=[
            pl.BlockSpec((tm, MLA_W), lambda j, i: (i, 0)),
            pl.BlockSpec((tm, ML_W), lambda j, i: (i, 0)),
            pl.BlockSpec((tm, tn), lambda j, i: (i, COL_GA // tn + j)),
            pl.BlockSpec((tm, tn), lambda j, i: (i, COL_GB // tn + j)),
            pl.BlockSpec((MLA_W, tn), lambda j, i: (0, j)),
            pl.BlockSpec((ML_W, tn), lambda j, i: (0, j)),
        ],
        out_specs=pl.BlockSpec((tm, tn), lambda j, i: (i, j)),
        out_shape=jax.ShapeDtypeStruct((t, d), BF16),
        compiler_params=_cparams(("arbitrary", "arbitrary"), 40),
        name="merge",
    )(out_a, hm, proj, proj, wa, wb)


def _outproj_kernel(mix_ref, x_ref, w_ref, gt_ref, sc_ref, sh_ref, g_ref, wr_ref, br_ref, x1_ref, h2_ref, lg_ref):
    tm = mix_ref.shape[0]
    rows = OUTPROJ_ROW_GROUP
    groups = [slice(r0, r0 + rows) for r0 in range(0, tm, rows)]
    y_next = _dot(mix_ref[groups[0], :], w_ref[...])
    for n, rs in enumerate(groups):
        y = y_next
        if n + 1 < len(groups):
            y_next = _dot(mix_ref[groups[n + 1], :], w_ref[...])
        x1 = x_ref[rs, :] + gt_ref[...] * y
        x1_ref[rs, :] = x1
        h2 = x1 * _rms_scale(x1, x1.shape[-1]) * g_ref[...]
        h2 = h2 * (1.0 + sc_ref[...]) + sh_ref[...]
        h2_ref[rs, :] = h2
        h_hi = h2.astype(BF16)
        h_lo = (h2 - h_hi.astype(F32)).astype(BF16)
        r = _dot(h_hi, wr_ref[...]) + _dot(h_lo, wr_ref[...])
        lg_ref[rs, :] = r[:, :LANES] + r[:, LANES:] + br_ref[...]


def _outproj(mixed, x2, w_out, mod4, norm_g, w_route2, b_route, seq):
    t, d = x2.shape
    tm = min(512, seq)
    bpr = seq // tm
    mod_spec = lambda k: pl.BlockSpec((None, None, 1, d), lambda i: (i // bpr, k, 0, 0))
    const = lambda i: (0, 0)
    return pl.pallas_call(
        _outproj_kernel,
        grid=(t // tm,),
        in_specs=[
            pl.BlockSpec((tm, d), lambda i: (i, 0)),
            pl.BlockSpec((tm, d), lambda i: (i, 0)),
            pl.BlockSpec((d, d), const, pipeline_mode=pl.Buffered(1)),
            mod_spec(2),
            mod_spec(4),
            mod_spec(3),
            pl.BlockSpec((1, d), const),
            pl.BlockSpec((d, 2 * LANES), const, pipeline_mode=pl.Buffered(1)),
            pl.BlockSpec((1, LANES), const),
        ],
        out_specs=[
            pl.BlockSpec((tm, d), lambda i: (i, 0)),
            pl.BlockSpec((tm, d), lambda i: (i, 0)),
            pl.BlockSpec((tm, LANES), lambda i: (i, 0)),
        ],
        out_shape=[
            jax.ShapeDtypeStruct((t, d), F32),
            jax.ShapeDtypeStruct((t, d), F32),
            jax.ShapeDtypeStruct((t, LANES), F32),
        ],
        compiler_params=_cparams(("arbitrary",), 56),
        name="outproj",
    )(mixed, x2, w_out, mod4, mod4, mod4, norm_g.reshape(1, d), w_route2, b_route)


def _route_kernel(lg_ref, pos_ref, wts_ref, cnt_ref, carry_ref, offs_ref, meta_s, wts_s):
    phase = pl.program_id(0)
    i = pl.program_id(1)
    tm = lg_ref.shape[0]
    lane = lax.broadcasted_iota(jnp.int32, (tm, LANES), 1)

    @pl.when((phase == 0) & (i == 0))
    def _():
        carry_ref[...] = jnp.zeros(carry_ref.shape, F32)

    @pl.when(phase == 0)
    def _():
        lg = lg_ref[...]
        big = jnp.int32(LANES)
        ninf = -jnp.inf

        def first_argmax(vals):
            mx = jnp.max(vals, axis=-1, keepdims=True)
            idx = jnp.min(jnp.where(vals == mx, lane, big), axis=-1, keepdims=True)
            return mx, idx

        gl = jnp.where(lane < N_GROUPS, lg, ninf)
        gmax, gsel = first_argmax(gl)
        g_w = 1.0 / jnp.sum(jnp.exp(gl - gmax), axis=-1, keepdims=True)
        lo = N_GROUPS + gsel * EXP_PER_GROUP
        in_grp = (lane >= lo) & (lane < lo + EXP_PER_GROUP)
        el = jnp.where(in_grp, lg, ninf)
        e1, i1 = first_argmax(el)
        e2, i2 = first_argmax(jnp.where(lane == i1, ninf, el))
        p2 = jnp.exp(e2 - e1)
        w1 = g_w / (1.0 + p2)
        w2 = g_w * p2 / (1.0 + p2)

        oh1 = lane == i1
        oh2 = lane == i2
        oh = jnp.where(oh1 | oh2, 1.0, 0.0)
        r = lax.broadcasted_iota(jnp.int32, (tm, tm), 0)
        c = lax.broadcasted_iota(jnp.int32, (tm, tm), 1)
        strict = jnp.where(c < r, 1.0, 0.0).astype(BF16)
        before = _dot(strict, oh.astype(BF16)) + carry_ref[...]
        rank1 = jnp.sum(jnp.where(oh1, before, 0.0), axis=-1, keepdims=True).astype(jnp.int32)
        rank2 = jnp.sum(jnp.where(oh2, before, 0.0), axis=-1, keepdims=True).astype(jnp.int32)
        carry_ref[...] = carry_ref[...] + jnp.sum(oh, axis=0, keepdims=True)
        meta_s[i] = jnp.where(lane == 0, i1, jnp.where(lane == 1, i2, jnp.where(lane == 2, rank1, jnp.where(lane == 3, rank2, 0))))
        wts_s[i] = jnp.where(lane == 0, w1, jnp.where(lane == 1, w2, 0.0))

    @pl.when((phase == 1) & (i == 0))
    def _():
        cnt = carry_ref[...]
        cnt_ref[...] = jnp.broadcast_to(cnt, cnt_ref.shape)
        padded = jnp.ceil(cnt * (1.0 / MOE_TILE)) * MOE_TILE
        r = lax.broadcasted_iota(jnp.int32, (LANES, LANES), 0)
        c = lax.broadcasted_iota(jnp.int32, (LANES, LANES), 1)
        upper = jnp.where(r < c, 1.0, 0.0).astype(BF16)
        padded8 = jnp.broadcast_to(padded, (SUBLANES, LANES)).astype(BF16)
        offs_ref[...] = _dot(padded8, upper)[:1, :]

    @pl.when(phase == 1)
    def _():
        meta = meta_s[i]
        offs = offs_ref[...]
        off1 = jnp.sum(jnp.where(lane == meta[:, 0:1], offs, 0.0), axis=-1, keepdims=True).astype(jnp.int32)
        off2 = jnp.sum(jnp.where(lane == meta[:, 1:2], offs, 0.0), axis=-1, keepdims=True).astype(jnp.int32)
        pos1 = off1 + meta[:, 2:3]
        pos2 = off2 + meta[:, 3:4]
        pos_ref[...] = jnp.where(lane == 0, pos1, jnp.where(lane == 1, pos2, 0))
        wts_ref[...] = wts_s[i]


def _route(logits):
    t = logits.shape[0]
    tm = min(512, t)
    nb = t // tm
    return pl.pallas_call(
        _route_kernel,
        grid=(2, nb),
        in_specs=[pl.BlockSpec((tm, LANES), lambda p, i: (i * (1 - p), 0))],
        out_specs=[
            pl.BlockSpec((tm, LANES), lambda p, i: (i * p, 0)),
            pl.BlockSpec((tm, LANES), lambda p, i: (i * p, 0)),
            pl.BlockSpec((SUBLANES, LANES), lambda p, i: (0, 0)),
        ],
        out_shape=[
            jax.ShapeDtypeStruct((t, LANES), jnp.int32),
            jax.ShapeDtypeStruct((t, LANES), F32),
            jax.ShapeDtypeStruct((SUBLANES, LANES), F32),
        ],
        scratch_shapes=[
            pltpu.VMEM((1, LANES), F32),
            pltpu.VMEM((1, LANES), F32),
            pltpu.VMEM((nb, tm, LANES), jnp.int32),
            pltpu.VMEM((nb, tm, LANES), F32),
        ],
        compiler_params=_cparams(("arbitrary", "arbitrary"), 32),
        name="route",
    )(logits)


def _row_gather_start(idx_at, src_hbm, dst, sem, rows):
    for r in range(rows):
        pltpu.make_async_copy(src_hbm.at[pl.ds(idx_at(r), 1)], dst.at[pl.ds(r, 1)], sem).start()


def _row_gather_wait(src_hbm, dst, sem, rows):
    pltpu.make_async_copy(src_hbm.at[pl.ds(0, rows)], dst, sem).wait()


def _dispatch_kernel(pos_ref, zrow_ref, h_ref, xs_hbm, zbuf, sems):
    i = pl.program_id(0)
    tm = h_ref.shape[0]
    tile = zbuf.shape[0]

    def zero_copy(z):
        return pltpu.make_async_copy(zbuf, xs_hbm.at[pl.ds(pl.multiple_of(zrow_ref[z], tile), tile)], sems.at[0])

    @pl.when(i == 0)
    def _():
        zbuf[...] = jnp.zeros(zbuf.shape, zbuf.dtype)
        for z in range(zrow_ref.shape[0]):
            @pl.when(zrow_ref[z] >= 0)
            def _():
                zero_copy(z).start()
        for z in range(zrow_ref.shape[0]):
            @pl.when(zrow_ref[z] >= 0)
            def _():
                zero_copy(z).wait()

    for k in range(2):
        base = (2 * i + k) * tm
        for r in range(tm):
            pltpu.make_async_copy(h_ref.at[pl.ds(r, 1)], xs_hbm.at[pl.ds(pos_ref[base + r], 1)], sems.at[1]).start()
    for k in range(2):
        pltpu.make_async_copy(h_ref, xs_hbm.at[pl.ds(0, tm)], sems.at[1]).wait()


def _dispatch(pos, zrow, h2, n_rows, seq):
    t, d = h2.shape
    tm = min(512, seq)
    pos_blocks = pos.reshape(t // tm, tm, 2).transpose(0, 2, 1).reshape(-1)
    grid_spec = pltpu.PrefetchScalarGridSpec(
        num_scalar_prefetch=2,
        grid=(t // tm,),
        in_specs=[pl.BlockSpec((tm, d), lambda i, p, z: (i, 0))],
        out_specs=pl.BlockSpec(memory_space=pl.ANY),
        scratch_shapes=[
            pltpu.VMEM((MOE_TILE, d), F32),
            pltpu.SemaphoreType.DMA((2,)),
        ],
    )
    return pl.pallas_call(
        _dispatch_kernel,
        grid_spec=grid_spec,
        out_shape=jax.ShapeDtypeStruct((n_rows, d), F32),
        compiler_params=_cparams(("arbitrary",), 40),
        name="moe_dispatch",
    )(pos_blocks, zrow, h2)


def _moe_kernel(plan_ref, x_ref, wg_hbm, wu_hbm, wd_hbm, y_ref, stg_g, stg_u, stg_d, wsem, wb_g, wb_u, wb_d):
    j = pl.program_id(0)
    nt = plan_ref[3, 0]

    def w_copies(e, s):
        return (pltpu.make_async_copy(wg_hbm.at[e], stg_g.at[s], wsem.at[s]),
                pltpu.make_async_copy(wu_hbm.at[e], stg_u.at[s], wsem.at[s]),
                pltpu.make_async_copy(wd_hbm.at[e], stg_d.at[s], wsem.at[s]))

    @pl.when(j == 0)
    def _():
        for cp in w_copies(plan_ref[3, 1], 0):
            cp.start()

        @pl.when(plan_ref[3, 2] >= 0)
        def _():
            for cp in w_copies(plan_ref[3, 2], 1):
                cp.start()

    @pl.when((j < nt) & (plan_ref[0, j] == 1))
    def _():
        s = plan_ref[1, j]
        for cp in w_copies(0, s):
            cp.wait()
        wb_g[...] = stg_g[s].astype(BF16)
        wb_u[...] = stg_u[s].astype(BF16)
        wb_d[...] = stg_d[s].astype(BF16)

        @pl.when(plan_ref[2, j] >= 0)
        def _():
            for cp in w_copies(plan_ref[2, j], s):
                cp.start()

    @pl.when(j < nt)
    def _():
        x = x_ref[...].astype(BF16)
        hg = _dot(x, wb_g[...])
        hu = _dot(x, wb_u[...])
        act = (hg * _sigmoid(hg) * hu).astype(BF16)
        y_ref[...] = _dot(act, wb_d[...])

    @pl.when(j >= nt)
    def _():
        y_ref[...] = jnp.zeros(y_ref.shape, y_ref.dtype)


def _moe(plan, xs, w_gate_e, w_up_e, w_down_e):
    d = xs.shape[1]
    f = w_gate_e.shape[-1]
    nt_max = plan.shape[1]
    tile = MOE_TILE
    any_spec = pl.BlockSpec(memory_space=pl.ANY)
    grid_spec = pltpu.PrefetchScalarGridSpec(
        num_scalar_prefetch=1,
        grid=(nt_max,),
        in_specs=[
            pl.BlockSpec((tile, d), lambda j, plan_: (jnp.minimum(j, plan_[3, 0] - 1), 0)),
            any_spec, any_spec, any_spec,
        ],
        out_specs=pl.BlockSpec((tile, d), lambda j, plan_: (j, 0)),
        scratch_shapes=[
            pltpu.VMEM((2, d, f), F32),
            pltpu.VMEM((2, d, f), F32),
            pltpu.VMEM((2, f, d), F32),
            pltpu.SemaphoreType.DMA((2,)),
            pltpu.VMEM((d, f), BF16),
            pltpu.VMEM((d, f), BF16),
            pltpu.VMEM((f, d), BF16),
        ],
    )
    return pl.pallas_call(
        _moe_kernel,
        grid_spec=grid_spec,
        out_shape=jax.ShapeDtypeStruct((nt_max * tile, d), F32),
        compiler_params=_cparams(("arbitrary",), 56),
        name="moe_experts",
    )(plan, xs, w_gate_e, w_up_e, w_down_e)


def _combine_kernel(pos_ref, y_hbm, x1_ref, gt_ref, wts_ref, o_ref, ybuf, sem):
    i = pl.program_id(0)
    n = pl.num_programs(0)
    tm = x1_ref.shape[0]
    slot = lax.rem(i, 2)

    def start(blk, s):
        for k in range(2):
            base = (2 * blk + k) * tm
            _row_gather_start(lambda r, base=base: pos_ref[base + r], y_hbm, ybuf.at[s, k], sem.at[s], tm)

    @pl.when(i == 0)
    def _():
        start(0, 0)

        @pl.when(n > 1)
        def _():
            start(1, 1)

    for k in range(2):
        _row_gather_wait(y_hbm, ybuf.at[slot, k], sem.at[slot], tm)
    w = wts_ref[...]
    moe = w[:, 0:1] * ybuf[slot, 0] + w[:, 1:2] * ybuf[slot, 1]
    o_ref[...] = x1_ref[...] + gt_ref[...] * moe

    @pl.when(i + 2 < n)
    def _():
        start(i + 2, slot)


def _combine(pos, ys, x1, mod4, wts, seq):
    t, d = x1.shape
    tm = min(256, seq)
    bpr = seq // tm
    pos_blocks = pos.reshape(t // tm, tm, 2).transpose(0, 2, 1).reshape(-1)
    grid_spec = pltpu.PrefetchScalarGridSpec(
        num_scalar_prefetch=1,
        grid=(t // tm,),
        in_specs=[
            pl.BlockSpec(memory_space=pl.ANY),
            pl.BlockSpec((tm, d), lambda i, p: (i, 0)),
            pl.BlockSpec((None, None, 1, d), lambda i, p: (i // bpr, 5, 0, 0)),
            pl.BlockSpec((tm, LANES), lambda i, p: (i, 0)),
        ],
        out_specs=pl.BlockSpec((tm, d), lambda i, p: (i, 0)),
        scratch_shapes=[
            pltpu.VMEM((2, 2, tm, d), F32),
            pltpu.SemaphoreType.DMA((2,)),
        ],
    )
    return pl.pallas_call(
        _combine_kernel,
        grid_spec=grid_spec,
        out_shape=jax.ShapeDtypeStruct((t, d), F32),
        compiler_params=_cparams(("arbitrary",), 48),
        name="moe_combine",
    )(pos_blocks, ys, x1, mod4, wts)


def _swap_halves(w):
    half = w.shape[-1] // 2
    return jnp.concatenate([w[..., half:], w[..., :half]], axis=-1)


def _layout_w_uq(w_uq):
    r = w_uq.shape[0]
    w = w_uq.reshape(r, MLA_HEADS, MLA_QK)
    rope = w[..., MLA_NOPE:]
    return jnp.concatenate([w[..., :MLA_NOPE], rope, _swap_halves(rope)], axis=-1).reshape(r, MLA_HEADS * QK_PAD).astype(BF16)


def _layout_w_ukv(w_ukv):
    r = w_ukv.shape[0]
    w = w_ukv.reshape(r, MLA_HEADS, MLA_NOPE + MLA_V)
    return jnp.concatenate([w[..., :MLA_NOPE].reshape(r, -1), w[..., MLA_NOPE:].reshape(r, -1)], axis=-1).astype(BF16)


def _rope_gain(g):
    g1 = g[MLA_NOPE:MLA_NOPE + MLA_ROPE // 2]
    g2 = g[MLA_NOPE + MLA_ROPE // 2:]
    return jnp.concatenate([g[:MLA_NOPE], g1, g2, -g2, g1]).reshape(1, QK_PAD)


def _layer(x2, cond_mod4, pos2, seq, w_in, q_a_norm_g, w_uq, kv_a_norm_g, w_ukv, q_norm_g, k_norm_g, conv_w, conv_b,
           b_mlstm_gates, mlstm_norm_g, w_proj_a, w_proj_b, w_out, norm_mix_g, norm_ffn_g, w_group, b_group,
           w_router, b_router, w_gate_e, w_up_e, w_down_e):
    t, d = x2.shape
    b = t // seq
    mod4 = cond_mod4

    proj, gates = _inproj(x2, mod4, norm_mix_g, w_in.T, seq)

    inv = ROPE_THETA ** (-jnp.arange(0, MLA_ROPE, 2, dtype=F32) / MLA_ROPE)
    inv_lanes = jnp.tile(inv, LANES // (MLA_ROPE // 2)).reshape(1, LANES)
    qt, k, vt = _mla_prep(proj, pos2, _layout_w_uq(w_uq), _layout_w_ukv(w_ukv), q_a_norm_g.reshape(1, -1),
                        kv_a_norm_g.reshape(1, -1), _rope_gain(q_norm_g), _rope_gain(k_norm_g), inv_lanes, seq)
    out_a = _flash(qt, k.reshape(b, seq, -1), vt).reshape(t, MLA_W)

    proj3 = proj.reshape(b, seq, -1)
    ml_qt, ml_k = _conv_silu(proj3, conv_w, conv_b)
    gbias = jnp.zeros((1, LANES), F32).at[0, GATE_LANE:GATE_LANE + 2 * ML_HEADS].set(b_mlstm_gates.reshape(-1))
    hm = _mlstm(ml_qt, ml_k, proj3, gates.reshape(b, seq, LANES), gbias, mlstm_norm_g).reshape(t, ML_W)

    mixed = _merge(out_a, hm, proj, w_proj_a.astype(BF16), w_proj_b.astype(BF16), seq)

    w_route = jnp.zeros((d, LANES), F32).at[:, :N_GROUPS].set(w_group).at[:, N_GROUPS:N_GROUPS + N_EXPERTS].set(w_router)
    b_route = jnp.zeros((1, LANES), F32).at[0, :N_GROUPS].set(b_group).at[0, N_GROUPS:N_GROUPS + N_EXPERTS].set(b_router)
    w_route_hi = w_route.astype(BF16)
    w_route2 = jnp.concatenate([w_route_hi, (w_route - w_route_hi.astype(F32)).astype(BF16)], axis=1)
    x1, h2, logits = _outproj(mixed, x2, w_out.astype(BF16), mod4, norm_ffn_g, w_route2, b_route, seq)

    posm, wts, counts = _route(logits)
    pos = posm[:, 0:2]

    tile = MOE_TILE
    i32 = jnp.int32
    cnt = counts[0, N_GROUPS:N_GROUPS + N_EXPERTS].astype(i32)
    padded = ((cnt + tile - 1) // tile) * tile
    ends = jnp.cumsum(padded)
    offs = ends - padded
    nt_max = (2 * t) // tile + N_EXPERTS
    n_tiles = ends[-1] // tile
    tile_idx = jnp.arange(nt_max, dtype=i32)
    tile_start = tile_idx * tile
    live = tile_idx < n_tiles
    texp = jnp.minimum(jnp.sum((ends[None, :] <= tile_start[:, None]).astype(i32), axis=1), N_EXPERTS - 1)
    active = cnt > 0
    order = jnp.cumsum(active.astype(i32)) - 1
    n_active = jnp.sum(active.astype(i32))
    experts = jnp.arange(N_EXPERTS, dtype=i32)
    by_order = jnp.sum(jnp.where(active[None, :] & (order[None, :] == experts[:, None]), experts[None, :], 0), axis=1)
    t_order = order[texp]
    first = (live & (tile_start == offs[texp])).astype(i32)
    ahead = t_order + 2
    prefetch = jnp.where(ahead < n_active, by_order[jnp.minimum(ahead, N_EXPERTS - 1)], -1)
    head = jnp.zeros((nt_max,), i32).at[0].set(n_tiles).at[1].set(by_order[0]).at[2].set(
        jnp.where(n_active > 1, by_order[1], -1))
    plan = jnp.stack([first, t_order % 2, prefetch, head]).astype(i32)

    last_tile = jnp.where(padded > cnt, ends - tile, -1)
    spare_idx = n_tiles + jnp.arange(N_EXPERTS, dtype=i32)
    spare_tile = jnp.where(spare_idx < nt_max, spare_idx * tile, -1)
    zrow = jnp.concatenate([last_tile, spare_tile]).astype(i32)

    xs = _dispatch(pos, zrow, h2, nt_max * tile, seq)
    ys = _moe(plan, xs, w_gate_e, w_up_e, w_down_e)

    return _combine(pos, ys, x1, mod4, wts, seq)


def kernel(x, c, positions, w_ada, b_ada, norm_mix_g, w_in, q_a_norm_g, w_uq, kv_a_norm_g, w_ukv, q_norm_g, k_norm_g, conv_w, conv_b, b_mlstm_gates, mlstm_norm_g, w_proj_a, w_proj_b, w_out, norm_ffn_g, w_group, b_group, w_router, b_router, w_gate_e, w_up_e, w_down_e):
    b, seq, d = x.shape
    depth = w_ada.shape[0]
    x2 = x.reshape(b * seq, d)
    pos2 = positions.reshape(b * seq, 1)
    c_pad = jnp.zeros((SUBLANES, d), F32).at[:b].set(c)
    for l in range(depth):
        mod = _adaln(c_pad, w_ada[l], b_ada[l])
        mod4 = mod[:b].reshape(b, 6, 1, d)
        x2 = _layer(x2, mod4, pos2, seq, w_in[l], q_a_norm_g[l], w_uq[l], kv_a_norm_g[l], w_ukv[l], q_norm_g[l],
                    k_norm_g[l], conv_w[l], conv_b[l], b_mlstm_gates[l], mlstm_norm_g[l], w_proj_a[l], w_proj_b[l],
                    w_out[l], norm_mix_g[l], norm_ffn_g[l], w_group[l], b_group[l], w_router[l], b_router[l],
                    w_gate_e[l], w_up_e[l], w_down_e[l])
    return x2.reshape(b, seq, d)
```

```python
import functools
import math

import jax
import jax.numpy as jnp
from jax import lax
from jax.experimental import pallas as pl
from jax.experimental.pallas import tpu as pltpu

F32 = jnp.float32
BF16 = jnp.bfloat16

LANES = 128
SUBLANES = 8

D_MODEL = 2048
MLA_HEADS = 8
MLA_NOPE = 128
MLA_ROPE = 64
MLA_QK = MLA_NOPE + MLA_ROPE
MLA_V = 128
Q_LORA = 512
KV_LORA = 256
ROPE_THETA = 10000.0
ML_HEADS = 8
ML_DQK = 128
ML_DV = 128
ML_CONV = 4
MLA_W = MLA_HEADS * MLA_V
ML_W = ML_HEADS * ML_DV
N_GROUPS = 4
EXP_PER_GROUP = 8
N_EXPERTS = N_GROUPS * EXP_PER_GROUP
D_FF_EXPERT = 512
EPS = 1e-6

QK_PAD = 2 * LANES

SRC_CQ = 0
SRC_KPE = Q_LORA + KV_LORA
SRC_QK = SRC_KPE + MLA_ROPE
SRC_V = SRC_QK + 2 * ML_HEADS * ML_DQK
SRC_O = SRC_V + ML_W
SRC_I = SRC_O + ML_W
SRC_GA = SRC_I + 2 * ML_HEADS
SRC_GB = SRC_GA + D_MODEL
GATE_LANE = SRC_I % LANES

IN_BLOCK = 1024
IN_BLOCK_SRC = (SRC_QK, SRC_QK + IN_BLOCK, SRC_GA, SRC_GA + IN_BLOCK, SRC_GB, SRC_GB + IN_BLOCK, SRC_V, SRC_O, SRC_CQ)
COL_QK = 0
COL_GA = 2048
COL_GB = 4096
COL_V = 6144
COL_O = 7168
COL_LAT = 8192
LAT_W = Q_LORA + KV_LORA + MLA_ROPE

MLSTM_CHUNK = 256
MOE_TILE = 256
FLASH_HEADS_PER_STEP = 2
OUTPROJ_ROW_GROUP = 256
MERGE_ROW_GROUP = 256


def _cparams(sem, vmem_mb):
    return pltpu.CompilerParams(dimension_semantics=sem, vmem_limit_bytes=vmem_mb * 1024 * 1024)


def _dot(a, b):
    return jnp.dot(a, b, preferred_element_type=F32)


def _dot_nt(a, b):
    return lax.dot_general(a, b, (((1,), (1,)), ((), ())), preferred_element_type=F32)


def _dot_tn(a, b):
    return lax.dot_general(a, b, (((0,), (0,)), ((), ())), preferred_element_type=F32)


def _sigmoid(x):
    return 1.0 / (1.0 + jnp.exp(-x))


def _rms_scale(x, width):
    return lax.rsqrt(jnp.sum(x * x, axis=-1, keepdims=True) * (1.0 / width) + EPS)


def _adaln_kernel(c_ref, w_ref, b_ref, o_ref):
    c = c_ref[...]
    cond = (c * _sigmoid(c)).astype(BF16)
    o_ref[...] = _dot(cond, w_ref[...].astype(BF16)) + b_ref[...]


def _adaln(c_pad, w_ada, b_ada):
    rows, d = c_pad.shape
    n = w_ada.shape[1]
    tn = 1024
    return pl.pallas_call(
        _adaln_kernel,
        grid=(n // tn,),
        in_specs=[
            pl.BlockSpec((rows, d), lambda j: (0, 0)),
            pl.BlockSpec((d, tn), lambda j: (0, j)),
            pl.BlockSpec((1, tn), lambda j: (0, j)),
        ],
        out_specs=pl.BlockSpec((rows, tn), lambda j: (0, j)),
        out_shape=jax.ShapeDtypeStruct((rows, n), F32),
        compiler_params=_cparams(("arbitrary",), 40),
        name="adaln",
    )(c_pad, w_ada, b_ada.reshape(1, n))


def _inproj_kernel(off_ref, x_ref, sc_ref, sh_ref, g_ref, w_ref, wgate_ref, proj_ref, gates_ref, h_ref):
    j = pl.program_id(1)

    @pl.when(j == 0)
    def _():
        x = x_ref[...]
        h = x * _rms_scale(x, x.shape[-1]) * g_ref[...]
        h = (h * (1.0 + sc_ref[...]) + sh_ref[...]).astype(BF16)
        h_ref[...] = h
        gates_ref[...] = _dot_nt(h, wgate_ref[...].astype(BF16))

    proj_ref[...] = _dot_nt(h_ref[...], w_ref[...].astype(BF16)).astype(proj_ref.dtype)


def _inproj(x2, mod4, norm_g, w_in_t, seq):
    t, d = x2.shape
    tm = min(1024, seq)
    tn = IN_BLOCK
    bpr = seq // tm
    assert all(o % SUBLANES == 0 for o in IN_BLOCK_SRC)
    offs = jnp.asarray([o // SUBLANES for o in IN_BLOCK_SRC], jnp.int32)
    nblk = len(IN_BLOCK_SRC)
    gate_tile = SRC_I // LANES
    grid_spec = pltpu.PrefetchScalarGridSpec(
        num_scalar_prefetch=1,
        grid=(t // tm, nblk),
        in_specs=[
            pl.BlockSpec((tm, d), lambda i, j, o: (i, 0)),
            pl.BlockSpec((None, None, 1, d), lambda i, j, o: (i // bpr, 1, 0, 0)),
            pl.BlockSpec((None, None, 1, d), lambda i, j, o: (i // bpr, 0, 0, 0)),
            pl.BlockSpec((1, d), lambda i, j, o: (0, 0)),
            pl.BlockSpec((pl.Element(tn), pl.Element(d)), lambda i, j, o: (o[j] * SUBLANES, 0)),
            pl.BlockSpec((LANES, d), lambda i, j, o: (gate_tile, 0)),
        ],
        out_specs=[
            pl.BlockSpec((tm, tn), lambda i, j, o: (i, j)),
            pl.BlockSpec((tm, LANES), lambda i, j, o: (i, 0)),
        ],
        scratch_shapes=[pltpu.VMEM((tm, d), BF16)],
    )
    return pl.pallas_call(
        _inproj_kernel,
        grid_spec=grid_spec,
        out_shape=[
            jax.ShapeDtypeStruct((t, nblk * tn), BF16),
            jax.ShapeDtypeStruct((t, LANES), F32),
        ],
        compiler_params=_cparams(("arbitrary", "arbitrary"), 56),
        name="inproj",
    )(offs, x2, mod4, mod4, norm_g.reshape(1, d), w_in_t, w_in_t)


def _mla_prep_kernel(lat_ref, pos_ref, wuq_ref, wukv_ref, gqa_ref, gkva_ref, gq_ref, gk_ref, inv_ref,
                     qt_ref, k_ref, vt_ref):
    lat = lat_ref[...].astype(F32)
    cq = lat[:, :Q_LORA]
    ckv = lat[:, Q_LORA:Q_LORA + KV_LORA]
    kc = lat[:, SRC_KPE:SRC_KPE + LANES]
    cqn = (cq * _rms_scale(cq, Q_LORA) * gqa_ref[...]).astype(BF16)
    ckvn = (ckv * _rms_scale(ckv, KV_LORA) * gkva_ref[...]).astype(BF16)
    qraw = _dot(cqn, wuq_ref[...])
    kv = _dot(ckvn, wukv_ref[...])

    ang = pos_ref[...].astype(F32) * inv_ref[...]
    lane = lax.broadcasted_iota(jnp.int32, ang.shape, 1)
    lo = lane < MLA_ROPE
    cs = jnp.cos(ang - jnp.where(lo, 0.0, 0.5 * math.pi))

    quarter = MLA_ROPE // 2
    want = jnp.where(lane < 3 * quarter, lane - quarter, lane - 3 * quarter)
    came = pltpu.roll(lane, quarter, 1)
    swapped = jnp.where(came == want, pltpu.roll(kc, quarter, 1), pltpu.roll(kc, 3 * quarter, 1))
    kc = jnp.where(lo, kc, swapped)

    gq = gq_ref[...]
    gk = gk_ref[...]
    gq_n, gq_r = gq[:, :LANES], gq[:, LANES:]
    gk_n, gk_r = gk[:, :LANES], gk[:, LANES:]

    def rope(chunk, g_r):
        a = chunk * (g_r * cs)
        return jnp.where(lo, a + pltpu.roll(a, MLA_ROPE, 1), 0.0)

    kpe_ss = jnp.sum(jnp.where(lo, kc * kc, 0.0), axis=-1, keepdims=True)
    k_rope = rope(kc, gk_r)
    scale = MLA_QK ** -0.5 * math.log2(math.e)
    for h in range(MLA_HEADS):
        kn = kv[:, h * MLA_NOPE:(h + 1) * MLA_NOPE]
        sk = lax.rsqrt((jnp.sum(kn * kn, axis=-1, keepdims=True) + kpe_ss) * (1.0 / MLA_QK) + EPS)
        k_ref[:, h * QK_PAD:h * QK_PAD + LANES] = (kn * sk * gk_n).astype(BF16)
        k_ref[:, h * QK_PAD + LANES:(h + 1) * QK_PAD] = (k_rope * sk).astype(BF16)
        qn = qraw[:, h * QK_PAD:h * QK_PAD + LANES]
        qr = qraw[:, h * QK_PAD + LANES:(h + 1) * QK_PAD]
        ss = jnp.sum(qn * qn, axis=-1, keepdims=True) + jnp.sum(jnp.where(lo, qr * qr, 0.0), axis=-1, keepdims=True)
        sq = lax.rsqrt(ss * (1.0 / MLA_QK) + EPS) * scale
        qt_ref[h * QK_PAD:h * QK_PAD + LANES, :] = (qn * sq * gq_n).T.astype(BF16)
        qt_ref[h * QK_PAD + LANES:(h + 1) * QK_PAD, :] = (rope(qr, gq_r) * sq).T.astype(BF16)
        vh = kv[:, MLA_HEADS * MLA_NOPE + h * MLA_V:MLA_HEADS * MLA_NOPE + (h + 1) * MLA_V]
        vt_ref[h * MLA_V:(h + 1) * MLA_V, :] = vh.T.astype(BF16)


def _mla_prep(proj, pos2, wuq_p, wukv_p, gqa, gkva, gq, gk, inv_lanes, seq):
    t = proj.shape[0]
    tm = min(512, seq)
    hq = MLA_HEADS * QK_PAD
    lat_blk = COL_LAT // 1024
    const = lambda i: (0, 0)
    return pl.pallas_call(
        _mla_prep_kernel,
        grid=(t // tm,),
        in_specs=[
            pl.BlockSpec((tm, 1024), lambda i: (i, lat_blk)),
            pl.BlockSpec((tm, 1), lambda i: (i, 0)),
            pl.BlockSpec(wuq_p.shape, const),
            pl.BlockSpec(wukv_p.shape, const),
            pl.BlockSpec(gqa.shape, const),
            pl.BlockSpec(gkva.shape, const),
            pl.BlockSpec(gq.shape, const),
            pl.BlockSpec(gk.shape, const),
            pl.BlockSpec(inv_lanes.shape, const),
        ],
        out_specs=[
            pl.BlockSpec((hq, tm), lambda i: (0, i)),
            pl.BlockSpec((tm, hq), lambda i: (i, 0)),
            pl.BlockSpec((MLA_W, tm), lambda i: (0, i)),
        ],
        out_shape=[
            jax.ShapeDtypeStruct((hq, t), BF16),
            jax.ShapeDtypeStruct((t, hq), BF16),
            jax.ShapeDtypeStruct((MLA_W, t), BF16),
        ],
        compiler_params=_cparams(("arbitrary",), 48),
        name="mla_prep",
    )(proj, pos2, wuq_p, wukv_p, gqa, gkva, gq, gk, inv_lanes)


def _flash_kernel(qt_ref, k_ref, vt_ref, o_ref, *, tq, tk):
    seq = k_ref.shape[0]
    heads = k_ref.shape[1] // QK_PAD

    def scores(h, k0, q0):
        kj = k_ref[k0:k0 + tk, h * QK_PAD:(h + 1) * QK_PAD]
        return _dot(kj, qt_ref[h * QK_PAD:(h + 1) * QK_PAD, q0:q0 + tq])

    def update(h, state, st, k0, q0):
        m, l, acc = state
        if k0 + tk - 1 > q0:
            key = lax.broadcasted_iota(jnp.int32, st.shape, 0) + k0
            qry = lax.broadcasted_iota(jnp.int32, st.shape, 1) + q0
            st = jnp.where(key <= qry, st, -jnp.inf)
        m_new = jnp.maximum(m, jnp.max(st, axis=0, keepdims=True))
        alpha = jnp.exp2(m - m_new)
        p = jnp.exp2(st - m_new)
        l = alpha * l + jnp.sum(p, axis=0, keepdims=True)
        acc = alpha * acc + _dot(vt_ref[h * MLA_V:(h + 1) * MLA_V, k0:k0 + tk], p.astype(BF16))
        return m_new, l, acc

    steps = [(h, qi * tq, j * tk) for qi in range(seq // tq) for j in range(qi * tq // tk + 1) for h in range(heads)]
    st_next = scores(steps[0][0], steps[0][2], steps[0][1])
    states = {}
    for n, (h, q0, k0) in enumerate(steps):
        st = st_next
        if n + 1 < len(steps):
            hn, qn, kn = steps[n + 1]
            st_next = scores(hn, kn, qn)
        if k0 == 0:
            states[h] = (jnp.full((1, tq), -jnp.inf, F32), jnp.zeros((1, tq), F32), jnp.zeros((MLA_V, tq), F32))
        states[h] = update(h, states[h], st, k0, q0)
        if k0 + tk >= q0 + tq:
            _, l, acc = states[h]
            o_ref[q0:q0 + tq, h * MLA_V:(h + 1) * MLA_V] = (acc / l).T.astype(o_ref.dtype)


def _flash(qt, k3, vt):
    b, seq, _ = k3.shape
    tq = min(512, seq)
    tk = min(512, seq)
    hp = FLASH_HEADS_PER_STEP
    kern = functools.partial(_flash_kernel, tq=tq, tk=tk)
    return pl.pallas_call(
        kern,
        grid=(b, MLA_HEADS // hp),
        in_specs=[
            pl.BlockSpec((hp * QK_PAD, seq), lambda i, h: (h, i)),
            pl.BlockSpec((None, seq, hp * QK_PAD), lambda i, h: (i, 0, h)),
            pl.BlockSpec((hp * MLA_V, seq), lambda i, h: (h, i)),
        ],
        out_specs=pl.BlockSpec((None, seq, hp * MLA_V), lambda i, h: (i, 0, h)),
        out_shape=jax.ShapeDtypeStruct((b, seq, MLA_W), BF16),
        compiler_params=_cparams(("arbitrary", "arbitrary"), 40),
        name="flash",
    )(qt, k3, vt)


def _conv_kernel(cur_ref, halo_ref, w_ref, b_ref, qt_ref, k_ref, buf_ref, *, k_scale):
    tm = cur_ref.shape[0]
    cols = cur_ref.shape[1]
    half = cols // 2
    first = pl.program_id(1) == 0
    halo = halo_ref[...].astype(F32)
    buf_ref[0:SUBLANES, :] = jnp.where(first, 0.0, halo)
    buf_ref[SUBLANES:SUBLANES + tm, :] = cur_ref[...].astype(F32)
    cw = 512
    for c in range(cols // cw):
        sl = slice(c * cw, (c + 1) * cw)
        acc = jnp.zeros((tm, cw), F32) + b_ref[:, sl]
        for j in range(ML_CONV):
            off = SUBLANES - (ML_CONV - 1) + j
            acc = acc + buf_ref[off:off + tm, sl] * w_ref[j:j + 1, sl]
        y = acc * _sigmoid(acc)
        if c * cw < half:
            qt_ref[sl, :] = y.T.astype(qt_ref.dtype)
        else:
            k_ref[:, c * cw - half:(c + 1) * cw - half] = (y * k_scale).astype(k_ref.dtype)


def _conv_silu(proj3, conv_w, conv_b):
    b, seq, _ = proj3.shape
    cols = 2 * ML_HEADS * ML_DQK
    half = cols // 2
    tm = min(512, seq)
    hb = tm // SUBLANES
    kern = functools.partial(_conv_kernel, k_scale=ML_DQK ** -0.5)
    return pl.pallas_call(
        kern,
        grid=(b, seq // tm),
        in_specs=[
            pl.BlockSpec((None, tm, cols), lambda i, s: (i, s, COL_QK // cols)),
            pl.BlockSpec((None, SUBLANES, cols), lambda i, s: (i, jnp.maximum(s * hb - 1, 0), COL_QK // cols)),
            pl.BlockSpec((ML_CONV, cols), lambda i, s: (0, 0)),
            pl.BlockSpec((1, cols), lambda i, s: (0, 0)),
        ],
        out_specs=[
            pl.BlockSpec((None, half, tm), lambda i, s: (i, 0, s)),
            pl.BlockSpec((None, tm, half), lambda i, s: (i, s, 0)),
        ],
        out_shape=[
            jax.ShapeDtypeStruct((b, half, seq), BF16),
            jax.ShapeDtypeStruct((b, seq, half), BF16),
        ],
        scratch_shapes=[pltpu.VMEM((tm + SUBLANES, cols), F32)],
        compiler_params=_cparams(("arbitrary", "arbitrary"), 40),
        name="conv_silu",
    )(proj3, proj3, conv_w, conv_b.reshape(1, cols))


def _log_sigmoid(x):
    return -(jnp.maximum(-x, 0.0) + jnp.log1p(jnp.exp(-jnp.abs(x))))


def _mlstm_kernel(qt_ref, k_ref, v_ref, o_ref, gates_ref, gbias_ref, ng_ref, out_ref, ct_ref, m_ref):
    L = k_ref.shape[0]

    @pl.when(pl.program_id(1) == 0)
    def _():
        ct_ref[...] = jnp.zeros(ct_ref.shape, F32)
        m_ref[...] = jnp.zeros(m_ref.shape, F32)

    g = gates_ref[...] + gbias_ref[...]
    gt = g.T
    lf = _log_sigmoid(g)
    lft = _log_sigmoid(gt)
    r = lax.broadcasted_iota(jnp.int32, (L, L), 0)
    c = lax.broadcasted_iota(jnp.int32, (L, L), 1)
    src_le_qry = r <= c
    tril = (c <= r).astype(F32)
    triu = src_le_qry.astype(F32)
    hi = lax.Precision.HIGHEST
    bcol_all = jnp.dot(tril, lf, preferred_element_type=F32, precision=hi)
    brow_all = jnp.dot(lft, triu, preferred_element_type=F32, precision=hi)
    row = lax.broadcasted_iota(jnp.int32, (ML_DV, L), 0)
    ones_row = jnp.where(row == 0, 1.0, 0.0)

    def lead_matmuls(h):
        hs = slice(h * ML_DQK, (h + 1) * ML_DQK)
        qt = qt_ref[hs, :]
        return _dot(k_ref[:, hs], qt), _dot(ct_ref[h].astype(BF16), qt)

    lead_next = lead_matmuls(0)
    for h in range(ML_HEADS):
        hs = slice(h * ML_DQK, (h + 1) * ML_DQK)
        qk_t, cq_t = lead_next
        if h + 1 < ML_HEADS:
            lead_next = lead_matmuls(h + 1)
        li, lf_ = GATE_LANE + h, GATE_LANE + ML_HEADS + h
        b_row = brow_all[lf_:lf_ + 1, :]
        i_row = gt[li:li + 1, :]
        u_col = g[:, li:li + 1] - bcol_all[:, lf_:lf_ + 1]
        m_prev = m_ref[h][:, :1]
        logw_t = jnp.where(src_le_qry, b_row + u_col, -jnp.inf)
        log_inter = b_row + m_prev
        m_t = jnp.maximum(jnp.max(logw_t, axis=0, keepdims=True), log_inter)
        w_t = jnp.exp(logw_t - m_t)
        a = jnp.exp(log_inter - m_t)
        kh = k_ref[:, hs]
        vt_aug = jnp.concatenate([v_ref[:, hs].astype(F32).T, ones_row], axis=0)
        s_t = qk_t * w_t
        ct = ct_ref[h]
        nd = _dot(vt_aug.astype(BF16), s_t.astype(BF16)) + a * cq_t
        num = nd[:ML_DV, :]
        den = nd[ML_DV:ML_DV + 1, :]
        hout_t = num * (1.0 / jnp.maximum(jnp.abs(den), jnp.exp(-m_t)))
        hn_t = hout_t * lax.rsqrt(jnp.sum(hout_t * hout_t, axis=0, keepdims=True) * (1.0 / ML_DV) + EPS)
        gate = _sigmoid(o_ref[:, hs].astype(F32))
        out_ref[:, hs] = (hn_t.T * ng_ref[:, hs] * gate).astype(out_ref.dtype)

        b_last = b_row[:, L - 1:L]
        logg = b_last - b_row + i_row
        m_new = jnp.maximum(b_last + m_prev, jnp.max(logg, axis=-1, keepdims=True))
        g_row = jnp.exp(logg - m_new)
        decay = jnp.exp(b_last + m_prev - m_new)
        ct_ref[h] = decay * ct + _dot((vt_aug * g_row).astype(BF16), kh)
        m_ref[h] = jnp.broadcast_to(m_new, m_ref.shape[1:])


def _mlstm(qt3, k3, proj3, gates3, gbias, norm_g):
    b, seq, w = k3.shape
    L = min(MLSTM_CHUNK, seq)
    return pl.pallas_call(
        _mlstm_kernel,
        grid=(b, seq // L),
        in_specs=[
            pl.BlockSpec((None, w, L), lambda i, c: (i, 0, c)),
            pl.BlockSpec((None, L, w), lambda i, c: (i, c, 0)),
            pl.BlockSpec((None, L, w), lambda i, c: (i, c, COL_V // w)),
            pl.BlockSpec((None, L, w), lambda i, c: (i, c, COL_O // w)),
            pl.BlockSpec((None, L, LANES), lambda i, c: (i, c, 0)),
            pl.BlockSpec((1, LANES), lambda i, c: (0, 0)),
            pl.BlockSpec((1, w), lambda i, c: (0, 0)),
        ],
        out_specs=pl.BlockSpec((None, L, w), lambda i, c: (i, c, 0)),
        out_shape=jax.ShapeDtypeStruct((b, seq, w), BF16),
        scratch_shapes=[
            pltpu.VMEM((ML_HEADS, 2 * ML_DV, ML_DQK), F32),
            pltpu.VMEM((ML_HEADS, 1, LANES), F32),
        ],
        compiler_params=_cparams(("arbitrary", "arbitrary"), 40),
        name="mlstm",
    )(qt3, k3, proj3, proj3, gates3, gbias, norm_g.reshape(1, w))


def _merge_kernel(a_ref, b_ref, ga_ref, gb_ref, wa_ref, wb_ref, o_ref):
    tm = a_ref.shape[0]
    rows = min(MERGE_ROW_GROUP, tm)
    groups = [slice(r0, r0 + rows) for r0 in range(0, tm, rows)]

    def proj(rs):
        return _dot(a_ref[rs, :], wa_ref[...]), _dot(b_ref[rs, :], wb_ref[...])

    nxt = proj(groups[0])
    for n, rs in enumerate(groups):
        pa, pb = nxt
        if n + 1 < len(groups):
            nxt = proj(groups[n + 1])
        mixed = _sigmoid(ga_ref[rs, :].astype(F32)) * pa + _sigmoid(gb_ref[rs, :].astype(F32)) * pb
        o_ref[rs, :] = mixed.astype(o_ref.dtype)


def _merge(out_a, hm, proj, wa, wb, seq):
    t = out_a.shape[0]
    d = wa.shape[1]
    tm = min(1024, seq)
    tn = 1024
    return pl.pallas_call(
        _merge_kernel,
        grid=(d // tn, t // tm),
        in_specs=[
            pl.BlockSpec((tm, MLA_W), lambda j, i: (i, 0)),
            pl.BlockSpec((tm, ML_W), lambda j, i: (i, 0)),
            pl.BlockSpec((tm, tn), lambda j, i: (i, COL_GA // tn + j)),
            pl.BlockSpec((tm, tn), lambda j, i: (i, COL_GB // tn + j)),
            pl.BlockSpec((MLA_W, tn), lambda j, i: (0, j)),
            pl.BlockSpec((ML_W, tn), lambda j, i: (0, j)),
        ],
        out_specs=pl.BlockSpec((tm, tn), lambda j, i: (i, j)),
        out_shape=jax.ShapeDtypeStruct((t, d), BF16),
        compiler_params=_cparams(("arbitrary", "arbitrary"), 40),
        name="merge",
    )(out_a, hm, proj, proj, wa, wb)


def _outproj_kernel(mix_ref, x_ref, w_ref, gt_ref, sc_ref, sh_ref, g_ref, wr_ref, br_ref, x1_ref, h2_ref, lg_ref):
    tm = mix_ref.shape[0]
    rows = OUTPROJ_ROW_GROUP
    groups = [slice(r0, r0 + rows) for r0 in range(0, tm, rows)]
    y_next = _dot(mix_ref[groups[0], :], w_ref[...])
    for n, rs in enumerate(groups):
        y = y_next
        if n + 1 < len(groups):
            y_next = _dot(mix_ref[groups[n + 1], :], w_ref[...])
        x1 = x_ref[rs, :] + gt_ref[...] * y
        x1_ref[rs, :] = x1
        h2 = x1 * _rms_scale(x1, x1.shape[-1]) * g_ref[...]
        h2 = h2 * (1.0 + sc_ref[...]) + sh_ref[...]
        h2_ref[rs, :] = h2
        h_hi = h2.astype(BF16)
        h_lo = (h2 - h_hi.astype(F32)).astype(BF16)
        r = _dot(h_hi, wr_ref[...]) + _dot(h_lo, wr_ref[...])
        lg_ref[rs, :] = r[:, :LANES] + r[:, LANES:] + br_ref[...]


def _outproj(mixed, x2, w_out, mod4, norm_g, w_route2, b_route, seq):
    t, d = x2.shape
    tm = min(512, seq)
    bpr = seq // tm
    mod_spec = lambda k: pl.BlockSpec((None, None, 1, d), lambda i: (i // bpr, k, 0, 0))
    const = lambda i: (0, 0)
    return pl.pallas_call(
        _outproj_kernel,
        grid=(t // tm,),
        in_specs=[
            pl.BlockSpec((tm, d), lambda i: (i, 0)),
            pl.BlockSpec((tm, d), lambda i: (i, 0)),
            pl.BlockSpec((d, d), const, pipeline_mode=pl.Buffered(1)),
            mod_spec(2),
            mod_spec(4),
            mod_spec(3),
            pl.BlockSpec((1, d), const),
            pl.BlockSpec((d, 2 * LANES), const, pipeline_mode=pl.Buffered(1)),
            pl.BlockSpec((1, LANES), const),
        ],
        out_specs=[
            pl.BlockSpec((tm, d), lambda i: (i, 0)),
            pl.BlockSpec((tm, d), lambda i: (i, 0)),
            pl.BlockSpec((tm, LANES), lambda i: (i, 0)),
        ],
        out_shape=[
            jax.ShapeDtypeStruct((t, d), F32),
            jax.ShapeDtypeStruct((t, d), F32),
            jax.ShapeDtypeStruct((t, LANES), F32),
        ],
        compiler_params=_cparams(("arbitrary",), 56),
        name="outproj",
    )(mixed, x2, w_out, mod4, mod4, mod4, norm_g.reshape(1, d), w_route2, b_route)


def _route_kernel(lg_ref, pos_ref, wts_ref, cnt_ref, carry_ref, offs_ref, meta_s, wts_s):
    phase = pl.program_id(0)
    i = pl.program_id(1)
    tm = lg_ref.shape[0]
    lane = lax.broadcasted_iota(jnp.int32, (tm, LANES), 1)

    @pl.when((phase == 0) & (i == 0))
    def _():
        carry_ref[...] = jnp.zeros(carry_ref.shape, F32)

    @pl.when(phase == 0)
    def _():
        lg = lg_ref[...]
        big = jnp.int32(LANES)
        ninf = -jnp.inf

        def first_argmax(vals):
            mx = jnp.max(vals, axis=-1, keepdims=True)
            idx = jnp.min(jnp.where(vals == mx, lane, big), axis=-1, keepdims=True)
            return mx, idx

        gl = jnp.where(lane < N_GROUPS, lg, ninf)
        gmax, gsel = first_argmax(gl)
        g_w = 1.0 / jnp.sum(jnp.exp(gl - gmax), axis=-1, keepdims=True)
        lo = N_GROUPS + gsel * EXP_PER_GROUP
        in_grp = (lane >= lo) & (lane < lo + EXP_PER_GROUP)
        el = jnp.where(in_grp, lg, ninf)
        e1, i1 = first_argmax(el)
        e2, i2 = first_argmax(jnp.where(lane == i1, ninf, el))
        p2 = jnp.exp(e2 - e1)
        w1 = g_w / (1.0 + p2)
        w2 = g_w * p2 / (1.0 + p2)

        oh1 = lane == i1
        oh2 = lane == i2
        oh = jnp.where(oh1 | oh2, 1.0, 0.0)
        r = lax.broadcasted_iota(jnp.int32, (tm, tm), 0)
        c = lax.broadcasted_iota(jnp.int32, (tm, tm), 1)
        strict = jnp.where(c < r, 1.0, 0.0).astype(BF16)
        before = _dot(strict, oh.astype(BF16)) + carry_ref[...]
        rank1 = jnp.sum(jnp.where(oh1, before, 0.0), axis=-1, keepdims=True).astype(jnp.int32)
        rank2 = jnp.sum(jnp.where(oh2, before, 0.0), axis=-1, keepdims=True).astype(jnp.int32)
        carry_ref[...] = carry_ref[...] + jnp.sum(oh, axis=0, keepdims=True)
        meta_s[i] = jnp.where(lane == 0, i1, jnp.where(lane == 1, i2, jnp.where(lane == 2, rank1, jnp.where(lane == 3, rank2, 0))))
        wts_s[i] = jnp.where(lane == 0, w1, jnp.where(lane == 1, w2, 0.0))

    @pl.when((phase == 1) & (i == 0))
    def _():
        cnt = carry_ref[...]
        cnt_ref[...] = jnp.broadcast_to(cnt, cnt_ref.shape)
        padded = jnp.ceil(cnt * (1.0 / MOE_TILE)) * MOE_TILE
        r = lax.broadcasted_iota(jnp.int32, (LANES, LANES), 0)
        c = lax.broadcasted_iota(jnp.int32, (LANES, LANES), 1)
        upper = jnp.where(r < c, 1.0, 0.0).astype(BF16)
        padded8 = jnp.broadcast_to(padded, (SUBLANES, LANES)).astype(BF16)
        offs_ref[...] = _dot(padded8, upper)[:1, :]

    @pl.when(phase == 1)
    def _():
        meta = meta_s[i]
        offs = offs_ref[...]
        off1 = jnp.sum(jnp.where(lane == meta[:, 0:1], offs, 0.0), axis=-1, keepdims=True).astype(jnp.int32)
        off2 = jnp.sum(jnp.where(lane == meta[:, 1:2], offs, 0.0), axis=-1, keepdims=True).astype(jnp.int32)
        pos1 = off1 + meta[:, 2:3]
        pos2 = off2 + meta[:, 3:4]
        pos_ref[...] = jnp.where(lane == 0, pos1, jnp.where(lane == 1, pos2, 0))
        wts_ref[...] = wts_s[i]


def _route(logits):
    t = logits.shape[0]
    tm = min(512, t)
    nb = t // tm
    return pl.pallas_call(
        _route_kernel,
        grid=(2, nb),
        in_specs=[pl.BlockSpec((tm, LANES), lambda p, i: (i * (1 - p), 0))],
        out_specs=[
            pl.BlockSpec((tm, LANES), lambda p, i: (i * p, 0)),
            pl.BlockSpec((tm, LANES), lambda p, i: (i * p, 0)),
            pl.BlockSpec((SUBLANES, LANES), lambda p, i: (0, 0)),
        ],
        out_shape=[
            jax.ShapeDtypeStruct((t, LANES), jnp.int32),
            jax.ShapeDtypeStruct((t, LANES), F32),
            jax.ShapeDtypeStruct((SUBLANES, LANES), F32),
        ],
        scratch_shapes=[
            pltpu.VMEM((1, LANES), F32),
            pltpu.VMEM((1, LANES), F32),
            pltpu.VMEM((nb, tm, LANES), jnp.int32),
            pltpu.VMEM((nb, tm, LANES), F32),
        ],
        compiler_params=_cparams(("arbitrary", "arbitrary"), 32),
        name="route",
    )(logits)


def _row_gather_start(idx_at, src_hbm, dst, sem, rows):
    for r in range(rows):
        pltpu.make_async_copy(src_hbm.at[pl.ds(idx_at(r), 1)], dst.at[pl.ds(r, 1)], sem).start(priority=r % 2)


def _row_gather_wait(src_hbm, dst, sem, rows):
    pltpu.make_async_copy(src_hbm.at[pl.ds(0, rows)], dst, sem).wait()


def _dispatch_kernel(pos_ref, zrow_ref, h_ref, xs_hbm, zbuf, sems):
    i = pl.program_id(0)
    tm = h_ref.shape[0]
    tile = zbuf.shape[0]

    def zero_copy(z):
        return pltpu.make_async_copy(zbuf, xs_hbm.at[pl.ds(pl.multiple_of(zrow_ref[z], tile), tile)], sems.at[0])

    @pl.when(i == 0)
    def _():
        zbuf[...] = jnp.zeros(zbuf.shape, zbuf.dtype)
        for z in range(zrow_ref.shape[0]):
            @pl.when(zrow_ref[z] >= 0)
            def _():
                zero_copy(z).start()
        for z in range(zrow_ref.shape[0]):
            @pl.when(zrow_ref[z] >= 0)
            def _():
                zero_copy(z).wait()

    for k in range(2):
        base = (2 * i + k) * tm
        for r in range(tm):
            pltpu.make_async_copy(h_ref.at[pl.ds(r, 1)], xs_hbm.at[pl.ds(pos_ref[base + r], 1)],
                                  sems.at[1]).start(priority=r % 2)
    for k in range(2):
        pltpu.make_async_copy(h_ref, xs_hbm.at[pl.ds(0, tm)], sems.at[1]).wait()


def _dispatch(pos, zrow, h2, n_rows, seq):
    t, d = h2.shape
    tm = min(1024, seq)
    pos_blocks = pos.reshape(t // tm, tm, 2).transpose(0, 2, 1).reshape(-1)
    grid_spec = pltpu.PrefetchScalarGridSpec(
        num_scalar_prefetch=2,
        grid=(t // tm,),
        in_specs=[pl.BlockSpec((tm, d), lambda i, p, z: (i, 0))],
        out_specs=pl.BlockSpec(memory_space=pl.ANY),
        scratch_shapes=[
            pltpu.VMEM((MOE_TILE, d), F32),
            pltpu.SemaphoreType.DMA((2,)),
        ],
    )
    return pl.pallas_call(
        _dispatch_kernel,
        grid_spec=grid_spec,
        out_shape=jax.ShapeDtypeStruct((n_rows, d), F32),
        compiler_params=_cparams(("arbitrary",), 40),
        name="moe_dispatch",
    )(pos_blocks, zrow, h2)


def _moe_kernel(plan_ref, x_ref, wg_hbm, wu_hbm, wd_hbm, y_ref, stg_g, stg_u, stg_d, wsem, wb_g, wb_u, wb_d):
    j = pl.program_id(0)
    nt = plan_ref[3, 0]

    def w_copies(e, s):
        return (pltpu.make_async_copy(wg_hbm.at[e], stg_g.at[s], wsem.at[s]),
                pltpu.make_async_copy(wu_hbm.at[e], stg_u.at[s], wsem.at[s]),
                pltpu.make_async_copy(wd_hbm.at[e], stg_d.at[s], wsem.at[s]))

    @pl.when(j == 0)
    def _():
        for cp in w_copies(plan_ref[3, 1], 0):
            cp.start()

        @pl.when(plan_ref[3, 2] >= 0)
        def _():
            for cp in w_copies(plan_ref[3, 2], 1):
                cp.start()

    @pl.when((j < nt) & (plan_ref[0, j] == 1))
    def _():
        s = plan_ref[1, j]
        for cp in w_copies(0, s):
            cp.wait()
        wb_g[...] = stg_g[s].astype(BF16)
        wb_u[...] = stg_u[s].astype(BF16)
        wb_d[...] = stg_d[s].astype(BF16)

        @pl.when(plan_ref[2, j] >= 0)
        def _():
            for cp in w_copies(plan_ref[2, j], s):
                cp.start()

    @pl.when(j < nt)
    def _():
        x = x_ref[...].astype(BF16)
        hg = _dot(x, wb_g[...])
        hu = _dot(x, wb_u[...])
        act = (hg * _sigmoid(hg) * hu).astype(BF16)
        y_ref[...] = _dot(act, wb_d[...])

    @pl.when(j >= nt)
    def _():
        y_ref[...] = jnp.zeros(y_ref.shape, y_ref.dtype)


def _moe(plan, xs, w_gate_e, w_up_e, w_down_e):
    d = xs.shape[1]
    f = w_gate_e.shape[-1]
    nt_max = plan.shape[1]
    tile = MOE_TILE
    any_spec = pl.BlockSpec(memory_space=pl.ANY)
    grid_spec = pltpu.PrefetchScalarGridSpec(
        num_scalar_prefetch=1,
        grid=(nt_max,),
        in_specs=[
            pl.BlockSpec((tile, d), lambda j, plan_: (jnp.minimum(j, plan_[3, 0] - 1), 0)),
            any_spec, any_spec, any_spec,
        ],
        out_specs=pl.BlockSpec((tile, d), lambda j, plan_: (j, 0)),
        scratch_shapes=[
            pltpu.VMEM((2, d, f), F32),
            pltpu.VMEM((2, d, f), F32),
            pltpu.VMEM((2, f, d), F32),
            pltpu.SemaphoreType.DMA((2,)),
            pltpu.VMEM((d, f), BF16),
            pltpu.VMEM((d, f), BF16),
            pltpu.VMEM((f, d), BF16),
        ],
    )
    return pl.pallas_call(
        _moe_kernel,
        grid_spec=grid_spec,
        out_shape=jax.ShapeDtypeStruct((nt_max * tile, d), F32),
        compiler_params=_cparams(("arbitrary",), 56),
        name="moe_experts",
    )(plan, xs, w_gate_e, w_up_e, w_down_e)


def _combine_kernel(pos_ref, y_hbm, x1_ref, gt_ref, wts_ref, o_ref, ybuf, sem):
    i = pl.program_id(0)
    n = pl.num_programs(0)
    tm = x1_ref.shape[0]
    slot = lax.rem(i, 2)

    def start(blk, s):
        for k in range(2):
            base = (2 * blk + k) * tm
            _row_gather_start(lambda r, base=base: pos_ref[base + r], y_hbm, ybuf.at[s, k], sem.at[s], tm)

    @pl.when(i == 0)
    def _():
        start(0, 0)

        @pl.when(n > 1)
        def _():
            start(1, 1)

    for k in range(2):
        _row_gather_wait(y_hbm, ybuf.at[slot, k], sem.at[slot], tm)
    w = wts_ref[...]
    moe = w[:, 0:1] * ybuf[slot, 0] + w[:, 1:2] * ybuf[slot, 1]
    o_ref[...] = x1_ref[...] + gt_ref[...] * moe

    @pl.when(i + 2 < n)
    def _():
        start(i + 2, slot)


def _combine(pos, ys, x1, mod4, wts, seq):
    t, d = x1.shape
    tm = min(256, seq)
    bpr = seq // tm
    pos_blocks = pos.reshape(t // tm, tm, 2).transpose(0, 2, 1).reshape(-1)
    grid_spec = pltpu.PrefetchScalarGridSpec(
        num_scalar_prefetch=1,
        grid=(t // tm,),
        in_specs=[
            pl.BlockSpec(memory_space=pl.ANY),
            pl.BlockSpec((tm, d), lambda i, p: (i, 0)),
            pl.BlockSpec((None, None, 1, d), lambda i, p: (i // bpr, 5, 0, 0)),
            pl.BlockSpec((tm, LANES), lambda i, p: (i, 0)),
        ],
        out_specs=pl.BlockSpec((tm, d), lambda i, p: (i, 0)),
        scratch_shapes=[
            pltpu.VMEM((2, 2, tm, d), F32),
            pltpu.SemaphoreType.DMA((2,)),
        ],
    )
    return pl.pallas_call(
        _combine_kernel,
        grid_spec=grid_spec,
        out_shape=jax.ShapeDtypeStruct((t, d), F32),
        compiler_params=_cparams(("arbitrary",), 48),
        name="moe_combine",
    )(pos_blocks, ys, x1, mod4, wts)


def _swap_halves(w):
    half = w.shape[-1] // 2
    return jnp.concatenate([w[..., half:], w[..., :half]], axis=-1)


def _layout_w_uq(w_uq):
    r = w_uq.shape[0]
    w = w_uq.reshape(r, MLA_HEADS, MLA_QK)
    rope = w[..., MLA_NOPE:]
    return jnp.concatenate([w[..., :MLA_NOPE], rope, _swap_halves(rope)], axis=-1).reshape(r, MLA_HEADS * QK_PAD).astype(BF16)


def _layout_w_ukv(w_ukv):
    r = w_ukv.shape[0]
    w = w_ukv.reshape(r, MLA_HEADS, MLA_NOPE + MLA_V)
    return jnp.concatenate([w[..., :MLA_NOPE].reshape(r, -1), w[..., MLA_NOPE:].reshape(r, -1)], axis=-1).astype(BF16)


def _rope_gain(g):
    g1 = g[MLA_NOPE:MLA_NOPE + MLA_ROPE // 2]
    g2 = g[MLA_NOPE + MLA_ROPE // 2:]
    return jnp.concatenate([g[:MLA_NOPE], g1, g2, -g2, g1]).reshape(1, QK_PAD)


def _layer(x2, cond_mod4, pos2, seq, w_in, q_a_norm_g, w_uq, kv_a_norm_g, w_ukv, q_norm_g, k_norm_g, conv_w, conv_b,
           b_mlstm_gates, mlstm_norm_g, w_proj_a, w_proj_b, w_out, norm_mix_g, norm_ffn_g, w_group, b_group,
           w_router, b_router, w_gate_e, w_up_e, w_down_e):
    t, d = x2.shape
    b = t // seq
    mod4 = cond_mod4

    proj, gates = _inproj(x2, mod4, norm_mix_g, w_in.T, seq)

    inv = ROPE_THETA ** (-jnp.arange(0, MLA_ROPE, 2, dtype=F32) / MLA_ROPE)
    inv_lanes = jnp.tile(inv, LANES // (MLA_ROPE // 2)).reshape(1, LANES)
    qt, k, vt = _mla_prep(proj, pos2, _layout_w_uq(w_uq), _layout_w_ukv(w_ukv), q_a_norm_g.reshape(1, -1),
                        kv_a_norm_g.reshape(1, -1), _rope_gain(q_norm_g), _rope_gain(k_norm_g), inv_lanes, seq)
    out_a = _flash(qt, k.reshape(b, seq, -1), vt).reshape(t, MLA_W)

    proj3 = proj.reshape(b, seq, -1)
    ml_qt, ml_k = _conv_silu(proj3, conv_w, conv_b)
    gbias = jnp.zeros((1, LANES), F32).at[0, GATE_LANE:GATE_LANE + 2 * ML_HEADS].set(b_mlstm_gates.reshape(-1))
    hm = _mlstm(ml_qt, ml_k, proj3, gates.reshape(b, seq, LANES), gbias, mlstm_norm_g).reshape(t, ML_W)

    mixed = _merge(out_a, hm, proj, w_proj_a.astype(BF16), w_proj_b.astype(BF16), seq)

    w_route = jnp.zeros((d, LANES), F32).at[:, :N_GROUPS].set(w_group).at[:, N_GROUPS:N_GROUPS + N_EXPERTS].set(w_router)
    b_route = jnp.zeros((1, LANES), F32).at[0, :N_GROUPS].set(b_group).at[0, N_GROUPS:N_GROUPS + N_EXPERTS].set(b_router)
    w_route_hi = w_route.astype(BF16)
    w_route2 = jnp.concatenate([w_route_hi, (w_route - w_route_hi.astype(F32)).astype(BF16)], axis=1)
    x1, h2, logits = _outproj(mixed, x2, w_out.astype(BF16), mod4, norm_ffn_g, w_route2, b_route, seq)

    posm, wts, counts = _route(logits)
    pos = posm[:, 0:2]

    tile = MOE_TILE
    i32 = jnp.int32
    cnt = counts[0, N_GROUPS:N_GROUPS + N_EXPERTS].astype(i32)
    padded = ((cnt + tile - 1) // tile) * tile
    ends = jnp.cumsum(padded)
    offs = ends - padded
    nt_max = (2 * t) // tile + N_EXPERTS
    n_tiles = ends[-1] // tile
    tile_idx = jnp.arange(nt_max, dtype=i32)
    tile_start = tile_idx * tile
    live = tile_idx < n_tiles
    texp = jnp.minimum(jnp.sum((ends[None, :] <= tile_start[:, None]).astype(i32), axis=1), N_EXPERTS - 1)
    active = cnt > 0
    order = jnp.cumsum(active.astype(i32)) - 1
    n_active = jnp.sum(active.astype(i32))
    experts = jnp.arange(N_EXPERTS, dtype=i32)
    by_order = jnp.sum(jnp.where(active[None, :] & (order[None, :] == experts[:, None]), experts[None, :], 0), axis=1)
    t_order = order[texp]
    first = (live & (tile_start == offs[texp])).astype(i32)
    ahead = t_order + 2
    prefetch = jnp.where(ahead < n_active, by_order[jnp.minimum(ahead, N_EXPERTS - 1)], -1)
    head = jnp.zeros((nt_max,), i32).at[0].set(n_tiles).at[1].set(by_order[0]).at[2].set(
        jnp.where(n_active > 1, by_order[1], -1))
    plan = jnp.stack([first, t_order % 2, prefetch, head]).astype(i32)

    last_tile = jnp.where(padded > cnt, ends - tile, -1)
    spare_idx = n_tiles + jnp.arange(N_EXPERTS, dtype=i32)
    spare_tile = jnp.where(spare_idx < nt_max, spare_idx * tile, -1)
    zrow = jnp.concatenate([last_tile, spare_tile]).astype(i32)

    xs = _dispatch(pos, zrow, h2, nt_max * tile, seq)
    ys = _moe(plan, xs, w_gate_e, w_up_e, w_down_e)

    return _combine(pos, ys, x1, mod4, wts, seq)


def kernel(x, c, positions, w_ada, b_ada, norm_mix_g, w_in, q_a_norm_g, w_uq, kv_a_norm_g, w_ukv, q_norm_g, k_norm_g, conv_w, conv_b, b_mlstm_gates, mlstm_norm_g, w_proj_a, w_proj_b, w_out, norm_ffn_g, w_group, b_group, w_router, b_router, w_gate_e, w_up_e, w_down_e):
    b, seq, d = x.shape
    depth = w_ada.shape[0]
    x2 = x.reshape(b * seq, d)
    pos2 = positions.reshape(b * seq, 1)
    c_pad = jnp.zeros((SUBLANES, d), F32).at[:b].set(c)
    for l in range(depth):
        mod = _adaln(c_pad, w_ada[l], b_ada[l])
        mod4 = mod[:b].reshape(b, 6, 1, d)
        x2 = _layer(x2, mod4, pos2, seq, w_in[l], q_a_norm_g[l], w_uq[l], kv_a_norm_g[l], w_ukv[l], q_norm_g[l],
                    k_norm_g[l], conv_w[l], conv_b[l], b_mlstm_gates[l], mlstm_norm_g[l], w_proj_a[l], w_proj_b[l],
                    w_out[l], norm_mix_g[l], norm_ffn_g[l], w_group[l], b_group[l], w_router[l], b_router[l],
                    w_gate_e[l], w_up_e[l], w_down_e[l])
    return x2.reshape(b, seq, d)
```

```python
import functools
import math

import jax
import jax.numpy as jnp
from jax import lax
from jax.experimental import pallas as pl
from jax.experimental.pallas import tpu as pltpu

F32 = jnp.float32
BF16 = jnp.bfloat16

LANES = 128
SUBLANES = 8

D_MODEL = 2048
MLA_HEADS = 8
MLA_NOPE = 128
MLA_ROPE = 64
MLA_QK = MLA_NOPE + MLA_ROPE
MLA_V = 128
Q_LORA = 512
KV_LORA = 256
ROPE_THETA = 10000.0
ML_HEADS = 8
ML_DQK = 128
ML_DV = 128
ML_CONV = 4
MLA_W = MLA_HEADS * MLA_V
ML_W = ML_HEADS * ML_DV
N_GROUPS = 4
EXP_PER_GROUP = 8
N_EXPERTS = N_GROUPS * EXP_PER_GROUP
D_FF_EXPERT = 512
EPS = 1e-6

QK_PAD = 2 * LANES

SRC_CQ = 0
SRC_KPE = Q_LORA + KV_LORA
SRC_QK = SRC_KPE + MLA_ROPE
SRC_V = SRC_QK + 2 * ML_HEADS * ML_DQK
SRC_O = SRC_V + ML_W
SRC_I = SRC_O + ML_W
SRC_GA = SRC_I + 2 * ML_HEADS
SRC_GB = SRC_GA + D_MODEL
GATE_LANE = SRC_I % LANES

IN_BLOCK = 1024
IN_BLOCK_SRC = (SRC_QK, SRC_QK + IN_BLOCK, SRC_GA, SRC_GA + IN_BLOCK, SRC_GB, SRC_GB + IN_BLOCK, SRC_V, SRC_O, SRC_CQ)
COL_QK = 0
COL_GA = 2048
COL_GB = 4096
COL_V = 6144
COL_O = 7168
COL_LAT = 8192
LAT_W = Q_LORA + KV_LORA + MLA_ROPE

MLSTM_CHUNK = 256
MOE_TILE = 256
MOE_WEIGHT_SLOTS = 3
FLASH_HEADS_PER_STEP = 2
OUTPROJ_ROW_GROUP = 256
MERGE_ROW_GROUP = 256


def _cparams(sem, vmem_mb):
    return pltpu.CompilerParams(dimension_semantics=sem, vmem_limit_bytes=vmem_mb * 1024 * 1024)


def _dot(a, b):
    return jnp.dot(a, b, preferred_element_type=F32)


def _dot_nt(a, b):
    return lax.dot_general(a, b, (((1,), (1,)), ((), ())), preferred_element_type=F32)


def _dot_tn(a, b):
    return lax.dot_general(a, b, (((0,), (0,)), ((), ())), preferred_element_type=F32)


def _sigmoid(x):
    return 1.0 / (1.0 + jnp.exp(-x))


def _rms_scale(x, width):
    return lax.rsqrt(jnp.sum(x * x, axis=-1, keepdims=True) * (1.0 / width) + EPS)


def _adaln_kernel(c_ref, w_ref, b_ref, o_ref):
    c = c_ref[...]
    cond = (c * _sigmoid(c)).astype(BF16)
    o_ref[...] = _dot(cond, w_ref[...].astype(BF16)) + b_ref[...]


def _adaln(c_pad, w_ada, b_ada):
    rows, d = c_pad.shape
    n = w_ada.shape[1]
    tn = 2048
    return pl.pallas_call(
        _adaln_kernel,
        grid=(n // tn,),
        in_specs=[
            pl.BlockSpec((rows, d), lambda j: (0, 0)),
            pl.BlockSpec((d, tn), lambda j: (0, j)),
            pl.BlockSpec((1, tn), lambda j: (0, j)),
        ],
        out_specs=pl.BlockSpec((rows, tn), lambda j: (0, j)),
        out_shape=jax.ShapeDtypeStruct((rows, n), F32),
        compiler_params=_cparams(("arbitrary",), 48),
        name="adaln",
    )(c_pad, w_ada, b_ada.reshape(1, n))


def _inproj_kernel(off_ref, x_ref, sc_ref, sh_ref, g_ref, w_ref, wgate_ref, proj_ref, gates_ref, h_ref):
    j = pl.program_id(1)

    @pl.when(j == 0)
    def _():
        x = x_ref[...]
        h = x * _rms_scale(x, x.shape[-1]) * g_ref[...]
        h = (h * (1.0 + sc_ref[...]) + sh_ref[...]).astype(BF16)
        h_ref[...] = h
        gates_ref[...] = _dot_nt(h, wgate_ref[...].astype(BF16))

    proj_ref[...] = _dot_nt(h_ref[...], w_ref[...].astype(BF16)).astype(proj_ref.dtype)


def _inproj(x2, mod4, norm_g, w_in_t, seq):
    t, d = x2.shape
    tm = min(1024, seq)
    tn = IN_BLOCK
    bpr = seq // tm
    assert all(o % SUBLANES == 0 for o in IN_BLOCK_SRC)
    offs = jnp.asarray([o // SUBLANES for o in IN_BLOCK_SRC], jnp.int32)
    nblk = len(IN_BLOCK_SRC)
    gate_tile = SRC_I // LANES
    grid_spec = pltpu.PrefetchScalarGridSpec(
        num_scalar_prefetch=1,
        grid=(t // tm, nblk),
        in_specs=[
            pl.BlockSpec((tm, d), lambda i, j, o: (i, 0)),
            pl.BlockSpec((None, None, 1, d), lambda i, j, o: (i // bpr, 1, 0, 0)),
            pl.BlockSpec((None, None, 1, d), lambda i, j, o: (i // bpr, 0, 0, 0)),
            pl.BlockSpec((1, d), lambda i, j, o: (0, 0)),
            pl.BlockSpec((pl.Element(tn), pl.Element(d)), lambda i, j, o: (o[j] * SUBLANES, 0)),
            pl.BlockSpec((LANES, d), lambda i, j, o: (gate_tile, 0)),
        ],
        out_specs=[
            pl.BlockSpec((tm, tn), lambda i, j, o: (i, j)),
            pl.BlockSpec((tm, LANES), lambda i, j, o: (i, 0)),
        ],
        scratch_shapes=[pltpu.VMEM((tm, d), BF16)],
    )
    return pl.pallas_call(
        _inproj_kernel,
        grid_spec=grid_spec,
        out_shape=[
            jax.ShapeDtypeStruct((t, nblk * tn), BF16),
            jax.ShapeDtypeStruct((t, LANES), F32),
        ],
        compiler_params=_cparams(("arbitrary", "arbitrary"), 56),
        name="inproj",
    )(offs, x2, mod4, mod4, norm_g.reshape(1, d), w_in_t, w_in_t)


def _mla_prep_kernel(lat_ref, pos_ref, wuq_ref, wukv_ref, gqa_ref, gkva_ref, gq_ref, gk_ref, inv_ref,
                     qt_ref, k_ref, vt_ref):
    lat = lat_ref[...].astype(F32)
    cq = lat[:, :Q_LORA]
    ckv = lat[:, Q_LORA:Q_LORA + KV_LORA]
    kc = lat[:, SRC_KPE:SRC_KPE + LANES]
    cqn = (cq * _rms_scale(cq, Q_LORA) * gqa_ref[...]).astype(BF16)
    ckvn = (ckv * _rms_scale(ckv, KV_LORA) * gkva_ref[...]).astype(BF16)
    qraw = _dot(cqn, wuq_ref[...])
    kv = _dot(ckvn, wukv_ref[...])

    ang = pos_ref[...].astype(F32) * inv_ref[...]
    lane = lax.broadcasted_iota(jnp.int32, ang.shape, 1)
    lo = lane < MLA_ROPE
    cs = jnp.cos(ang - jnp.where(lo, 0.0, 0.5 * math.pi))

    quarter = MLA_ROPE // 2
    want = jnp.where(lane < 3 * quarter, lane - quarter, lane - 3 * quarter)
    came = pltpu.roll(lane, quarter, 1)
    swapped = jnp.where(came == want, pltpu.roll(kc, quarter, 1), pltpu.roll(kc, 3 * quarter, 1))
    kc = jnp.where(lo, kc, swapped)

    gq = gq_ref[...]
    gk = gk_ref[...]
    gq_n, gq_r = gq[:, :LANES], gq[:, LANES:]
    gk_n, gk_r = gk[:, :LANES], gk[:, LANES:]

    def rope(chunk, g_r):
        a = chunk * (g_r * cs)
        return jnp.where(lo, a + pltpu.roll(a, MLA_ROPE, 1), 0.0)

    kpe_ss = jnp.sum(jnp.where(lo, kc * kc, 0.0), axis=-1, keepdims=True)
    k_rope = rope(kc, gk_r)
    scale = MLA_QK ** -0.5 * math.log2(math.e)
    for h in range(MLA_HEADS):
        kn = kv[:, h * MLA_NOPE:(h + 1) * MLA_NOPE]
        sk = lax.rsqrt((jnp.sum(kn * kn, axis=-1, keepdims=True) + kpe_ss) * (1.0 / MLA_QK) + EPS)
        k_ref[:, h * QK_PAD:h * QK_PAD + LANES] = (kn * sk * gk_n).astype(BF16)
        k_ref[:, h * QK_PAD + LANES:(h + 1) * QK_PAD] = (k_rope * sk).astype(BF16)
        qn = qraw[:, h * QK_PAD:h * QK_PAD + LANES]
        qr = qraw[:, h * QK_PAD + LANES:(h + 1) * QK_PAD]
        ss = jnp.sum(qn * qn, axis=-1, keepdims=True) + jnp.sum(jnp.where(lo, qr * qr, 0.0), axis=-1, keepdims=True)
        sq = lax.rsqrt(ss * (1.0 / MLA_QK) + EPS) * scale
        qt_ref[h * QK_PAD:h * QK_PAD + LANES, :] = (qn * sq * gq_n).astype(BF16).T
        qt_ref[h * QK_PAD + LANES:(h + 1) * QK_PAD, :] = (rope(qr, gq_r) * sq).astype(BF16).T
        vh = kv[:, MLA_HEADS * MLA_NOPE + h * MLA_V:MLA_HEADS * MLA_NOPE + (h + 1) * MLA_V]
        vt_ref[h * MLA_V:(h + 1) * MLA_V, :] = vh.astype(BF16).T


def _mla_prep(proj, pos2, wuq_p, wukv_p, gqa, gkva, gq, gk, inv_lanes, seq):
    t = proj.shape[0]
    tm = min(512, seq)
    hq = MLA_HEADS * QK_PAD
    lat_blk = COL_LAT // 1024
    const = lambda i: (0, 0)
    return pl.pallas_call(
        _mla_prep_kernel,
        grid=(t // tm,),
        in_specs=[
            pl.BlockSpec((tm, 1024), lambda i: (i, lat_blk)),
            pl.BlockSpec((tm, 1), lambda i: (i, 0)),
            pl.BlockSpec(wuq_p.shape, const),
            pl.BlockSpec(wukv_p.shape, const),
            pl.BlockSpec(gqa.shape, const),
            pl.BlockSpec(gkva.shape, const),
            pl.BlockSpec(gq.shape, const),
            pl.BlockSpec(gk.shape, const),
            pl.BlockSpec(inv_lanes.shape, const),
        ],
        out_specs=[
            pl.BlockSpec((hq, tm), lambda i: (0, i)),
            pl.BlockSpec((tm, hq), lambda i: (i, 0)),
            pl.BlockSpec((MLA_W, tm), lambda i: (0, i)),
        ],
        out_shape=[
            jax.ShapeDtypeStruct((hq, t), BF16),
            jax.ShapeDtypeStruct((t, hq), BF16),
            jax.ShapeDtypeStruct((MLA_W, t), BF16),
        ],
        compiler_params=_cparams(("arbitrary",), 48),
        name="mla_prep",
    )(proj, pos2, wuq_p, wukv_p, gqa, gkva, gq, gk, inv_lanes)


def _flash_kernel(qt_ref, k_ref, vt_ref, o_ref, *, tq, tk):
    seq = k_ref.shape[0]
    heads = k_ref.shape[1] // QK_PAD

    def scores(h, k0, q0):
        kj = k_ref[k0:k0 + tk, h * QK_PAD:(h + 1) * QK_PAD]
        return _dot(kj, qt_ref[h * QK_PAD:(h + 1) * QK_PAD, q0:q0 + tq])

    def update(h, state, st, k0, q0):
        m, l, acc = state
        if k0 + tk - 1 > q0:
            key = lax.broadcasted_iota(jnp.int32, st.shape, 0) + k0
            qry = lax.broadcasted_iota(jnp.int32, st.shape, 1) + q0
            st = jnp.where(key <= qry, st, -jnp.inf)
        m_new = jnp.maximum(m, jnp.max(st, axis=0, keepdims=True))
        alpha = jnp.exp2(m - m_new)
        p = jnp.exp2(st - m_new)
        l = alpha * l + jnp.sum(p, axis=0, keepdims=True)
        acc = alpha * acc + _dot(vt_ref[h * MLA_V:(h + 1) * MLA_V, k0:k0 + tk], p.astype(BF16))
        return m_new, l, acc

    steps = [(h, qi * tq, j * tk) for qi in range(seq // tq) for j in range(qi * tq // tk + 1) for h in range(heads)]
    st_next = scores(steps[0][0], steps[0][2], steps[0][1])
    states = {}
    for n, (h, q0, k0) in enumerate(steps):
        st = st_next
        if n + 1 < len(steps):
            hn, qn, kn = steps[n + 1]
            st_next = scores(hn, kn, qn)
        if k0 == 0:
            states[h] = (jnp.full((1, tq), -jnp.inf, F32), jnp.zeros((1, tq), F32), jnp.zeros((MLA_V, tq), F32))
        states[h] = update(h, states[h], st, k0, q0)
        if k0 + tk >= q0 + tq:
            _, l, acc = states[h]
            o_ref[q0:q0 + tq, h * MLA_V:(h + 1) * MLA_V] = (acc / l).T.astype(o_ref.dtype)


def _flash(qt, k3, vt):
    b, seq, _ = k3.shape
    tq = min(512, seq)
    tk = min(512, seq)
    hp = FLASH_HEADS_PER_STEP
    kern = functools.partial(_flash_kernel, tq=tq, tk=tk)
    return pl.pallas_call(
        kern,
        grid=(b, MLA_HEADS // hp),
        in_specs=[
            pl.BlockSpec((hp * QK_PAD, seq), lambda i, h: (h, i)),
            pl.BlockSpec((None, seq, hp * QK_PAD), lambda i, h: (i, 0, h)),
            pl.BlockSpec((hp * MLA_V, seq), lambda i, h: (h, i)),
        ],
        out_specs=pl.BlockSpec((None, seq, hp * MLA_V), lambda i, h: (i, 0, h)),
        out_shape=jax.ShapeDtypeStruct((b, seq, MLA_W), BF16),
        compiler_params=_cparams(("arbitrary", "arbitrary"), 40),
        name="flash",
    )(qt, k3, vt)


def _conv_kernel(cur_ref, halo_ref, w_ref, b_ref, qt_ref, k_ref, buf_ref, *, k_scale):
    tm = cur_ref.shape[0]
    cols = cur_ref.shape[1]
    half = cols // 2
    first = pl.program_id(1) == 0
    halo = halo_ref[...].astype(F32)
    buf_ref[0:SUBLANES, :] = jnp.where(first, 0.0, halo)
    buf_ref[SUBLANES:SUBLANES + tm, :] = cur_ref[...].astype(F32)
    cw = 512
    for c in range(cols // cw):
        sl = slice(c * cw, (c + 1) * cw)
        acc = jnp.zeros((tm, cw), F32) + b_ref[:, sl]
        for j in range(ML_CONV):
            off = SUBLANES - (ML_CONV - 1) + j
            acc = acc + buf_ref[off:off + tm, sl] * w_ref[j:j + 1, sl]
        y = acc * _sigmoid(acc)
        if c * cw < half:
            qt_ref[sl, :] = y.astype(qt_ref.dtype).T
        else:
            k_ref[:, c * cw - half:(c + 1) * cw - half] = (y * k_scale).astype(k_ref.dtype)


def _conv_silu(proj3, conv_w, conv_b):
    b, seq, _ = proj3.shape
    cols = 2 * ML_HEADS * ML_DQK
    half = cols // 2
    tm = min(512, seq)
    hb = tm // SUBLANES
    kern = functools.partial(_conv_kernel, k_scale=ML_DQK ** -0.5)
    return pl.pallas_call(
        kern,
        grid=(b, seq // tm),
        in_specs=[
            pl.BlockSpec((None, tm, cols), lambda i, s: (i, s, COL_QK // cols)),
            pl.BlockSpec((None, SUBLANES, cols), lambda i, s: (i, jnp.maximum(s * hb - 1, 0), COL_QK // cols)),
            pl.BlockSpec((ML_CONV, cols), lambda i, s: (0, 0)),
            pl.BlockSpec((1, cols), lambda i, s: (0, 0)),
        ],
        out_specs=[
            pl.BlockSpec((None, half, tm), lambda i, s: (i, 0, s)),
            pl.BlockSpec((None, tm, half), lambda i, s: (i, s, 0)),
        ],
        out_shape=[
            jax.ShapeDtypeStruct((b, half, seq), BF16),
            jax.ShapeDtypeStruct((b, seq, half), BF16),
        ],
        scratch_shapes=[pltpu.VMEM((tm + SUBLANES, cols), F32)],
        compiler_params=_cparams(("arbitrary", "arbitrary"), 40),
        name="conv_silu",
    )(proj3, proj3, conv_w, conv_b.reshape(1, cols))


def _log_sigmoid(x):
    return -(jnp.maximum(-x, 0.0) + jnp.log1p(jnp.exp(-jnp.abs(x))))


def _mlstm_kernel(qt_ref, k_ref, v_ref, o_ref, gates_ref, gbias_ref, ng_ref, out_ref, ct_ref, m_ref):
    L = k_ref.shape[0]

    @pl.when(pl.program_id(1) == 0)
    def _():
        ct_ref[...] = jnp.zeros(ct_ref.shape, F32)
        m_ref[...] = jnp.zeros(m_ref.shape, F32)

    g = gates_ref[...] + gbias_ref[...]
    gt = g.T
    lf = _log_sigmoid(g)
    lft = _log_sigmoid(gt)
    r = lax.broadcasted_iota(jnp.int32, (L, L), 0)
    c = lax.broadcasted_iota(jnp.int32, (L, L), 1)
    src_le_qry = r <= c
    tril = (c <= r).astype(F32)
    triu = src_le_qry.astype(F32)
    hi = lax.Precision.HIGHEST
    bcol_all = jnp.dot(tril, lf, preferred_element_type=F32, precision=hi)
    brow_all = jnp.dot(lft, triu, preferred_element_type=F32, precision=hi)
    row = lax.broadcasted_iota(jnp.int32, (ML_DV, L), 0)
    ones_row = jnp.where(row == 0, 1.0, 0.0)

    def lead_matmuls(h):
        hs = slice(h * ML_DQK, (h + 1) * ML_DQK)
        qt = qt_ref[hs, :]
        return _dot(k_ref[:, hs], qt), _dot(ct_ref[h].astype(BF16), qt)

    lead_next = lead_matmuls(0)
    for h in range(ML_HEADS):
        hs = slice(h * ML_DQK, (h + 1) * ML_DQK)
        qk_t, cq_t = lead_next
        if h + 1 < ML_HEADS:
            lead_next = lead_matmuls(h + 1)
        li, lf_ = GATE_LANE + h, GATE_LANE + ML_HEADS + h
        b_row = brow_all[lf_:lf_ + 1, :]
        i_row = gt[li:li + 1, :]
        u_col = g[:, li:li + 1] - bcol_all[:, lf_:lf_ + 1]
        m_prev = m_ref[h][:, :1]
        logw_t = jnp.where(src_le_qry, b_row + u_col, -jnp.inf)
        log_inter = b_row + m_prev
        m_t = jnp.maximum(jnp.max(logw_t, axis=0, keepdims=True), log_inter)
        w_t = jnp.exp(logw_t - m_t)
        a = jnp.exp(log_inter - m_t)
        kh = k_ref[:, hs]
        vt_aug = jnp.concatenate([v_ref[:, hs].astype(F32).T, ones_row], axis=0)
        s_t = qk_t * w_t
        ct = ct_ref[h]
        nd = _dot(vt_aug.astype(BF16), s_t.astype(BF16)) + a * cq_t
        num = nd[:ML_DV, :]
        den = nd[ML_DV:ML_DV + 1, :]
        hout_t = num * (1.0 / jnp.maximum(jnp.abs(den), jnp.exp(-m_t)))
        hn_t = hout_t * lax.rsqrt(jnp.sum(hout_t * hout_t, axis=0, keepdims=True) * (1.0 / ML_DV) + EPS)
        gate = _sigmoid(o_ref[:, hs].astype(F32))
        out_ref[:, hs] = (hn_t.T * ng_ref[:, hs] * gate).astype(out_ref.dtype)

        b_last = b_row[:, L - 1:L]
        logg = b_last - b_row + i_row
        m_new = jnp.maximum(b_last + m_prev, jnp.max(logg, axis=-1, keepdims=True))
        g_row = jnp.exp(logg - m_new)
        decay = jnp.exp(b_last + m_prev - m_new)
        ct_ref[h] = decay * ct + _dot((vt_aug * g_row).astype(BF16), kh)
        m_ref[h] = jnp.broadcast_to(m_new, m_ref.shape[1:])


def _mlstm(qt3, k3, proj3, gates3, gbias, norm_g):
    b, seq, w = k3.shape
    L = min(MLSTM_CHUNK, seq)
    return pl.pallas_call(
        _mlstm_kernel,
        grid=(b, seq // L),
        in_specs=[
            pl.BlockSpec((None, w, L), lambda i, c: (i, 0, c)),
            pl.BlockSpec((None, L, w), lambda i, c: (i, c, 0)),
            pl.BlockSpec((None, L, w), lambda i, c: (i, c, COL_V // w)),
            pl.BlockSpec((None, L, w), lambda i, c: (i, c, COL_O // w)),
            pl.BlockSpec((None, L, LANES), lambda i, c: (i, c, 0)),
            pl.BlockSpec((1, LANES), lambda i, c: (0, 0)),
            pl.BlockSpec((1, w), lambda i, c: (0, 0)),
        ],
        out_specs=pl.BlockSpec((None, L, w), lambda i, c: (i, c, 0)),
        out_shape=jax.ShapeDtypeStruct((b, seq, w), BF16),
        scratch_shapes=[
            pltpu.VMEM((ML_HEADS, 2 * ML_DV, ML_DQK), F32),
            pltpu.VMEM((ML_HEADS, 1, LANES), F32),
        ],
        compiler_params=_cparams(("arbitrary", "arbitrary"), 40),
        name="mlstm",
    )(qt3, k3, proj3, proj3, gates3, gbias, norm_g.reshape(1, w))


def _merge_kernel(a_ref, b_ref, ga_ref, gb_ref, wa_ref, wb_ref, o_ref):
    tm = a_ref.shape[0]
    rows = min(MERGE_ROW_GROUP, tm)
    groups = [slice(r0, r0 + rows) for r0 in range(0, tm, rows)]

    def proj(rs):
        return _dot(a_ref[rs, :], wa_ref[...]), _dot(b_ref[rs, :], wb_ref[...])

    nxt = proj(groups[0])
    for n, rs in enumerate(groups):
        pa, pb = nxt
        if n + 1 < len(groups):
            nxt = proj(groups[n + 1])
        mixed = _sigmoid(ga_ref[rs, :].astype(F32)) * pa + _sigmoid(gb_ref[rs, :].astype(F32)) * pb
        o_ref[rs, :] = mixed.astype(o_ref.dtype)


def _merge(out_a, hm, proj, wa, wb, seq):
    t = out_a.shape[0]
    d = wa.shape[1]
    tm = min(1024, seq)
    tn = 1024
    return pl.pallas_call(
        _merge_kernel,
        grid=(d // tn, t // tm),
        in_specs=[
            pl.BlockSpec((tm, MLA_W), lambda j, i: (i, 0)),
            pl.BlockSpec((tm, ML_W), lambda j, i: (i, 0)),
            pl.BlockSpec((tm, tn), lambda j, i: (i, COL_GA // tn + j)),
            pl.BlockSpec((tm, tn), lambda j, i: (i, COL_GB // tn + j)),
            pl.BlockSpec((MLA_W, tn), lambda j, i: (0, j)),
            pl.BlockSpec((ML_W, tn), lambda j, i: (0, j)),
        ],
        out_specs=pl.BlockSpec((tm, tn), lambda j, i: (i, j)),
        out_shape=jax.ShapeDtypeStruct((t, d), BF16),
        compiler_params=_cparams(("arbitrary", "arbitrary"), 40),
        name="merge",
    )(out_a, hm, proj, proj, wa, wb)


def _outproj_kernel(mix_ref, x_ref, w_ref, gt_ref, sc_ref, sh_ref, g_ref, wr_ref, br_ref, x1_ref, h2_ref, lg_ref):
    tm = mix_ref.shape[0]
    rows = OUTPROJ_ROW_GROUP
    groups = [slice(r0, r0 + rows) for r0 in range(0, tm, rows)]
    y_next = _dot(mix_ref[groups[0], :], w_ref[...])
    for n, rs in enumerate(groups):
        y = y_next
        if n + 1 < len(groups):
            y_next = _dot(mix_ref[groups[n + 1], :], w_ref[...])
        x1 = x_ref[rs, :] + gt_ref[...] * y
        x1_ref[rs, :] = x1
        h2 = x1 * _rms_scale(x1, x1.shape[-1]) * g_ref[...]
        h2 = h2 * (1.0 + sc_ref[...]) + sh_ref[...]
        h2_ref[rs, :] = h2
        h_hi = h2.astype(BF16)
        h_lo = (h2 - h_hi.astype(F32)).astype(BF16)
        r = _dot(h_hi, wr_ref[...]) + _dot(h_lo, wr_ref[...])
        lg_ref[rs, :] = r[:, :LANES] + r[:, LANES:] + br_ref[...]


def _outproj(mixed, x2, w_out, mod4, norm_g, w_route2, b_route, seq):
    t, d = x2.shape
    tm = min(512, seq)
    bpr = seq // tm
    mod_spec = lambda k: pl.BlockSpec((None, None, 1, d), lambda i: (i // bpr, k, 0, 0))
    const = lambda i: (0, 0)
    return pl.pallas_call(
        _outproj_kernel,
        grid=(t // tm,),
        in_specs=[
            pl.BlockSpec((tm, d), lambda i: (i, 0)),
            pl.BlockSpec((tm, d), lambda i: (i, 0)),
            pl.BlockSpec((d, d), const, pipeline_mode=pl.Buffered(1)),
            mod_spec(2),
            mod_spec(4),
            mod_spec(3),
            pl.BlockSpec((1, d), const),
            pl.BlockSpec((d, 2 * LANES), const, pipeline_mode=pl.Buffered(1)),
            pl.BlockSpec((1, LANES), const),
        ],
        out_specs=[
            pl.BlockSpec((tm, d), lambda i: (i, 0)),
            pl.BlockSpec((tm, d), lambda i: (i, 0)),
            pl.BlockSpec((tm, LANES), lambda i: (i, 0)),
        ],
        out_shape=[
            jax.ShapeDtypeStruct((t, d), F32),
            jax.ShapeDtypeStruct((t, d), F32),
            jax.ShapeDtypeStruct((t, LANES), F32),
        ],
        compiler_params=_cparams(("arbitrary",), 56),
        name="outproj",
    )(mixed, x2, w_out, mod4, mod4, mod4, norm_g.reshape(1, d), w_route2, b_route)


def _route_kernel(lg_ref, pos_ref, wts_ref, cnt_ref, carry_ref, offs_ref, meta_s, wts_s):
    phase = pl.program_id(0)
    i = pl.program_id(1)
    tm = lg_ref.shape[0]
    lane = lax.broadcasted_iota(jnp.int32, (tm, LANES), 1)

    @pl.when((phase == 0) & (i == 0))
    def _():
        carry_ref[...] = jnp.zeros(carry_ref.shape, F32)

    @pl.when(phase == 0)
    def _():
        lg = lg_ref[...]
        big = jnp.int32(LANES)
        ninf = -jnp.inf

        def first_argmax(vals):
            mx = jnp.max(vals, axis=-1, keepdims=True)
            idx = jnp.min(jnp.where(vals == mx, lane, big), axis=-1, keepdims=True)
            return mx, idx

        gl = jnp.where(lane < N_GROUPS, lg, ninf)
        gmax, gsel = first_argmax(gl)
        g_w = 1.0 / jnp.sum(jnp.exp(gl - gmax), axis=-1, keepdims=True)
        lo = N_GROUPS + gsel * EXP_PER_GROUP
        in_grp = (lane >= lo) & (lane < lo + EXP_PER_GROUP)
        el = jnp.where(in_grp, lg, ninf)
        e1, i1 = first_argmax(el)
        e2, i2 = first_argmax(jnp.where(lane == i1, ninf, el))
        p2 = jnp.exp(e2 - e1)
        w1 = g_w / (1.0 + p2)
        w2 = g_w * p2 / (1.0 + p2)

        oh1 = lane == i1
        oh2 = lane == i2
        oh = jnp.where(oh1 | oh2, 1.0, 0.0)
        r = lax.broadcasted_iota(jnp.int32, (tm, tm), 0)
        c = lax.broadcasted_iota(jnp.int32, (tm, tm), 1)
        strict = jnp.where(c < r, 1.0, 0.0).astype(BF16)
        before = _dot(strict, oh.astype(BF16)) + carry_ref[...]
        rank1 = jnp.sum(jnp.where(oh1, before, 0.0), axis=-1, keepdims=True).astype(jnp.int32)
        rank2 = jnp.sum(jnp.where(oh2, before, 0.0), axis=-1, keepdims=True).astype(jnp.int32)
        carry_ref[...] = carry_ref[...] + jnp.sum(oh, axis=0, keepdims=True)
        meta_s[i] = jnp.where(lane == 0, i1, jnp.where(lane == 1, i2, jnp.where(lane == 2, rank1, jnp.where(lane == 3, rank2, 0))))
        wts_s[i] = jnp.where(lane == 0, w1, jnp.where(lane == 1, w2, 0.0))

    @pl.when((phase == 1) & (i == 0))
    def _():
        cnt = carry_ref[...]
        cnt_ref[...] = jnp.broadcast_to(cnt, cnt_ref.shape)
        padded = jnp.ceil(cnt * (1.0 / MOE_TILE)) * MOE_TILE
        r = lax.broadcasted_iota(jnp.int32, (LANES, LANES), 0)
        c = lax.broadcasted_iota(jnp.int32, (LANES, LANES), 1)
        upper = jnp.where(r < c, 1.0, 0.0).astype(BF16)
        padded8 = jnp.broadcast_to(padded, (SUBLANES, LANES)).astype(BF16)
        offs_ref[...] = _dot(padded8, upper)[:1, :]

    @pl.when(phase == 1)
    def _():
        meta = meta_s[i]
        offs = offs_ref[...]
        off1 = jnp.sum(jnp.where(lane == meta[:, 0:1], offs, 0.0), axis=-1, keepdims=True).astype(jnp.int32)
        off2 = jnp.sum(jnp.where(lane == meta[:, 1:2], offs, 0.0), axis=-1, keepdims=True).astype(jnp.int32)
        pos1 = off1 + meta[:, 2:3]
        pos2 = off2 + meta[:, 3:4]
        pos_ref[...] = jnp.where(lane == 0, pos1, jnp.where(lane == 1, pos2, 0))
        wts_ref[...] = wts_s[i]


def _route(logits):
    t = logits.shape[0]
    tm = min(1024, t)
    nb = t // tm
    return pl.pallas_call(
        _route_kernel,
        grid=(2, nb),
        in_specs=[pl.BlockSpec((tm, LANES), lambda p, i: (i * (1 - p), 0))],
        out_specs=[
            pl.BlockSpec((tm, LANES), lambda p, i: (i * p, 0)),
            pl.BlockSpec((tm, LANES), lambda p, i: (i * p, 0)),
            pl.BlockSpec((SUBLANES, LANES), lambda p, i: (0, 0)),
        ],
        out_shape=[
            jax.ShapeDtypeStruct((t, LANES), jnp.int32),
            jax.ShapeDtypeStruct((t, LANES), F32),
            jax.ShapeDtypeStruct((SUBLANES, LANES), F32),
        ],
        scratch_shapes=[
            pltpu.VMEM((1, LANES), F32),
            pltpu.VMEM((1, LANES), F32),
            pltpu.VMEM((nb, tm, LANES), jnp.int32),
            pltpu.VMEM((nb, tm, LANES), F32),
        ],
        compiler_params=_cparams(("arbitrary", "arbitrary"), 32),
        name="route",
    )(logits)


def _row_gather_start(idx_at, src_hbm, dst, sem, rows):
    for r in range(rows):
        pltpu.make_async_copy(src_hbm.at[pl.ds(idx_at(r), 1)], dst.at[pl.ds(r, 1)], sem).start(priority=r % 2)


def _row_gather_wait(src_hbm, dst, sem, rows):
    pltpu.make_async_copy(src_hbm.at[pl.ds(0, rows)], dst, sem).wait()


def _dispatch_kernel(pos_ref, zrow_ref, h_ref, xs_hbm, zbuf, sems):
    i = pl.program_id(0)
    tm = h_ref.shape[0]
    tile = zbuf.shape[0]

    def zero_copy(z):
        return pltpu.make_async_copy(zbuf, xs_hbm.at[pl.ds(pl.multiple_of(zrow_ref[z], tile), tile)], sems.at[0])

    @pl.when(i == 0)
    def _():
        zbuf[...] = jnp.zeros(zbuf.shape, zbuf.dtype)
        for z in range(zrow_ref.shape[0]):
            @pl.when(zrow_ref[z] >= 0)
            def _():
                zero_copy(z).start()
        for z in range(zrow_ref.shape[0]):
            @pl.when(zrow_ref[z] >= 0)
            def _():
                zero_copy(z).wait()

    for k in range(2):
        base = (2 * i + k) * tm
        for r in range(tm):
            pltpu.make_async_copy(h_ref.at[pl.ds(r, 1)], xs_hbm.at[pl.ds(pos_ref[base + r], 1)],
                                  sems.at[1]).start(priority=r % 2)
    for k in range(2):
        pltpu.make_async_copy(h_ref, xs_hbm.at[pl.ds(0, tm)], sems.at[1]).wait()


def _dispatch(pos, zrow, h2, n_rows, seq):
    t, d = h2.shape
    tm = min(1024, seq)
    pos_blocks = pos.reshape(t // tm, tm, 2).transpose(0, 2, 1).reshape(-1)
    grid_spec = pltpu.PrefetchScalarGridSpec(
        num_scalar_prefetch=2,
        grid=(t // tm,),
        in_specs=[pl.BlockSpec((tm, d), lambda i, p, z: (i, 0))],
        out_specs=pl.BlockSpec(memory_space=pl.ANY),
        scratch_shapes=[
            pltpu.VMEM((MOE_TILE, d), F32),
            pltpu.SemaphoreType.DMA((2,)),
        ],
    )
    return pl.pallas_call(
        _dispatch_kernel,
        grid_spec=grid_spec,
        out_shape=jax.ShapeDtypeStruct((n_rows, d), F32),
        compiler_params=_cparams(("arbitrary",), 40),
        name="moe_dispatch",
    )(pos_blocks, zrow, h2)


def _moe_kernel(plan_ref, x_ref, wg_hbm, wu_hbm, wd_hbm, y_ref, stg_g, stg_u, stg_d, wsem):
    j = pl.program_id(0)
    nt = plan_ref[3, 0]
    nslot = stg_g.shape[0]

    def w_copies(e, s):
        return (pltpu.make_async_copy(wg_hbm.at[e], stg_g.at[s], wsem.at[s]),
                pltpu.make_async_copy(wu_hbm.at[e], stg_u.at[s], wsem.at[s]),
                pltpu.make_async_copy(wd_hbm.at[e], stg_d.at[s], wsem.at[s]))

    @pl.when(j == 0)
    def _():
        for s in range(nslot - 1):
            @pl.when(plan_ref[3, 1 + s] >= 0)
            def _():
                for cp in w_copies(plan_ref[3, 1 + s], s):
                    cp.start()

    @pl.when((j < nt) & (plan_ref[0, j] == 1))
    def _():
        s = plan_ref[1, j]
        for cp in w_copies(0, s):
            cp.wait()

        @pl.when(plan_ref[2, j] >= 0)
        def _():
            free = lax.rem(s + nslot - 1, nslot)
            for cp in w_copies(plan_ref[2, j], free):
                cp.start()

    @pl.when(j < nt)
    def _():
        s = plan_ref[1, j]
        x = x_ref[...].astype(BF16)
        hg = _dot(x, stg_g[s].astype(BF16))
        hu = _dot(x, stg_u[s].astype(BF16))
        act = (hg * _sigmoid(hg) * hu).astype(BF16)
        y_ref[...] = _dot(act, stg_d[s].astype(BF16))

    @pl.when(j >= nt)
    def _():
        y_ref[...] = jnp.zeros(y_ref.shape, y_ref.dtype)


def _moe(plan, xs, w_gate_e, w_up_e, w_down_e):
    d = xs.shape[1]
    f = w_gate_e.shape[-1]
    nt_max = plan.shape[1]
    tile = MOE_TILE
    any_spec = pl.BlockSpec(memory_space=pl.ANY)
    grid_spec = pltpu.PrefetchScalarGridSpec(
        num_scalar_prefetch=1,
        grid=(nt_max,),
        in_specs=[
            pl.BlockSpec((tile, d), lambda j, plan_: (jnp.minimum(j, plan_[3, 0] - 1), 0)),
            any_spec, any_spec, any_spec,
        ],
        out_specs=pl.BlockSpec((tile, d), lambda j, plan_: (j, 0)),
        scratch_shapes=[
            pltpu.VMEM((MOE_WEIGHT_SLOTS, d, f), F32),
            pltpu.VMEM((MOE_WEIGHT_SLOTS, d, f), F32),
            pltpu.VMEM((MOE_WEIGHT_SLOTS, f, d), F32),
            pltpu.SemaphoreType.DMA((MOE_WEIGHT_SLOTS,)),
        ],
    )
    return pl.pallas_call(
        _moe_kernel,
        grid_spec=grid_spec,
        out_shape=jax.ShapeDtypeStruct((nt_max * tile, d), F32),
        compiler_params=_cparams(("arbitrary",), 56),
        name="moe_experts",
    )(plan, xs, w_gate_e, w_up_e, w_down_e)


def _combine_kernel(pos_ref, y_hbm, x1_ref, gt_ref, wts_ref, o_ref, ybuf, sem):
    i = pl.program_id(0)
    n = pl.num_programs(0)
    tm = x1_ref.shape[0]
    slot = lax.rem(i, 2)

    def start(blk, s):
        for k in range(2):
            base = (2 * blk + k) * tm
            _row_gather_start(lambda r, base=base: pos_ref[base + r], y_hbm, ybuf.at[s, k], sem.at[s], tm)

    @pl.when(i == 0)
    def _():
        start(0, 0)

        @pl.when(n > 1)
        def _():
            start(1, 1)

    for k in range(2):
        _row_gather_wait(y_hbm, ybuf.at[slot, k], sem.at[slot], tm)
    w = wts_ref[...]
    moe = w[:, 0:1] * ybuf[slot, 0] + w[:, 1:2] * ybuf[slot, 1]
    o_ref[...] = x1_ref[...] + gt_ref[...] * moe

    @pl.when(i + 2 < n)
    def _():
        start(i + 2, slot)


def _combine(pos, ys, x1, mod4, wts, seq):
    t, d = x1.shape
    tm = min(512, seq)
    bpr = seq // tm
    pos_blocks = pos.reshape(t // tm, tm, 2).transpose(0, 2, 1).reshape(-1)
    grid_spec = pltpu.PrefetchScalarGridSpec(
        num_scalar_prefetch=1,
        grid=(t // tm,),
        in_specs=[
            pl.BlockSpec(memory_space=pl.ANY),
            pl.BlockSpec((tm, d), lambda i, p: (i, 0)),
            pl.BlockSpec((None, None, 1, d), lambda i, p: (i // bpr, 5, 0, 0)),
            pl.BlockSpec((tm, LANES), lambda i, p: (i, 0)),
        ],
        out_specs=pl.BlockSpec((tm, d), lambda i, p: (i, 0)),
        scratch_shapes=[
            pltpu.VMEM((2, 2, tm, d), F32),
            pltpu.SemaphoreType.DMA((2,)),
        ],
    )
    return pl.pallas_call(
        _combine_kernel,
        grid_spec=grid_spec,
        out_shape=jax.ShapeDtypeStruct((t, d), F32),
        compiler_params=_cparams(("arbitrary",), 48),
        name="moe_combine",
    )(pos_blocks, ys, x1, mod4, wts)


def _swap_halves(w):
    half = w.shape[-1] // 2
    return jnp.concatenate([w[..., half:], w[..., :half]], axis=-1)


def _layout_w_uq(w_uq):
    r = w_uq.shape[0]
    w = w_uq.reshape(r, MLA_HEADS, MLA_QK)
    rope = w[..., MLA_NOPE:]
    return jnp.concatenate([w[..., :MLA_NOPE], rope, _swap_halves(rope)], axis=-1).reshape(r, MLA_HEADS * QK_PAD).astype(BF16)


def _layout_w_ukv(w_ukv):
    r = w_ukv.shape[0]
    w = w_ukv.reshape(r, MLA_HEADS, MLA_NOPE + MLA_V)
    return jnp.concatenate([w[..., :MLA_NOPE].reshape(r, -1), w[..., MLA_NOPE:].reshape(r, -1)], axis=-1).astype(BF16)


def _rope_gain(g):
    g1 = g[MLA_NOPE:MLA_NOPE + MLA_ROPE // 2]
    g2 = g[MLA_NOPE + MLA_ROPE // 2:]
    return jnp.concatenate([g[:MLA_NOPE], g1, g2, -g2, g1]).reshape(1, QK_PAD)


def _layer(x2, cond_mod4, pos2, seq, w_in, q_a_norm_g, w_uq, kv_a_norm_g, w_ukv, q_norm_g, k_norm_g, conv_w, conv_b,
           b_mlstm_gates, mlstm_norm_g, w_proj_a, w_proj_b, w_out, norm_mix_g, norm_ffn_g, w_group, b_group,
           w_router, b_router, w_gate_e, w_up_e, w_down_e):
    t, d = x2.shape
    b = t // seq
    mod4 = cond_mod4

    proj, gates = _inproj(x2, mod4, norm_mix_g, w_in.T, seq)

    inv = ROPE_THETA ** (-jnp.arange(0, MLA_ROPE, 2, dtype=F32) / MLA_ROPE)
    inv_lanes = jnp.tile(inv, LANES // (MLA_ROPE // 2)).reshape(1, LANES)
    qt, k, vt = _mla_prep(proj, pos2, _layout_w_uq(w_uq), _layout_w_ukv(w_ukv), q_a_norm_g.reshape(1, -1),
                        kv_a_norm_g.reshape(1, -1), _rope_gain(q_norm_g), _rope_gain(k_norm_g), inv_lanes, seq)
    out_a = _flash(qt, k.reshape(b, seq, -1), vt).reshape(t, MLA_W)

    proj3 = proj.reshape(b, seq, -1)
    ml_qt, ml_k = _conv_silu(proj3, conv_w, conv_b)
    gbias = jnp.zeros((1, LANES), F32).at[0, GATE_LANE:GATE_LANE + 2 * ML_HEADS].set(b_mlstm_gates.reshape(-1))
    hm = _mlstm(ml_qt, ml_k, proj3, gates.reshape(b, seq, LANES), gbias, mlstm_norm_g).reshape(t, ML_W)

    mixed = _merge(out_a, hm, proj, w_proj_a.astype(BF16), w_proj_b.astype(BF16), seq)

    route_pad = LANES - N_GROUPS - N_EXPERTS
    w_route = jnp.concatenate([w_group, w_router, jnp.zeros((d, route_pad), F32)], axis=1)
    b_route = jnp.concatenate([b_group, b_router, jnp.zeros((route_pad,), F32)]).reshape(1, LANES)
    w_route_hi = w_route.astype(BF16)
    w_route2 = jnp.concatenate([w_route_hi, (w_route - w_route_hi.astype(F32)).astype(BF16)], axis=1)
    x1, h2, logits = _outproj(mixed, x2, w_out.astype(BF16), mod4, norm_ffn_g, w_route2, b_route, seq)

    posm, wts, counts = _route(logits)
    pos = posm[:, 0:2]

    tile = MOE_TILE
    i32 = jnp.int32
    cnt = counts[0, N_GROUPS:N_GROUPS + N_EXPERTS].astype(i32)
    padded = ((cnt + tile - 1) // tile) * tile
    ends = jnp.cumsum(padded)
    offs = ends - padded
    nt_max = (2 * t) // tile + N_EXPERTS
    n_tiles = ends[-1] // tile
    tile_idx = jnp.arange(nt_max, dtype=i32)
    tile_start = tile_idx * tile
    live = tile_idx < n_tiles
    texp = jnp.minimum(jnp.sum((ends[None, :] <= tile_start[:, None]).astype(i32), axis=1), N_EXPERTS - 1)
    active = cnt > 0
    order = jnp.cumsum(active.astype(i32)) - 1
    n_active = jnp.sum(active.astype(i32))
    experts = jnp.arange(N_EXPERTS, dtype=i32)
    by_order = jnp.sum(jnp.where(active[None, :] & (order[None, :] == experts[:, None]), experts[None, :], 0), axis=1)
    t_order = order[texp]
    first = (live & (tile_start == offs[texp])).astype(i32)
    nslot = MOE_WEIGHT_SLOTS
    ahead = t_order + nslot - 1
    prefetch = jnp.where(ahead < n_active, by_order[jnp.minimum(ahead, N_EXPERTS - 1)], -1)
    lead = jnp.where(experts[:nslot - 1] < n_active, by_order[:nslot - 1], -1)
    head = jnp.zeros((nt_max,), i32).at[0].set(n_tiles).at[1:nslot].set(lead)
    plan = jnp.stack([first, t_order % nslot, prefetch, head]).astype(i32)

    last_tile = jnp.where(padded > cnt, ends - tile, -1)
    spare_idx = n_tiles + jnp.arange(N_EXPERTS, dtype=i32)
    spare_tile = jnp.where(spare_idx < nt_max, spare_idx * tile, -1)
    zrow = jnp.concatenate([last_tile, spare_tile]).astype(i32)

    xs = _dispatch(pos, zrow, h2, nt_max * tile, seq)
    ys = _moe(plan, xs, w_gate_e, w_up_e, w_down_e)

    return _combine(pos, ys, x1, mod4, wts, seq)


def kernel(x, c, positions, w_ada, b_ada, norm_mix_g, w_in, q_a_norm_g, w_uq, kv_a_norm_g, w_ukv, q_norm_g, k_norm_g, conv_w, conv_b, b_mlstm_gates, mlstm_norm_g, w_proj_a, w_proj_b, w_out, norm_ffn_g, w_group, b_group, w_router, b_router, w_gate_e, w_up_e, w_down_e):
    b, seq, d = x.shape
    depth = w_ada.shape[0]
    x2 = x.reshape(b * seq, d)
    pos2 = positions.reshape(b * seq, 1)
    c_pad = jnp.zeros((SUBLANES, d), F32).at[:b].set(c)
    for l in range(depth):
        mod = _adaln(c_pad, w_ada[l], b_ada[l])
        mod4 = mod[:b].reshape(b, 6, 1, d)
        x2 = _layer(x2, mod4, pos2, seq, w_in[l], q_a_norm_g[l], w_uq[l], kv_a_norm_g[l], w_ukv[l], q_norm_g[l],
                    k_norm_g[l], conv_w[l], conv_b[l], b_mlstm_gates[l], mlstm_norm_g[l], w_proj_a[l], w_proj_b[l],
                    w_out[l], norm_mix_g[l], norm_ffn_g[l], w_group[l], b_group[l], w_router[l], b_router[l],
                    w_gate_e[l], w_up_e[l], w_down_e[l])
    return x2.reshape(b, seq, d)
```

```python
import functools
import math

import jax
import jax.numpy as jnp
from jax import lax
from jax.experimental import pallas as pl
from jax.experimental.pallas import tpu as pltpu

F32 = jnp.float32
BF16 = jnp.bfloat16

LANES = 128
SUBLANES = 8

D_MODEL = 2048
MLA_HEADS = 8
MLA_NOPE = 128
MLA_ROPE = 64
MLA_QK = MLA_NOPE + MLA_ROPE
MLA_V = 128
Q_LORA = 512
KV_LORA = 256
ROPE_THETA = 10000.0
ML_HEADS = 8
ML_DQK = 128
ML_DV = 128
ML_CONV = 4
MLA_W = MLA_HEADS * MLA_V
ML_W = ML_HEADS * ML_DV
N_GROUPS = 4
EXP_PER_GROUP = 8
N_EXPERTS = N_GROUPS * EXP_PER_GROUP
EPS = 1e-6

QK_PAD = 2 * LANES

SRC_CQ = 0
SRC_KPE = Q_LORA + KV_LORA
SRC_QK = SRC_KPE + MLA_ROPE
SRC_V = SRC_QK + 2 * ML_HEADS * ML_DQK
SRC_O = SRC_V + ML_W
SRC_I = SRC_O + ML_W
SRC_GA = SRC_I + 2 * ML_HEADS
SRC_GB = SRC_GA + D_MODEL
GATE_LANE = SRC_I % LANES

IN_BLOCK = 1024
IN_BLOCK_SRC = (SRC_QK, SRC_QK + IN_BLOCK, SRC_GA, SRC_GA + IN_BLOCK, SRC_GB, SRC_GB + IN_BLOCK, SRC_V, SRC_O, SRC_CQ)
COL_QK = 0
COL_GA = 2048
COL_GB = 4096
COL_V = 6144
COL_O = 7168
COL_LAT = 8192
LAT_W = Q_LORA + KV_LORA + MLA_ROPE

MLSTM_CHUNK = 256
MOE_TILE = 256
MOE_WEIGHT_SLOTS = 3
FLASH_HEADS_PER_STEP = 2
OUTPROJ_ROW_GROUP = 256
MERGE_ROW_GROUP = 256


def _cparams(sem, vmem_mb):
    return pltpu.CompilerParams(dimension_semantics=sem, vmem_limit_bytes=vmem_mb * 1024 * 1024)


def _dot(a, b):
    return jnp.dot(a, b, preferred_element_type=F32)


def _dot_nt(a, b):
    return lax.dot_general(a, b, (((1,), (1,)), ((), ())), preferred_element_type=F32)


def _sigmoid(x):
    return 1.0 / (1.0 + jnp.exp(-x))


def _rms_scale(x, width):
    return lax.rsqrt(jnp.sum(x * x, axis=-1, keepdims=True) * (1.0 / width) + EPS)


def _adaln_kernel(c_ref, w_ref, b_ref, o_ref):
    c = c_ref[...]
    cond = (c * _sigmoid(c)).astype(BF16)
    o_ref[...] = _dot(cond, w_ref[...].astype(BF16)) + b_ref[...]


def _adaln(c_pad, w_ada, b_ada):
    rows, d = c_pad.shape
    n = w_ada.shape[1]
    tn = 2048
    return pl.pallas_call(
        _adaln_kernel,
        grid=(n // tn,),
        in_specs=[
            pl.BlockSpec((rows, d), lambda j: (0, 0)),
            pl.BlockSpec((d, tn), lambda j: (0, j)),
            pl.BlockSpec((1, tn), lambda j: (0, j)),
        ],
        out_specs=pl.BlockSpec((rows, tn), lambda j: (0, j)),
        out_shape=jax.ShapeDtypeStruct((rows, n), F32),
        compiler_params=_cparams(("arbitrary",), 48),
        name="adaln",
    )(c_pad, w_ada, b_ada.reshape(1, n))


def _inproj_kernel(off_ref, x_ref, sc_ref, sh_ref, g_ref, w_ref, wgate_ref, proj_ref, gates_ref, h_ref):
    j = pl.program_id(1)

    @pl.when(j == 0)
    def _():
        x = x_ref[...]
        h = x * _rms_scale(x, x.shape[-1]) * g_ref[...]
        h = (h * (1.0 + sc_ref[...]) + sh_ref[...]).astype(BF16)
        h_ref[...] = h
        gates_ref[...] = _dot_nt(h, wgate_ref[...].astype(BF16))

    proj_ref[...] = _dot_nt(h_ref[...], w_ref[...].astype(BF16)).astype(proj_ref.dtype)


def _inproj(x2, mod4, norm_g, w_in_t, seq):
    t, d = x2.shape
    tm = min(1024, seq)
    tn = IN_BLOCK
    bpr = seq // tm
    assert all(o % SUBLANES == 0 for o in IN_BLOCK_SRC)
    offs = jnp.asarray([o // SUBLANES for o in IN_BLOCK_SRC], jnp.int32)
    nblk = len(IN_BLOCK_SRC)
    gate_tile = SRC_I // LANES
    grid_spec = pltpu.PrefetchScalarGridSpec(
        num_scalar_prefetch=1,
        grid=(t // tm, nblk),
        in_specs=[
            pl.BlockSpec((tm, d), lambda i, j, o: (i, 0)),
            pl.BlockSpec((None, None, 1, d), lambda i, j, o: (i // bpr, 1, 0, 0)),
            pl.BlockSpec((None, None, 1, d), lambda i, j, o: (i // bpr, 0, 0, 0)),
            pl.BlockSpec((1, d), lambda i, j, o: (0, 0)),
            pl.BlockSpec((pl.Element(tn), pl.Element(d)), lambda i, j, o: (o[j] * SUBLANES, 0)),
            pl.BlockSpec((LANES, d), lambda i, j, o: (gate_tile, 0)),
        ],
        out_specs=[
            pl.BlockSpec((tm, tn), lambda i, j, o: (i, j)),
            pl.BlockSpec((tm, LANES), lambda i, j, o: (i, 0)),
        ],
        scratch_shapes=[pltpu.VMEM((tm, d), BF16)],
    )
    return pl.pallas_call(
        _inproj_kernel,
        grid_spec=grid_spec,
        out_shape=[
            jax.ShapeDtypeStruct((t, nblk * tn), BF16),
            jax.ShapeDtypeStruct((t, LANES), F32),
        ],
        compiler_params=_cparams(("arbitrary", "arbitrary"), 56),
        name="inproj",
    )(offs, x2, mod4, mod4, norm_g.reshape(1, d), w_in_t, w_in_t)


def _mla_prep_kernel(lat_ref, pos_ref, wuq_ref, wukv_ref, gqa_ref, gkva_ref, gq_ref, gk_ref, inv_ref,
                     qt_ref, k_ref, vt_ref):
    lat = lat_ref[...].astype(F32)
    cq = lat[:, :Q_LORA]
    ckv = lat[:, Q_LORA:Q_LORA + KV_LORA]
    kc = lat[:, SRC_KPE:SRC_KPE + LANES]
    cqn = (cq * _rms_scale(cq, Q_LORA) * gqa_ref[...]).astype(BF16)
    ckvn = (ckv * _rms_scale(ckv, KV_LORA) * gkva_ref[...]).astype(BF16)
    qraw = _dot(cqn, wuq_ref[...])
    kv = _dot(ckvn, wukv_ref[...])

    ang = pos_ref[...].astype(F32) * inv_ref[...]
    lane = lax.broadcasted_iota(jnp.int32, ang.shape, 1)
    lo = lane < MLA_ROPE
    cs = jnp.cos(ang - jnp.where(lo, 0.0, 0.5 * math.pi))

    quarter = MLA_ROPE // 2
    want = jnp.where(lane < 3 * quarter, lane - quarter, lane - 3 * quarter)
    came = pltpu.roll(lane, quarter, 1)
    swapped = jnp.where(came == want, pltpu.roll(kc, quarter, 1), pltpu.roll(kc, 3 * quarter, 1))
    kc = jnp.where(lo, kc, swapped)

    gq = gq_ref[...]
    gk = gk_ref[...]
    gq_n, gq_r = gq[:, :LANES], gq[:, LANES:]
    gk_n, gk_r = gk[:, :LANES], gk[:, LANES:]

    def rope(chunk, g_cs):
        a = chunk * g_cs
        return jnp.where(lo, a + pltpu.roll(a, MLA_ROPE, 1), 0.0)

    gq_cs = gq_r * cs
    kpe_ss = jnp.sum(jnp.where(lo, kc * kc, 0.0), axis=-1, keepdims=True)
    k_rope = rope(kc, gk_r * cs)
    scale = MLA_QK ** -0.5 * math.log2(math.e)
    for h in range(MLA_HEADS):
        kn = kv[:, h * MLA_NOPE:(h + 1) * MLA_NOPE]
        sk = lax.rsqrt((jnp.sum(kn * kn, axis=-1, keepdims=True) + kpe_ss) * (1.0 / MLA_QK) + EPS)
        k_ref[:, h * QK_PAD:h * QK_PAD + LANES] = (kn * sk * gk_n).astype(BF16)
        k_ref[:, h * QK_PAD + LANES:(h + 1) * QK_PAD] = (k_rope * sk).astype(BF16)
        qn = qraw[:, h * QK_PAD:h * QK_PAD + LANES]
        qr = qraw[:, h * QK_PAD + LANES:(h + 1) * QK_PAD]
        ss = jnp.sum(qn * qn, axis=-1, keepdims=True) + jnp.sum(jnp.where(lo, qr * qr, 0.0), axis=-1, keepdims=True)
        sq = lax.rsqrt(ss * (1.0 / MLA_QK) + EPS) * scale
        qt_ref[h * QK_PAD:h * QK_PAD + LANES, :] = (qn * sq * gq_n).astype(BF16).T
        qt_ref[h * QK_PAD + LANES:(h + 1) * QK_PAD, :] = (rope(qr, gq_cs) * sq).astype(BF16).T
        vh = kv[:, MLA_HEADS * MLA_NOPE + h * MLA_V:MLA_HEADS * MLA_NOPE + (h + 1) * MLA_V]
        vt_ref[h * MLA_V:(h + 1) * MLA_V, :] = vh.astype(BF16).T


def _mla_prep(proj, pos2, wuq_p, wukv_p, gqa, gkva, gq, gk, inv_lanes, seq):
    t = proj.shape[0]
    tm = min(512, seq)
    hq = MLA_HEADS * QK_PAD
    lat_blk = COL_LAT // IN_BLOCK
    const = lambda i: (0, 0)
    return pl.pallas_call(
        _mla_prep_kernel,
        grid=(t // tm,),
        in_specs=[
            pl.BlockSpec((tm, IN_BLOCK), lambda i: (i, lat_blk)),
            pl.BlockSpec((tm, 1), lambda i: (i, 0)),
            pl.BlockSpec(wuq_p.shape, const),
            pl.BlockSpec(wukv_p.shape, const),
            pl.BlockSpec(gqa.shape, const),
            pl.BlockSpec(gkva.shape, const),
            pl.BlockSpec(gq.shape, const),
            pl.BlockSpec(gk.shape, const),
            pl.BlockSpec(inv_lanes.shape, const),
        ],
        out_specs=[
            pl.BlockSpec((hq, tm), lambda i: (0, i)),
            pl.BlockSpec((tm, hq), lambda i: (i, 0)),
            pl.BlockSpec((MLA_W, tm), lambda i: (0, i)),
        ],
        out_shape=[
            jax.ShapeDtypeStruct((hq, t), BF16),
            jax.ShapeDtypeStruct((t, hq), BF16),
            jax.ShapeDtypeStruct((MLA_W, t), BF16),
        ],
        compiler_params=_cparams(("arbitrary",), 48),
        name="mla_prep",
    )(proj, pos2, wuq_p, wukv_p, gqa, gkva, gq, gk, inv_lanes)


def _flash_kernel(qt_ref, k_ref, vt_ref, o_ref, *, tq, tk):
    seq = k_ref.shape[0]
    heads = k_ref.shape[1] // QK_PAD

    def scores(h, k0, q0):
        kj = k_ref[k0:k0 + tk, h * QK_PAD:(h + 1) * QK_PAD]
        return _dot(kj, qt_ref[h * QK_PAD:(h + 1) * QK_PAD, q0:q0 + tq])

    def update(h, state, st, k0, q0):
        m, l, acc = state
        if k0 + tk - 1 > q0:
            key = lax.broadcasted_iota(jnp.int32, st.shape, 0) + k0
            qry = lax.broadcasted_iota(jnp.int32, st.shape, 1) + q0
            st = jnp.where(key <= qry, st, -jnp.inf)
        m_new = jnp.maximum(m, jnp.max(st, axis=0, keepdims=True))
        alpha = jnp.exp2(m - m_new)
        p = jnp.exp2(st - m_new)
        l = alpha * l + jnp.sum(p, axis=0, keepdims=True)
        acc = alpha * acc + _dot(vt_ref[h * MLA_V:(h + 1) * MLA_V, k0:k0 + tk], p.astype(BF16))
        return m_new, l, acc

    steps = [(h, qi * tq, j * tk) for qi in range(seq // tq) for j in range(qi * tq // tk + 1) for h in range(heads)]
    st_next = scores(steps[0][0], steps[0][2], steps[0][1])
    states = {}
    for n, (h, q0, k0) in enumerate(steps):
        st = st_next
        if n + 1 < len(steps):
            hn, qn, kn = steps[n + 1]
            st_next = scores(hn, kn, qn)
        if k0 == 0:
            states[h] = (jnp.full((1, tq), -jnp.inf, F32), jnp.zeros((1, tq), F32), jnp.zeros((MLA_V, tq), F32))
        states[h] = update(h, states[h], st, k0, q0)
        if k0 + tk >= q0 + tq:
            _, l, acc = states[h]
            o_ref[q0:q0 + tq, h * MLA_V:(h + 1) * MLA_V] = (acc / l).T.astype(o_ref.dtype)


def _flash(qt, k3, vt):
    b, seq, _ = k3.shape
    tq = min(512, seq)
    tk = min(512, seq)
    hp = FLASH_HEADS_PER_STEP
    kern = functools.partial(_flash_kernel, tq=tq, tk=tk)
    return pl.pallas_call(
        kern,
        grid=(b, MLA_HEADS // hp),
        in_specs=[
            pl.BlockSpec((hp * QK_PAD, seq), lambda i, h: (h, i)),
            pl.BlockSpec((None, seq, hp * QK_PAD), lambda i, h: (i, 0, h)),
            pl.BlockSpec((hp * MLA_V, seq), lambda i, h: (h, i)),
        ],
        out_specs=pl.BlockSpec((None, seq, hp * MLA_V), lambda i, h: (i, 0, h)),
        out_shape=jax.ShapeDtypeStruct((b, seq, MLA_W), BF16),
        compiler_params=_cparams(("arbitrary", "arbitrary"), 40),
        name="flash",
    )(qt, k3, vt)


def _conv_kernel(cur_ref, halo_ref, w_ref, b_ref, qt_ref, k_ref, buf_ref, *, k_scale):
    tm = cur_ref.shape[0]
    cols = cur_ref.shape[1]
    half = cols // 2
    first = pl.program_id(1) == 0
    halo = halo_ref[...].astype(F32)
    buf_ref[0:SUBLANES, :] = jnp.where(first, 0.0, halo)
    buf_ref[SUBLANES:SUBLANES + tm, :] = cur_ref[...].astype(F32)
    cw = 512
    for c in range(cols // cw):
        sl = slice(c * cw, (c + 1) * cw)
        acc = jnp.zeros((tm, cw), F32) + b_ref[:, sl]
        for j in range(ML_CONV):
            off = SUBLANES - (ML_CONV - 1) + j
            acc = acc + buf_ref[off:off + tm, sl] * w_ref[j:j + 1, sl]
        y = acc * _sigmoid(acc)
        if c * cw < half:
            qt_ref[sl, :] = y.astype(qt_ref.dtype).T
        else:
            k_ref[:, c * cw - half:(c + 1) * cw - half] = (y * k_scale).astype(k_ref.dtype)


def _conv_silu(proj3, conv_w, conv_b):
    b, seq, _ = proj3.shape
    cols = 2 * ML_HEADS * ML_DQK
    half = cols // 2
    tm = min(512, seq)
    hb = tm // SUBLANES
    kern = functools.partial(_conv_kernel, k_scale=ML_DQK ** -0.5)
    return pl.pallas_call(
        kern,
        grid=(b, seq // tm),
        in_specs=[
            pl.BlockSpec((None, tm, cols), lambda i, s: (i, s, COL_QK // cols)),
            pl.BlockSpec((None, SUBLANES, cols), lambda i, s: (i, jnp.maximum(s * hb - 1, 0), COL_QK // cols)),
            pl.BlockSpec((ML_CONV, cols), lambda i, s: (0, 0)),
            pl.BlockSpec((1, cols), lambda i, s: (0, 0)),
        ],
        out_specs=[
            pl.BlockSpec((None, half, tm), lambda i, s: (i, 0, s)),
            pl.BlockSpec((None, tm, half), lambda i, s: (i, s, 0)),
        ],
        out_shape=[
            jax.ShapeDtypeStruct((b, half, seq), BF16),
            jax.ShapeDtypeStruct((b, seq, half), BF16),
        ],
        scratch_shapes=[pltpu.VMEM((tm + SUBLANES, cols), F32)],
        compiler_params=_cparams(("arbitrary", "arbitrary"), 40),
        name="conv_silu",
    )(proj3, proj3, conv_w, conv_b.reshape(1, cols))


def _log_sigmoid(x):
    return -(jnp.maximum(-x, 0.0) + jnp.log1p(jnp.exp(-jnp.abs(x))))


def _mlstm_kernel(qt_ref, k_ref, v_ref, o_ref, gates_ref, gbias_ref, ng_ref, out_ref, ct_ref, m_ref):
    L = k_ref.shape[0]

    @pl.when(pl.program_id(1) == 0)
    def _():
        ct_ref[...] = jnp.zeros(ct_ref.shape, F32)
        m_ref[...] = jnp.zeros(m_ref.shape, F32)

    g = gates_ref[...] + gbias_ref[...]
    gt = g.T
    lf = _log_sigmoid(g)
    lft = _log_sigmoid(gt)
    r = lax.broadcasted_iota(jnp.int32, (L, L), 0)
    c = lax.broadcasted_iota(jnp.int32, (L, L), 1)
    src_le_qry = r <= c
    tril = (c <= r).astype(F32)
    triu = src_le_qry.astype(F32)
    hi = lax.Precision.HIGHEST
    bcol_all = jnp.dot(tril, lf, preferred_element_type=F32, precision=hi)
    brow_all = jnp.dot(lft, triu, preferred_element_type=F32, precision=hi)
    row = lax.broadcasted_iota(jnp.int32, (ML_DV, L), 0)
    ones_row = jnp.where(row == 0, 1.0, 0.0)

    def lead_matmuls(h):
        hs = slice(h * ML_DQK, (h + 1) * ML_DQK)
        qt = qt_ref[hs, :]
        return _dot(k_ref[:, hs], qt), _dot(ct_ref[h].astype(BF16), qt)

    lead_next = lead_matmuls(0)
    for h in range(ML_HEADS):
        hs = slice(h * ML_DQK, (h + 1) * ML_DQK)
        qk_t, cq_t = lead_next
        if h + 1 < ML_HEADS:
            lead_next = lead_matmuls(h + 1)
        li, lf_ = GATE_LANE + h, GATE_LANE + ML_HEADS + h
        b_row = brow_all[lf_:lf_ + 1, :]
        i_row = gt[li:li + 1, :]
        u_col = g[:, li:li + 1] - bcol_all[:, lf_:lf_ + 1]
        m_prev = m_ref[h][:, :1]
        logw_t = jnp.where(src_le_qry, b_row + u_col, -jnp.inf)
        log_inter = b_row + m_prev
        m_t = jnp.maximum(jnp.max(logw_t, axis=0, keepdims=True), log_inter)
        w_t = jnp.exp(logw_t - m_t)
        a = jnp.exp(log_inter - m_t)
        kh = k_ref[:, hs]
        vt_aug = jnp.concatenate([v_ref[:, hs].astype(F32).T, ones_row], axis=0)
        s_t = qk_t * w_t
        ct = ct_ref[h]
        nd = _dot(vt_aug.astype(BF16), s_t.astype(BF16)) + a * cq_t
        num = nd[:ML_DV, :]
        den = nd[ML_DV:ML_DV + 1, :]
        hout_t = num * (1.0 / jnp.maximum(jnp.abs(den), jnp.exp(-m_t)))
        hn_t = hout_t * lax.rsqrt(jnp.sum(hout_t * hout_t, axis=0, keepdims=True) * (1.0 / ML_DV) + EPS)
        gate = _sigmoid(o_ref[:, hs].astype(F32))
        out_ref[:, hs] = (hn_t.T * ng_ref[:, hs] * gate).astype(out_ref.dtype)

        b_last = b_row[:, L - 1:L]
        logg = b_last - b_row + i_row
        m_new = jnp.maximum(b_last + m_prev, jnp.max(logg, axis=-1, keepdims=True))
        g_row = jnp.exp(logg - m_new)
        decay = jnp.exp(b_last + m_prev - m_new)
        ct_ref[h] = decay * ct + _dot((vt_aug * g_row).astype(BF16), kh)
        m_ref[h] = jnp.broadcast_to(m_new, m_ref.shape[1:])


def _mlstm(qt3, k3, proj3, gates3, gbias, norm_g):
    b, seq, w = k3.shape
    L = min(MLSTM_CHUNK, seq)
    return pl.pallas_call(
        _mlstm_kernel,
        grid=(b, seq // L),
        in_specs=[
            pl.BlockSpec((None, w, L), lambda i, c: (i, 0, c)),
            pl.BlockSpec((None, L, w), lambda i, c: (i, c, 0)),
            pl.BlockSpec((None, L, w), lambda i, c: (i, c, COL_V // w)),
            pl.BlockSpec((None, L, w), lambda i, c: (i, c, COL_O // w)),
            pl.BlockSpec((None, L, LANES), lambda i, c: (i, c, 0)),
            pl.BlockSpec((1, LANES), lambda i, c: (0, 0)),
            pl.BlockSpec((1, w), lambda i, c: (0, 0)),
        ],
        out_specs=pl.BlockSpec((None, L, w), lambda i, c: (i, c, 0)),
        out_shape=jax.ShapeDtypeStruct((b, seq, w), BF16),
        scratch_shapes=[
            pltpu.VMEM((ML_HEADS, 2 * ML_DV, ML_DQK), F32),
            pltpu.VMEM((ML_HEADS, 1, LANES), F32),
        ],
        compiler_params=_cparams(("arbitrary", "arbitrary"), 40),
        name="mlstm",
    )(qt3, k3, proj3, proj3, gates3, gbias, norm_g.reshape(1, w))


def _merge_kernel(a_ref, b_ref, ga_ref, gb_ref, wa_ref, wb_ref, o_ref):
    tm = a_ref.shape[0]
    rows = min(MERGE_ROW_GROUP, tm)
    groups = [slice(r0, r0 + rows) for r0 in range(0, tm, rows)]

    def proj(rs):
        return _dot(a_ref[rs, :], wa_ref[...]), _dot(b_ref[rs, :], wb_ref[...])

    nxt = proj(groups[0])
    for n, rs in enumerate(groups):
        pa, pb = nxt
        if n + 1 < len(groups):
            nxt = proj(groups[n + 1])
        mixed = _sigmoid(ga_ref[rs, :].astype(F32)) * pa + _sigmoid(gb_ref[rs, :].astype(F32)) * pb
        o_ref[rs, :] = mixed.astype(o_ref.dtype)


def _merge(out_a, hm, proj, wa, wb, seq):
    t = out_a.shape[0]
    d = wa.shape[1]
    tm = min(1024, seq)
    tn = 1024
    return pl.pallas_call(
        _merge_kernel,
        grid=(d // tn, t // tm),
        in_specs=[
            pl.BlockSpec((tm, MLA_W), lambda j, i: (i, 0)),
            pl.BlockSpec((tm, ML_W), lambda j, i: (i, 0)),
            pl.BlockSpec((tm, tn), lambda j, i: (i, COL_GA // tn + j)),
            pl.BlockSpec((tm, tn), lambda j, i: (i, COL_GB // tn + j)),
            pl.BlockSpec((MLA_W, tn), lambda j, i: (0, j)),
            pl.BlockSpec((ML_W, tn), lambda j, i: (0, j)),
        ],
        out_specs=pl.BlockSpec((tm, tn), lambda j, i: (i, j)),
        out_shape=jax.ShapeDtypeStruct((t, d), BF16),
        compiler_params=_cparams(("arbitrary", "arbitrary"), 40),
        name="merge",
    )(out_a, hm, proj, proj, wa, wb)


def _outproj_kernel(mix_ref, x_ref, w_ref, gt_ref, sc_ref, sh_ref, g_ref, wr_ref, br_ref, x1_ref, h2_ref, lg_ref):
    tm = mix_ref.shape[0]
    rows = OUTPROJ_ROW_GROUP
    groups = [slice(r0, r0 + rows) for r0 in range(0, tm, rows)]
    y_next = _dot(mix_ref[groups[0], :], w_ref[...])
    for n, rs in enumerate(groups):
        y = y_next
        if n + 1 < len(groups):
            y_next = _dot(mix_ref[groups[n + 1], :], w_ref[...])
        x1 = x_ref[rs, :] + gt_ref[...] * y
        x1_ref[rs, :] = x1
        h2 = x1 * _rms_scale(x1, x1.shape[-1]) * g_ref[...]
        h2 = h2 * (1.0 + sc_ref[...]) + sh_ref[...]
        h2_ref[rs, :] = h2
        h_hi = h2.astype(BF16)
        h_lo = (h2 - h_hi.astype(F32)).astype(BF16)
        r = _dot(h_hi, wr_ref[...]) + _dot(h_lo, wr_ref[...])
        lg_ref[rs, :] = r[:, :LANES] + r[:, LANES:] + br_ref[...]


def _outproj(mixed, x2, w_out, mod4, norm_g, w_route2, b_route, seq):
    t, d = x2.shape
    tm = min(512, seq)
    bpr = seq // tm
    mod_spec = lambda k: pl.BlockSpec((None, None, 1, d), lambda i: (i // bpr, k, 0, 0))
    const = lambda i: (0, 0)
    return pl.pallas_call(
        _outproj_kernel,
        grid=(t // tm,),
        in_specs=[
            pl.BlockSpec((tm, d), lambda i: (i, 0)),
            pl.BlockSpec((tm, d), lambda i: (i, 0)),
            pl.BlockSpec((d, d), const, pipeline_mode=pl.Buffered(1)),
            mod_spec(2),
            mod_spec(4),
            mod_spec(3),
            pl.BlockSpec((1, d), const),
            pl.BlockSpec((d, 2 * LANES), const, pipeline_mode=pl.Buffered(1)),
            pl.BlockSpec((1, LANES), const),
        ],
        out_specs=[
            pl.BlockSpec((tm, d), lambda i: (i, 0)),
            pl.BlockSpec((tm, d), lambda i: (i, 0)),
            pl.BlockSpec((tm, LANES), lambda i: (i, 0)),
        ],
        out_shape=[
            jax.ShapeDtypeStruct((t, d), F32),
            jax.ShapeDtypeStruct((t, d), F32),
            jax.ShapeDtypeStruct((t, LANES), F32),
        ],
        compiler_params=_cparams(("arbitrary",), 56),
        name="outproj",
    )(mixed, x2, w_out, mod4, mod4, mod4, norm_g.reshape(1, d), w_route2, b_route)


def _route_kernel(lg_ref, pos_ref, wts_ref, cnt_ref, carry_ref, offs_ref, meta_s, wts_s):
    phase = pl.program_id(0)
    i = pl.program_id(1)
    tm = lg_ref.shape[0]
    lane = lax.broadcasted_iota(jnp.int32, (tm, LANES), 1)

    @pl.when((phase == 0) & (i == 0))
    def _():
        carry_ref[...] = jnp.zeros(carry_ref.shape, F32)

    @pl.when(phase == 0)
    def _():
        lg = lg_ref[...]
        big = jnp.int32(LANES)
        ninf = -jnp.inf

        def first_argmax(vals):
            mx = jnp.max(vals, axis=-1, keepdims=True)
            idx = jnp.min(jnp.where(vals == mx, lane, big), axis=-1, keepdims=True)
            return mx, idx

        gl = jnp.where(lane < N_GROUPS, lg, ninf)
        gmax, gsel = first_argmax(gl)
        g_w = 1.0 / jnp.sum(jnp.exp(gl - gmax), axis=-1, keepdims=True)
        lo = N_GROUPS + gsel * EXP_PER_GROUP
        in_grp = (lane >= lo) & (lane < lo + EXP_PER_GROUP)
        el = jnp.where(in_grp, lg, ninf)
        e1, i1 = first_argmax(el)
        e2, i2 = first_argmax(jnp.where(lane == i1, ninf, el))
        p2 = jnp.exp(e2 - e1)
        w1 = g_w / (1.0 + p2)
        w2 = g_w * p2 / (1.0 + p2)

        oh1 = lane == i1
        oh2 = lane == i2
        oh = jnp.where(oh1 | oh2, 1.0, 0.0)
        r = lax.broadcasted_iota(jnp.int32, (tm, tm), 0)
        c = lax.broadcasted_iota(jnp.int32, (tm, tm), 1)
        strict = jnp.where(c < r, 1.0, 0.0).astype(BF16)
        before = _dot(strict, oh.astype(BF16)) + carry_ref[...]
        rank1 = jnp.sum(jnp.where(oh1, before, 0.0), axis=-1, keepdims=True).astype(jnp.int32)
        rank2 = jnp.sum(jnp.where(oh2, before, 0.0), axis=-1, keepdims=True).astype(jnp.int32)
        carry_ref[...] = carry_ref[...] + jnp.sum(oh, axis=0, keepdims=True)
        meta_s[i] = jnp.where(lane == 0, i1, jnp.where(lane == 1, i2, jnp.where(lane == 2, rank1, jnp.where(lane == 3, rank2, 0))))
        wts_s[i] = jnp.where(lane == 0, w1, jnp.where(lane == 1, w2, 0.0))

    @pl.when((phase == 1) & (i == 0))
    def _():
        cnt = carry_ref[...]
        cnt_ref[...] = jnp.broadcast_to(cnt, cnt_ref.shape)
        padded = jnp.ceil(cnt * (1.0 / MOE_TILE)) * MOE_TILE
        r = lax.broadcasted_iota(jnp.int32, (LANES, LANES), 0)
        c = lax.broadcasted_iota(jnp.int32, (LANES, LANES), 1)
        upper = jnp.where(r < c, 1.0, 0.0).astype(BF16)
        padded8 = jnp.broadcast_to(padded, (SUBLANES, LANES)).astype(BF16)
        offs_ref[...] = _dot(padded8, upper)[:1, :]

    @pl.when(phase == 1)
    def _():
        meta = meta_s[i]
        offs = offs_ref[...]
        off1 = jnp.sum(jnp.where(lane == meta[:, 0:1], offs, 0.0), axis=-1, keepdims=True).astype(jnp.int32)
        off2 = jnp.sum(jnp.where(lane == meta[:, 1:2], offs, 0.0), axis=-1, keepdims=True).astype(jnp.int32)
        pos1 = off1 + meta[:, 2:3]
        pos2 = off2 + meta[:, 3:4]
        pos_ref[...] = jnp.where(lane == 0, pos1, jnp.where(lane == 1, pos2, 0))
        wts_ref[...] = wts_s[i]


def _route(logits):
    t = logits.shape[0]
    tm = min(1024, t)
    nb = t // tm
    assert (2 * t) // MOE_TILE + N_EXPERTS <= 256, "tile-padded offsets must stay exact in bf16 (8 significant bits)"
    return pl.pallas_call(
        _route_kernel,
        grid=(2, nb),
        in_specs=[pl.BlockSpec((tm, LANES), lambda p, i: (i * (1 - p), 0))],
        out_specs=[
            pl.BlockSpec((tm, LANES), lambda p, i: (i * p, 0)),
            pl.BlockSpec((tm, LANES), lambda p, i: (i * p, 0)),
            pl.BlockSpec((SUBLANES, LANES), lambda p, i: (0, 0)),
        ],
        out_shape=[
            jax.ShapeDtypeStruct((t, LANES), jnp.int32),
            jax.ShapeDtypeStruct((t, LANES), F32),
            jax.ShapeDtypeStruct((SUBLANES, LANES), F32),
        ],
        scratch_shapes=[
            pltpu.VMEM((1, LANES), F32),
            pltpu.VMEM((1, LANES), F32),
            pltpu.VMEM((nb, tm, LANES), jnp.int32),
            pltpu.VMEM((nb, tm, LANES), F32),
        ],
        compiler_params=_cparams(("arbitrary", "arbitrary"), 32),
        name="route",
    )(logits)


def _row_gather_start(idx_at, src_hbm, dst, sem, rows):
    for r in range(rows):
        pltpu.make_async_copy(src_hbm.at[pl.ds(idx_at(r), 1)], dst.at[pl.ds(r, 1)], sem).start(priority=r % 2)


def _row_gather_wait(src_hbm, dst, sem, rows):
    pltpu.make_async_copy(src_hbm.at[pl.ds(0, rows)], dst, sem).wait()


def _dispatch_kernel(pos_ref, zrow_ref, h_ref, xs_hbm, zbuf, sems):
    i = pl.program_id(0)
    tm = h_ref.shape[0]
    tile = zbuf.shape[0]

    def zero_copy(z):
        return pltpu.make_async_copy(zbuf, xs_hbm.at[pl.ds(pl.multiple_of(zrow_ref[z], tile), tile)], sems.at[0])

    @pl.when(i == 0)
    def _():
        zbuf[...] = jnp.zeros(zbuf.shape, zbuf.dtype)
        for z in range(zrow_ref.shape[0]):
            @pl.when(zrow_ref[z] >= 0)
            def _():
                zero_copy(z).start()
        for z in range(zrow_ref.shape[0]):
            @pl.when(zrow_ref[z] >= 0)
            def _():
                zero_copy(z).wait()

    for k in range(2):
        base = (2 * i + k) * tm
        for r in range(tm):
            pltpu.make_async_copy(h_ref.at[pl.ds(r, 1)], xs_hbm.at[pl.ds(pos_ref[base + r], 1)],
                                  sems.at[1]).start(priority=r % 2)
    for k in range(2):
        pltpu.make_async_copy(h_ref, xs_hbm.at[pl.ds(0, tm)], sems.at[1]).wait()


def _dispatch(pos, zrow, h2, n_rows, seq):
    t, d = h2.shape
    tm = min(1024, seq)
    pos_blocks = pos.reshape(t // tm, tm, 2).transpose(0, 2, 1).reshape(-1)
    grid_spec = pltpu.PrefetchScalarGridSpec(
        num_scalar_prefetch=2,
        grid=(t // tm,),
        in_specs=[pl.BlockSpec((tm, d), lambda i, p, z: (i, 0))],
        out_specs=pl.BlockSpec(memory_space=pl.ANY),
        scratch_shapes=[
            pltpu.VMEM((MOE_TILE, d), F32),
            pltpu.SemaphoreType.DMA((2,)),
        ],
    )
    return pl.pallas_call(
        _dispatch_kernel,
        grid_spec=grid_spec,
        out_shape=jax.ShapeDtypeStruct((n_rows, d), F32),
        compiler_params=_cparams(("arbitrary",), 40),
        name="moe_dispatch",
    )(pos_blocks, zrow, h2)


def _moe_kernel(plan_ref, x_ref, wg_hbm, wu_hbm, wd_hbm, y_ref, stg_g, stg_u, stg_d, wsem):
    j = pl.program_id(0)
    nt = plan_ref[3, 0]
    nslot = stg_g.shape[0]

    def w_copies(e, s):
        return (pltpu.make_async_copy(wg_hbm.at[e], stg_g.at[s], wsem.at[s]),
                pltpu.make_async_copy(wu_hbm.at[e], stg_u.at[s], wsem.at[s]),
                pltpu.make_async_copy(wd_hbm.at[e], stg_d.at[s], wsem.at[s]))

    @pl.when(j == 0)
    def _():
        for s in range(nslot - 1):
            @pl.when(plan_ref[3, 1 + s] >= 0)
            def _():
                for cp in w_copies(plan_ref[3, 1 + s], s):
                    cp.start()

    @pl.when((j < nt) & (plan_ref[0, j] == 1))
    def _():
        s = plan_ref[1, j]
        for cp in w_copies(0, s):
            cp.wait()

        @pl.when(plan_ref[2, j] >= 0)
        def _():
            free = lax.rem(s + nslot - 1, nslot)
            for cp in w_copies(plan_ref[2, j], free):
                cp.start()

    @pl.when(j < nt)
    def _():
        s = plan_ref[1, j]
        x = x_ref[...].astype(BF16)
        hg = _dot(x, stg_g[s].astype(BF16))
        hu = _dot(x, stg_u[s].astype(BF16))
        act = (hg * _sigmoid(hg) * hu).astype(BF16)
        y_ref[...] = _dot(act, stg_d[s].astype(BF16))

    @pl.when(j >= nt)
    def _():
        y_ref[...] = jnp.zeros(y_ref.shape, y_ref.dtype)


def _moe(plan, xs, w_gate_e, w_up_e, w_down_e):
    d = xs.shape[1]
    f = w_gate_e.shape[-1]
    nt_max = plan.shape[1]
    tile = MOE_TILE
    any_spec = pl.BlockSpec(memory_space=pl.ANY)
    grid_spec = pltpu.PrefetchScalarGridSpec(
        num_scalar_prefetch=1,
        grid=(nt_max,),
        in_specs=[
            pl.BlockSpec((tile, d), lambda j, plan_: (jnp.minimum(j, plan_[3, 0] - 1), 0)),
            any_spec, any_spec, any_spec,
        ],
        out_specs=pl.BlockSpec((tile, d), lambda j, plan_: (j, 0)),
        scratch_shapes=[
            pltpu.VMEM((MOE_WEIGHT_SLOTS, d, f), F32),
            pltpu.VMEM((MOE_WEIGHT_SLOTS, d, f), F32),
            pltpu.VMEM((MOE_WEIGHT_SLOTS, f, d), F32),
            pltpu.SemaphoreType.DMA((MOE_WEIGHT_SLOTS,)),
        ],
    )
    return pl.pallas_call(
        _moe_kernel,
        grid_spec=grid_spec,
        out_shape=jax.ShapeDtypeStruct((nt_max * tile, d), F32),
        compiler_params=_cparams(("arbitrary",), 56),
        name="moe_experts",
    )(plan, xs, w_gate_e, w_up_e, w_down_e)


def _combine_kernel(pos_ref, y_hbm, x1_ref, gt_ref, wts_ref, o_ref, ybuf, sem):
    i = pl.program_id(0)
    n = pl.num_programs(0)
    tm = x1_ref.shape[0]
    slot = lax.rem(i, 2)

    def start(blk, s):
        for k in range(2):
            base = (2 * blk + k) * tm
            _row_gather_start(lambda r, base=base: pos_ref[base + r], y_hbm, ybuf.at[s, k], sem.at[s], tm)

    @pl.when(i == 0)
    def _():
        start(0, 0)

        @pl.when(n > 1)
        def _():
            start(1, 1)

    for k in range(2):
        _row_gather_wait(y_hbm, ybuf.at[slot, k], sem.at[slot], tm)
    w = wts_ref[...]
    moe = w[:, 0:1] * ybuf[slot, 0] + w[:, 1:2] * ybuf[slot, 1]
    o_ref[...] = x1_ref[...] + gt_ref[...] * moe

    @pl.when(i + 2 < n)
    def _():
        start(i + 2, slot)


def _combine(pos, ys, x1, mod4, wts, seq):
    t, d = x1.shape
    tm = min(512, seq)
    bpr = seq // tm
    pos_blocks = pos.reshape(t // tm, tm, 2).transpose(0, 2, 1).reshape(-1)
    grid_spec = pltpu.PrefetchScalarGridSpec(
        num_scalar_prefetch=1,
        grid=(t // tm,),
        in_specs=[
            pl.BlockSpec(memory_space=pl.ANY),
            pl.BlockSpec((tm, d), lambda i, p: (i, 0)),
            pl.BlockSpec((None, None, 1, d), lambda i, p: (i // bpr, 5, 0, 0)),
            pl.BlockSpec((tm, LANES), lambda i, p: (i, 0)),
        ],
        out_specs=pl.BlockSpec((tm, d), lambda i, p: (i, 0)),
        scratch_shapes=[
            pltpu.VMEM((2, 2, tm, d), F32),
            pltpu.SemaphoreType.DMA((2,)),
        ],
    )
    return pl.pallas_call(
        _combine_kernel,
        grid_spec=grid_spec,
        out_shape=jax.ShapeDtypeStruct((t, d), F32),
        compiler_params=_cparams(("arbitrary",), 48),
        name="moe_combine",
    )(pos_blocks, ys, x1, mod4, wts)


def _swap_halves(w):
    half = w.shape[-1] // 2
    return jnp.concatenate([w[..., half:], w[..., :half]], axis=-1)


def _layout_w_uq(w_uq):
    r = w_uq.shape[0]
    w = w_uq.reshape(r, MLA_HEADS, MLA_QK)
    rope = w[..., MLA_NOPE:]
    return jnp.concatenate([w[..., :MLA_NOPE], rope, _swap_halves(rope)], axis=-1).reshape(r, MLA_HEADS * QK_PAD).astype(BF16)


def _layout_w_ukv(w_ukv):
    r = w_ukv.shape[0]
    w = w_ukv.reshape(r, MLA_HEADS, MLA_NOPE + MLA_V)
    return jnp.concatenate([w[..., :MLA_NOPE].reshape(r, -1), w[..., MLA_NOPE:].reshape(r, -1)], axis=-1).astype(BF16)


def _rope_gain(g):
    g1 = g[MLA_NOPE:MLA_NOPE + MLA_ROPE // 2]
    g2 = g[MLA_NOPE + MLA_ROPE // 2:]
    return jnp.concatenate([g[:MLA_NOPE], g1, g2, -g2, g1]).reshape(1, QK_PAD)


def _layer(x2, cond_mod4, pos2, seq, w_in, q_a_norm_g, w_uq, kv_a_norm_g, w_ukv, q_norm_g, k_norm_g, conv_w, conv_b,
           b_mlstm_gates, mlstm_norm_g, w_proj_a, w_proj_b, w_out, norm_mix_g, norm_ffn_g, w_group, b_group,
           w_router, b_router, w_gate_e, w_up_e, w_down_e):
    t, d = x2.shape
    b = t // seq
    mod4 = cond_mod4

    proj, gates = _inproj(x2, mod4, norm_mix_g, w_in.T, seq)

    inv = ROPE_THETA ** (-jnp.arange(0, MLA_ROPE, 2, dtype=F32) / MLA_ROPE)
    inv_lanes = jnp.tile(inv, LANES // (MLA_ROPE // 2)).reshape(1, LANES)
    qt, k, vt = _mla_prep(proj, pos2, _layout_w_uq(w_uq), _layout_w_ukv(w_ukv), q_a_norm_g.reshape(1, -1),
                        kv_a_norm_g.reshape(1, -1), _rope_gain(q_norm_g), _rope_gain(k_norm_g), inv_lanes, seq)
    out_a = _flash(qt, k.reshape(b, seq, -1), vt).reshape(t, MLA_W)

    proj3 = proj.reshape(b, seq, -1)
    ml_qt, ml_k = _conv_silu(proj3, conv_w, conv_b)
    gbias = jnp.zeros((1, LANES), F32).at[0, GATE_LANE:GATE_LANE + 2 * ML_HEADS].set(b_mlstm_gates.reshape(-1))
    hm = _mlstm(ml_qt, ml_k, proj3, gates.reshape(b, seq, LANES), gbias, mlstm_norm_g).reshape(t, ML_W)

    mixed = _merge(out_a, hm, proj, w_proj_a.astype(BF16), w_proj_b.astype(BF16), seq)

    route_pad = LANES - N_GROUPS - N_EXPERTS
    w_route = jnp.concatenate([w_group, w_router, jnp.zeros((d, route_pad), F32)], axis=1)
    b_route = jnp.concatenate([b_group, b_router, jnp.zeros((route_pad,), F32)]).reshape(1, LANES)
    w_route_hi = w_route.astype(BF16)
    w_route2 = jnp.concatenate([w_route_hi, (w_route - w_route_hi.astype(F32)).astype(BF16)], axis=1)
    x1, h2, logits = _outproj(mixed, x2, w_out.astype(BF16), mod4, norm_ffn_g, w_route2, b_route, seq)

    posm, wts, counts = _route(logits)
    pos = posm[:, 0:2]

    tile = MOE_TILE
    i32 = jnp.int32
    cnt = counts[0, N_GROUPS:N_GROUPS + N_EXPERTS].astype(i32)
    padded = ((cnt + tile - 1) // tile) * tile
    ends = jnp.cumsum(padded)
    offs = ends - padded
    nt_max = (2 * t) // tile + N_EXPERTS
    n_tiles = ends[-1] // tile
    tile_idx = jnp.arange(nt_max, dtype=i32)
    tile_start = tile_idx * tile
    live = tile_idx < n_tiles
    texp = jnp.minimum(jnp.sum((ends[None, :] <= tile_start[:, None]).astype(i32), axis=1), N_EXPERTS - 1)
    active = cnt > 0
    order = jnp.cumsum(active.astype(i32)) - 1
    n_active = jnp.sum(active.astype(i32))
    experts = jnp.arange(N_EXPERTS, dtype=i32)
    by_order = jnp.sum(jnp.where(active[None, :] & (order[None, :] == experts[:, None]), experts[None, :], 0), axis=1)
    t_order = order[texp]
    first = (live & (tile_start == offs[texp])).astype(i32)
    nslot = MOE_WEIGHT_SLOTS
    ahead = t_order + nslot - 1
    prefetch = jnp.where(ahead < n_active, by_order[jnp.minimum(ahead, N_EXPERTS - 1)], -1)
    lead = jnp.where(experts[:nslot - 1] < n_active, by_order[:nslot - 1], -1)
    head = jnp.zeros((nt_max,), i32).at[0].set(n_tiles).at[1:nslot].set(lead)
    plan = jnp.stack([first, t_order % nslot, prefetch, head]).astype(i32)

    last_tile = jnp.where(padded > cnt, ends - tile, -1)
    spare_idx = n_tiles + jnp.arange(N_EXPERTS, dtype=i32)
    spare_tile = jnp.where(spare_idx < nt_max, spare_idx * tile, -1)
    zrow = jnp.concatenate([last_tile, spare_tile]).astype(i32)

    xs = _dispatch(pos, zrow, h2, nt_max * tile, seq)
    ys = _moe(plan, xs, w_gate_e, w_up_e, w_down_e)

    return _combine(pos, ys, x1, mod4, wts, seq)


def kernel(x, c, positions, w_ada, b_ada, norm_mix_g, w_in, q_a_norm_g, w_uq, kv_a_norm_g, w_ukv, q_norm_g, k_norm_g, conv_w, conv_b, b_mlstm_gates, mlstm_norm_g, w_proj_a, w_proj_b, w_out, norm_ffn_g, w_group, b_group, w_router, b_router, w_gate_e, w_up_e, w_down_e):
    b, seq, d = x.shape
    depth = w_ada.shape[0]
    x2 = x.reshape(b * seq, d)
    pos2 = positions.reshape(b * seq, 1)
    c_pad = jnp.zeros((SUBLANES, d), F32).at[:b].set(c)
    for l in range(depth):
        mod = _adaln(c_pad, w_ada[l], b_ada[l])
        mod4 = mod[:b].reshape(b, 6, 1, d)
        x2 = _layer(x2, mod4, pos2, seq, w_in[l], q_a_norm_g[l], w_uq[l], kv_a_norm_g[l], w_ukv[l], q_norm_g[l],
                    k_norm_g[l], conv_w[l], conv_b[l], b_mlstm_gates[l], mlstm_norm_g[l], w_proj_a[l], w_proj_b[l],
                    w_out[l], norm_mix_g[l], norm_ffn_g[l], w_group[l], b_group[l], w_router[l], b_router[l],
                    w_gate_e[l], w_up_e[l], w_down_e[l])
    return x2.reshape(b, seq, d)
```

```python
import functools
import math

import jax
import jax.numpy as jnp
from jax import lax
from jax.experimental import pallas as pl
from jax.experimental.pallas import tpu as pltpu

F32 = jnp.float32
BF16 = jnp.bfloat16

LANES = 128
SUBLANES = 8

D_MODEL = 2048
MLA_HEADS = 8
MLA_NOPE = 128
MLA_ROPE = 64
MLA_QK = MLA_NOPE + MLA_ROPE
MLA_V = 128
Q_LORA = 512
KV_LORA = 256
ROPE_THETA = 10000.0
ML_HEADS = 8
ML_DQK = 128
ML_DV = 128
ML_CONV = 4
MLA_W = MLA_HEADS * MLA_V
ML_W = ML_HEADS * ML_DV
N_GROUPS = 4
EXP_PER_GROUP = 8
N_EXPERTS = N_GROUPS * EXP_PER_GROUP
EPS = 1e-6

QK_PAD = 2 * LANES

SRC_CQ = 0
SRC_KPE = Q_LORA + KV_LORA
SRC_QK = SRC_KPE + MLA_ROPE
SRC_V = SRC_QK + 2 * ML_HEADS * ML_DQK
SRC_O = SRC_V + ML_W
SRC_I = SRC_O + ML_W
SRC_GA = SRC_I + 2 * ML_HEADS
SRC_GB = SRC_GA + D_MODEL
GATE_LANE = SRC_I % LANES

IN_BLOCK = 1024
IN_BLOCK_SRC = (SRC_QK, SRC_QK + IN_BLOCK, SRC_GA, SRC_GA + IN_BLOCK, SRC_GB, SRC_GB + IN_BLOCK, SRC_V, SRC_O, SRC_CQ)
COL_QK = 0
COL_GA = 2048
COL_GB = 4096
COL_V = 6144
COL_O = 7168
COL_LAT = 8192
LAT_W = Q_LORA + KV_LORA + MLA_ROPE

MLSTM_CHUNK = 256
MOE_TILE = 256
MOE_WEIGHT_SLOTS = 3
FLASH_HEADS_PER_STEP = 2
FLASH_Q_TILE = 512
FLASH_K_TILE = 512
OUTPROJ_ROW_GROUP = 256
MERGE_ROW_GROUP = 256


def _cparams(sem, vmem_mb):
    return pltpu.CompilerParams(dimension_semantics=sem, vmem_limit_bytes=vmem_mb * 1024 * 1024)


def _dot(a, b):
    return jnp.dot(a, b, preferred_element_type=F32)


def _dot_nt(a, b):
    return lax.dot_general(a, b, (((1,), (1,)), ((), ())), preferred_element_type=F32)


def _sigmoid(x):
    return 1.0 / (1.0 + jnp.exp(-x))


def _rms_scale(x, width):
    return lax.rsqrt(jnp.sum(x * x, axis=-1, keepdims=True) * (1.0 / width) + EPS)


def _adaln_kernel(c_ref, w_ref, b_ref, o_ref):
    c = c_ref[...]
    cond = (c * _sigmoid(c)).astype(BF16)
    o_ref[...] = _dot(cond, w_ref[...].astype(BF16)) + b_ref[...]


def _adaln(c_pad, w_ada, b_ada):
    rows, d = c_pad.shape
    n = w_ada.shape[1]
    tn = 2048
    return pl.pallas_call(
        _adaln_kernel,
        grid=(n // tn,),
        in_specs=[
            pl.BlockSpec((rows, d), lambda j: (0, 0)),
            pl.BlockSpec((d, tn), lambda j: (0, j)),
            pl.BlockSpec((1, tn), lambda j: (0, j)),
        ],
        out_specs=pl.BlockSpec((rows, tn), lambda j: (0, j)),
        out_shape=jax.ShapeDtypeStruct((rows, n), F32),
        compiler_params=_cparams(("arbitrary",), 48),
        name="adaln",
    )(c_pad, w_ada, b_ada.reshape(1, n))


def _inproj_kernel(off_ref, x_ref, sc_ref, sh_ref, g_ref, w_ref, wgate_ref, proj_ref, gates_ref, h_ref):
    j = pl.program_id(1)

    @pl.when(j == 0)
    def _():
        x = x_ref[...]
        h = x * _rms_scale(x, x.shape[-1]) * g_ref[...]
        h = (h * (1.0 + sc_ref[...]) + sh_ref[...]).astype(BF16)
        h_ref[...] = h
        gates_ref[...] = _dot_nt(h, wgate_ref[...].astype(BF16))

    proj_ref[...] = _dot_nt(h_ref[...], w_ref[...].astype(BF16)).astype(proj_ref.dtype)


def _inproj(x2, mod4, norm_g, w_in_t, seq):
    t, d = x2.shape
    tm = min(1024, seq)
    tn = IN_BLOCK
    bpr = seq // tm
    assert all(o % SUBLANES == 0 for o in IN_BLOCK_SRC)
    offs = jnp.asarray([o // SUBLANES for o in IN_BLOCK_SRC], jnp.int32)
    nblk = len(IN_BLOCK_SRC)
    gate_tile = SRC_I // LANES
    grid_spec = pltpu.PrefetchScalarGridSpec(
        num_scalar_prefetch=1,
        grid=(t // tm, nblk),
        in_specs=[
            pl.BlockSpec((tm, d), lambda i, j, o: (i, 0)),
            pl.BlockSpec((None, None, 1, d), lambda i, j, o: (i // bpr, 1, 0, 0)),
            pl.BlockSpec((None, None, 1, d), lambda i, j, o: (i // bpr, 0, 0, 0)),
            pl.BlockSpec((1, d), lambda i, j, o: (0, 0)),
            pl.BlockSpec((pl.Element(tn), pl.Element(d)), lambda i, j, o: (o[j] * SUBLANES, 0)),
            pl.BlockSpec((LANES, d), lambda i, j, o: (gate_tile, 0)),
        ],
        out_specs=[
            pl.BlockSpec((tm, tn), lambda i, j, o: (i, j)),
            pl.BlockSpec((tm, LANES), lambda i, j, o: (i, 0)),
        ],
        scratch_shapes=[pltpu.VMEM((tm, d), BF16)],
    )
    return pl.pallas_call(
        _inproj_kernel,
        grid_spec=grid_spec,
        out_shape=[
            jax.ShapeDtypeStruct((t, nblk * tn), BF16),
            jax.ShapeDtypeStruct((t, LANES), F32),
        ],
        compiler_params=_cparams(("arbitrary", "arbitrary"), 56),
        name="inproj",
    )(offs, x2, mod4, mod4, norm_g.reshape(1, d), w_in_t, w_in_t)


def _mla_prep_kernel(lat_ref, pos_ref, wuq_ref, wukv_ref, gqa_ref, gkva_ref, gq_ref, gk_ref, inv_ref,
                     qt_ref, k_ref, vt_ref):
    lat = lat_ref[...].astype(F32)
    cq = lat[:, :Q_LORA]
    ckv = lat[:, Q_LORA:Q_LORA + KV_LORA]
    kc = lat[:, SRC_KPE:SRC_KPE + LANES]
    cqn = (cq * _rms_scale(cq, Q_LORA) * gqa_ref[...]).astype(BF16)
    ckvn = (ckv * _rms_scale(ckv, KV_LORA) * gkva_ref[...]).astype(BF16)
    qraw = _dot(cqn, wuq_ref[...])
    kv = _dot(ckvn, wukv_ref[...])

    ang = pos_ref[...].astype(F32) * inv_ref[...]
    lane = lax.broadcasted_iota(jnp.int32, ang.shape, 1)
    lo = lane < MLA_ROPE
    cs = jnp.cos(ang - jnp.where(lo, 0.0, 0.5 * math.pi))

    quarter = MLA_ROPE // 2
    want = jnp.where(lane < 3 * quarter, lane - quarter, lane - 3 * quarter)
    came = pltpu.roll(lane, quarter, 1)
    swapped = jnp.where(came == want, pltpu.roll(kc, quarter, 1), pltpu.roll(kc, 3 * quarter, 1))
    kc = jnp.where(lo, kc, swapped)

    gq = gq_ref[...]
    gk = gk_ref[...]
    gq_n, gq_r = gq[:, :LANES], gq[:, LANES:]
    gk_n, gk_r = gk[:, :LANES], gk[:, LANES:]

    def rope(chunk, g_cs):
        a = chunk * g_cs
        return jnp.where(lo, a + pltpu.roll(a, MLA_ROPE, 1), 0.0)

    gq_cs = gq_r * cs
    kpe_ss = jnp.sum(jnp.where(lo, kc * kc, 0.0), axis=-1, keepdims=True)
    k_rope = rope(kc, gk_r * cs)
    scale = MLA_QK ** -0.5 * math.log2(math.e)
    for h in range(MLA_HEADS):
        kn = kv[:, h * MLA_NOPE:(h + 1) * MLA_NOPE]
        sk = lax.rsqrt((jnp.sum(kn * kn, axis=-1, keepdims=True) + kpe_ss) * (1.0 / MLA_QK) + EPS)
        k_ref[:, h * QK_PAD:h * QK_PAD + LANES] = (kn * sk * gk_n).astype(BF16)
        k_ref[:, h * QK_PAD + LANES:(h + 1) * QK_PAD] = (k_rope * sk).astype(BF16)
        qn = qraw[:, h * QK_PAD:h * QK_PAD + LANES]
        qr = qraw[:, h * QK_PAD + LANES:(h + 1) * QK_PAD]
        ss = jnp.sum(qn * qn, axis=-1, keepdims=True) + jnp.sum(jnp.where(lo, qr * qr, 0.0), axis=-1, keepdims=True)
        sq = lax.rsqrt(ss * (1.0 / MLA_QK) + EPS) * scale
        qt_ref[h * QK_PAD:h * QK_PAD + LANES, :] = (qn * sq * gq_n).astype(BF16).T
        qt_ref[h * QK_PAD + LANES:(h + 1) * QK_PAD, :] = (rope(qr, gq_cs) * sq).astype(BF16).T
        vh = kv[:, MLA_HEADS * MLA_NOPE + h * MLA_V:MLA_HEADS * MLA_NOPE + (h + 1) * MLA_V]
        vt_ref[h * MLA_V:(h + 1) * MLA_V, :] = vh.astype(BF16).T


def _mla_prep(proj, pos2, wuq_p, wukv_p, gqa, gkva, gq, gk, inv_lanes, seq):
    t = proj.shape[0]
    tm = min(512, seq)
    hq = MLA_HEADS * QK_PAD
    lat_blk = COL_LAT // IN_BLOCK
    const = lambda i: (0, 0)
    return pl.pallas_call(
        _mla_prep_kernel,
        grid=(t // tm,),
        in_specs=[
            pl.BlockSpec((tm, IN_BLOCK), lambda i: (i, lat_blk)),
            pl.BlockSpec((tm, 1), lambda i: (i, 0)),
            pl.BlockSpec(wuq_p.shape, const),
            pl.BlockSpec(wukv_p.shape, const),
            pl.BlockSpec(gqa.shape, const),
            pl.BlockSpec(gkva.shape, const),
            pl.BlockSpec(gq.shape, const),
            pl.BlockSpec(gk.shape, const),
            pl.BlockSpec(inv_lanes.shape, const),
        ],
        out_specs=[
            pl.BlockSpec((hq, tm), lambda i: (0, i)),
            pl.BlockSpec((tm, hq), lambda i: (i, 0)),
            pl.BlockSpec((MLA_W, tm), lambda i: (0, i)),
        ],
        out_shape=[
            jax.ShapeDtypeStruct((hq, t), BF16),
            jax.ShapeDtypeStruct((t, hq), BF16),
            jax.ShapeDtypeStruct((MLA_W, t), BF16),
        ],
        compiler_params=_cparams(("arbitrary",), 48),
        name="mla_prep",
    )(proj, pos2, wuq_p, wukv_p, gqa, gkva, gq, gk, inv_lanes)


def _flash_kernel(qt_ref, k_ref, vt_ref, o_ref, *, tq, tk):
    seq = k_ref.shape[0]
    heads = k_ref.shape[1] // QK_PAD

    def scores(h, k0, q0):
        kj = k_ref[k0:k0 + tk, h * QK_PAD:(h + 1) * QK_PAD]
        return _dot(kj, qt_ref[h * QK_PAD:(h + 1) * QK_PAD, q0:q0 + tq])

    def update(h, state, st, k0, q0):
        m, l, acc = state
        if k0 + tk - 1 > q0:
            key = lax.broadcasted_iota(jnp.int32, st.shape, 0) + k0
            qry = lax.broadcasted_iota(jnp.int32, st.shape, 1) + q0
            st = jnp.where(key <= qry, st, -jnp.inf)
        m_new = jnp.maximum(m, jnp.max(st, axis=0, keepdims=True))
        alpha = jnp.exp2(m - m_new)
        p = jnp.exp2(st - m_new)
        l = alpha * l + jnp.sum(p, axis=0, keepdims=True)
        acc = alpha * acc + _dot(vt_ref[h * MLA_V:(h + 1) * MLA_V, k0:k0 + tk], p.astype(BF16))
        return m_new, l, acc

    steps = [(h, qi * tq, j * tk) for qi in range(seq // tq) for j in range((qi + 1) * tq // tk) for h in range(heads)]
    st_next = scores(steps[0][0], steps[0][2], steps[0][1])
    states = {}
    for n, (h, q0, k0) in enumerate(steps):
        st = st_next
        if n + 1 < len(steps):
            hn, qn, kn = steps[n + 1]
            st_next = scores(hn, kn, qn)
        if k0 == 0:
            states[h] = (jnp.full((1, tq), -jnp.inf, F32), jnp.zeros((1, tq), F32), jnp.zeros((MLA_V, tq), F32))
        states[h] = update(h, states[h], st, k0, q0)
        if k0 + tk >= q0 + tq:
            _, l, acc = states[h]
            o_ref[q0:q0 + tq, h * MLA_V:(h + 1) * MLA_V] = (acc / l).T.astype(o_ref.dtype)


def _flash(qt, k3, vt):
    b, seq, _ = k3.shape
    tq = min(FLASH_Q_TILE, seq)
    tk = min(FLASH_K_TILE, seq)
    hp = FLASH_HEADS_PER_STEP
    kern = functools.partial(_flash_kernel, tq=tq, tk=tk)
    return pl.pallas_call(
        kern,
        grid=(b, MLA_HEADS // hp),
        in_specs=[
            pl.BlockSpec((hp * QK_PAD, seq), lambda i, h: (h, i)),
            pl.BlockSpec((None, seq, hp * QK_PAD), lambda i, h: (i, 0, h)),
            pl.BlockSpec((hp * MLA_V, seq), lambda i, h: (h, i)),
        ],
        out_specs=pl.BlockSpec((None, seq, hp * MLA_V), lambda i, h: (i, 0, h)),
        out_shape=jax.ShapeDtypeStruct((b, seq, MLA_W), BF16),
        compiler_params=_cparams(("arbitrary", "arbitrary"), 40),
        name="flash",
    )(qt, k3, vt)


def _conv_kernel(cur_ref, halo_ref, w_ref, b_ref, qt_ref, k_ref, buf_ref, *, k_scale):
    tm = cur_ref.shape[0]
    cols = cur_ref.shape[1]
    half = cols // 2
    first = pl.program_id(1) == 0
    halo = halo_ref[...].astype(F32)
    buf_ref[0:SUBLANES, :] = jnp.where(first, 0.0, halo)
    buf_ref[SUBLANES:SUBLANES + tm, :] = cur_ref[...].astype(F32)
    cw = 512
    for c in range(cols // cw):
        sl = slice(c * cw, (c + 1) * cw)
        acc = jnp.zeros((tm, cw), F32) + b_ref[:, sl]
        for j in range(ML_CONV):
            off = SUBLANES - (ML_CONV - 1) + j
            acc = acc + buf_ref[off:off + tm, sl] * w_ref[j:j + 1, sl]
        y = acc * _sigmoid(acc)
        if c * cw < half:
            qt_ref[sl, :] = y.astype(qt_ref.dtype).T
        else:
            k_ref[:, c * cw - half:(c + 1) * cw - half] = (y * k_scale).astype(k_ref.dtype)


def _conv_silu(proj3, conv_w, conv_b):
    b, seq, _ = proj3.shape
    cols = 2 * ML_HEADS * ML_DQK
    half = cols // 2
    tm = min(512, seq)
    hb = tm // SUBLANES
    kern = functools.partial(_conv_kernel, k_scale=ML_DQK ** -0.5)
    return pl.pallas_call(
        kern,
        grid=(b, seq // tm),
        in_specs=[
            pl.BlockSpec((None, tm, cols), lambda i, s: (i, s, COL_QK // cols)),
            pl.BlockSpec((None, SUBLANES, cols), lambda i, s: (i, jnp.maximum(s * hb - 1, 0), COL_QK // cols)),
            pl.BlockSpec((ML_CONV, cols), lambda i, s: (0, 0)),
            pl.BlockSpec((1, cols), lambda i, s: (0, 0)),
        ],
        out_specs=[
            pl.BlockSpec((None, half, tm), lambda i, s: (i, 0, s)),
            pl.BlockSpec((None, tm, half), lambda i, s: (i, s, 0)),
        ],
        out_shape=[
            jax.ShapeDtypeStruct((b, half, seq), BF16),
            jax.ShapeDtypeStruct((b, seq, half), BF16),
        ],
        scratch_shapes=[pltpu.VMEM((tm + SUBLANES, cols), F32)],
        compiler_params=_cparams(("arbitrary", "arbitrary"), 40),
        name="conv_silu",
    )(proj3, proj3, conv_w, conv_b.reshape(1, cols))


def _log_sigmoid(x):
    return -(jnp.maximum(-x, 0.0) + jnp.log1p(jnp.exp(-jnp.abs(x))))


def _mlstm_kernel(qt_ref, k_ref, v_ref, o_ref, gates_ref, gbias_ref, ng_ref, out_ref, ct_ref, m_ref):
    L = k_ref.shape[0]

    @pl.when(pl.program_id(1) == 0)
    def _():
        ct_ref[...] = jnp.zeros(ct_ref.shape, F32)
        m_ref[...] = jnp.zeros(m_ref.shape, F32)

    g = gates_ref[...] + gbias_ref[...]
    gt = g.T
    lf = _log_sigmoid(g)
    lft = _log_sigmoid(gt)
    r = lax.broadcasted_iota(jnp.int32, (L, L), 0)
    c = lax.broadcasted_iota(jnp.int32, (L, L), 1)
    src_le_qry = r <= c
    tril = (c <= r).astype(F32)
    triu = src_le_qry.astype(F32)
    hi = lax.Precision.HIGHEST
    bcol_all = jnp.dot(tril, lf, preferred_element_type=F32, precision=hi)
    brow_all = jnp.dot(lft, triu, preferred_element_type=F32, precision=hi)
    row = lax.broadcasted_iota(jnp.int32, (ML_DV, L), 0)
    ones_row = jnp.where(row == 0, 1.0, 0.0)

    def lead_matmuls(h):
        hs = slice(h * ML_DQK, (h + 1) * ML_DQK)
        qt = qt_ref[hs, :]
        return _dot(k_ref[:, hs], qt), _dot(ct_ref[h].astype(BF16), qt)

    lead_next = lead_matmuls(0)
    for h in range(ML_HEADS):
        hs = slice(h * ML_DQK, (h + 1) * ML_DQK)
        qk_t, cq_t = lead_next
        if h + 1 < ML_HEADS:
            lead_next = lead_matmuls(h + 1)
        li, lf_ = GATE_LANE + h, GATE_LANE + ML_HEADS + h
        b_row = brow_all[lf_:lf_ + 1, :]
        i_row = gt[li:li + 1, :]
        u_col = g[:, li:li + 1] - bcol_all[:, lf_:lf_ + 1]
        m_prev = m_ref[h][:, :1]
        logw_t = jnp.where(src_le_qry, b_row + u_col, -jnp.inf)
        log_inter = b_row + m_prev
        m_t = jnp.maximum(jnp.max(logw_t, axis=0, keepdims=True), log_inter)
        w_t = jnp.exp(logw_t - m_t)
        a = jnp.exp(log_inter - m_t)
        kh = k_ref[:, hs]
        vt_aug = jnp.concatenate([v_ref[:, hs].astype(F32).T, ones_row], axis=0)
        s_t = qk_t * w_t
        ct = ct_ref[h]
        nd = _dot(vt_aug.astype(BF16), s_t.astype(BF16)) + a * cq_t
        num = nd[:ML_DV, :]
        den = nd[ML_DV:ML_DV + 1, :]
        hout_t = num * (1.0 / jnp.maximum(jnp.abs(den), jnp.exp(-m_t)))
        hn_t = hout_t * lax.rsqrt(jnp.sum(hout_t * hout_t, axis=0, keepdims=True) * (1.0 / ML_DV) + EPS)
        gate = _sigmoid(o_ref[:, hs].astype(F32))
        out_ref[:, hs] = (hn_t.T * ng_ref[:, hs] * gate).astype(out_ref.dtype)

        b_last = b_row[:, L - 1:L]
        logg = b_last - b_row + i_row
        m_new = jnp.maximum(b_last + m_prev, jnp.max(logg, axis=-1, keepdims=True))
        g_row = jnp.exp(logg - m_new)
        decay = jnp.exp(b_last + m_prev - m_new)
        ct_ref[h] = decay * ct + _dot((vt_aug * g_row).astype(BF16), kh)
        m_ref[h] = jnp.broadcast_to(m_new, m_ref.shape[1:])


def _mlstm(qt3, k3, proj3, gates3, gbias, norm_g):
    b, seq, w = k3.shape
    L = min(MLSTM_CHUNK, seq)
    return pl.pallas_call(
        _mlstm_kernel,
        grid=(b, seq // L),
        in_specs=[
            pl.BlockSpec((None, w, L), lambda i, c: (i, 0, c)),
            pl.BlockSpec((None, L, w), lambda i, c: (i, c, 0)),
            pl.BlockSpec((None, L, w), lambda i, c: (i, c, COL_V // w)),
            pl.BlockSpec((None, L, w), lambda i, c: (i, c, COL_O // w)),
            pl.BlockSpec((None, L, LANES), lambda i, c: (i, c, 0)),
            pl.BlockSpec((1, LANES), lambda i, c: (0, 0)),
            pl.BlockSpec((1, w), lambda i, c: (0, 0)),
        ],
        out_specs=pl.BlockSpec((None, L, w), lambda i, c: (i, c, 0)),
        out_shape=jax.ShapeDtypeStruct((b, seq, w), BF16),
        scratch_shapes=[
            pltpu.VMEM((ML_HEADS, 2 * ML_DV, ML_DQK), F32),
            pltpu.VMEM((ML_HEADS, 1, LANES), F32),
        ],
        compiler_params=_cparams(("arbitrary", "arbitrary"), 40),
        name="mlstm",
    )(qt3, k3, proj3, proj3, gates3, gbias, norm_g.reshape(1, w))


def _merge_kernel(a_ref, b_ref, ga_ref, gb_ref, wa_ref, wb_ref, o_ref):
    tm = a_ref.shape[0]
    rows = min(MERGE_ROW_GROUP, tm)
    groups = [slice(r0, r0 + rows) for r0 in range(0, tm, rows)]

    def proj(rs):
        return _dot(a_ref[rs, :], wa_ref[...]), _dot(b_ref[rs, :], wb_ref[...])

    nxt = proj(groups[0])
    for n, rs in enumerate(groups):
        pa, pb = nxt
        if n + 1 < len(groups):
            nxt = proj(groups[n + 1])
        mixed = _sigmoid(ga_ref[rs, :].astype(F32)) * pa + _sigmoid(gb_ref[rs, :].astype(F32)) * pb
        o_ref[rs, :] = mixed.astype(o_ref.dtype)


def _merge(out_a, hm, proj, wa, wb, seq):
    t = out_a.shape[0]
    d = wa.shape[1]
    tm = min(512, seq)
    tn = 2048
    return pl.pallas_call(
        _merge_kernel,
        grid=(d // tn, t // tm),
        in_specs=[
            pl.BlockSpec((tm, MLA_W), lambda j, i: (i, 0)),
            pl.BlockSpec((tm, ML_W), lambda j, i: (i, 0)),
            pl.BlockSpec((tm, tn), lambda j, i: (i, COL_GA // tn + j)),
            pl.BlockSpec((tm, tn), lambda j, i: (i, COL_GB // tn + j)),
            pl.BlockSpec((MLA_W, tn), lambda j, i: (0, j)),
            pl.BlockSpec((ML_W, tn), lambda j, i: (0, j)),
        ],
        out_specs=pl.BlockSpec((tm, tn), lambda j, i: (i, j)),
        out_shape=jax.ShapeDtypeStruct((t, d), BF16),
        compiler_params=_cparams(("arbitrary", "arbitrary"), 40),
        name="merge",
    )(out_a, hm, proj, proj, wa, wb)


def _outproj_kernel(mix_ref, x_ref, w_ref, gt_ref, sc_ref, sh_ref, g_ref, wr_ref, br_ref, x1_ref, h2_ref, lg_ref):
    tm = mix_ref.shape[0]
    rows = OUTPROJ_ROW_GROUP
    groups = [slice(r0, r0 + rows) for r0 in range(0, tm, rows)]
    y_next = _dot(mix_ref[groups[0], :], w_ref[...])
    for n, rs in enumerate(groups):
        y = y_next
        if n + 1 < len(groups):
            y_next = _dot(mix_ref[groups[n + 1], :], w_ref[...])
        x1 = x_ref[rs, :] + gt_ref[...] * y
        x1_ref[rs, :] = x1
        h2 = x1 * _rms_scale(x1, x1.shape[-1]) * g_ref[...]
        h2 = h2 * (1.0 + sc_ref[...]) + sh_ref[...]
        h2_ref[rs, :] = h2
        h_hi = h2.astype(BF16)
        h_lo = (h2 - h_hi.astype(F32)).astype(BF16)
        r = _dot(h_hi, wr_ref[...]) + _dot(h_lo, wr_ref[...])
        lg_ref[rs, :] = r[:, :LANES] + r[:, LANES:] + br_ref[...]


def _outproj(mixed, x2, w_out, mod4, norm_g, w_route2, b_route, seq):
    t, d = x2.shape
    tm = min(512, seq)
    bpr = seq // tm
    mod_spec = lambda k: pl.BlockSpec((None, None, 1, d), lambda i: (i // bpr, k, 0, 0))
    const = lambda i: (0, 0)
    return pl.pallas_call(
        _outproj_kernel,
        grid=(t // tm,),
        in_specs=[
            pl.BlockSpec((tm, d), lambda i: (i, 0)),
            pl.BlockSpec((tm, d), lambda i: (i, 0)),
            pl.BlockSpec((d, d), const, pipeline_mode=pl.Buffered(1)),
            mod_spec(2),
            mod_spec(4),
            mod_spec(3),
            pl.BlockSpec((1, d), const),
            pl.BlockSpec((d, 2 * LANES), const, pipeline_mode=pl.Buffered(1)),
            pl.BlockSpec((1, LANES), const),
        ],
        out_specs=[
            pl.BlockSpec((tm, d), lambda i: (i, 0)),
            pl.BlockSpec((tm, d), lambda i: (i, 0)),
            pl.BlockSpec((tm, LANES), lambda i: (i, 0)),
        ],
        out_shape=[
            jax.ShapeDtypeStruct((t, d), F32),
            jax.ShapeDtypeStruct((t, d), F32),
            jax.ShapeDtypeStruct((t, LANES), F32),
        ],
        compiler_params=_cparams(("arbitrary",), 56),
        name="outproj",
    )(mixed, x2, w_out, mod4, mod4, mod4, norm_g.reshape(1, d), w_route2, b_route)


def _route_kernel(lg_ref, pos_ref, wts_ref, cnt_ref, carry_ref, offs_ref, meta_s, wts_s):
    phase = pl.program_id(0)
    i = pl.program_id(1)
    tm = lg_ref.shape[0]
    lane = lax.broadcasted_iota(jnp.int32, (tm, LANES), 1)

    @pl.when((phase == 0) & (i == 0))
    def _():
        carry_ref[...] = jnp.zeros(carry_ref.shape, F32)

    @pl.when(phase == 0)
    def _():
        lg = lg_ref[...]
        big = jnp.int32(LANES)
        ninf = -jnp.inf

        def first_argmax(vals):
            mx = jnp.max(vals, axis=-1, keepdims=True)
            idx = jnp.min(jnp.where(vals == mx, lane, big), axis=-1, keepdims=True)
            return mx, idx

        gl = jnp.where(lane < N_GROUPS, lg, ninf)
        gmax, gsel = first_argmax(gl)
        g_w = 1.0 / jnp.sum(jnp.exp(gl - gmax), axis=-1, keepdims=True)
        lo = N_GROUPS + gsel * EXP_PER_GROUP
        in_grp = (lane >= lo) & (lane < lo + EXP_PER_GROUP)
        el = jnp.where(in_grp, lg, ninf)
        e1, i1 = first_argmax(el)
        e2, i2 = first_argmax(jnp.where(lane == i1, ninf, el))
        p2 = jnp.exp(e2 - e1)
        w1 = g_w / (1.0 + p2)
        w2 = g_w * p2 / (1.0 + p2)

        oh1 = lane == i1
        oh2 = lane == i2
        oh = jnp.where(oh1 | oh2, 1.0, 0.0)
        r = lax.broadcasted_iota(jnp.int32, (tm, tm), 0)
        c = lax.broadcasted_iota(jnp.int32, (tm, tm), 1)
        strict = jnp.where(c < r, 1.0, 0.0).astype(BF16)
        before = _dot(strict, oh.astype(BF16)) + carry_ref[...]
        rank1 = jnp.sum(jnp.where(oh1, before, 0.0), axis=-1, keepdims=True).astype(jnp.int32)
        rank2 = jnp.sum(jnp.where(oh2, before, 0.0), axis=-1, keepdims=True).astype(jnp.int32)
        carry_ref[...] = carry_ref[...] + jnp.sum(oh, axis=0, keepdims=True)
        meta_s[i] = jnp.where(lane == 0, i1, jnp.where(lane == 1, i2, jnp.where(lane == 2, rank1, jnp.where(lane == 3, rank2, 0))))
        wts_s[i] = jnp.where(lane == 0, w1, jnp.where(lane == 1, w2, 0.0))

    @pl.when((phase == 1) & (i == 0))
    def _():
        cnt = carry_ref[...]
        cnt_ref[...] = jnp.broadcast_to(cnt, cnt_ref.shape)
        padded = jnp.ceil(cnt * (1.0 / MOE_TILE)) * MOE_TILE
        r = lax.broadcasted_iota(jnp.int32, (LANES, LANES), 0)
        c = lax.broadcasted_iota(jnp.int32, (LANES, LANES), 1)
        upper = jnp.where(r < c, 1.0, 0.0).astype(BF16)
        padded8 = jnp.broadcast_to(padded, (SUBLANES, LANES)).astype(BF16)
        offs_ref[...] = _dot(padded8, upper)[:1, :]

    @pl.when(phase == 1)
    def _():
        meta = meta_s[i]
        offs = offs_ref[...]
        off1 = jnp.sum(jnp.where(lane == meta[:, 0:1], offs, 0.0), axis=-1, keepdims=True).astype(jnp.int32)
        off2 = jnp.sum(jnp.where(lane == meta[:, 1:2], offs, 0.0), axis=-1, keepdims=True).astype(jnp.int32)
        pos1 = off1 + meta[:, 2:3]
        pos2 = off2 + meta[:, 3:4]
        pos_ref[...] = jnp.where(lane == 0, pos1, jnp.where(lane == 1, pos2, 0))
        wts_ref[...] = wts_s[i]


def _route(logits):
    t = logits.shape[0]
    tm = min(1024, t)
    nb = t // tm
    assert (2 * t) // MOE_TILE + N_EXPERTS <= 256, "tile-padded offsets must stay exact in bf16 (8 significant bits)"
    return pl.pallas_call(
        _route_kernel,
        grid=(2, nb),
        in_specs=[pl.BlockSpec((tm, LANES), lambda p, i: (i * (1 - p), 0))],
        out_specs=[
            pl.BlockSpec((tm, LANES), lambda p, i: (i * p, 0)),
            pl.BlockSpec((tm, LANES), lambda p, i: (i * p, 0)),
            pl.BlockSpec((SUBLANES, LANES), lambda p, i: (0, 0)),
        ],
        out_shape=[
            jax.ShapeDtypeStruct((t, LANES), jnp.int32),
            jax.ShapeDtypeStruct((t, LANES), F32),
            jax.ShapeDtypeStruct((SUBLANES, LANES), F32),
        ],
        scratch_shapes=[
            pltpu.VMEM((1, LANES), F32),
            pltpu.VMEM((1, LANES), F32),
            pltpu.VMEM((nb, tm, LANES), jnp.int32),
            pltpu.VMEM((nb, tm, LANES), F32),
        ],
        compiler_params=_cparams(("arbitrary", "arbitrary"), 32),
        name="route",
    )(logits)


def _row_gather_start(idx_at, src_hbm, dst, sem, rows):
    for r in range(rows):
        pltpu.make_async_copy(src_hbm.at[pl.ds(idx_at(r), 1)], dst.at[pl.ds(r, 1)], sem).start(priority=r % 2)


def _row_gather_wait(src_hbm, dst, sem, rows):
    pltpu.make_async_copy(src_hbm.at[pl.ds(0, rows)], dst, sem).wait()


def _dispatch_kernel(pos_ref, zrow_ref, h_ref, xs_hbm, zbuf, sems):
    i = pl.program_id(0)
    tm = h_ref.shape[0]
    tile = zbuf.shape[0]

    def zero_copy(z):
        return pltpu.make_async_copy(zbuf, xs_hbm.at[pl.ds(pl.multiple_of(zrow_ref[z], tile), tile)], sems.at[0])

    @pl.when(i == 0)
    def _():
        zbuf[...] = jnp.zeros(zbuf.shape, zbuf.dtype)
        for z in range(zrow_ref.shape[0]):
            @pl.when(zrow_ref[z] >= 0)
            def _():
                zero_copy(z).start()
        for z in range(zrow_ref.shape[0]):
            @pl.when(zrow_ref[z] >= 0)
            def _():
                zero_copy(z).wait()

    for k in range(2):
        base = (2 * i + k) * tm
        for r in range(tm):
            pltpu.make_async_copy(h_ref.at[pl.ds(r, 1)], xs_hbm.at[pl.ds(pos_ref[base + r], 1)],
                                  sems.at[1]).start(priority=r % 2)
    for k in range(2):
        pltpu.make_async_copy(h_ref, xs_hbm.at[pl.ds(0, tm)], sems.at[1]).wait()


def _dispatch(pos, zrow, h2, n_rows, seq):
    t, d = h2.shape
    tm = min(1024, seq)
    pos_blocks = pos.reshape(t // tm, tm, 2).transpose(0, 2, 1).reshape(-1)
    grid_spec = pltpu.PrefetchScalarGridSpec(
        num_scalar_prefetch=2,
        grid=(t // tm,),
        in_specs=[pl.BlockSpec((tm, d), lambda i, p, z: (i, 0))],
        out_specs=pl.BlockSpec(memory_space=pl.ANY),
        scratch_shapes=[
            pltpu.VMEM((MOE_TILE, d), F32),
            pltpu.SemaphoreType.DMA((2,)),
        ],
    )
    return pl.pallas_call(
        _dispatch_kernel,
        grid_spec=grid_spec,
        out_shape=jax.ShapeDtypeStruct((n_rows, d), F32),
        compiler_params=_cparams(("arbitrary",), 40),
        name="moe_dispatch",
    )(pos_blocks, zrow, h2)


def _moe_kernel(plan_ref, x_ref, wg_hbm, wu_hbm, wd_hbm, y_ref, stg_g, stg_u, stg_d, wsem):
    j = pl.program_id(0)
    nt = plan_ref[3, 0]
    nslot = stg_g.shape[0]

    def w_copies(e, s):
        return (pltpu.make_async_copy(wg_hbm.at[e], stg_g.at[s], wsem.at[s]),
                pltpu.make_async_copy(wu_hbm.at[e], stg_u.at[s], wsem.at[s]),
                pltpu.make_async_copy(wd_hbm.at[e], stg_d.at[s], wsem.at[s]))

    @pl.when(j == 0)
    def _():
        for s in range(nslot - 1):
            @pl.when(plan_ref[3, 1 + s] >= 0)
            def _():
                for cp in w_copies(plan_ref[3, 1 + s], s):
                    cp.start()

    @pl.when((j < nt) & (plan_ref[0, j] == 1))
    def _():
        s = plan_ref[1, j]
        for cp in w_copies(0, s):
            cp.wait()

        @pl.when(plan_ref[2, j] >= 0)
        def _():
            free = lax.rem(s + nslot - 1, nslot)
            for cp in w_copies(plan_ref[2, j], free):
                cp.start()

    @pl.when(j < nt)
    def _():
        s = plan_ref[1, j]
        x = x_ref[...].astype(BF16)
        hg = _dot(x, stg_g[s].astype(BF16))
        hu = _dot(x, stg_u[s].astype(BF16))
        act = (hg * _sigmoid(hg) * hu).astype(BF16)
        y_ref[...] = _dot(act, stg_d[s].astype(BF16))

    @pl.when(j >= nt)
    def _():
        y_ref[...] = jnp.zeros(y_ref.shape, y_ref.dtype)


def _moe(plan, xs, w_gate_e, w_up_e, w_down_e):
    d = xs.shape[1]
    f = w_gate_e.shape[-1]
    nt_max = plan.shape[1]
    tile = MOE_TILE
    any_spec = pl.BlockSpec(memory_space=pl.ANY)
    grid_spec = pltpu.PrefetchScalarGridSpec(
        num_scalar_prefetch=1,
        grid=(nt_max,),
        in_specs=[
            pl.BlockSpec((tile, d), lambda j, plan_: (jnp.minimum(j, plan_[3, 0] - 1), 0)),
            any_spec, any_spec, any_spec,
        ],
        out_specs=pl.BlockSpec((tile, d), lambda j, plan_: (j, 0)),
        scratch_shapes=[
            pltpu.VMEM((MOE_WEIGHT_SLOTS, d, f), F32),
            pltpu.VMEM((MOE_WEIGHT_SLOTS, d, f), F32),
            pltpu.VMEM((MOE_WEIGHT_SLOTS, f, d), F32),
            pltpu.SemaphoreType.DMA((MOE_WEIGHT_SLOTS,)),
        ],
    )
    return pl.pallas_call(
        _moe_kernel,
        grid_spec=grid_spec,
        out_shape=jax.ShapeDtypeStruct((nt_max * tile, d), F32),
        compiler_params=_cparams(("arbitrary",), 56),
        name="moe_experts",
    )(plan, xs, w_gate_e, w_up_e, w_down_e)


def _combine_kernel(pos_ref, y_hbm, x1_ref, gt_ref, wts_ref, o_ref, ybuf, sem):
    i = pl.program_id(0)
    n = pl.num_programs(0)
    tm = x1_ref.shape[0]
    slot = lax.rem(i, 2)

    def start(blk, s):
        for k in range(2):
            base = (2 * blk + k) * tm
            _row_gather_start(lambda r, base=base: pos_ref[base + r], y_hbm, ybuf.at[s, k], sem.at[s], tm)

    @pl.when(i == 0)
    def _():
        start(0, 0)

        @pl.when(n > 1)
        def _():
            start(1, 1)

    for k in range(2):
        _row_gather_wait(y_hbm, ybuf.at[slot, k], sem.at[slot], tm)
    w = wts_ref[...]
    moe = w[:, 0:1] * ybuf[slot, 0] + w[:, 1:2] * ybuf[slot, 1]
    o_ref[...] = x1_ref[...] + gt_ref[...] * moe

    @pl.when(i + 2 < n)
    def _():
        start(i + 2, slot)


def _combine(pos, ys, x1, mod4, wts, seq):
    t, d = x1.shape
    tm = min(512, seq)
    bpr = seq // tm
    pos_blocks = pos.reshape(t // tm, tm, 2).transpose(0, 2, 1).reshape(-1)
    grid_spec = pltpu.PrefetchScalarGridSpec(
        num_scalar_prefetch=1,
        grid=(t // tm,),
        in_specs=[
            pl.BlockSpec(memory_space=pl.ANY),
            pl.BlockSpec((tm, d), lambda i, p: (i, 0)),
            pl.BlockSpec((None, None, 1, d), lambda i, p: (i // bpr, 5, 0, 0)),
            pl.BlockSpec((tm, LANES), lambda i, p: (i, 0)),
        ],
        out_specs=pl.BlockSpec((tm, d), lambda i, p: (i, 0)),
        scratch_shapes=[
            pltpu.VMEM((2, 2, tm, d), F32),
            pltpu.SemaphoreType.DMA((2,)),
        ],
    )
    return pl.pallas_call(
        _combine_kernel,
        grid_spec=grid_spec,
        out_shape=jax.ShapeDtypeStruct((t, d), F32),
        compiler_params=_cparams(("arbitrary",), 48),
        name="moe_combine",
    )(pos_blocks, ys, x1, mod4, wts)


def _swap_halves(w):
    half = w.shape[-1] // 2
    return jnp.concatenate([w[..., half:], w[..., :half]], axis=-1)


def _layout_w_uq(w_uq):
    r = w_uq.shape[0]
    w = w_uq.reshape(r, MLA_HEADS, MLA_QK)
    rope = w[..., MLA_NOPE:]
    return jnp.concatenate([w[..., :MLA_NOPE], rope, _swap_halves(rope)], axis=-1).reshape(r, MLA_HEADS * QK_PAD).astype(BF16)


def _layout_w_ukv(w_ukv):
    r = w_ukv.shape[0]
    w = w_ukv.reshape(r, MLA_HEADS, MLA_NOPE + MLA_V)
    return jnp.concatenate([w[..., :MLA_NOPE].reshape(r, -1), w[..., MLA_NOPE:].reshape(r, -1)], axis=-1).astype(BF16)


def _rope_gain(g):
    g1 = g[MLA_NOPE:MLA_NOPE + MLA_ROPE // 2]
    g2 = g[MLA_NOPE + MLA_ROPE // 2:]
    return jnp.concatenate([g[:MLA_NOPE], g1, g2, -g2, g1]).reshape(1, QK_PAD)


def _layer(x2, cond_mod4, pos2, seq, w_in, q_a_norm_g, w_uq, kv_a_norm_g, w_ukv, q_norm_g, k_norm_g, conv_w, conv_b,
           b_mlstm_gates, mlstm_norm_g, w_proj_a, w_proj_b, w_out, norm_mix_g, norm_ffn_g, w_group, b_group,
           w_router, b_router, w_gate_e, w_up_e, w_down_e):
    t, d = x2.shape
    b = t // seq
    mod4 = cond_mod4

    proj, gates = _inproj(x2, mod4, norm_mix_g, w_in.T, seq)

    inv = ROPE_THETA ** (-jnp.arange(0, MLA_ROPE, 2, dtype=F32) / MLA_ROPE)
    inv_lanes = jnp.tile(inv, LANES // (MLA_ROPE // 2)).reshape(1, LANES)
    qt, k, vt = _mla_prep(proj, pos2, _layout_w_uq(w_uq), _layout_w_ukv(w_ukv), q_a_norm_g.reshape(1, -1),
                        kv_a_norm_g.reshape(1, -1), _rope_gain(q_norm_g), _rope_gain(k_norm_g), inv_lanes, seq)
    out_a = _flash(qt, k.reshape(b, seq, -1), vt).reshape(t, MLA_W)

    proj3 = proj.reshape(b, seq, -1)
    ml_qt, ml_k = _conv_silu(proj3, conv_w, conv_b)
    gbias = jnp.zeros((1, LANES), F32).at[0, GATE_LANE:GATE_LANE + 2 * ML_HEADS].set(b_mlstm_gates.reshape(-1))
    hm = _mlstm(ml_qt, ml_k, proj3, gates.reshape(b, seq, LANES), gbias, mlstm_norm_g).reshape(t, ML_W)

    mixed = _merge(out_a, hm, proj, w_proj_a.astype(BF16), w_proj_b.astype(BF16), seq)

    route_pad = LANES - N_GROUPS - N_EXPERTS
    w_route = jnp.concatenate([w_group, w_router, jnp.zeros((d, route_pad), F32)], axis=1)
    b_route = jnp.concatenate([b_group, b_router, jnp.zeros((route_pad,), F32)]).reshape(1, LANES)
    w_route_hi = w_route.astype(BF16)
    w_route2 = jnp.concatenate([w_route_hi, (w_route - w_route_hi.astype(F32)).astype(BF16)], axis=1)
    x1, h2, logits = _outproj(mixed, x2, w_out.astype(BF16), mod4, norm_ffn_g, w_route2, b_route, seq)

    posm, wts, counts = _route(logits)
    pos = posm[:, 0:2]

    tile = MOE_TILE
    i32 = jnp.int32
    cnt = counts[0, N_GROUPS:N_GROUPS + N_EXPERTS].astype(i32)
    padded = ((cnt + tile - 1) // tile) * tile
    ends = jnp.cumsum(padded)
    offs = ends - padded
    nt_max = (2 * t) // tile + N_EXPERTS
    n_tiles = ends[-1] // tile
    tile_idx = jnp.arange(nt_max, dtype=i32)
    tile_start = tile_idx * tile
    live = tile_idx < n_tiles
    texp = jnp.minimum(jnp.sum((ends[None, :] <= tile_start[:, None]).astype(i32), axis=1), N_EXPERTS - 1)
    active = cnt > 0
    order = jnp.cumsum(active.astype(i32)) - 1
    n_active = jnp.sum(active.astype(i32))
    experts = jnp.arange(N_EXPERTS, dtype=i32)
    by_order = jnp.sum(jnp.where(active[None, :] & (order[None, :] == experts[:, None]), experts[None, :], 0), axis=1)
    t_order = order[texp]
    first = (live & (tile_start == offs[texp])).astype(i32)
    nslot = MOE_WEIGHT_SLOTS
    ahead = t_order + nslot - 1
    prefetch = jnp.where(ahead < n_active, by_order[jnp.minimum(ahead, N_EXPERTS - 1)], -1)
    lead = jnp.where(experts[:nslot - 1] < n_active, by_order[:nslot - 1], -1)
    head = jnp.zeros((nt_max,), i32).at[0].set(n_tiles).at[1:nslot].set(lead)
    plan = jnp.stack([first, t_order % nslot, prefetch, head]).astype(i32)

    last_tile = jnp.where(padded > cnt, ends - tile, -1)
    spare_idx = n_tiles + jnp.arange(N_EXPERTS, dtype=i32)
    spare_tile = jnp.where(spare_idx < nt_max, spare_idx * tile, -1)
    zrow = jnp.concatenate([last_tile, spare_tile]).astype(i32)

    xs = _dispatch(pos, zrow, h2, nt_max * tile, seq)
    ys = _moe(plan, xs, w_gate_e, w_up_e, w_down_e)

    return _combine(pos, ys, x1, mod4, wts, seq)


def kernel(x, c, positions, w_ada, b_ada, norm_mix_g, w_in, q_a_norm_g, w_uq, kv_a_norm_g, w_ukv, q_norm_g, k_norm_g, conv_w, conv_b, b_mlstm_gates, mlstm_norm_g, w_proj_a, w_proj_b, w_out, norm_ffn_g, w_group, b_group, w_router, b_router, w_gate_e, w_up_e, w_down_e):
    b, seq, d = x.shape
    depth = w_ada.shape[0]
    x2 = x.reshape(b * seq, d)
    pos2 = positions.reshape(b * seq, 1)
    c_pad = jnp.zeros((SUBLANES, d), F32).at[:b].set(c)
    for l in range(depth):
        mod = _adaln(c_pad, w_ada[l], b_ada[l])
        mod4 = mod[:b].reshape(b, 6, 1, d)
        x2 = _layer(x2, mod4, pos2, seq, w_in[l], q_a_norm_g[l], w_uq[l], kv_a_norm_g[l], w_ukv[l], q_norm_g[l],
                    k_norm_g[l], conv_w[l], conv_b[l], b_mlstm_gates[l], mlstm_norm_g[l], w_proj_a[l], w_proj_b[l],
                    w_out[l], norm_mix_g[l], norm_ffn_g[l], w_group[l], b_group[l], w_router[l], b_router[l],
                    w_gate_e[l], w_up_e[l], w_down_e[l])
    return x2.reshape(b, seq, d)
```

```python
import functools
import math

import jax
import jax.numpy as jnp
from jax import lax
from jax.experimental import pallas as pl
from jax.experimental.pallas import tpu as pltpu

F32 = jnp.float32
BF16 = jnp.bfloat16

LANES = 128
SUBLANES = 8

D_MODEL = 2048
MLA_HEADS = 8
MLA_NOPE = 128
MLA_ROPE = 64
MLA_QK = MLA_NOPE + MLA_ROPE
MLA_V = 128
Q_LORA = 512
KV_LORA = 256
ROPE_THETA = 10000.0
ML_HEADS = 8
ML_DQK = 128
ML_DV = 128
ML_CONV = 4
MLA_W = MLA_HEADS * MLA_V
ML_W = ML_HEADS * ML_DV
N_GROUPS = 4
EXP_PER_GROUP = 8
N_EXPERTS = N_GROUPS * EXP_PER_GROUP
EPS = 1e-6

QK_PAD = 2 * LANES

SRC_CQ = 0
SRC_KPE = Q_LORA + KV_LORA
SRC_QK = SRC_KPE + MLA_ROPE
SRC_V = SRC_QK + 2 * ML_HEADS * ML_DQK
SRC_O = SRC_V + ML_W
SRC_I = SRC_O + ML_W
SRC_GA = SRC_I + 2 * ML_HEADS
SRC_GB = SRC_GA + D_MODEL
GATE_LANE = SRC_I % LANES

IN_BLOCK = 1024
IN_BLOCK_SRC = (SRC_QK, SRC_QK + IN_BLOCK, SRC_GA, SRC_GA + IN_BLOCK, SRC_GB, SRC_GB + IN_BLOCK, SRC_V, SRC_O, SRC_CQ)
COL_QK = 0
COL_GA = 2048
COL_GB = 4096
COL_V = 6144
COL_O = 7168
COL_LAT = 8192
LAT_W = Q_LORA + KV_LORA + MLA_ROPE

MLSTM_CHUNK = 256
MOE_TILE = 256
MOE_WEIGHT_SLOTS = 3
FLASH_HEADS_PER_STEP = 2
FLASH_Q_TILE = 512
FLASH_K_TILE = 512
OUTPROJ_ROW_GROUP = 256
MERGE_ROW_GROUP = 256


def _cparams(sem, vmem_mb):
    return pltpu.CompilerParams(dimension_semantics=sem, vmem_limit_bytes=vmem_mb * 1024 * 1024)


def _dot(a, b):
    return jnp.dot(a, b, preferred_element_type=F32)


def _dot_nt(a, b):
    return lax.dot_general(a, b, (((1,), (1,)), ((), ())), preferred_element_type=F32)


def _sigmoid(x):
    return 1.0 / (1.0 + jnp.exp(-x))


def _rms_scale(x, width):
    return lax.rsqrt(jnp.sum(x * x, axis=-1, keepdims=True) * (1.0 / width) + EPS)


def _adaln_kernel(c_ref, w_ref, b_ref, o_ref):
    c = c_ref[...]
    cond = (c * _sigmoid(c)).astype(BF16)
    o_ref[...] = _dot(cond, w_ref[...].astype(BF16)) + b_ref[...]


def _adaln(c_pad, w_ada, b_ada):
    rows, d = c_pad.shape
    n = w_ada.shape[1]
    tn = 2048
    return pl.pallas_call(
        _adaln_kernel,
        grid=(n // tn,),
        in_specs=[
            pl.BlockSpec((rows, d), lambda j: (0, 0)),
            pl.BlockSpec((d, tn), lambda j: (0, j)),
            pl.BlockSpec((1, tn), lambda j: (0, j)),
        ],
        out_specs=pl.BlockSpec((rows, tn), lambda j: (0, j)),
        out_shape=jax.ShapeDtypeStruct((rows, n), F32),
        compiler_params=_cparams(("arbitrary",), 48),
        name="adaln",
    )(c_pad, w_ada, b_ada.reshape(1, n))


def _inproj_kernel(off_ref, x_ref, sc_ref, sh_ref, g_ref, w_ref, wgate_ref, proj_ref, gates_ref, h_ref):
    j = pl.program_id(1)

    @pl.when(j == 0)
    def _():
        x = x_ref[...]
        h = x * _rms_scale(x, x.shape[-1]) * g_ref[...]
        h = (h * (1.0 + sc_ref[...]) + sh_ref[...]).astype(BF16)
        h_ref[...] = h
        gates_ref[...] = _dot_nt(h, wgate_ref[...].astype(BF16))

    proj_ref[...] = _dot_nt(h_ref[...], w_ref[...].astype(BF16)).astype(proj_ref.dtype)


def _inproj(x2, mod4, norm_g, w_in_t, seq):
    t, d = x2.shape
    tm = min(1024, seq)
    tn = IN_BLOCK
    bpr = seq // tm
    assert all(o % SUBLANES == 0 for o in IN_BLOCK_SRC)
    offs = jnp.asarray([o // SUBLANES for o in IN_BLOCK_SRC], jnp.int32)
    nblk = len(IN_BLOCK_SRC)
    gate_tile = SRC_I // LANES
    grid_spec = pltpu.PrefetchScalarGridSpec(
        num_scalar_prefetch=1,
        grid=(t // tm, nblk),
        in_specs=[
            pl.BlockSpec((tm, d), lambda i, j, o: (i, 0)),
            pl.BlockSpec((None, None, 1, d), lambda i, j, o: (i // bpr, 1, 0, 0)),
            pl.BlockSpec((None, None, 1, d), lambda i, j, o: (i // bpr, 0, 0, 0)),
            pl.BlockSpec((1, d), lambda i, j, o: (0, 0)),
            pl.BlockSpec((pl.Element(tn), pl.Element(d)), lambda i, j, o: (o[j] * SUBLANES, 0)),
            pl.BlockSpec((LANES, d), lambda i, j, o: (gate_tile, 0)),
        ],
        out_specs=[
            pl.BlockSpec((tm, tn), lambda i, j, o: (i, j)),
            pl.BlockSpec((tm, LANES), lambda i, j, o: (i, 0)),
        ],
        scratch_shapes=[pltpu.VMEM((tm, d), BF16)],
    )
    return pl.pallas_call(
        _inproj_kernel,
        grid_spec=grid_spec,
        out_shape=[
            jax.ShapeDtypeStruct((t, nblk * tn), BF16),
            jax.ShapeDtypeStruct((t, LANES), F32),
        ],
        compiler_params=_cparams(("arbitrary", "arbitrary"), 56),
        name="inproj",
    )(offs, x2, mod4, mod4, norm_g.reshape(1, d), w_in_t, w_in_t)


def _mla_prep_kernel(lat_ref, pos_ref, wuq_ref, wukv_ref, gqa_ref, gkva_ref, gq_ref, gk_ref, inv_ref,
                     qt_ref, k_ref, vt_ref):
    lat = lat_ref[...].astype(F32)
    cq = lat[:, :Q_LORA]
    ckv = lat[:, Q_LORA:Q_LORA + KV_LORA]
    kc = lat[:, SRC_KPE:SRC_KPE + LANES]
    cqn = (cq * _rms_scale(cq, Q_LORA) * gqa_ref[...]).astype(BF16)
    ckvn = (ckv * _rms_scale(ckv, KV_LORA) * gkva_ref[...]).astype(BF16)
    qraw = _dot(cqn, wuq_ref[...])
    kv = _dot(ckvn, wukv_ref[...])

    ang = pos_ref[...].astype(F32) * inv_ref[...]
    lane = lax.broadcasted_iota(jnp.int32, ang.shape, 1)
    lo = lane < MLA_ROPE
    cs = jnp.cos(ang - jnp.where(lo, 0.0, 0.5 * math.pi))

    quarter = MLA_ROPE // 2
    want = jnp.where(lane < 3 * quarter, lane - quarter, lane - 3 * quarter)
    came = pltpu.roll(lane, quarter, 1)
    swapped = jnp.where(came == want, pltpu.roll(kc, quarter, 1), pltpu.roll(kc, 3 * quarter, 1))
    kc = jnp.where(lo, kc, swapped)

    gq = gq_ref[...]
    gk = gk_ref[...]
    gq_n, gq_r = gq[:, :LANES], gq[:, LANES:]
    gk_n, gk_r = gk[:, :LANES], gk[:, LANES:]

    def rope(chunk, g_cs):
        a = chunk * g_cs
        return jnp.where(lo, a + pltpu.roll(a, MLA_ROPE, 1), 0.0)

    gq_cs = gq_r * cs
    kpe_ss = jnp.sum(jnp.where(lo, kc * kc, 0.0), axis=-1, keepdims=True)
    k_rope = rope(kc, gk_r * cs)
    scale = MLA_QK ** -0.5 * math.log2(math.e)
    for h in range(MLA_HEADS):
        kn = kv[:, h * MLA_NOPE:(h + 1) * MLA_NOPE]
        sk = lax.rsqrt((jnp.sum(kn * kn, axis=-1, keepdims=True) + kpe_ss) * (1.0 / MLA_QK) + EPS)
        k_ref[:, h * QK_PAD:h * QK_PAD + LANES] = (kn * sk * gk_n).astype(BF16)
        k_ref[:, h * QK_PAD + LANES:(h + 1) * QK_PAD] = (k_rope * sk).astype(BF16)
        qn = qraw[:, h * QK_PAD:h * QK_PAD + LANES]
        qr = qraw[:, h * QK_PAD + LANES:(h + 1) * QK_PAD]
        ss = jnp.sum(qn * qn, axis=-1, keepdims=True) + jnp.sum(jnp.where(lo, qr * qr, 0.0), axis=-1, keepdims=True)
        sq = lax.rsqrt(ss * (1.0 / MLA_QK) + EPS) * scale
        qt_ref[h * QK_PAD:h * QK_PAD + LANES, :] = (qn * sq * gq_n).astype(BF16).T
        qt_ref[h * QK_PAD + LANES:(h + 1) * QK_PAD, :] = (rope(qr, gq_cs) * sq).astype(BF16).T
        vh = kv[:, MLA_HEADS * MLA_NOPE + h * MLA_V:MLA_HEADS * MLA_NOPE + (h + 1) * MLA_V]
        vt_ref[h * MLA_V:(h + 1) * MLA_V, :] = vh.astype(BF16).T


def _mla_prep(proj, pos2, wuq_p, wukv_p, gqa, gkva, gq, gk, inv_lanes, seq):
    t = proj.shape[0]
    tm = min(512, seq)
    hq = MLA_HEADS * QK_PAD
    lat_blk = COL_LAT // IN_BLOCK
    const = lambda i: (0, 0)
    return pl.pallas_call(
        _mla_prep_kernel,
        grid=(t // tm,),
        in_specs=[
            pl.BlockSpec((tm, IN_BLOCK), lambda i: (i, lat_blk)),
            pl.BlockSpec((tm, 1), lambda i: (i, 0)),
            pl.BlockSpec(wuq_p.shape, const),
            pl.BlockSpec(wukv_p.shape, const),
            pl.BlockSpec(gqa.shape, const),
            pl.BlockSpec(gkva.shape, const),
            pl.BlockSpec(gq.shape, const),
            pl.BlockSpec(gk.shape, const),
            pl.BlockSpec(inv_lanes.shape, const),
        ],
        out_specs=[
            pl.BlockSpec((hq, tm), lambda i: (0, i)),
            pl.BlockSpec((tm, hq), lambda i: (i, 0)),
            pl.BlockSpec((MLA_W, tm), lambda i: (0, i)),
        ],
        out_shape=[
            jax.ShapeDtypeStruct((hq, t), BF16),
            jax.ShapeDtypeStruct((t, hq), BF16),
            jax.ShapeDtypeStruct((MLA_W, t), BF16),
        ],
        compiler_params=_cparams(("arbitrary",), 48),
        name="mla_prep",
    )(proj, pos2, wuq_p, wukv_p, gqa, gkva, gq, gk, inv_lanes)


def _flash_kernel(qt_ref, k_ref, vt_ref, o_ref, *, tq, tk):
    seq = k_ref.shape[0]
    heads = k_ref.shape[1] // QK_PAD

    def scores(h, k0, q0):
        kj = k_ref[k0:k0 + tk, h * QK_PAD:(h + 1) * QK_PAD]
        return _dot(kj, qt_ref[h * QK_PAD:(h + 1) * QK_PAD, q0:q0 + tq])

    def update(h, state, st, k0, q0):
        m, l, acc = state
        if k0 + tk - 1 > q0:
            key = lax.broadcasted_iota(jnp.int32, st.shape, 0) + k0
            qry = lax.broadcasted_iota(jnp.int32, st.shape, 1) + q0
            st = jnp.where(key <= qry, st, -jnp.inf)
        m_new = jnp.maximum(m, jnp.max(st, axis=0, keepdims=True))
        alpha = jnp.exp2(m - m_new)
        p = jnp.exp2(st - m_new)
        l = alpha * l + jnp.sum(p, axis=0, keepdims=True)
        acc = alpha * acc + _dot(vt_ref[h * MLA_V:(h + 1) * MLA_V, k0:k0 + tk], p.astype(BF16))
        return m_new, l, acc

    steps = [(h, qi * tq, j * tk) for qi in range(seq // tq) for j in range((qi + 1) * tq // tk) for h in range(heads)]
    st_next = scores(steps[0][0], steps[0][2], steps[0][1])
    states = {}
    for n, (h, q0, k0) in enumerate(steps):
        st = st_next
        if n + 1 < len(steps):
            hn, qn, kn = steps[n + 1]
            st_next = scores(hn, kn, qn)
        if k0 == 0:
            states[h] = (jnp.full((1, tq), -jnp.inf, F32), jnp.zeros((1, tq), F32), jnp.zeros((MLA_V, tq), F32))
        states[h] = update(h, states[h], st, k0, q0)
        if k0 + tk >= q0 + tq:
            _, l, acc = states[h]
            o_ref[q0:q0 + tq, h * MLA_V:(h + 1) * MLA_V] = (acc / l).T.astype(o_ref.dtype)


def _flash(qt, k3, vt):
    b, seq, _ = k3.shape
    tq = min(FLASH_Q_TILE, seq)
    tk = min(FLASH_K_TILE, seq)
    hp = FLASH_HEADS_PER_STEP
    kern = functools.partial(_flash_kernel, tq=tq, tk=tk)
    return pl.pallas_call(
        kern,
        grid=(b, MLA_HEADS // hp),
        in_specs=[
            pl.BlockSpec((hp * QK_PAD, seq), lambda i, h: (h, i)),
            pl.BlockSpec((None, seq, hp * QK_PAD), lambda i, h: (i, 0, h)),
            pl.BlockSpec((hp * MLA_V, seq), lambda i, h: (h, i)),
        ],
        out_specs=pl.BlockSpec((None, seq, hp * MLA_V), lambda i, h: (i, 0, h)),
        out_shape=jax.ShapeDtypeStruct((b, seq, MLA_W), BF16),
        compiler_params=_cparams(("arbitrary", "arbitrary"), 40),
        name="flash",
    )(qt, k3, vt)


def _log_sigmoid(x):
    return -(jnp.maximum(-x, 0.0) + jnp.log1p(jnp.exp(-jnp.abs(x))))


def _conv_silu_qk(cur_ref, halo_ref, w_ref, b_ref, buf_ref, qt_ref, k_ref, first):
    L = cur_ref.shape[0]
    cols = cur_ref.shape[1]
    half = cols // 2
    buf_ref[0:SUBLANES, :] = jnp.where(first, 0.0, halo_ref[...].astype(F32))
    buf_ref[SUBLANES:SUBLANES + L, :] = cur_ref[...].astype(F32)
    cw = 512
    for c in range(cols // cw):
        sl = slice(c * cw, (c + 1) * cw)
        acc = jnp.zeros((L, cw), F32) + b_ref[:, sl]
        for j in range(ML_CONV):
            off = SUBLANES - (ML_CONV - 1) + j
            acc = acc + buf_ref[off:off + L, sl] * w_ref[j:j + 1, sl]
        y = acc * _sigmoid(acc)
        if c * cw < half:
            qt_ref[sl, :] = y.astype(qt_ref.dtype).T
        else:
            k_ref[:, c * cw - half:(c + 1) * cw - half] = (y * (ML_DQK ** -0.5)).astype(k_ref.dtype)


def _mlstm_kernel(cur_ref, halo_ref, cw_ref, cb_ref, v_ref, o_ref, gates_ref, gbias_ref, ng_ref, out_ref,
                  ct_ref, m_ref, buf_ref, qt_ref, k_ref):
    L = cur_ref.shape[0]
    first = pl.program_id(1) == 0

    @pl.when(first)
    def _():
        ct_ref[...] = jnp.zeros(ct_ref.shape, F32)
        m_ref[...] = jnp.zeros(m_ref.shape, F32)

    _conv_silu_qk(cur_ref, halo_ref, cw_ref, cb_ref, buf_ref, qt_ref, k_ref, first)

    g = gates_ref[...] + gbias_ref[...]
    gt = g.T
    lf = _log_sigmoid(g)
    lft = _log_sigmoid(gt)
    r = lax.broadcasted_iota(jnp.int32, (L, L), 0)
    c = lax.broadcasted_iota(jnp.int32, (L, L), 1)
    src_le_qry = r <= c
    tril = (c <= r).astype(F32)
    triu = src_le_qry.astype(F32)
    hi = lax.Precision.HIGHEST
    bcol_all = jnp.dot(tril, lf, preferred_element_type=F32, precision=hi)
    brow_all = jnp.dot(lft, triu, preferred_element_type=F32, precision=hi)
    row = lax.broadcasted_iota(jnp.int32, (ML_DV, L), 0)
    ones_row = jnp.where(row == 0, 1.0, 0.0)

    def lead_matmuls(h):
        hs = slice(h * ML_DQK, (h + 1) * ML_DQK)
        qt = qt_ref[hs, :]
        return _dot(k_ref[:, hs], qt), _dot(ct_ref[h].astype(BF16), qt)

    lead_next = lead_matmuls(0)
    for h in range(ML_HEADS):
        hs = slice(h * ML_DQK, (h + 1) * ML_DQK)
        qk_t, cq_t = lead_next
        if h + 1 < ML_HEADS:
            lead_next = lead_matmuls(h + 1)
        li, lf_ = GATE_LANE + h, GATE_LANE + ML_HEADS + h
        b_row = brow_all[lf_:lf_ + 1, :]
        i_row = gt[li:li + 1, :]
        u_col = g[:, li:li + 1] - bcol_all[:, lf_:lf_ + 1]
        m_prev = m_ref[h][:, :1]
        logw_t = jnp.where(src_le_qry, b_row + u_col, -jnp.inf)
        log_inter = b_row + m_prev
        m_t = jnp.maximum(jnp.max(logw_t, axis=0, keepdims=True), log_inter)
        w_t = jnp.exp(logw_t - m_t)
        a = jnp.exp(log_inter - m_t)
        kh = k_ref[:, hs]
        vt_aug = jnp.concatenate([v_ref[:, hs].astype(F32).T, ones_row], axis=0)
        s_t = qk_t * w_t
        ct = ct_ref[h]
        nd = _dot(vt_aug.astype(BF16), s_t.astype(BF16)) + a * cq_t
        num = nd[:ML_DV, :]
        den = nd[ML_DV:ML_DV + 1, :]
        hout_t = num * (1.0 / jnp.maximum(jnp.abs(den), jnp.exp(-m_t)))
        hn_t = hout_t * lax.rsqrt(jnp.sum(hout_t * hout_t, axis=0, keepdims=True) * (1.0 / ML_DV) + EPS)
        gate = _sigmoid(o_ref[:, hs].astype(F32))
        out_ref[:, hs] = (hn_t.T * ng_ref[:, hs] * gate).astype(out_ref.dtype)

        b_last = b_row[:, L - 1:L]
        logg = b_last - b_row + i_row
        m_new = jnp.maximum(b_last + m_prev, jnp.max(logg, axis=-1, keepdims=True))
        g_row = jnp.exp(logg - m_new)
        decay = jnp.exp(b_last + m_prev - m_new)
        ct_ref[h] = decay * ct + _dot((vt_aug * g_row).astype(BF16), kh)
        m_ref[h] = jnp.broadcast_to(m_new, m_ref.shape[1:])


def _mlstm(proj3, conv_w, conv_b, gates3, gbias, norm_g):
    b, seq, _ = proj3.shape
    w = ML_W
    qk_cols = 2 * ML_HEADS * ML_DQK
    L = min(MLSTM_CHUNK, seq)
    hb = L // SUBLANES
    return pl.pallas_call(
        _mlstm_kernel,
        grid=(b, seq // L),
        in_specs=[
            pl.BlockSpec((None, L, qk_cols), lambda i, c: (i, c, COL_QK // qk_cols)),
            pl.BlockSpec((None, SUBLANES, qk_cols), lambda i, c: (i, jnp.maximum(c * hb - 1, 0), COL_QK // qk_cols)),
            pl.BlockSpec((ML_CONV, qk_cols), lambda i, c: (0, 0)),
            pl.BlockSpec((1, qk_cols), lambda i, c: (0, 0)),
            pl.BlockSpec((None, L, w), lambda i, c: (i, c, COL_V // w)),
            pl.BlockSpec((None, L, w), lambda i, c: (i, c, COL_O // w)),
            pl.BlockSpec((None, L, LANES), lambda i, c: (i, c, 0)),
            pl.BlockSpec((1, LANES), lambda i, c: (0, 0)),
            pl.BlockSpec((1, w), lambda i, c: (0, 0)),
        ],
        out_specs=pl.BlockSpec((None, L, w), lambda i, c: (i, c, 0)),
        out_shape=jax.ShapeDtypeStruct((b, seq, w), BF16),
        scratch_shapes=[
            pltpu.VMEM((ML_HEADS, 2 * ML_DV, ML_DQK), F32),
            pltpu.VMEM((ML_HEADS, 1, LANES), F32),
            pltpu.VMEM((L + SUBLANES, qk_cols), F32),
            pltpu.VMEM((ML_HEADS * ML_DQK, L), BF16),
            pltpu.VMEM((L, ML_HEADS * ML_DQK), BF16),
        ],
        compiler_params=_cparams(("arbitrary", "arbitrary"), 40),
        name="mlstm",
    )(proj3, proj3, conv_w, conv_b.reshape(1, qk_cols), proj3, proj3, gates3, gbias, norm_g.reshape(1, w))


def _merge_kernel(a_ref, b_ref, ga_ref, gb_ref, wa_ref, wb_ref, o_ref):
    tm = a_ref.shape[0]
    rows = min(MERGE_ROW_GROUP, tm)
    groups = [slice(r0, r0 + rows) for r0 in range(0, tm, rows)]

    def proj(rs):
        return _dot(a_ref[rs, :], wa_ref[...]), _dot(b_ref[rs, :], wb_ref[...])

    nxt = proj(groups[0])
    for n, rs in enumerate(groups):
        pa, pb = nxt
        if n + 1 < len(groups):
            nxt = proj(groups[n + 1])
        mixed = _sigmoid(ga_ref[rs, :].astype(F32)) * pa + _sigmoid(gb_ref[rs, :].astype(F32)) * pb
        o_ref[rs, :] = mixed.astype(o_ref.dtype)


def _merge(out_a, hm, proj, wa, wb, seq):
    t = out_a.shape[0]
    d = wa.shape[1]
    tm = min(512, seq)
    tn = 2048
    return pl.pallas_call(
        _merge_kernel,
        grid=(d // tn, t // tm),
        in_specs=[
            pl.BlockSpec((tm, MLA_W), lambda j, i: (i, 0)),
            pl.BlockSpec((tm, ML_W), lambda j, i: (i, 0)),
            pl.BlockSpec((tm, tn), lambda j, i: (i, COL_GA // tn + j)),
            pl.BlockSpec((tm, tn), lambda j, i: (i, COL_GB // tn + j)),
            pl.BlockSpec((MLA_W, tn), lambda j, i: (0, j)),
            pl.BlockSpec((ML_W, tn), lambda j, i: (0, j)),
        ],
        out_specs=pl.BlockSpec((tm, tn), lambda j, i: (i, j)),
        out_shape=jax.ShapeDtypeStruct((t, d), BF16),
        compiler_params=_cparams(("arbitrary", "arbitrary"), 40),
        name="merge",
    )(out_a, hm, proj, proj, wa, wb)


def _outproj_kernel(mix_ref, x_ref, w_ref, gt_ref, sc_ref, sh_ref, g_ref, wr_ref, br_ref, x1_ref, h2_ref, lg_ref):
    tm = mix_ref.shape[0]
    rows = OUTPROJ_ROW_GROUP
    groups = [slice(r0, r0 + rows) for r0 in range(0, tm, rows)]
    y_next = _dot(mix_ref[groups[0], :], w_ref[...])
    for n, rs in enumerate(groups):
        y = y_next
        if n + 1 < len(groups):
            y_next = _dot(mix_ref[groups[n + 1], :], w_ref[...])
        x1 = x_ref[rs, :] + gt_ref[...] * y
        x1_ref[rs, :] = x1
        h2 = x1 * _rms_scale(x1, x1.shape[-1]) * g_ref[...]
        h2 = h2 * (1.0 + sc_ref[...]) + sh_ref[...]
        h2_ref[rs, :] = h2
        h_hi = h2.astype(BF16)
        h_lo = (h2 - h_hi.astype(F32)).astype(BF16)
        r = _dot(h_hi, wr_ref[...]) + _dot(h_lo, wr_ref[...])
        lg_ref[rs, :] = r[:, :LANES] + r[:, LANES:] + br_ref[...]


def _outproj(mixed, x2, w_out, mod4, norm_g, w_route2, b_route, seq):
    t, d = x2.shape
    tm = min(512, seq)
    bpr = seq // tm
    mod_spec = lambda k: pl.BlockSpec((None, None, 1, d), lambda i: (i // bpr, k, 0, 0))
    const = lambda i: (0, 0)
    return pl.pallas_call(
        _outproj_kernel,
        grid=(t // tm,),
        in_specs=[
            pl.BlockSpec((tm, d), lambda i: (i, 0)),
            pl.BlockSpec((tm, d), lambda i: (i, 0)),
            pl.BlockSpec((d, d), const, pipeline_mode=pl.Buffered(1)),
            mod_spec(2),
            mod_spec(4),
            mod_spec(3),
            pl.BlockSpec((1, d), const),
            pl.BlockSpec((d, 2 * LANES), const, pipeline_mode=pl.Buffered(1)),
            pl.BlockSpec((1, LANES), const),
        ],
        out_specs=[
            pl.BlockSpec((tm, d), lambda i: (i, 0)),
            pl.BlockSpec((tm, d), lambda i: (i, 0)),
            pl.BlockSpec((tm, LANES), lambda i: (i, 0)),
        ],
        out_shape=[
            jax.ShapeDtypeStruct((t, d), F32),
            jax.ShapeDtypeStruct((t, d), F32),
            jax.ShapeDtypeStruct((t, LANES), F32),
        ],
        compiler_params=_cparams(("arbitrary",), 56),
        name="outproj",
    )(mixed, x2, w_out, mod4, mod4, mod4, norm_g.reshape(1, d), w_route2, b_route)


def _route_kernel(lg_ref, pos_ref, wts_ref, cnt_ref, carry_ref, offs_ref, meta_s, wts_s):
    phase = pl.program_id(0)
    i = pl.program_id(1)
    tm = lg_ref.shape[0]
    lane = lax.broadcasted_iota(jnp.int32, (tm, LANES), 1)

    @pl.when((phase == 0) & (i == 0))
    def _():
        carry_ref[...] = jnp.zeros(carry_ref.shape, F32)

    @pl.when(phase == 0)
    def _():
        lg = lg_ref[...]
        big = jnp.int32(LANES)
        ninf = -jnp.inf

        def first_argmax(vals):
            mx = jnp.max(vals, axis=-1, keepdims=True)
            idx = jnp.min(jnp.where(vals == mx, lane, big), axis=-1, keepdims=True)
            return mx, idx

        gl = jnp.where(lane < N_GROUPS, lg, ninf)
        gmax, gsel = first_argmax(gl)
        g_w = 1.0 / jnp.sum(jnp.exp(gl - gmax), axis=-1, keepdims=True)
        lo = N_GROUPS + gsel * EXP_PER_GROUP
        in_grp = (lane >= lo) & (lane < lo + EXP_PER_GROUP)
        el = jnp.where(in_grp, lg, ninf)
        e1, i1 = first_argmax(el)
        e2, i2 = first_argmax(jnp.where(lane == i1, ninf, el))
        p2 = jnp.exp(e2 - e1)
        w1 = g_w / (1.0 + p2)
        w2 = g_w * p2 / (1.0 + p2)

        oh1 = lane == i1
        oh2 = lane == i2
        oh = jnp.where(oh1 | oh2, 1.0, 0.0)
        r = lax.broadcasted_iota(jnp.int32, (tm, tm), 0)
        c = lax.broadcasted_iota(jnp.int32, (tm, tm), 1)
        strict = jnp.where(c < r, 1.0, 0.0).astype(BF16)
        before = _dot(strict, oh.astype(BF16)) + carry_ref[...]
        rank1 = jnp.sum(jnp.where(oh1, before, 0.0), axis=-1, keepdims=True).astype(jnp.int32)
        rank2 = jnp.sum(jnp.where(oh2, before, 0.0), axis=-1, keepdims=True).astype(jnp.int32)
        carry_ref[...] = carry_ref[...] + jnp.sum(oh, axis=0, keepdims=True)
        meta_s[i] = jnp.where(lane == 0, i1, jnp.where(lane == 1, i2, jnp.where(lane == 2, rank1, jnp.where(lane == 3, rank2, 0))))
        wts_s[i] = jnp.where(lane == 0, w1, jnp.where(lane == 1, w2, 0.0))

    @pl.when((phase == 1) & (i == 0))
    def _():
        cnt = carry_ref[...]
        cnt_ref[...] = jnp.broadcast_to(cnt, cnt_ref.shape)
        padded = jnp.ceil(cnt * (1.0 / MOE_TILE)) * MOE_TILE
        r = lax.broadcasted_iota(jnp.int32, (LANES, LANES), 0)
        c = lax.broadcasted_iota(jnp.int32, (LANES, LANES), 1)
        upper = jnp.where(r < c, 1.0, 0.0).astype(BF16)
        padded8 = jnp.broadcast_to(padded, (SUBLANES, LANES)).astype(BF16)
        offs_ref[...] = _dot(padded8, upper)[:1, :]

    @pl.when(phase == 1)
    def _():
        meta = meta_s[i]
        offs = offs_ref[...]
        off1 = jnp.sum(jnp.where(lane == meta[:, 0:1], offs, 0.0), axis=-1, keepdims=True).astype(jnp.int32)
        off2 = jnp.sum(jnp.where(lane == meta[:, 1:2], offs, 0.0), axis=-1, keepdims=True).astype(jnp.int32)
        pos1 = off1 + meta[:, 2:3]
        pos2 = off2 + meta[:, 3:4]
        pos_ref[...] = jnp.where(lane == 0, pos1, jnp.where(lane == 1, pos2, 0))
        wts_ref[...] = wts_s[i]


def _route(logits):
    t = logits.shape[0]
    tm = min(1024, t)
    nb = t // tm
    assert (2 * t) // MOE_TILE + N_EXPERTS <= 256, "tile-padded offsets must stay exact in bf16 (8 significant bits)"
    return pl.pallas_call(
        _route_kernel,
        grid=(2, nb),
        in_specs=[pl.BlockSpec((tm, LANES), lambda p, i: (i * (1 - p), 0))],
        out_specs=[
            pl.BlockSpec((tm, LANES), lambda p, i: (i * p, 0)),
            pl.BlockSpec((tm, LANES), lambda p, i: (i * p, 0)),
            pl.BlockSpec((SUBLANES, LANES), lambda p, i: (0, 0)),
        ],
        out_shape=[
            jax.ShapeDtypeStruct((t, LANES), jnp.int32),
            jax.ShapeDtypeStruct((t, LANES), F32),
            jax.ShapeDtypeStruct((SUBLANES, LANES), F32),
        ],
        scratch_shapes=[
            pltpu.VMEM((1, LANES), F32),
            pltpu.VMEM((1, LANES), F32),
            pltpu.VMEM((nb, tm, LANES), jnp.int32),
            pltpu.VMEM((nb, tm, LANES), F32),
        ],
        compiler_params=_cparams(("arbitrary", "arbitrary"), 32),
        name="route",
    )(logits)


def _row_gather_start(idx_at, src_hbm, dst, sem, rows):
    for r in range(rows):
        pltpu.make_async_copy(src_hbm.at[pl.ds(idx_at(r), 1)], dst.at[pl.ds(r, 1)], sem).start(priority=r % 2)


def _row_gather_wait(src_hbm, dst, sem, rows):
    pltpu.make_async_copy(src_hbm.at[pl.ds(0, rows)], dst, sem).wait()


def _dispatch_kernel(pos_ref, zrow_ref, h_ref, xs_hbm, zbuf, sems):
    i = pl.program_id(0)
    tm = h_ref.shape[0]
    tile = zbuf.shape[0]

    def zero_copy(z):
        return pltpu.make_async_copy(zbuf, xs_hbm.at[pl.ds(pl.multiple_of(zrow_ref[z], tile), tile)], sems.at[0])

    @pl.when(i == 0)
    def _():
        zbuf[...] = jnp.zeros(zbuf.shape, zbuf.dtype)
        for z in range(zrow_ref.shape[0]):
            @pl.when(zrow_ref[z] >= 0)
            def _():
                zero_copy(z).start()
        for z in range(zrow_ref.shape[0]):
            @pl.when(zrow_ref[z] >= 0)
            def _():
                zero_copy(z).wait()

    for k in range(2):
        base = (2 * i + k) * tm
        for r in range(tm):
            pltpu.make_async_copy(h_ref.at[pl.ds(r, 1)], xs_hbm.at[pl.ds(pos_ref[base + r], 1)],
                                  sems.at[1]).start(priority=r % 2)
    for k in range(2):
        pltpu.make_async_copy(h_ref, xs_hbm.at[pl.ds(0, tm)], sems.at[1]).wait()


def _dispatch(pos, zrow, h2, n_rows, seq):
    t, d = h2.shape
    tm = min(1024, seq)
    pos_blocks = pos.reshape(t // tm, tm, 2).transpose(0, 2, 1).reshape(-1)
    grid_spec = pltpu.PrefetchScalarGridSpec(
        num_scalar_prefetch=2,
        grid=(t // tm,),
        in_specs=[pl.BlockSpec((tm, d), lambda i, p, z: (i, 0))],
        out_specs=pl.BlockSpec(memory_space=pl.ANY),
        scratch_shapes=[
            pltpu.VMEM((MOE_TILE, d), F32),
            pltpu.SemaphoreType.DMA((2,)),
        ],
    )
    return pl.pallas_call(
        _dispatch_kernel,
        grid_spec=grid_spec,
        out_shape=jax.ShapeDtypeStruct((n_rows, d), F32),
        compiler_params=_cparams(("arbitrary",), 40),
        name="moe_dispatch",
    )(pos_blocks, zrow, h2)


def _moe_kernel(plan_ref, x_ref, wg_hbm, wu_hbm, wd_hbm, y_ref, stg_g, stg_u, stg_d, wsem):
    j = pl.program_id(0)
    nt = plan_ref[3, 0]
    nslot = stg_g.shape[0]

    def w_copies(e, s):
        return (pltpu.make_async_copy(wg_hbm.at[e], stg_g.at[s], wsem.at[s]),
                pltpu.make_async_copy(wu_hbm.at[e], stg_u.at[s], wsem.at[s]),
                pltpu.make_async_copy(wd_hbm.at[e], stg_d.at[s], wsem.at[s]))

    @pl.when(j == 0)
    def _():
        for s in range(nslot - 1):
            @pl.when(plan_ref[3, 1 + s] >= 0)
            def _():
                for cp in w_copies(plan_ref[3, 1 + s], s):
                    cp.start()

    @pl.when((j < nt) & (plan_ref[0, j] == 1))
    def _():
        s = plan_ref[1, j]
        for cp in w_copies(0, s):
            cp.wait()

        @pl.when(plan_ref[2, j] >= 0)
        def _():
            free = lax.rem(s + nslot - 1, nslot)
            for cp in w_copies(plan_ref[2, j], free):
                cp.start()

    @pl.when(j < nt)
    def _():
        s = plan_ref[1, j]
        x = x_ref[...].astype(BF16)
        hg = _dot(x, stg_g[s].astype(BF16))
        hu = _dot(x, stg_u[s].astype(BF16))
        act = (hg * _sigmoid(hg) * hu).astype(BF16)
        y_ref[...] = _dot(act, stg_d[s].astype(BF16))

    @pl.when(j >= nt)
    def _():
        y_ref[...] = jnp.zeros(y_ref.shape, y_ref.dtype)


def _moe(plan, xs, w_gate_e, w_up_e, w_down_e):
    d = xs.shape[1]
    f = w_gate_e.shape[-1]
    nt_max = plan.shape[1]
    tile = MOE_TILE
    any_spec = pl.BlockSpec(memory_space=pl.ANY)
    grid_spec = pltpu.PrefetchScalarGridSpec(
        num_scalar_prefetch=1,
        grid=(nt_max,),
        in_specs=[
            pl.BlockSpec((tile, d), lambda j, plan_: (jnp.minimum(j, plan_[3, 0] - 1), 0)),
            any_spec, any_spec, any_spec,
        ],
        out_specs=pl.BlockSpec((tile, d), lambda j, plan_: (j, 0)),
        scratch_shapes=[
            pltpu.VMEM((MOE_WEIGHT_SLOTS, d, f), F32),
            pltpu.VMEM((MOE_WEIGHT_SLOTS, d, f), F32),
            pltpu.VMEM((MOE_WEIGHT_SLOTS, f, d), F32),
            pltpu.SemaphoreType.DMA((MOE_WEIGHT_SLOTS,)),
        ],
    )
    return pl.pallas_call(
        _moe_kernel,
        grid_spec=grid_spec,
        out_shape=jax.ShapeDtypeStruct((nt_max * tile, d), F32),
        compiler_params=_cparams(("arbitrary",), 56),
        name="moe_experts",
    )(plan, xs, w_gate_e, w_up_e, w_down_e)


def _combine_kernel(pos_ref, y_hbm, x1_ref, gt_ref, wts_ref, o_ref, ybuf, sem):
    i = pl.program_id(0)
    n = pl.num_programs(0)
    tm = x1_ref.shape[0]
    slot = lax.rem(i, 2)

    def start(blk, s):
        for k in range(2):
            base = (2 * blk + k) * tm
            _row_gather_start(lambda r, base=base: pos_ref[base + r], y_hbm, ybuf.at[s, k], sem.at[s], tm)

    @pl.when(i == 0)
    def _():
        start(0, 0)

        @pl.when(n > 1)
        def _():
            start(1, 1)

    for k in range(2):
        _row_gather_wait(y_hbm, ybuf.at[slot, k], sem.at[slot], tm)
    w = wts_ref[...]
    moe = w[:, 0:1] * ybuf[slot, 0] + w[:, 1:2] * ybuf[slot, 1]
    o_ref[...] = x1_ref[...] + gt_ref[...] * moe

    @pl.when(i + 2 < n)
    def _():
        start(i + 2, slot)


def _combine(pos, ys, x1, mod4, wts, seq):
    t, d = x1.shape
    tm = min(512, seq)
    bpr = seq // tm
    pos_blocks = pos.reshape(t // tm, tm, 2).transpose(0, 2, 1).reshape(-1)
    grid_spec = pltpu.PrefetchScalarGridSpec(
        num_scalar_prefetch=1,
        grid=(t // tm,),
        in_specs=[
            pl.BlockSpec(memory_space=pl.ANY),
            pl.BlockSpec((tm, d), lambda i, p: (i, 0)),
            pl.BlockSpec((None, None, 1, d), lambda i, p: (i // bpr, 5, 0, 0)),
            pl.BlockSpec((tm, LANES), lambda i, p: (i, 0)),
        ],
        out_specs=pl.BlockSpec((tm, d), lambda i, p: (i, 0)),
        scratch_shapes=[
            pltpu.VMEM((2, 2, tm, d), F32),
            pltpu.SemaphoreType.DMA((2,)),
        ],
    )
    return pl.pallas_call(
        _combine_kernel,
        grid_spec=grid_spec,
        out_shape=jax.ShapeDtypeStruct((t, d), F32),
        compiler_params=_cparams(("arbitrary",), 48),
        name="moe_combine",
    )(pos_blocks, ys, x1, mod4, wts)


def _swap_halves(w):
    half = w.shape[-1] // 2
    return jnp.concatenate([w[..., half:], w[..., :half]], axis=-1)


def _layout_w_uq(w_uq):
    r = w_uq.shape[0]
    w = w_uq.reshape(r, MLA_HEADS, MLA_QK)
    rope = w[..., MLA_NOPE:]
    return jnp.concatenate([w[..., :MLA_NOPE], rope, _swap_halves(rope)], axis=-1).reshape(r, MLA_HEADS * QK_PAD).astype(BF16)


def _layout_w_ukv(w_ukv):
    r = w_ukv.shape[0]
    w = w_ukv.reshape(r, MLA_HEADS, MLA_NOPE + MLA_V)
    return jnp.concatenate([w[..., :MLA_NOPE].reshape(r, -1), w[..., MLA_NOPE:].reshape(r, -1)], axis=-1).astype(BF16)


def _rope_gain(g):
    g1 = g[MLA_NOPE:MLA_NOPE + MLA_ROPE // 2]
    g2 = g[MLA_NOPE + MLA_ROPE // 2:]
    return jnp.concatenate([g[:MLA_NOPE], g1, g2, -g2, g1]).reshape(1, QK_PAD)


def _layer(x2, cond_mod4, pos2, seq, w_in, q_a_norm_g, w_uq, kv_a_norm_g, w_ukv, q_norm_g, k_norm_g, conv_w, conv_b,
           b_mlstm_gates, mlstm_norm_g, w_proj_a, w_proj_b, w_out, norm_mix_g, norm_ffn_g, w_group, b_group,
           w_router, b_router, w_gate_e, w_up_e, w_down_e):
    t, d = x2.shape
    b = t // seq
    mod4 = cond_mod4

    proj, gates = _inproj(x2, mod4, norm_mix_g, w_in.T, seq)

    inv = ROPE_THETA ** (-jnp.arange(0, MLA_ROPE, 2, dtype=F32) / MLA_ROPE)
    inv_lanes = jnp.tile(inv, LANES // (MLA_ROPE // 2)).reshape(1, LANES)
    qt, k, vt = _mla_prep(proj, pos2, _layout_w_uq(w_uq), _layout_w_ukv(w_ukv), q_a_norm_g.reshape(1, -1),
                        kv_a_norm_g.reshape(1, -1), _rope_gain(q_norm_g), _rope_gain(k_norm_g), inv_lanes, seq)
    out_a = _flash(qt, k.reshape(b, seq, -1), vt).reshape(t, MLA_W)

    proj3 = proj.reshape(b, seq, -1)
    gbias = jnp.zeros((1, LANES), F32).at[0, GATE_LANE:GATE_LANE + 2 * ML_HEADS].set(b_mlstm_gates.reshape(-1))
    hm = _mlstm(proj3, conv_w, conv_b, gates.reshape(b, seq, LANES), gbias, mlstm_norm_g).reshape(t, ML_W)

    mixed = _merge(out_a, hm, proj, w_proj_a.astype(BF16), w_proj_b.astype(BF16), seq)

    route_pad = LANES - N_GROUPS - N_EXPERTS
    w_route = jnp.concatenate([w_group, w_router, jnp.zeros((d, route_pad), F32)], axis=1)
    b_route = jnp.concatenate([b_group, b_router, jnp.zeros((route_pad,), F32)]).reshape(1, LANES)
    w_route_hi = w_route.astype(BF16)
    w_route2 = jnp.concatenate([w_route_hi, (w_route - w_route_hi.astype(F32)).astype(BF16)], axis=1)
    x1, h2, logits = _outproj(mixed, x2, w_out.astype(BF16), mod4, norm_ffn_g, w_route2, b_route, seq)

    posm, wts, counts = _route(logits)
    pos = posm[:, 0:2]

    tile = MOE_TILE
    i32 = jnp.int32
    cnt = counts[0, N_GROUPS:N_GROUPS + N_EXPERTS].astype(i32)
    padded = ((cnt + tile - 1) // tile) * tile
    ends = jnp.cumsum(padded)
    offs = ends - padded
    nt_max = (2 * t) // tile + N_EXPERTS
    n_tiles = ends[-1] // tile
    tile_idx = jnp.arange(nt_max, dtype=i32)
    tile_start = tile_idx * tile
    live = tile_idx < n_tiles
    texp = jnp.minimum(jnp.sum((ends[None, :] <= tile_start[:, None]).astype(i32), axis=1), N_EXPERTS - 1)
    active = cnt > 0
    order = jnp.cumsum(active.astype(i32)) - 1
    n_active = jnp.sum(active.astype(i32))
    experts = jnp.arange(N_EXPERTS, dtype=i32)
    by_order = jnp.sum(jnp.where(active[None, :] & (order[None, :] == experts[:, None]), experts[None, :], 0), axis=1)
    t_order = order[texp]
    first = (live & (tile_start == offs[texp])).astype(i32)
    nslot = MOE_WEIGHT_SLOTS
    ahead = t_order + nslot - 1
    prefetch = jnp.where(ahead < n_active, by_order[jnp.minimum(ahead, N_EXPERTS - 1)], -1)
    lead = jnp.where(experts[:nslot - 1] < n_active, by_order[:nslot - 1], -1)
    head = jnp.zeros((nt_max,), i32).at[0].set(n_tiles).at[1:nslot].set(lead)
    plan = jnp.stack([first, t_order % nslot, prefetch, head]).astype(i32)

    last_tile = jnp.where(padded > cnt, ends - tile, -1)
    spare_idx = n_tiles + jnp.arange(N_EXPERTS, dtype=i32)
    spare_tile = jnp.where(spare_idx < nt_max, spare_idx * tile, -1)
    zrow = jnp.concatenate([last_tile, spare_tile]).astype(i32)

    xs = _dispatch(pos, zrow, h2, nt_max * tile, seq)
    ys = _moe(plan, xs, w_gate_e, w_up_e, w_down_e)

    return _combine(pos, ys, x1, mod4, wts, seq)


def kernel(x, c, positions, w_ada, b_ada, norm_mix_g, w_in, q_a_norm_g, w_uq, kv_a_norm_g, w_ukv, q_norm_g, k_norm_g, conv_w, conv_b, b_mlstm_gates, mlstm_norm_g, w_proj_a, w_proj_b, w_out, norm_ffn_g, w_group, b_group, w_router, b_router, w_gate_e, w_up_e, w_down_e):
    b, seq, d = x.shape
    depth = w_ada.shape[0]
    x2 = x.reshape(b * seq, d)
    pos2 = positions.reshape(b * seq, 1)
    c_pad = jnp.zeros((SUBLANES, d), F32).at[:b].set(c)
    for l in range(depth):
        mod = _adaln(c_pad, w_ada[l], b_ada[l])
        mod4 = mod[:b].reshape(b, 6, 1, d)
        x2 = _layer(x2, mod4, pos2, seq, w_in[l], q_a_norm_g[l], w_uq[l], kv_a_norm_g[l], w_ukv[l], q_norm_g[l],
                    k_norm_g[l], conv_w[l], conv_b[l], b_mlstm_gates[l], mlstm_norm_g[l], w_proj_a[l], w_proj_b[l],
                    w_out[l], norm_mix_g[l], norm_ffn_g[l], w_group[l], b_group[l], w_router[l], b_router[l],
                    w_gate_e[l], w_up_e[l], w_down_e[l])
    return x2.reshape(b, seq, d)
```

```python
import functools
import math

import jax
import jax.numpy as jnp
from jax import lax
from jax.experimental import pallas as pl
from jax.experimental.pallas import tpu as pltpu

F32 = jnp.float32
BF16 = jnp.bfloat16

LANES = 128
SUBLANES = 8

D_MODEL = 2048
MLA_HEADS = 8
MLA_NOPE = 128
MLA_ROPE = 64
MLA_QK = MLA_NOPE + MLA_ROPE
MLA_V = 128
Q_LORA = 512
KV_LORA = 256
ROPE_THETA = 10000.0
ML_HEADS = 8
ML_DQK = 128
ML_DV = 128
ML_CONV = 4
MLA_W = MLA_HEADS * MLA_V
ML_W = ML_HEADS * ML_DV
N_GROUPS = 4
EXP_PER_GROUP = 8
N_EXPERTS = N_GROUPS * EXP_PER_GROUP
EPS = 1e-6

QK_PAD = 2 * LANES

SRC_CQ = 0
SRC_KPE = Q_LORA + KV_LORA
SRC_QK = SRC_KPE + MLA_ROPE
SRC_V = SRC_QK + 2 * ML_HEADS * ML_DQK
SRC_O = SRC_V + ML_W
SRC_I = SRC_O + ML_W
SRC_GA = SRC_I + 2 * ML_HEADS
SRC_GB = SRC_GA + D_MODEL
GATE_LANE = SRC_I % LANES

IN_BLOCK = 1024
IN_BLOCK_SRC = (SRC_QK, SRC_QK + IN_BLOCK, SRC_GA, SRC_GA + IN_BLOCK, SRC_GB, SRC_GB + IN_BLOCK, SRC_V, SRC_O, SRC_CQ)
COL_QK = 0
COL_GA = 2048
COL_GB = 4096
COL_V = 6144
COL_O = 7168
COL_LAT = 8192
LAT_W = Q_LORA + KV_LORA + MLA_ROPE

MLSTM_CHUNK = 256
MOE_TILE = 256
MOE_WEIGHT_SLOTS = 3
ROUTE_RANK_ROWS = 256
FLASH_HEADS_PER_STEP = 2
FLASH_Q_TILE = 512
FLASH_K_TILE = 512
OUTPROJ_ROW_GROUP = 256
MERGE_ROW_GROUP = 256


def _cparams(sem, vmem_mb):
    return pltpu.CompilerParams(dimension_semantics=sem, vmem_limit_bytes=vmem_mb * 1024 * 1024)


def _dot(a, b):
    return jnp.dot(a, b, preferred_element_type=F32)


def _dot_nt(a, b):
    return lax.dot_general(a, b, (((1,), (1,)), ((), ())), preferred_element_type=F32)


def _sigmoid(x):
    return 1.0 / (1.0 + jnp.exp(-x))


def _rms_scale(x, width):
    return lax.rsqrt(jnp.sum(x * x, axis=-1, keepdims=True) * (1.0 / width) + EPS)


def _adaln_kernel(c_ref, w_ref, b_ref, o_ref):
    c = c_ref[...]
    cond = (c * _sigmoid(c)).astype(BF16)
    o_ref[...] = _dot(cond, w_ref[...].astype(BF16)) + b_ref[...]


def _adaln(c_pad, w_ada, b_ada):
    rows, d = c_pad.shape
    n = w_ada.shape[1]
    tn = 2048
    return pl.pallas_call(
        _adaln_kernel,
        grid=(n // tn,),
        in_specs=[
            pl.BlockSpec((rows, d), lambda j: (0, 0)),
            pl.BlockSpec((d, tn), lambda j: (0, j)),
            pl.BlockSpec((1, tn), lambda j: (0, j)),
        ],
        out_specs=pl.BlockSpec((rows, tn), lambda j: (0, j)),
        out_shape=jax.ShapeDtypeStruct((rows, n), F32),
        compiler_params=_cparams(("arbitrary",), 48),
        name="adaln",
    )(c_pad, w_ada, b_ada.reshape(1, n))


def _inproj_kernel(off_ref, x_ref, sc_ref, sh_ref, g_ref, w_ref, wgate_ref, proj_ref, gates_ref, h_ref):
    j = pl.program_id(1)

    @pl.when(j == 0)
    def _():
        x = x_ref[...]
        h = x * _rms_scale(x, x.shape[-1]) * g_ref[...]
        h = (h * (1.0 + sc_ref[...]) + sh_ref[...]).astype(BF16)
        h_ref[...] = h
        gates_ref[...] = _dot_nt(h, wgate_ref[...].astype(BF16))

    proj_ref[...] = _dot_nt(h_ref[...], w_ref[...].astype(BF16)).astype(proj_ref.dtype)


def _inproj(x2, mod4, norm_g, w_in_t, seq):
    t, d = x2.shape
    tm = min(1024, seq)
    tn = IN_BLOCK
    bpr = seq // tm
    assert all(o % SUBLANES == 0 for o in IN_BLOCK_SRC)
    offs = jnp.asarray([o // SUBLANES for o in IN_BLOCK_SRC], jnp.int32)
    nblk = len(IN_BLOCK_SRC)
    gate_tile = SRC_I // LANES
    grid_spec = pltpu.PrefetchScalarGridSpec(
        num_scalar_prefetch=1,
        grid=(t // tm, nblk),
        in_specs=[
            pl.BlockSpec((tm, d), lambda i, j, o: (i, 0)),
            pl.BlockSpec((None, None, 1, d), lambda i, j, o: (i // bpr, 1, 0, 0)),
            pl.BlockSpec((None, None, 1, d), lambda i, j, o: (i // bpr, 0, 0, 0)),
            pl.BlockSpec((1, d), lambda i, j, o: (0, 0)),
            pl.BlockSpec((pl.Element(tn), pl.Element(d)), lambda i, j, o: (o[j] * SUBLANES, 0)),
            pl.BlockSpec((LANES, d), lambda i, j, o: (gate_tile, 0)),
        ],
        out_specs=[
            pl.BlockSpec((tm, tn), lambda i, j, o: (i, j)),
            pl.BlockSpec((tm, LANES), lambda i, j, o: (i, 0)),
        ],
        scratch_shapes=[pltpu.VMEM((tm, d), BF16)],
    )
    return pl.pallas_call(
        _inproj_kernel,
        grid_spec=grid_spec,
        out_shape=[
            jax.ShapeDtypeStruct((t, nblk * tn), BF16),
            jax.ShapeDtypeStruct((t, LANES), F32),
        ],
        compiler_params=_cparams(("arbitrary", "arbitrary"), 56),
        name="inproj",
    )(offs, x2, mod4, mod4, norm_g.reshape(1, d), w_in_t, w_in_t)


def _mla_prep_kernel(lat_ref, pos_ref, wuq_ref, wukv_ref, gqa_ref, gkva_ref, gq_ref, gk_ref, inv_ref,
                     qt_ref, k_ref, vt_ref):
    lat = lat_ref[...].astype(F32)
    cq = lat[:, :Q_LORA]
    ckv = lat[:, Q_LORA:Q_LORA + KV_LORA]
    kc = lat[:, SRC_KPE:SRC_KPE + LANES]
    cqn = (cq * _rms_scale(cq, Q_LORA) * gqa_ref[...]).astype(BF16)
    ckvn = (ckv * _rms_scale(ckv, KV_LORA) * gkva_ref[...]).astype(BF16)
    qraw = _dot(cqn, wuq_ref[...])
    kv = _dot(ckvn, wukv_ref[...])

    ang = pos_ref[...].astype(F32) * inv_ref[...]
    lane = lax.broadcasted_iota(jnp.int32, ang.shape, 1)
    lo = lane < MLA_ROPE
    cs = jnp.cos(ang - jnp.where(lo, 0.0, 0.5 * math.pi))

    quarter = MLA_ROPE // 2
    want = jnp.where(lane < 3 * quarter, lane - quarter, lane - 3 * quarter)
    came = pltpu.roll(lane, quarter, 1)
    swapped = jnp.where(came == want, pltpu.roll(kc, quarter, 1), pltpu.roll(kc, 3 * quarter, 1))
    kc = jnp.where(lo, kc, swapped)

    gq = gq_ref[...]
    gk = gk_ref[...]
    gq_n, gq_r = gq[:, :LANES], gq[:, LANES:]
    gk_n, gk_r = gk[:, :LANES], gk[:, LANES:]

    def rope(chunk, g_cs):
        a = chunk * g_cs
        return jnp.where(lo, a + pltpu.roll(a, MLA_ROPE, 1), 0.0)

    gq_cs = gq_r * cs
    kpe_ss = jnp.sum(jnp.where(lo, kc * kc, 0.0), axis=-1, keepdims=True)
    k_rope = rope(kc, gk_r * cs)
    scale = MLA_QK ** -0.5 * math.log2(math.e)
    for h in range(MLA_HEADS):
        kn = kv[:, h * MLA_NOPE:(h + 1) * MLA_NOPE]
        sk = lax.rsqrt((jnp.sum(kn * kn, axis=-1, keepdims=True) + kpe_ss) * (1.0 / MLA_QK) + EPS)
        k_ref[:, h * QK_PAD:h * QK_PAD + LANES] = (kn * sk * gk_n).astype(BF16)
        k_ref[:, h * QK_PAD + LANES:(h + 1) * QK_PAD] = (k_rope * sk).astype(BF16)
        qn = qraw[:, h * QK_PAD:h * QK_PAD + LANES]
        qr = qraw[:, h * QK_PAD + LANES:(h + 1) * QK_PAD]
        ss = jnp.sum(qn * qn, axis=-1, keepdims=True) + jnp.sum(jnp.where(lo, qr * qr, 0.0), axis=-1, keepdims=True)
        sq = lax.rsqrt(ss * (1.0 / MLA_QK) + EPS) * scale
        qt_ref[h * QK_PAD:h * QK_PAD + LANES, :] = (qn * sq * gq_n).astype(BF16).T
        qt_ref[h * QK_PAD + LANES:(h + 1) * QK_PAD, :] = (rope(qr, gq_cs) * sq).astype(BF16).T
        vh = kv[:, MLA_HEADS * MLA_NOPE + h * MLA_V:MLA_HEADS * MLA_NOPE + (h + 1) * MLA_V]
        vt_ref[h * MLA_V:(h + 1) * MLA_V, :] = vh.astype(BF16).T


def _mla_prep(proj, pos2, wuq_p, wukv_p, gqa, gkva, gq, gk, inv_lanes, seq):
    t = proj.shape[0]
    tm = min(512, seq)
    hq = MLA_HEADS * QK_PAD
    lat_blk = COL_LAT // IN_BLOCK
    const = lambda i: (0, 0)
    return pl.pallas_call(
        _mla_prep_kernel,
        grid=(t // tm,),
        in_specs=[
            pl.BlockSpec((tm, IN_BLOCK), lambda i: (i, lat_blk)),
            pl.BlockSpec((tm, 1), lambda i: (i, 0)),
            pl.BlockSpec(wuq_p.shape, const),
            pl.BlockSpec(wukv_p.shape, const),
            pl.BlockSpec(gqa.shape, const),
            pl.BlockSpec(gkva.shape, const),
            pl.BlockSpec(gq.shape, const),
            pl.BlockSpec(gk.shape, const),
            pl.BlockSpec(inv_lanes.shape, const),
        ],
        out_specs=[
            pl.BlockSpec((hq, tm), lambda i: (0, i)),
            pl.BlockSpec((tm, hq), lambda i: (i, 0)),
            pl.BlockSpec((MLA_W, tm), lambda i: (0, i)),
        ],
        out_shape=[
            jax.ShapeDtypeStruct((hq, t), BF16),
            jax.ShapeDtypeStruct((t, hq), BF16),
            jax.ShapeDtypeStruct((MLA_W, t), BF16),
        ],
        compiler_params=_cparams(("arbitrary",), 48),
        name="mla_prep",
    )(proj, pos2, wuq_p, wukv_p, gqa, gkva, gq, gk, inv_lanes)


def _flash_kernel(qt_ref, k_ref, vt_ref, o_ref, *, tq, tk):
    seq = k_ref.shape[0]
    heads = k_ref.shape[1] // QK_PAD

    def scores(h, k0, q0):
        kj = k_ref[k0:k0 + tk, h * QK_PAD:(h + 1) * QK_PAD]
        return _dot(kj, qt_ref[h * QK_PAD:(h + 1) * QK_PAD, q0:q0 + tq])

    def update(h, state, st, k0, q0):
        m, l, acc = state
        if k0 + tk - 1 > q0:
            key = lax.broadcasted_iota(jnp.int32, st.shape, 0) + k0
            qry = lax.broadcasted_iota(jnp.int32, st.shape, 1) + q0
            st = jnp.where(key <= qry, st, -jnp.inf)
        m_new = jnp.maximum(m, jnp.max(st, axis=0, keepdims=True))
        alpha = jnp.exp2(m - m_new)
        p = jnp.exp2(st - m_new)
        l = alpha * l + jnp.sum(p, axis=0, keepdims=True)
        acc = alpha * acc + _dot(vt_ref[h * MLA_V:(h + 1) * MLA_V, k0:k0 + tk], p.astype(BF16))
        return m_new, l, acc

    steps = [(h, qi * tq, j * tk) for qi in range(seq // tq) for j in range((qi + 1) * tq // tk) for h in range(heads)]
    st_next = scores(steps[0][0], steps[0][2], steps[0][1])
    states = {}
    for n, (h, q0, k0) in enumerate(steps):
        st = st_next
        if n + 1 < len(steps):
            hn, qn, kn = steps[n + 1]
            st_next = scores(hn, kn, qn)
        if k0 == 0:
            states[h] = (jnp.full((1, tq), -jnp.inf, F32), jnp.zeros((1, tq), F32), jnp.zeros((MLA_V, tq), F32))
        states[h] = update(h, states[h], st, k0, q0)
        if k0 + tk >= q0 + tq:
            _, l, acc = states[h]
            o_ref[q0:q0 + tq, h * MLA_V:(h + 1) * MLA_V] = (acc / l).T.astype(o_ref.dtype)


def _flash(qt, k3, vt):
    b, seq, _ = k3.shape
    tq = min(FLASH_Q_TILE, seq)
    tk = min(FLASH_K_TILE, seq)
    hp = FLASH_HEADS_PER_STEP
    kern = functools.partial(_flash_kernel, tq=tq, tk=tk)
    return pl.pallas_call(
        kern,
        grid=(b, MLA_HEADS // hp),
        in_specs=[
            pl.BlockSpec((hp * QK_PAD, seq), lambda i, h: (h, i)),
            pl.BlockSpec((None, seq, hp * QK_PAD), lambda i, h: (i, 0, h)),
            pl.BlockSpec((hp * MLA_V, seq), lambda i, h: (h, i)),
        ],
        out_specs=pl.BlockSpec((None, seq, hp * MLA_V), lambda i, h: (i, 0, h)),
        out_shape=jax.ShapeDtypeStruct((b, seq, MLA_W), BF16),
        compiler_params=_cparams(("arbitrary", "arbitrary"), 40),
        name="flash",
    )(qt, k3, vt)


def _log_sigmoid(x):
    return -(jnp.maximum(-x, 0.0) + jnp.log1p(jnp.exp(-jnp.abs(x))))


def _conv_silu_qk(cur_ref, halo_ref, w_ref, b_ref, buf_ref, qt_ref, k_ref, first):
    L = cur_ref.shape[0]
    cols = cur_ref.shape[1]
    half = cols // 2
    buf_ref[0:SUBLANES, :] = jnp.where(first, 0.0, halo_ref[...].astype(F32))
    buf_ref[SUBLANES:SUBLANES + L, :] = cur_ref[...].astype(F32)
    cw = 512
    for c in range(cols // cw):
        sl = slice(c * cw, (c + 1) * cw)
        acc = jnp.zeros((L, cw), F32) + b_ref[:, sl]
        for j in range(ML_CONV):
            off = SUBLANES - (ML_CONV - 1) + j
            acc = acc + buf_ref[off:off + L, sl] * w_ref[j:j + 1, sl]
        y = acc * _sigmoid(acc)
        if c * cw < half:
            qt_ref[sl, :] = y.astype(qt_ref.dtype).T
        else:
            k_ref[:, c * cw - half:(c + 1) * cw - half] = (y * (ML_DQK ** -0.5)).astype(k_ref.dtype)


def _mlstm_kernel(cur_ref, halo_ref, cw_ref, cb_ref, v_ref, o_ref, gates_ref, gbias_ref, ng_ref, out_ref,
                  ct_ref, m_ref, buf_ref, qt_ref, k_ref):
    L = cur_ref.shape[0]
    first = pl.program_id(1) == 0

    @pl.when(first)
    def _():
        ct_ref[...] = jnp.zeros(ct_ref.shape, F32)
        m_ref[...] = jnp.zeros(m_ref.shape, F32)

    _conv_silu_qk(cur_ref, halo_ref, cw_ref, cb_ref, buf_ref, qt_ref, k_ref, first)

    g = gates_ref[...] + gbias_ref[...]
    gt = g.T
    lf = _log_sigmoid(g)
    lft = _log_sigmoid(gt)
    r = lax.broadcasted_iota(jnp.int32, (L, L), 0)
    c = lax.broadcasted_iota(jnp.int32, (L, L), 1)
    src_le_qry = r <= c
    tril = (c <= r).astype(F32)
    triu = src_le_qry.astype(F32)
    hi = lax.Precision.HIGHEST
    bcol_all = jnp.dot(tril, lf, preferred_element_type=F32, precision=hi)
    brow_all = jnp.dot(lft, triu, preferred_element_type=F32, precision=hi)
    row = lax.broadcasted_iota(jnp.int32, (ML_DV, L), 0)
    ones_row = jnp.where(row == 0, 1.0, 0.0)

    def lead_matmuls(h):
        hs = slice(h * ML_DQK, (h + 1) * ML_DQK)
        qt = qt_ref[hs, :]
        return _dot(k_ref[:, hs], qt), _dot(ct_ref[h].astype(BF16), qt)

    lead_next = lead_matmuls(0)
    for h in range(ML_HEADS):
        hs = slice(h * ML_DQK, (h + 1) * ML_DQK)
        qk_t, cq_t = lead_next
        if h + 1 < ML_HEADS:
            lead_next = lead_matmuls(h + 1)
        li, lf_ = GATE_LANE + h, GATE_LANE + ML_HEADS + h
        b_row = brow_all[lf_:lf_ + 1, :]
        i_row = gt[li:li + 1, :]
        u_col = g[:, li:li + 1] - bcol_all[:, lf_:lf_ + 1]
        m_prev = m_ref[h][:, :1]
        logw_t = jnp.where(src_le_qry, b_row + u_col, -jnp.inf)
        log_inter = b_row + m_prev
        m_t = jnp.maximum(jnp.max(logw_t, axis=0, keepdims=True), log_inter)
        w_t = jnp.exp(logw_t - m_t)
        a = jnp.exp(log_inter - m_t)
        kh = k_ref[:, hs]
        vt_aug = jnp.concatenate([v_ref[:, hs].astype(F32).T, ones_row], axis=0)
        s_t = qk_t * w_t
        ct = ct_ref[h]
        nd = _dot(vt_aug.astype(BF16), s_t.astype(BF16)) + a * cq_t
        num = nd[:ML_DV, :]
        den = nd[ML_DV:ML_DV + 1, :]
        hout_t = num * (1.0 / jnp.maximum(jnp.abs(den), jnp.exp(-m_t)))
        hn_t = hout_t * lax.rsqrt(jnp.sum(hout_t * hout_t, axis=0, keepdims=True) * (1.0 / ML_DV) + EPS)
        gate = _sigmoid(o_ref[:, hs].astype(F32))
        out_ref[:, hs] = (hn_t.T * ng_ref[:, hs] * gate).astype(out_ref.dtype)

        b_last = b_row[:, L - 1:L]
        logg = b_last - b_row + i_row
        m_new = jnp.maximum(b_last + m_prev, jnp.max(logg, axis=-1, keepdims=True))
        g_row = jnp.exp(logg - m_new)
        decay = jnp.exp(b_last + m_prev - m_new)
        ct_ref[h] = decay * ct + _dot((vt_aug * g_row).astype(BF16), kh)
        m_ref[h] = jnp.broadcast_to(m_new, m_ref.shape[1:])


def _mlstm(proj3, conv_w, conv_b, gates3, gbias, norm_g):
    b, seq, _ = proj3.shape
    w = ML_W
    qk_cols = 2 * ML_HEADS * ML_DQK
    L = min(MLSTM_CHUNK, seq)
    hb = L // SUBLANES
    return pl.pallas_call(
        _mlstm_kernel,
        grid=(b, seq // L),
        in_specs=[
            pl.BlockSpec((None, L, qk_cols), lambda i, c: (i, c, COL_QK // qk_cols)),
            pl.BlockSpec((None, SUBLANES, qk_cols), lambda i, c: (i, jnp.maximum(c * hb - 1, 0), COL_QK // qk_cols)),
            pl.BlockSpec((ML_CONV, qk_cols), lambda i, c: (0, 0)),
            pl.BlockSpec((1, qk_cols), lambda i, c: (0, 0)),
            pl.BlockSpec((None, L, w), lambda i, c: (i, c, COL_V // w)),
            pl.BlockSpec((None, L, w), lambda i, c: (i, c, COL_O // w)),
            pl.BlockSpec((None, L, LANES), lambda i, c: (i, c, 0)),
            pl.BlockSpec((1, LANES), lambda i, c: (0, 0)),
            pl.BlockSpec((1, w), lambda i, c: (0, 0)),
        ],
        out_specs=pl.BlockSpec((None, L, w), lambda i, c: (i, c, 0)),
        out_shape=jax.ShapeDtypeStruct((b, seq, w), BF16),
        scratch_shapes=[
            pltpu.VMEM((ML_HEADS, 2 * ML_DV, ML_DQK), F32),
            pltpu.VMEM((ML_HEADS, 1, LANES), F32),
            pltpu.VMEM((L + SUBLANES, qk_cols), F32),
            pltpu.VMEM((ML_HEADS * ML_DQK, L), BF16),
            pltpu.VMEM((L, ML_HEADS * ML_DQK), BF16),
        ],
        compiler_params=_cparams(("arbitrary", "arbitrary"), 40),
        name="mlstm",
    )(proj3, proj3, conv_w, conv_b.reshape(1, qk_cols), proj3, proj3, gates3, gbias, norm_g.reshape(1, w))


def _merge_kernel(a_ref, b_ref, ga_ref, gb_ref, wa_ref, wb_ref, o_ref):
    tm = a_ref.shape[0]
    rows = min(MERGE_ROW_GROUP, tm)
    groups = [slice(r0, r0 + rows) for r0 in range(0, tm, rows)]

    def proj(rs):
        return _dot(a_ref[rs, :], wa_ref[...]), _dot(b_ref[rs, :], wb_ref[...])

    nxt = proj(groups[0])
    for n, rs in enumerate(groups):
        pa, pb = nxt
        if n + 1 < len(groups):
            nxt = proj(groups[n + 1])
        mixed = _sigmoid(ga_ref[rs, :].astype(F32)) * pa + _sigmoid(gb_ref[rs, :].astype(F32)) * pb
        o_ref[rs, :] = mixed.astype(o_ref.dtype)


def _merge(out_a, hm, proj, wa, wb, seq):
    t = out_a.shape[0]
    d = wa.shape[1]
    tm = min(512, seq)
    tn = 2048
    return pl.pallas_call(
        _merge_kernel,
        grid=(d // tn, t // tm),
        in_specs=[
            pl.BlockSpec((tm, MLA_W), lambda j, i: (i, 0)),
            pl.BlockSpec((tm, ML_W), lambda j, i: (i, 0)),
            pl.BlockSpec((tm, tn), lambda j, i: (i, COL_GA // tn + j)),
            pl.BlockSpec((tm, tn), lambda j, i: (i, COL_GB // tn + j)),
            pl.BlockSpec((MLA_W, tn), lambda j, i: (0, j)),
            pl.BlockSpec((ML_W, tn), lambda j, i: (0, j)),
        ],
        out_specs=pl.BlockSpec((tm, tn), lambda j, i: (i, j)),
        out_shape=jax.ShapeDtypeStruct((t, d), BF16),
        compiler_params=_cparams(("arbitrary", "arbitrary"), 40),
        name="merge",
    )(out_a, hm, proj, proj, wa, wb)


def _outproj_kernel(mix_ref, x_ref, w_ref, gt_ref, sc_ref, sh_ref, g_ref, wr_ref, br_ref, x1_ref, h2_ref, lg_ref):
    tm = mix_ref.shape[0]
    rows = OUTPROJ_ROW_GROUP
    groups = [slice(r0, r0 + rows) for r0 in range(0, tm, rows)]
    y_next = _dot(mix_ref[groups[0], :], w_ref[...])
    for n, rs in enumerate(groups):
        y = y_next
        if n + 1 < len(groups):
            y_next = _dot(mix_ref[groups[n + 1], :], w_ref[...])
        x1 = x_ref[rs, :] + gt_ref[...] * y
        x1_ref[rs, :] = x1
        h2 = x1 * _rms_scale(x1, x1.shape[-1]) * g_ref[...]
        h2 = h2 * (1.0 + sc_ref[...]) + sh_ref[...]
        h2_ref[rs, :] = h2
        h_hi = h2.astype(BF16)
        h_lo = (h2 - h_hi.astype(F32)).astype(BF16)
        r = _dot(h_hi, wr_ref[...]) + _dot(h_lo, wr_ref[...])
        lg_ref[rs, :] = r[:, :LANES] + r[:, LANES:] + br_ref[...]


def _outproj(mixed, x2, w_out, mod4, norm_g, w_route2, b_route, seq):
    t, d = x2.shape
    tm = min(512, seq)
    bpr = seq // tm
    mod_spec = lambda k: pl.BlockSpec((None, None, 1, d), lambda i: (i // bpr, k, 0, 0))
    const = lambda i: (0, 0)
    return pl.pallas_call(
        _outproj_kernel,
        grid=(t // tm,),
        in_specs=[
            pl.BlockSpec((tm, d), lambda i: (i, 0)),
            pl.BlockSpec((tm, d), lambda i: (i, 0)),
            pl.BlockSpec((d, d), const, pipeline_mode=pl.Buffered(1)),
            mod_spec(2),
            mod_spec(4),
            mod_spec(3),
            pl.BlockSpec((1, d), const),
            pl.BlockSpec((d, 2 * LANES), const, pipeline_mode=pl.Buffered(1)),
            pl.BlockSpec((1, LANES), const),
        ],
        out_specs=[
            pl.BlockSpec((tm, d), lambda i: (i, 0)),
            pl.BlockSpec((tm, d), lambda i: (i, 0)),
            pl.BlockSpec((tm, LANES), lambda i: (i, 0)),
        ],
        out_shape=[
            jax.ShapeDtypeStruct((t, d), F32),
            jax.ShapeDtypeStruct((t, d), F32),
            jax.ShapeDtypeStruct((t, LANES), F32),
        ],
        compiler_params=_cparams(("arbitrary",), 56),
        name="outproj",
    )(mixed, x2, w_out, mod4, mod4, mod4, norm_g.reshape(1, d), w_route2, b_route)


def _route_kernel(lg_ref, pos_ref, wts_ref, cnt_ref, carry_ref, offs_ref, meta_s, wts_s):
    phase = pl.program_id(0)
    i = pl.program_id(1)
    tm = lg_ref.shape[0]
    lane = lax.broadcasted_iota(jnp.int32, (tm, LANES), 1)

    @pl.when((phase == 0) & (i == 0))
    def _():
        carry_ref[...] = jnp.zeros(carry_ref.shape, F32)

    @pl.when(phase == 0)
    def _():
        lg = lg_ref[...]
        big = jnp.int32(LANES)
        ninf = -jnp.inf

        def first_argmax(vals):
            mx = jnp.max(vals, axis=-1, keepdims=True)
            idx = jnp.min(jnp.where(vals == mx, lane, big), axis=-1, keepdims=True)
            return mx, idx

        gl = jnp.where(lane < N_GROUPS, lg, ninf)
        gmax, gsel = first_argmax(gl)
        g_w = 1.0 / jnp.sum(jnp.exp(gl - gmax), axis=-1, keepdims=True)
        lo = N_GROUPS + gsel * EXP_PER_GROUP
        in_grp = (lane >= lo) & (lane < lo + EXP_PER_GROUP)
        el = jnp.where(in_grp, lg, ninf)
        e1, i1 = first_argmax(el)
        e2, i2 = first_argmax(jnp.where(lane == i1, ninf, el))
        p2 = jnp.exp(e2 - e1)
        w1 = g_w / (1.0 + p2)
        w2 = g_w * p2 / (1.0 + p2)

        oh1 = lane == i1
        oh2 = lane == i2
        oh = jnp.where(oh1 | oh2, 1.0, 0.0)
        sub = min(ROUTE_RANK_ROWS, tm)
        r = lax.broadcasted_iota(jnp.int32, (sub, sub), 0)
        c = lax.broadcasted_iota(jnp.int32, (sub, sub), 1)
        strict = jnp.where(c < r, 1.0, 0.0).astype(BF16)
        carry = carry_ref[...]
        parts = []
        for r0 in range(0, tm, sub):
            oh_sub = oh[r0:r0 + sub, :]
            parts.append(_dot(strict, oh_sub.astype(BF16)) + carry)
            carry = carry + jnp.sum(oh_sub, axis=0, keepdims=True)
        before = jnp.concatenate(parts, axis=0)
        rank1 = jnp.sum(jnp.where(oh1, before, 0.0), axis=-1, keepdims=True).astype(jnp.int32)
        rank2 = jnp.sum(jnp.where(oh2, before, 0.0), axis=-1, keepdims=True).astype(jnp.int32)
        carry_ref[...] = carry
        meta_s[i] = jnp.where(lane == 0, i1, jnp.where(lane == 1, i2, jnp.where(lane == 2, rank1, jnp.where(lane == 3, rank2, 0))))
        wts_s[i] = jnp.where(lane == 0, w1, jnp.where(lane == 1, w2, 0.0))

    @pl.when((phase == 1) & (i == 0))
    def _():
        cnt = carry_ref[...]
        cnt_ref[...] = jnp.broadcast_to(cnt, cnt_ref.shape)
        padded = jnp.ceil(cnt * (1.0 / MOE_TILE)) * MOE_TILE
        r = lax.broadcasted_iota(jnp.int32, (LANES, LANES), 0)
        c = lax.broadcasted_iota(jnp.int32, (LANES, LANES), 1)
        upper = jnp.where(r < c, 1.0, 0.0).astype(BF16)
        padded8 = jnp.broadcast_to(padded, (SUBLANES, LANES)).astype(BF16)
        offs_ref[...] = _dot(padded8, upper)[:1, :]

    @pl.when(phase == 1)
    def _():
        meta = meta_s[i]
        offs = offs_ref[...]
        off1 = jnp.sum(jnp.where(lane == meta[:, 0:1], offs, 0.0), axis=-1, keepdims=True).astype(jnp.int32)
        off2 = jnp.sum(jnp.where(lane == meta[:, 1:2], offs, 0.0), axis=-1, keepdims=True).astype(jnp.int32)
        pos1 = off1 + meta[:, 2:3]
        pos2 = off2 + meta[:, 3:4]
        pos_ref[...] = jnp.where(lane == 0, pos1, jnp.where(lane == 1, pos2, 0))
        wts_ref[...] = wts_s[i]


def _route(logits):
    t = logits.shape[0]
    tm = min(1024, t)
    nb = t // tm
    assert (2 * t) // MOE_TILE + N_EXPERTS <= 256, "tile-padded offsets must stay exact in bf16 (8 significant bits)"
    return pl.pallas_call(
        _route_kernel,
        grid=(2, nb),
        in_specs=[pl.BlockSpec((tm, LANES), lambda p, i: (i * (1 - p), 0))],
        out_specs=[
            pl.BlockSpec((tm, LANES), lambda p, i: (i * p, 0)),
            pl.BlockSpec((tm, LANES), lambda p, i: (i * p, 0)),
            pl.BlockSpec((SUBLANES, LANES), lambda p, i: (0, 0)),
        ],
        out_shape=[
            jax.ShapeDtypeStruct((t, LANES), jnp.int32),
            jax.ShapeDtypeStruct((t, LANES), F32),
            jax.ShapeDtypeStruct((SUBLANES, LANES), F32),
        ],
        scratch_shapes=[
            pltpu.VMEM((1, LANES), F32),
            pltpu.VMEM((1, LANES), F32),
            pltpu.VMEM((nb, tm, LANES), jnp.int32),
            pltpu.VMEM((nb, tm, LANES), F32),
        ],
        compiler_params=_cparams(("arbitrary", "arbitrary"), 32),
        name="route",
    )(logits)


def _row_gather_start(idx_at, src_hbm, dst, sem, rows):
    for r in range(rows):
        pltpu.make_async_copy(src_hbm.at[pl.ds(idx_at(r), 1)], dst.at[pl.ds(r, 1)], sem).start(priority=r % 2)


def _row_gather_wait(src_hbm, dst, sem, rows):
    pltpu.make_async_copy(src_hbm.at[pl.ds(0, rows)], dst, sem).wait()


def _dispatch_kernel(pos_ref, zrow_ref, h_ref, xs_hbm, zbuf, sems):
    i = pl.program_id(0)
    tm = h_ref.shape[0]
    tile = zbuf.shape[0]

    def zero_copy(z):
        return pltpu.make_async_copy(zbuf, xs_hbm.at[pl.ds(pl.multiple_of(zrow_ref[z], tile), tile)], sems.at[0])

    @pl.when(i == 0)
    def _():
        zbuf[...] = jnp.zeros(zbuf.shape, zbuf.dtype)
        for z in range(zrow_ref.shape[0]):
            @pl.when(zrow_ref[z] >= 0)
            def _():
                zero_copy(z).start()
        for z in range(zrow_ref.shape[0]):
            @pl.when(zrow_ref[z] >= 0)
            def _():
                zero_copy(z).wait()

    for k in range(2):
        base = (2 * i + k) * tm
        for r in range(tm):
            pltpu.make_async_copy(h_ref.at[pl.ds(r, 1)], xs_hbm.at[pl.ds(pos_ref[base + r], 1)],
                                  sems.at[1]).start(priority=r % 2)
    for k in range(2):
        pltpu.make_async_copy(h_ref, xs_hbm.at[pl.ds(0, tm)], sems.at[1]).wait()


def _dispatch(pos, zrow, h2, n_rows, seq):
    t, d = h2.shape
    tm = min(1024, seq)
    pos_blocks = pos.reshape(t // tm, tm, 2).transpose(0, 2, 1).reshape(-1)
    grid_spec = pltpu.PrefetchScalarGridSpec(
        num_scalar_prefetch=2,
        grid=(t // tm,),
        in_specs=[pl.BlockSpec((tm, d), lambda i, p, z: (i, 0))],
        out_specs=pl.BlockSpec(memory_space=pl.ANY),
        scratch_shapes=[
            pltpu.VMEM((MOE_TILE, d), F32),
            pltpu.SemaphoreType.DMA((2,)),
        ],
    )
    return pl.pallas_call(
        _dispatch_kernel,
        grid_spec=grid_spec,
        out_shape=jax.ShapeDtypeStruct((n_rows, d), F32),
        compiler_params=_cparams(("arbitrary",), 40),
        name="moe_dispatch",
    )(pos_blocks, zrow, h2)


def _moe_kernel(plan_ref, x_ref, wg_hbm, wu_hbm, wd_hbm, y_ref, stg_g, stg_u, stg_d, wsem):
    j = pl.program_id(0)
    nt = plan_ref[3, 0]
    nslot = stg_g.shape[0]

    def w_copies(e, s):
        return (pltpu.make_async_copy(wg_hbm.at[e], stg_g.at[s], wsem.at[s]),
                pltpu.make_async_copy(wu_hbm.at[e], stg_u.at[s], wsem.at[s]),
                pltpu.make_async_copy(wd_hbm.at[e], stg_d.at[s], wsem.at[s]))

    @pl.when(j == 0)
    def _():
        for s in range(nslot - 1):
            @pl.when(plan_ref[3, 1 + s] >= 0)
            def _():
                for cp in w_copies(plan_ref[3, 1 + s], s):
                    cp.start()

    @pl.when((j < nt) & (plan_ref[0, j] == 1))
    def _():
        s = plan_ref[1, j]
        for cp in w_copies(0, s):
            cp.wait()

        @pl.when(plan_ref[2, j] >= 0)
        def _():
            free = lax.rem(s + nslot - 1, nslot)
            for cp in w_copies(plan_ref[2, j], free):
                cp.start()

    @pl.when(j < nt)
    def _():
        s = plan_ref[1, j]
        x = x_ref[...].astype(BF16)
        hg = _dot(x, stg_g[s].astype(BF16))
        hu = _dot(x, stg_u[s].astype(BF16))
        act = (hg * _sigmoid(hg) * hu).astype(BF16)
        y_ref[...] = _dot(act, stg_d[s].astype(BF16))

    @pl.when(j >= nt)
    def _():
        y_ref[...] = jnp.zeros(y_ref.shape, y_ref.dtype)


def _moe(plan, xs, w_gate_e, w_up_e, w_down_e):
    d = xs.shape[1]
    f = w_gate_e.shape[-1]
    nt_max = plan.shape[1]
    tile = MOE_TILE
    any_spec = pl.BlockSpec(memory_space=pl.ANY)
    grid_spec = pltpu.PrefetchScalarGridSpec(
        num_scalar_prefetch=1,
        grid=(nt_max,),
        in_specs=[
            pl.BlockSpec((tile, d), lambda j, plan_: (jnp.minimum(j, plan_[3, 0] - 1), 0)),
            any_spec, any_spec, any_spec,
        ],
        out_specs=pl.BlockSpec((tile, d), lambda j, plan_: (j, 0)),
        scratch_shapes=[
            pltpu.VMEM((MOE_WEIGHT_SLOTS, d, f), F32),
            pltpu.VMEM((MOE_WEIGHT_SLOTS, d, f), F32),
            pltpu.VMEM((MOE_WEIGHT_SLOTS, f, d), F32),
            pltpu.SemaphoreType.DMA((MOE_WEIGHT_SLOTS,)),
        ],
    )
    return pl.pallas_call(
        _moe_kernel,
        grid_spec=grid_spec,
        out_shape=jax.ShapeDtypeStruct((nt_max * tile, d), F32),
        compiler_params=_cparams(("arbitrary",), 56),
        name="moe_experts",
    )(plan, xs, w_gate_e, w_up_e, w_down_e)


def _combine_kernel(pos_ref, y_hbm, x1_ref, gt_ref, wts_ref, o_ref, ybuf, sem):
    i = pl.program_id(0)
    n = pl.num_programs(0)
    tm = x1_ref.shape[0]
    slot = lax.rem(i, 2)

    def start(blk, s):
        for k in range(2):
            base = (2 * blk + k) * tm
            _row_gather_start(lambda r, base=base: pos_ref[base + r], y_hbm, ybuf.at[s, k], sem.at[s], tm)

    @pl.when(i == 0)
    def _():
        start(0, 0)

        @pl.when(n > 1)
        def _():
            start(1, 1)

    for k in range(2):
        _row_gather_wait(y_hbm, ybuf.at[slot, k], sem.at[slot], tm)
    w = wts_ref[...]
    moe = w[:, 0:1] * ybuf[slot, 0] + w[:, 1:2] * ybuf[slot, 1]
    o_ref[...] = x1_ref[...] + gt_ref[...] * moe

    @pl.when(i + 2 < n)
    def _():
        start(i + 2, slot)


def _combine(pos, ys, x1, mod4, wts, seq):
    t, d = x1.shape
    tm = min(512, seq)
    bpr = seq // tm
    pos_blocks = pos.reshape(t // tm, tm, 2).transpose(0, 2, 1).reshape(-1)
    grid_spec = pltpu.PrefetchScalarGridSpec(
        num_scalar_prefetch=1,
        grid=(t // tm,),
        in_specs=[
            pl.BlockSpec(memory_space=pl.ANY),
            pl.BlockSpec((tm, d), lambda i, p: (i, 0)),
            pl.BlockSpec((None, None, 1, d), lambda i, p: (i // bpr, 5, 0, 0)),
            pl.BlockSpec((tm, LANES), lambda i, p: (i, 0)),
        ],
        out_specs=pl.BlockSpec((tm, d), lambda i, p: (i, 0)),
        scratch_shapes=[
            pltpu.VMEM((2, 2, tm, d), F32),
            pltpu.SemaphoreType.DMA((2,)),
        ],
    )
    return pl.pallas_call(
        _combine_kernel,
        grid_spec=grid_spec,
        out_shape=jax.ShapeDtypeStruct((t, d), F32),
        compiler_params=_cparams(("arbitrary",), 48),
        name="moe_combine",
    )(pos_blocks, ys, x1, mod4, wts)


def _swap_halves(w):
    half = w.shape[-1] // 2
    return jnp.concatenate([w[..., half:], w[..., :half]], axis=-1)


def _layout_w_uq(w_uq):
    r = w_uq.shape[0]
    w = w_uq.reshape(r, MLA_HEADS, MLA_QK)
    rope = w[..., MLA_NOPE:]
    return jnp.concatenate([w[..., :MLA_NOPE], rope, _swap_halves(rope)], axis=-1).reshape(r, MLA_HEADS * QK_PAD).astype(BF16)


def _layout_w_ukv(w_ukv):
    r = w_ukv.shape[0]
    w = w_ukv.reshape(r, MLA_HEADS, MLA_NOPE + MLA_V)
    return jnp.concatenate([w[..., :MLA_NOPE].reshape(r, -1), w[..., MLA_NOPE:].reshape(r, -1)], axis=-1).astype(BF16)


def _rope_gain(g):
    g1 = g[MLA_NOPE:MLA_NOPE + MLA_ROPE // 2]
    g2 = g[MLA_NOPE + MLA_ROPE // 2:]
    return jnp.concatenate([g[:MLA_NOPE], g1, g2, -g2, g1]).reshape(1, QK_PAD)


def _layer(x2, cond_mod4, pos2, seq, w_in, q_a_norm_g, w_uq, kv_a_norm_g, w_ukv, q_norm_g, k_norm_g, conv_w, conv_b,
           b_mlstm_gates, mlstm_norm_g, w_proj_a, w_proj_b, w_out, norm_mix_g, norm_ffn_g, w_group, b_group,
           w_router, b_router, w_gate_e, w_up_e, w_down_e):
    t, d = x2.shape
    b = t // seq
    mod4 = cond_mod4

    proj, gates = _inproj(x2, mod4, norm_mix_g, w_in.T, seq)

    inv = ROPE_THETA ** (-jnp.arange(0, MLA_ROPE, 2, dtype=F32) / MLA_ROPE)
    inv_lanes = jnp.tile(inv, LANES // (MLA_ROPE // 2)).reshape(1, LANES)
    qt, k, vt = _mla_prep(proj, pos2, _layout_w_uq(w_uq), _layout_w_ukv(w_ukv), q_a_norm_g.reshape(1, -1),
                        kv_a_norm_g.reshape(1, -1), _rope_gain(q_norm_g), _rope_gain(k_norm_g), inv_lanes, seq)
    out_a = _flash(qt, k.reshape(b, seq, -1), vt).reshape(t, MLA_W)

    proj3 = proj.reshape(b, seq, -1)
    gbias = jnp.zeros((1, LANES), F32).at[0, GATE_LANE:GATE_LANE + 2 * ML_HEADS].set(b_mlstm_gates.reshape(-1))
    hm = _mlstm(proj3, conv_w, conv_b, gates.reshape(b, seq, LANES), gbias, mlstm_norm_g).reshape(t, ML_W)

    mixed = _merge(out_a, hm, proj, w_proj_a.astype(BF16), w_proj_b.astype(BF16), seq)

    route_pad = LANES - N_GROUPS - N_EXPERTS
    w_route = jnp.concatenate([w_group, w_router, jnp.zeros((d, route_pad), F32)], axis=1)
    b_route = jnp.concatenate([b_group, b_router, jnp.zeros((route_pad,), F32)]).reshape(1, LANES)
    w_route_hi = w_route.astype(BF16)
    w_route2 = jnp.concatenate([w_route_hi, (w_route - w_route_hi.astype(F32)).astype(BF16)], axis=1)
    x1, h2, logits = _outproj(mixed, x2, w_out.astype(BF16), mod4, norm_ffn_g, w_route2, b_route, seq)

    posm, wts, counts = _route(logits)
    pos = posm[:, 0:2]

    tile = MOE_TILE
    i32 = jnp.int32
    cnt = counts[0, N_GROUPS:N_GROUPS + N_EXPERTS].astype(i32)
    padded = ((cnt + tile - 1) // tile) * tile
    ends = jnp.cumsum(padded)
    offs = ends - padded
    nt_max = (2 * t) // tile + N_EXPERTS
    n_tiles = ends[-1] // tile
    tile_idx = jnp.arange(nt_max, dtype=i32)
    tile_start = tile_idx * tile
    live = tile_idx < n_tiles
    texp = jnp.minimum(jnp.sum((ends[None, :] <= tile_start[:, None]).astype(i32), axis=1), N_EXPERTS - 1)
    active = cnt > 0
    order = jnp.cumsum(active.astype(i32)) - 1
    n_active = jnp.sum(active.astype(i32))
    experts = jnp.arange(N_EXPERTS, dtype=i32)
    by_order = jnp.sum(jnp.where(active[None, :] & (order[None, :] == experts[:, None]), experts[None, :], 0), axis=1)
    t_order = order[texp]
    first = (live & (tile_start == offs[texp])).astype(i32)
    nslot = MOE_WEIGHT_SLOTS
    ahead = t_order + nslot - 1
    prefetch = jnp.where(ahead < n_active, by_order[jnp.minimum(ahead, N_EXPERTS - 1)], -1)
    lead = jnp.where(experts[:nslot - 1] < n_active, by_order[:nslot - 1], -1)
    head = jnp.zeros((nt_max,), i32).at[0].set(n_tiles).at[1:nslot].set(lead)
    plan = jnp.stack([first, t_order % nslot, prefetch, head]).astype(i32)

    last_tile = jnp.where(padded > cnt, ends - tile, -1)
    spare_idx = n_tiles + jnp.arange(N_EXPERTS, dtype=i32)
    spare_tile = jnp.where(spare_idx < nt_max, spare_idx * tile, -1)
    zrow = jnp.concatenate([last_tile, spare_tile]).astype(i32)

    xs = _dispatch(pos, zrow, h2, nt_max * tile, seq)
    ys = _moe(plan, xs, w_gate_e, w_up_e, w_down_e)

    return _combine(pos, ys, x1, mod4, wts, seq)


def kernel(x, c, positions, w_ada, b_ada, norm_mix_g, w_in, q_a_norm_g, w_uq, kv_a_norm_g, w_ukv, q_norm_g, k_norm_g, conv_w, conv_b, b_mlstm_gates, mlstm_norm_g, w_proj_a, w_proj_b, w_out, norm_ffn_g, w_group, b_group, w_router, b_router, w_gate_e, w_up_e, w_down_e):
    b, seq, d = x.shape
    depth = w_ada.shape[0]
    x2 = x.reshape(b * seq, d)
    pos2 = positions.reshape(b * seq, 1)
    c_pad = jnp.zeros((SUBLANES, d), F32).at[:b].set(c)
    for l in range(depth):
        mod = _adaln(c_pad, w_ada[l], b_ada[l])
        mod4 = mod[:b].reshape(b, 6, 1, d)
        x2 = _layer(x2, mod4, pos2, seq, w_in[l], q_a_norm_g[l], w_uq[l], kv_a_norm_g[l], w_ukv[l], q_norm_g[l],
                    k_norm_g[l], conv_w[l], conv_b[l], b_mlstm_gates[l], mlstm_norm_g[l], w_proj_a[l], w_proj_b[l],
                    w_out[l], norm_mix_g[l], norm_ffn_g[l], w_group[l], b_group[l], w_router[l], b_router[l],
                    w_gate_e[l], w_up_e[l], w_down_e[l])
    return x2.reshape(b, seq, d)
```

```python
import functools
import math

import jax
import jax.numpy as jnp
from jax import lax
from jax.experimental import pallas as pl
from jax.experimental.pallas import tpu as pltpu

F32 = jnp.float32
BF16 = jnp.bfloat16

LANES = 128
SUBLANES = 8

D_MODEL = 2048
MLA_HEADS = 8
MLA_NOPE = 128
MLA_ROPE = 64
MLA_QK = MLA_NOPE + MLA_ROPE
MLA_V = 128
Q_LORA = 512
KV_LORA = 256
ROPE_THETA = 10000.0
ML_HEADS = 8
ML_DQK = 128
ML_DV = 128
ML_CONV = 4
MLA_W = MLA_HEADS * MLA_V
ML_W = ML_HEADS * ML_DV
N_GROUPS = 4
EXP_PER_GROUP = 8
N_EXPERTS = N_GROUPS * EXP_PER_GROUP
EPS = 1e-6

QK_PAD = 2 * LANES

SRC_CQ = 0
SRC_KPE = Q_LORA + KV_LORA
SRC_QK = SRC_KPE + MLA_ROPE
SRC_V = SRC_QK + 2 * ML_HEADS * ML_DQK
SRC_O = SRC_V + ML_W
SRC_I = SRC_O + ML_W
SRC_GA = SRC_I + 2 * ML_HEADS
SRC_GB = SRC_GA + D_MODEL
GATE_LANE = SRC_I % LANES

IN_BLOCK = 1024
IN_BLOCK_SRC = (SRC_QK, SRC_QK + IN_BLOCK, SRC_GA, SRC_GA + IN_BLOCK, SRC_GB, SRC_GB + IN_BLOCK, SRC_V, SRC_O, SRC_CQ)
COL_QK = 0
COL_GA = 2048
COL_GB = 4096
COL_V = 6144
COL_O = 7168
COL_LAT = 8192
LAT_W = Q_LORA + KV_LORA + MLA_ROPE

MLSTM_CHUNK = 256
MOE_TILE = 256
MOE_WEIGHT_SLOTS = 3
ROUTE_RANK_ROWS = 256
FLASH_HEADS_PER_STEP = 2
FLASH_Q_TILE = 512
FLASH_K_TILE = 512
OUTPROJ_ROW_GROUP = 256
MERGE_ROW_GROUP = 256


def _cparams(sem, vmem_mb):
    return pltpu.CompilerParams(dimension_semantics=sem, vmem_limit_bytes=vmem_mb * 1024 * 1024)


def _dot(a, b):
    return jnp.dot(a, b, preferred_element_type=F32)


def _dot_nt(a, b):
    return lax.dot_general(a, b, (((1,), (1,)), ((), ())), preferred_element_type=F32)


def _sigmoid(x):
    return 1.0 / (1.0 + jnp.exp(-x))


def _rms_scale(x, width):
    return lax.rsqrt(jnp.sum(x * x, axis=-1, keepdims=True) * (1.0 / width) + EPS)


def _adaln_kernel(c_ref, w_ref, b_ref, o_ref):
    c = c_ref[...]
    cond = (c * _sigmoid(c)).astype(BF16)
    o_ref[...] = _dot(cond, w_ref[...].astype(BF16)) + b_ref[...]


def _adaln(c_pad, w_ada, b_ada):
    rows, d = c_pad.shape
    n = w_ada.shape[1]
    tn = 2048
    return pl.pallas_call(
        _adaln_kernel,
        grid=(n // tn,),
        in_specs=[
            pl.BlockSpec((rows, d), lambda j: (0, 0)),
            pl.BlockSpec((d, tn), lambda j: (0, j)),
            pl.BlockSpec((1, tn), lambda j: (0, j)),
        ],
        out_specs=pl.BlockSpec((rows, tn), lambda j: (0, j)),
        out_shape=jax.ShapeDtypeStruct((rows, n), F32),
        compiler_params=_cparams(("arbitrary",), 48),
        name="adaln",
    )(c_pad, w_ada, b_ada.reshape(1, n))


def _inproj_kernel(off_ref, x_ref, sc_ref, sh_ref, g_ref, w_ref, wgate_ref, proj_ref, gates_ref, h_ref):
    j = pl.program_id(1)

    @pl.when(j == 0)
    def _():
        x = x_ref[...]
        h = x * _rms_scale(x, x.shape[-1]) * g_ref[...]
        h = (h * (1.0 + sc_ref[...]) + sh_ref[...]).astype(BF16)
        h_ref[...] = h
        gates_ref[...] = _dot_nt(h, wgate_ref[...].astype(BF16))

    proj_ref[...] = _dot_nt(h_ref[...], w_ref[...].astype(BF16)).astype(proj_ref.dtype)


def _inproj(x2, mod4, norm_g, w_in_t, seq):
    t, d = x2.shape
    tm = min(1024, seq)
    tn = IN_BLOCK
    bpr = seq // tm
    assert all(o % SUBLANES == 0 for o in IN_BLOCK_SRC)
    offs = jnp.asarray([o // SUBLANES for o in IN_BLOCK_SRC], jnp.int32)
    nblk = len(IN_BLOCK_SRC)
    gate_tile = SRC_I // LANES
    grid_spec = pltpu.PrefetchScalarGridSpec(
        num_scalar_prefetch=1,
        grid=(t // tm, nblk),
        in_specs=[
            pl.BlockSpec((tm, d), lambda i, j, o: (i, 0)),
            pl.BlockSpec((None, None, 1, d), lambda i, j, o: (i // bpr, 1, 0, 0)),
            pl.BlockSpec((None, None, 1, d), lambda i, j, o: (i // bpr, 0, 0, 0)),
            pl.BlockSpec((1, d), lambda i, j, o: (0, 0)),
            pl.BlockSpec((pl.Element(tn), pl.Element(d)), lambda i, j, o: (o[j] * SUBLANES, 0)),
            pl.BlockSpec((LANES, d), lambda i, j, o: (gate_tile, 0)),
        ],
        out_specs=[
            pl.BlockSpec((tm, tn), lambda i, j, o: (i, j)),
            pl.BlockSpec((tm, LANES), lambda i, j, o: (i, 0)),
        ],
        scratch_shapes=[pltpu.VMEM((tm, d), BF16)],
    )
    return pl.pallas_call(
        _inproj_kernel,
        grid_spec=grid_spec,
        out_shape=[
            jax.ShapeDtypeStruct((t, nblk * tn), BF16),
            jax.ShapeDtypeStruct((t, LANES), F32),
        ],
        compiler_params=_cparams(("arbitrary", "arbitrary"), 56),
        name="inproj",
    )(offs, x2, mod4, mod4, norm_g.reshape(1, d), w_in_t, w_in_t)


def _mla_prep_kernel(lat_ref, pos_ref, wuq_ref, wukv_ref, gqa_ref, gkva_ref, gq_ref, gk_ref, inv_ref,
                     qt_ref, k_ref, vt_ref):
    lat = lat_ref[...].astype(F32)
    cq = lat[:, :Q_LORA]
    ckv = lat[:, Q_LORA:Q_LORA + KV_LORA]
    kc = lat[:, SRC_KPE:SRC_KPE + LANES]
    cqn = (cq * _rms_scale(cq, Q_LORA) * gqa_ref[...]).astype(BF16)
    ckvn = (ckv * _rms_scale(ckv, KV_LORA) * gkva_ref[...]).astype(BF16)
    qraw = _dot(cqn, wuq_ref[...])
    kv = _dot(ckvn, wukv_ref[...])

    ang = pos_ref[...].astype(F32) * inv_ref[...]
    lane = lax.broadcasted_iota(jnp.int32, ang.shape, 1)
    lo = lane < MLA_ROPE
    cs = jnp.cos(ang - jnp.where(lo, 0.0, 0.5 * math.pi))

    quarter = MLA_ROPE // 2
    want = jnp.where(lane < 3 * quarter, lane - quarter, lane - 3 * quarter)
    came = pltpu.roll(lane, quarter, 1)
    swapped = jnp.where(came == want, pltpu.roll(kc, quarter, 1), pltpu.roll(kc, 3 * quarter, 1))
    kc = jnp.where(lo, kc, swapped)

    gq = gq_ref[...]
    gk = gk_ref[...]
    gq_n, gq_r = gq[:, :LANES], gq[:, LANES:]
    gk_n, gk_r = gk[:, :LANES], gk[:, LANES:]

    def rope(chunk, g_cs):
        a = chunk * g_cs
        return jnp.where(lo, a + pltpu.roll(a, MLA_ROPE, 1), 0.0)

    gq_cs = gq_r * cs
    kpe_ss = jnp.sum(jnp.where(lo, kc * kc, 0.0), axis=-1, keepdims=True)
    k_rope = rope(kc, gk_r * cs)
    scale = MLA_QK ** -0.5 * math.log2(math.e)
    for h in range(MLA_HEADS):
        kn = kv[:, h * MLA_NOPE:(h + 1) * MLA_NOPE]
        sk = lax.rsqrt((jnp.sum(kn * kn, axis=-1, keepdims=True) + kpe_ss) * (1.0 / MLA_QK) + EPS)
        k_ref[:, h * QK_PAD:h * QK_PAD + LANES] = (kn * sk * gk_n).astype(BF16)
        k_ref[:, h * QK_PAD + LANES:(h + 1) * QK_PAD] = (k_rope * sk).astype(BF16)
        qn = qraw[:, h * QK_PAD:h * QK_PAD + LANES]
        qr = qraw[:, h * QK_PAD + LANES:(h + 1) * QK_PAD]
        ss = jnp.sum(qn * qn, axis=-1, keepdims=True) + jnp.sum(jnp.where(lo, qr * qr, 0.0), axis=-1, keepdims=True)
        sq = lax.rsqrt(ss * (1.0 / MLA_QK) + EPS) * scale
        qt_ref[h * QK_PAD:h * QK_PAD + LANES, :] = (qn * sq * gq_n).astype(BF16).T
        qt_ref[h * QK_PAD + LANES:(h + 1) * QK_PAD, :] = (rope(qr, gq_cs) * sq).astype(BF16).T
        vh = kv[:, MLA_HEADS * MLA_NOPE + h * MLA_V:MLA_HEADS * MLA_NOPE + (h + 1) * MLA_V]
        vt_ref[h * MLA_V:(h + 1) * MLA_V, :] = vh.astype(BF16).T


def _mla_prep(proj, pos2, wuq_p, wukv_p, gqa, gkva, gq, gk, inv_lanes, seq):
    t = proj.shape[0]
    tm = min(512, seq)
    hq = MLA_HEADS * QK_PAD
    lat_blk = COL_LAT // IN_BLOCK
    const = lambda i: (0, 0)
    return pl.pallas_call(
        _mla_prep_kernel,
        grid=(t // tm,),
        in_specs=[
            pl.BlockSpec((tm, IN_BLOCK), lambda i: (i, lat_blk)),
            pl.BlockSpec((tm, 1), lambda i: (i, 0)),
            pl.BlockSpec(wuq_p.shape, const),
            pl.BlockSpec(wukv_p.shape, const),
            pl.BlockSpec(gqa.shape, const),
            pl.BlockSpec(gkva.shape, const),
            pl.BlockSpec(gq.shape, const),
            pl.BlockSpec(gk.shape, const),
            pl.BlockSpec(inv_lanes.shape, const),
        ],
        out_specs=[
            pl.BlockSpec((hq, tm), lambda i: (0, i)),
            pl.BlockSpec((tm, hq), lambda i: (i, 0)),
            pl.BlockSpec((MLA_W, tm), lambda i: (0, i)),
        ],
        out_shape=[
            jax.ShapeDtypeStruct((hq, t), BF16),
            jax.ShapeDtypeStruct((t, hq), BF16),
            jax.ShapeDtypeStruct((MLA_W, t), BF16),
        ],
        compiler_params=_cparams(("arbitrary",), 48),
        name="mla_prep",
    )(proj, pos2, wuq_p, wukv_p, gqa, gkva, gq, gk, inv_lanes)


def _flash_kernel(qt_ref, k_ref, vt_ref, o_ref, *, tq, tk):
    seq = k_ref.shape[0]
    heads = k_ref.shape[1] // QK_PAD

    def scores(h, k0, q0):
        kj = k_ref[k0:k0 + tk, h * QK_PAD:(h + 1) * QK_PAD]
        return _dot(kj, qt_ref[h * QK_PAD:(h + 1) * QK_PAD, q0:q0 + tq])

    def update(h, state, st, k0, q0):
        m, l, acc = state
        if k0 + tk - 1 > q0:
            key = lax.broadcasted_iota(jnp.int32, st.shape, 0) + k0
            qry = lax.broadcasted_iota(jnp.int32, st.shape, 1) + q0
            st = jnp.where(key <= qry, st, -jnp.inf)
        m_new = jnp.maximum(m, jnp.max(st, axis=0, keepdims=True))
        alpha = jnp.exp2(m - m_new)
        p = jnp.exp2(st - m_new)
        l = alpha * l + jnp.sum(p, axis=0, keepdims=True)
        acc = alpha * acc + _dot(vt_ref[h * MLA_V:(h + 1) * MLA_V, k0:k0 + tk], p.astype(BF16))
        return m_new, l, acc

    steps = [(h, qi * tq, j * tk) for qi in range(seq // tq) for j in range((qi + 1) * tq // tk) for h in range(heads)]
    st_next = scores(steps[0][0], steps[0][2], steps[0][1])
    states = {}
    for n, (h, q0, k0) in enumerate(steps):
        st = st_next
        if n + 1 < len(steps):
            hn, qn, kn = steps[n + 1]
            st_next = scores(hn, kn, qn)
        if k0 == 0:
            states[h] = (jnp.full((1, tq), -jnp.inf, F32), jnp.zeros((1, tq), F32), jnp.zeros((MLA_V, tq), F32))
        states[h] = update(h, states[h], st, k0, q0)
        if k0 + tk >= q0 + tq:
            _, l, acc = states[h]
            o_ref[q0:q0 + tq, h * MLA_V:(h + 1) * MLA_V] = (acc / l).T.astype(o_ref.dtype)


def _flash(qt, k3, vt):
    b, seq, _ = k3.shape
    tq = min(FLASH_Q_TILE, seq)
    tk = min(FLASH_K_TILE, seq)
    hp = FLASH_HEADS_PER_STEP
    kern = functools.partial(_flash_kernel, tq=tq, tk=tk)
    return pl.pallas_call(
        kern,
        grid=(b, MLA_HEADS // hp),
        in_specs=[
            pl.BlockSpec((hp * QK_PAD, seq), lambda i, h: (h, i)),
            pl.BlockSpec((None, seq, hp * QK_PAD), lambda i, h: (i, 0, h)),
            pl.BlockSpec((hp * MLA_V, seq), lambda i, h: (h, i)),
        ],
        out_specs=pl.BlockSpec((None, seq, hp * MLA_V), lambda i, h: (i, 0, h)),
        out_shape=jax.ShapeDtypeStruct((b, seq, MLA_W), BF16),
        compiler_params=_cparams(("arbitrary", "arbitrary"), 40),
        name="flash",
    )(qt, k3, vt)


def _log_sigmoid(x):
    return -(jnp.maximum(-x, 0.0) + jnp.log1p(jnp.exp(-jnp.abs(x))))


def _conv_silu_qk(cur_ref, halo_ref, w_ref, b_ref, buf_ref, qt_ref, k_ref, first):
    L = cur_ref.shape[0]
    cols = cur_ref.shape[1]
    half = cols // 2
    buf_ref[0:SUBLANES, :] = jnp.where(first, 0.0, halo_ref[...].astype(F32))
    buf_ref[SUBLANES:SUBLANES + L, :] = cur_ref[...].astype(F32)
    cw = 512
    for c in range(cols // cw):
        sl = slice(c * cw, (c + 1) * cw)
        acc = jnp.zeros((L, cw), F32) + b_ref[:, sl]
        for j in range(ML_CONV):
            off = SUBLANES - (ML_CONV - 1) + j
            acc = acc + buf_ref[off:off + L, sl] * w_ref[j:j + 1, sl]
        y = acc * _sigmoid(acc)
        if c * cw < half:
            qt_ref[sl, :] = y.astype(qt_ref.dtype).T
        else:
            k_ref[:, c * cw - half:(c + 1) * cw - half] = (y * (ML_DQK ** -0.5)).astype(k_ref.dtype)


def _mlstm_kernel(cur_ref, halo_ref, cw_ref, cb_ref, v_ref, o_ref, gates_ref, gbias_ref, ng_ref, out_ref,
                  ct_ref, m_ref, buf_ref, qt_ref, k_ref):
    L = cur_ref.shape[0]
    first = pl.program_id(1) == 0

    @pl.when(first)
    def _():
        ct_ref[...] = jnp.zeros(ct_ref.shape, F32)
        m_ref[...] = jnp.zeros(m_ref.shape, F32)

    _conv_silu_qk(cur_ref, halo_ref, cw_ref, cb_ref, buf_ref, qt_ref, k_ref, first)

    g = gates_ref[...] + gbias_ref[...]
    gt = g.T
    lf = _log_sigmoid(g)
    lft = _log_sigmoid(gt)
    r = lax.broadcasted_iota(jnp.int32, (L, L), 0)
    c = lax.broadcasted_iota(jnp.int32, (L, L), 1)
    src_le_qry = r <= c
    tril = (c <= r).astype(F32)
    triu = src_le_qry.astype(F32)
    hi = lax.Precision.HIGHEST
    bcol_all = jnp.dot(tril, lf, preferred_element_type=F32, precision=hi)
    brow_all = jnp.dot(lft, triu, preferred_element_type=F32, precision=hi)
    row = lax.broadcasted_iota(jnp.int32, (ML_DV, L), 0)
    ones_row = jnp.where(row == 0, 1.0, 0.0)

    def lead_matmuls(h):
        hs = slice(h * ML_DQK, (h + 1) * ML_DQK)
        qt = qt_ref[hs, :]
        return _dot(k_ref[:, hs], qt), _dot(ct_ref[h].astype(BF16), qt)

    lead_next = lead_matmuls(0)
    for h in range(ML_HEADS):
        hs = slice(h * ML_DQK, (h + 1) * ML_DQK)
        qk_t, cq_t = lead_next
        if h + 1 < ML_HEADS:
            lead_next = lead_matmuls(h + 1)
        li, lf_ = GATE_LANE + h, GATE_LANE + ML_HEADS + h
        b_row = brow_all[lf_:lf_ + 1, :]
        i_row = gt[li:li + 1, :]
        u_col = g[:, li:li + 1] - bcol_all[:, lf_:lf_ + 1]
        m_prev = m_ref[h][:, :1]
        logw_t = jnp.where(src_le_qry, b_row + u_col, -jnp.inf)
        log_inter = b_row + m_prev
        m_t = jnp.maximum(jnp.max(logw_t, axis=0, keepdims=True), log_inter)
        w_t = jnp.exp(logw_t - m_t)
        a = jnp.exp(log_inter - m_t)
        kh = k_ref[:, hs]
        vt_aug = jnp.concatenate([v_ref[:, hs].astype(F32).T, ones_row], axis=0)
        s_t = qk_t * w_t
        ct = ct_ref[h]
        nd = _dot(vt_aug.astype(BF16), s_t.astype(BF16)) + a * cq_t
        num = nd[:ML_DV, :]
        den = nd[ML_DV:ML_DV + 1, :]
        hout_t = num * (1.0 / jnp.maximum(jnp.abs(den), jnp.exp(-m_t)))
        hn_t = hout_t * lax.rsqrt(jnp.sum(hout_t * hout_t, axis=0, keepdims=True) * (1.0 / ML_DV) + EPS)
        gate = _sigmoid(o_ref[:, hs].astype(F32))
        out_ref[:, hs] = (hn_t.T * ng_ref[:, hs] * gate).astype(out_ref.dtype)

        b_last = b_row[:, L - 1:L]
        logg = b_last - b_row + i_row
        m_new = jnp.maximum(b_last + m_prev, jnp.max(logg, axis=-1, keepdims=True))
        g_row = jnp.exp(logg - m_new)
        decay = jnp.exp(b_last + m_prev - m_new)
        ct_ref[h] = decay * ct + _dot((vt_aug * g_row).astype(BF16), kh)
        m_ref[h] = jnp.broadcast_to(m_new, m_ref.shape[1:])


def _mlstm(proj3, conv_w, conv_b, gates3, gbias, norm_g):
    b, seq, _ = proj3.shape
    w = ML_W
    qk_cols = 2 * ML_HEADS * ML_DQK
    L = min(MLSTM_CHUNK, seq)
    hb = L // SUBLANES
    return pl.pallas_call(
        _mlstm_kernel,
        grid=(b, seq // L),
        in_specs=[
            pl.BlockSpec((None, L, qk_cols), lambda i, c: (i, c, COL_QK // qk_cols)),
            pl.BlockSpec((None, SUBLANES, qk_cols), lambda i, c: (i, jnp.maximum(c * hb - 1, 0), COL_QK // qk_cols)),
            pl.BlockSpec((ML_CONV, qk_cols), lambda i, c: (0, 0)),
            pl.BlockSpec((1, qk_cols), lambda i, c: (0, 0)),
            pl.BlockSpec((None, L, w), lambda i, c: (i, c, COL_V // w)),
            pl.BlockSpec((None, L, w), lambda i, c: (i, c, COL_O // w)),
            pl.BlockSpec((None, L, LANES), lambda i, c: (i, c, 0)),
            pl.BlockSpec((1, LANES), lambda i, c: (0, 0)),
            pl.BlockSpec((1, w), lambda i, c: (0, 0)),
        ],
        out_specs=pl.BlockSpec((None, L, w), lambda i, c: (i, c, 0)),
        out_shape=jax.ShapeDtypeStruct((b, seq, w), BF16),
        scratch_shapes=[
            pltpu.VMEM((ML_HEADS, 2 * ML_DV, ML_DQK), F32),
            pltpu.VMEM((ML_HEADS, 1, LANES), F32),
            pltpu.VMEM((L + SUBLANES, qk_cols), F32),
            pltpu.VMEM((ML_HEADS * ML_DQK, L), BF16),
            pltpu.VMEM((L, ML_HEADS * ML_DQK), BF16),
        ],
        compiler_params=_cparams(("arbitrary", "arbitrary"), 40),
        name="mlstm",
    )(proj3, proj3, conv_w, conv_b.reshape(1, qk_cols), proj3, proj3, gates3, gbias, norm_g.reshape(1, w))


def _merge_kernel(a_ref, b_ref, ga_ref, gb_ref, wa_ref, wb_ref, o_ref):
    tm = a_ref.shape[0]
    rows = min(MERGE_ROW_GROUP, tm)
    groups = [slice(r0, r0 + rows) for r0 in range(0, tm, rows)]

    def proj(rs):
        return _dot(a_ref[rs, :], wa_ref[...]), _dot(b_ref[rs, :], wb_ref[...])

    nxt = proj(groups[0])
    for n, rs in enumerate(groups):
        pa, pb = nxt
        if n + 1 < len(groups):
            nxt = proj(groups[n + 1])
        mixed = _sigmoid(ga_ref[rs, :].astype(F32)) * pa + _sigmoid(gb_ref[rs, :].astype(F32)) * pb
        o_ref[rs, :] = mixed.astype(o_ref.dtype)


def _merge(out_a, hm, proj, wa, wb, seq):
    t = out_a.shape[0]
    d = wa.shape[1]
    tm = min(512, seq)
    tn = 2048
    return pl.pallas_call(
        _merge_kernel,
        grid=(d // tn, t // tm),
        in_specs=[
            pl.BlockSpec((tm, MLA_W), lambda j, i: (i, 0)),
            pl.BlockSpec((tm, ML_W), lambda j, i: (i, 0)),
            pl.BlockSpec((tm, tn), lambda j, i: (i, COL_GA // tn + j)),
            pl.BlockSpec((tm, tn), lambda j, i: (i, COL_GB // tn + j)),
            pl.BlockSpec((MLA_W, tn), lambda j, i: (0, j)),
            pl.BlockSpec((ML_W, tn), lambda j, i: (0, j)),
        ],
        out_specs=pl.BlockSpec((tm, tn), lambda j, i: (i, j)),
        out_shape=jax.ShapeDtypeStruct((t, d), BF16),
        compiler_params=_cparams(("arbitrary", "arbitrary"), 40),
        name="merge",
    )(out_a, hm, proj, proj, wa, wb)


def _outproj_kernel(mix_ref, x_ref, w_ref, gt_ref, sc_ref, sh_ref, g_ref, wr_ref, br_ref, x1_ref, h2_ref, lg_ref):
    tm = mix_ref.shape[0]
    rows = OUTPROJ_ROW_GROUP
    groups = [slice(r0, r0 + rows) for r0 in range(0, tm, rows)]
    y_next = _dot(mix_ref[groups[0], :], w_ref[...])
    for n, rs in enumerate(groups):
        y = y_next
        if n + 1 < len(groups):
            y_next = _dot(mix_ref[groups[n + 1], :], w_ref[...])
        x1 = x_ref[rs, :] + gt_ref[...] * y
        x1_ref[rs, :] = x1
        h2 = x1 * _rms_scale(x1, x1.shape[-1]) * g_ref[...]
        h2 = h2 * (1.0 + sc_ref[...]) + sh_ref[...]
        h2_ref[rs, :] = h2
        h_hi = h2.astype(BF16)
        h_lo = (h2 - h_hi.astype(F32)).astype(BF16)
        r = _dot(h_hi, wr_ref[...]) + _dot(h_lo, wr_ref[...])
        lg_ref[rs, :] = r[:, :LANES] + r[:, LANES:] + br_ref[...]


def _outproj(mixed, x2, w_out, mod4, norm_g, w_route2, b_route, seq):
    t, d = x2.shape
    tm = min(512, seq)
    bpr = seq // tm
    mod_spec = lambda k: pl.BlockSpec((None, None, 1, d), lambda i: (i // bpr, k, 0, 0))
    const = lambda i: (0, 0)
    return pl.pallas_call(
        _outproj_kernel,
        grid=(t // tm,),
        in_specs=[
            pl.BlockSpec((tm, d), lambda i: (i, 0)),
            pl.BlockSpec((tm, d), lambda i: (i, 0)),
            pl.BlockSpec((d, d), const, pipeline_mode=pl.Buffered(1)),
            mod_spec(2),
            mod_spec(4),
            mod_spec(3),
            pl.BlockSpec((1, d), const),
            pl.BlockSpec((d, 2 * LANES), const, pipeline_mode=pl.Buffered(1)),
            pl.BlockSpec((1, LANES), const),
        ],
        out_specs=[
            pl.BlockSpec((tm, d), lambda i: (i, 0)),
            pl.BlockSpec((tm, d), lambda i: (i, 0)),
            pl.BlockSpec((tm, LANES), lambda i: (i, 0)),
        ],
        out_shape=[
            jax.ShapeDtypeStruct((t, d), F32),
            jax.ShapeDtypeStruct((t, d), F32),
            jax.ShapeDtypeStruct((t, LANES), F32),
        ],
        compiler_params=_cparams(("arbitrary",), 56),
        name="outproj",
    )(mixed, x2, w_out, mod4, mod4, mod4, norm_g.reshape(1, d), w_route2, b_route)


def _route_kernel(lg_ref, pos_ref, wts_ref, cnt_ref, carry_ref, offs_ref, meta_s, wts_s):
    phase = pl.program_id(0)
    i = pl.program_id(1)
    tm = lg_ref.shape[0]
    lane = lax.broadcasted_iota(jnp.int32, (tm, LANES), 1)

    @pl.when((phase == 0) & (i == 0))
    def _():
        carry_ref[...] = jnp.zeros(carry_ref.shape, F32)

    @pl.when(phase == 0)
    def _():
        lg = lg_ref[...]
        big = jnp.int32(LANES)
        ninf = -jnp.inf

        def first_argmax(vals):
            mx = jnp.max(vals, axis=-1, keepdims=True)
            idx = jnp.min(jnp.where(vals == mx, lane, big), axis=-1, keepdims=True)
            return mx, idx

        gl = jnp.where(lane < N_GROUPS, lg, ninf)
        gmax, gsel = first_argmax(gl)
        g_w = 1.0 / jnp.sum(jnp.exp(gl - gmax), axis=-1, keepdims=True)
        lo = N_GROUPS + gsel * EXP_PER_GROUP
        in_grp = (lane >= lo) & (lane < lo + EXP_PER_GROUP)
        el = jnp.where(in_grp, lg, ninf)
        e1, i1 = first_argmax(el)
        e2, i2 = first_argmax(jnp.where(lane == i1, ninf, el))
        p2 = jnp.exp(e2 - e1)
        w1 = g_w / (1.0 + p2)
        w2 = g_w * p2 / (1.0 + p2)

        oh1 = lane == i1
        oh2 = lane == i2
        oh = jnp.where(oh1 | oh2, 1.0, 0.0)
        sub = min(ROUTE_RANK_ROWS, tm)
        r = lax.broadcasted_iota(jnp.int32, (sub, sub), 0)
        c = lax.broadcasted_iota(jnp.int32, (sub, sub), 1)
        strict = jnp.where(c < r, 1.0, 0.0).astype(BF16)
        carry = carry_ref[...]
        parts = []
        for r0 in range(0, tm, sub):
            oh_sub = oh[r0:r0 + sub, :]
            parts.append(_dot(strict, oh_sub.astype(BF16)) + carry)
            carry = carry + jnp.sum(oh_sub, axis=0, keepdims=True)
        before = jnp.concatenate(parts, axis=0)
        rank1 = jnp.sum(jnp.where(oh1, before, 0.0), axis=-1, keepdims=True).astype(jnp.int32)
        rank2 = jnp.sum(jnp.where(oh2, before, 0.0), axis=-1, keepdims=True).astype(jnp.int32)
        carry_ref[...] = carry
        meta_s[i] = jnp.where(lane == 0, i1, jnp.where(lane == 1, i2, jnp.where(lane == 2, rank1, jnp.where(lane == 3, rank2, 0))))
        wts_s[i] = jnp.where(lane == 0, w1, jnp.where(lane == 1, w2, 0.0))

    @pl.when((phase == 1) & (i == 0))
    def _():
        cnt = carry_ref[...]
        cnt_ref[...] = jnp.broadcast_to(cnt, cnt_ref.shape)
        padded = jnp.ceil(cnt * (1.0 / MOE_TILE)) * MOE_TILE
        r = lax.broadcasted_iota(jnp.int32, (LANES, LANES), 0)
        c = lax.broadcasted_iota(jnp.int32, (LANES, LANES), 1)
        upper = jnp.where(r < c, 1.0, 0.0).astype(BF16)
        padded8 = jnp.broadcast_to(padded, (SUBLANES, LANES)).astype(BF16)
        offs_ref[...] = _dot(padded8, upper)[:1, :]

    @pl.when(phase == 1)
    def _():
        meta = meta_s[i]
        offs = offs_ref[...]
        off1 = jnp.sum(jnp.where(lane == meta[:, 0:1], offs, 0.0), axis=-1, keepdims=True).astype(jnp.int32)
        off2 = jnp.sum(jnp.where(lane == meta[:, 1:2], offs, 0.0), axis=-1, keepdims=True).astype(jnp.int32)
        pos1 = off1 + meta[:, 2:3]
        pos2 = off2 + meta[:, 3:4]
        pos_ref[...] = jnp.where(lane == 0, pos1, jnp.where(lane == 1, pos2, 0))
        wts_ref[...] = wts_s[i]


def _route(logits):
    t = logits.shape[0]
    tm = min(1024, t)
    nb = t // tm
    assert (2 * t) // MOE_TILE + N_EXPERTS <= 256, "tile-padded offsets must stay exact in bf16 (8 significant bits)"
    return pl.pallas_call(
        _route_kernel,
        grid=(2, nb),
        in_specs=[pl.BlockSpec((tm, LANES), lambda p, i: (i * (1 - p), 0))],
        out_specs=[
            pl.BlockSpec((tm, LANES), lambda p, i: (i * p, 0)),
            pl.BlockSpec((tm, LANES), lambda p, i: (i * p, 0)),
            pl.BlockSpec((SUBLANES, LANES), lambda p, i: (0, 0)),
        ],
        out_shape=[
            jax.ShapeDtypeStruct((t, LANES), jnp.int32),
            jax.ShapeDtypeStruct((t, LANES), F32),
            jax.ShapeDtypeStruct((SUBLANES, LANES), F32),
        ],
        scratch_shapes=[
            pltpu.VMEM((1, LANES), F32),
            pltpu.VMEM((1, LANES), F32),
            pltpu.VMEM((nb, tm, LANES), jnp.int32),
            pltpu.VMEM((nb, tm, LANES), F32),
        ],
        compiler_params=_cparams(("arbitrary", "arbitrary"), 32),
        name="route",
    )(logits)


def _row_gather_start(idx_at, src_hbm, dst, sem, rows):
    for r in range(rows):
        pltpu.make_async_copy(src_hbm.at[pl.ds(idx_at(r), 1)], dst.at[pl.ds(r, 1)], sem).start(priority=r % 2)


def _row_gather_wait(src_hbm, dst, sem, rows):
    pltpu.make_async_copy(src_hbm.at[pl.ds(0, rows)], dst, sem).wait()


def _zero_piece_sizes(tile):
    sizes = [tile]
    size = tile // 2
    while size >= SUBLANES:
        sizes.append(size)
        size //= 2
    return sizes + [1] * (SUBLANES - 1)


def _zero_pieces(tile):
    return [(n * N_EXPERTS + e, size) for n, size in enumerate(_zero_piece_sizes(tile)) for e in range(N_EXPERTS)]


def _dispatch_kernel(pos_ref, zrow_ref, h_ref, xs_hbm, zbuf, sems):
    i = pl.program_id(0)
    tm = h_ref.shape[0]
    pieces = _zero_pieces(zbuf.shape[0])

    def zero_copy(z, size):
        start = zrow_ref[z] if size == 1 else pl.multiple_of(zrow_ref[z], SUBLANES)
        return pltpu.make_async_copy(zbuf.at[pl.ds(0, size)], xs_hbm.at[pl.ds(start, size)], sems.at[0])

    @pl.when(i == 0)
    def _():
        zbuf[...] = jnp.zeros(zbuf.shape, zbuf.dtype)
        for z, size in pieces:
            @pl.when(zrow_ref[z] >= 0)
            def _():
                zero_copy(z, size).start()

    for k in range(2):
        base = (2 * i + k) * tm
        for r in range(tm):
            pltpu.make_async_copy(h_ref.at[pl.ds(r, 1)], xs_hbm.at[pl.ds(pos_ref[base + r], 1)],
                                  sems.at[1]).start(priority=r % 2)
    for k in range(2):
        pltpu.make_async_copy(h_ref, xs_hbm.at[pl.ds(0, tm)], sems.at[1]).wait()

    @pl.when(i == 0)
    def _():
        for z, size in pieces:
            @pl.when(zrow_ref[z] >= 0)
            def _():
                zero_copy(z, size).wait()


def _dispatch(pos, zrow, h2, n_rows, seq):
    t, d = h2.shape
    tm = min(1024, seq)
    pos_blocks = pos.reshape(t // tm, tm, 2).transpose(0, 2, 1).reshape(-1)
    grid_spec = pltpu.PrefetchScalarGridSpec(
        num_scalar_prefetch=2,
        grid=(t // tm,),
        in_specs=[pl.BlockSpec((tm, d), lambda i, p, z: (i, 0))],
        out_specs=pl.BlockSpec(memory_space=pl.ANY),
        scratch_shapes=[
            pltpu.VMEM((MOE_TILE, d), F32),
            pltpu.SemaphoreType.DMA((2,)),
        ],
    )
    return pl.pallas_call(
        _dispatch_kernel,
        grid_spec=grid_spec,
        out_shape=jax.ShapeDtypeStruct((n_rows, d), F32),
        compiler_params=_cparams(("arbitrary",), 40),
        name="moe_dispatch",
    )(pos_blocks, zrow, h2)


def _moe_kernel(plan_ref, x_ref, wg_hbm, wu_hbm, wd_hbm, y_ref, stg_g, stg_u, stg_d, wsem):
    j = pl.program_id(0)
    nt = plan_ref[3, 0]
    nslot = stg_g.shape[0]

    def w_copies(e, s):
        return (pltpu.make_async_copy(wg_hbm.at[e], stg_g.at[s], wsem.at[s]),
                pltpu.make_async_copy(wu_hbm.at[e], stg_u.at[s], wsem.at[s]),
                pltpu.make_async_copy(wd_hbm.at[e], stg_d.at[s], wsem.at[s]))

    @pl.when(j == 0)
    def _():
        for s in range(nslot - 1):
            @pl.when(plan_ref[3, 1 + s] >= 0)
            def _():
                for cp in w_copies(plan_ref[3, 1 + s], s):
                    cp.start()

    @pl.when((j < nt) & (plan_ref[0, j] == 1))
    def _():
        s = plan_ref[1, j]
        for cp in w_copies(0, s):
            cp.wait()

        @pl.when(plan_ref[2, j] >= 0)
        def _():
            free = lax.rem(s + nslot - 1, nslot)
            for cp in w_copies(plan_ref[2, j], free):
                cp.start()

    @pl.when(j < nt)
    def _():
        s = plan_ref[1, j]
        x = x_ref[...].astype(BF16)
        hg = _dot(x, stg_g[s].astype(BF16))
        hu = _dot(x, stg_u[s].astype(BF16))
        act = (hg * _sigmoid(hg) * hu).astype(BF16)
        y_ref[...] = _dot(act, stg_d[s].astype(BF16))

    @pl.when(j >= nt)
    def _():
        y_ref[...] = jnp.zeros(y_ref.shape, y_ref.dtype)


def _moe(plan, xs, w_gate_e, w_up_e, w_down_e):
    d = xs.shape[1]
    f = w_gate_e.shape[-1]
    nt_max = plan.shape[1]
    tile = MOE_TILE
    any_spec = pl.BlockSpec(memory_space=pl.ANY)
    grid_spec = pltpu.PrefetchScalarGridSpec(
        num_scalar_prefetch=1,
        grid=(nt_max,),
        in_specs=[
            pl.BlockSpec((tile, d), lambda j, plan_: (jnp.minimum(j, plan_[3, 0] - 1), 0)),
            any_spec, any_spec, any_spec,
        ],
        out_specs=pl.BlockSpec((tile, d), lambda j, plan_: (j, 0)),
        scratch_shapes=[
            pltpu.VMEM((MOE_WEIGHT_SLOTS, d, f), F32),
            pltpu.VMEM((MOE_WEIGHT_SLOTS, d, f), F32),
            pltpu.VMEM((MOE_WEIGHT_SLOTS, f, d), F32),
            pltpu.SemaphoreType.DMA((MOE_WEIGHT_SLOTS,)),
        ],
    )
    return pl.pallas_call(
        _moe_kernel,
        grid_spec=grid_spec,
        out_shape=jax.ShapeDtypeStruct((nt_max * tile, d), F32),
        compiler_params=_cparams(("arbitrary",), 56),
        name="moe_experts",
    )(plan, xs, w_gate_e, w_up_e, w_down_e)


def _combine_kernel(pos_ref, y_hbm, x1_ref, gt_ref, wts_ref, o_ref, ybuf, sem):
    i = pl.program_id(0)
    n = pl.num_programs(0)
    tm = x1_ref.shape[0]
    slot = lax.rem(i, 2)

    def start(blk, s):
        for k in range(2):
            base = (2 * blk + k) * tm
            _row_gather_start(lambda r, base=base: pos_ref[base + r], y_hbm, ybuf.at[s, k], sem.at[s], tm)

    @pl.when(i == 0)
    def _():
        start(0, 0)

        @pl.when(n > 1)
        def _():
            start(1, 1)

    for k in range(2):
        _row_gather_wait(y_hbm, ybuf.at[slot, k], sem.at[slot], tm)
    w = wts_ref[...]
    moe = w[:, 0:1] * ybuf[slot, 0] + w[:, 1:2] * ybuf[slot, 1]
    o_ref[...] = x1_ref[...] + gt_ref[...] * moe

    @pl.when(i + 2 < n)
    def _():
        start(i + 2, slot)


def _combine(pos, ys, x1, mod4, wts, seq):
    t, d = x1.shape
    tm = min(512, seq)
    bpr = seq // tm
    pos_blocks = pos.reshape(t // tm, tm, 2).transpose(0, 2, 1).reshape(-1)
    grid_spec = pltpu.PrefetchScalarGridSpec(
        num_scalar_prefetch=1,
        grid=(t // tm,),
        in_specs=[
            pl.BlockSpec(memory_space=pl.ANY),
            pl.BlockSpec((tm, d), lambda i, p: (i, 0)),
            pl.BlockSpec((None, None, 1, d), lambda i, p: (i // bpr, 5, 0, 0)),
            pl.BlockSpec((tm, LANES), lambda i, p: (i, 0)),
        ],
        out_specs=pl.BlockSpec((tm, d), lambda i, p: (i, 0)),
        scratch_shapes=[
            pltpu.VMEM((2, 2, tm, d), F32),
            pltpu.SemaphoreType.DMA((2,)),
        ],
    )
    return pl.pallas_call(
        _combine_kernel,
        grid_spec=grid_spec,
        out_shape=jax.ShapeDtypeStruct((t, d), F32),
        compiler_params=_cparams(("arbitrary",), 48),
        name="moe_combine",
    )(pos_blocks, ys, x1, mod4, wts)


def _swap_halves(w):
    half = w.shape[-1] // 2
    return jnp.concatenate([w[..., half:], w[..., :half]], axis=-1)


def _layout_w_uq(w_uq):
    r = w_uq.shape[0]
    w = w_uq.reshape(r, MLA_HEADS, MLA_QK)
    rope = w[..., MLA_NOPE:]
    return jnp.concatenate([w[..., :MLA_NOPE], rope, _swap_halves(rope)], axis=-1).reshape(r, MLA_HEADS * QK_PAD).astype(BF16)


def _layout_w_ukv(w_ukv):
    r = w_ukv.shape[0]
    w = w_ukv.reshape(r, MLA_HEADS, MLA_NOPE + MLA_V)
    return jnp.concatenate([w[..., :MLA_NOPE].reshape(r, -1), w[..., MLA_NOPE:].reshape(r, -1)], axis=-1).astype(BF16)


def _rope_gain(g):
    g1 = g[MLA_NOPE:MLA_NOPE + MLA_ROPE // 2]
    g2 = g[MLA_NOPE + MLA_ROPE // 2:]
    return jnp.concatenate([g[:MLA_NOPE], g1, g2, -g2, g1]).reshape(1, QK_PAD)


def _layer(x2, cond_mod4, pos2, seq, w_in, q_a_norm_g, w_uq, kv_a_norm_g, w_ukv, q_norm_g, k_norm_g, conv_w, conv_b,
           b_mlstm_gates, mlstm_norm_g, w_proj_a, w_proj_b, w_out, norm_mix_g, norm_ffn_g, w_group, b_group,
           w_router, b_router, w_gate_e, w_up_e, w_down_e):
    t, d = x2.shape
    b = t // seq
    mod4 = cond_mod4

    proj, gates = _inproj(x2, mod4, norm_mix_g, w_in.T, seq)

    inv = ROPE_THETA ** (-jnp.arange(0, MLA_ROPE, 2, dtype=F32) / MLA_ROPE)
    inv_lanes = jnp.tile(inv, LANES // (MLA_ROPE // 2)).reshape(1, LANES)
    qt, k, vt = _mla_prep(proj, pos2, _layout_w_uq(w_uq), _layout_w_ukv(w_ukv), q_a_norm_g.reshape(1, -1),
                        kv_a_norm_g.reshape(1, -1), _rope_gain(q_norm_g), _rope_gain(k_norm_g), inv_lanes, seq)
    out_a = _flash(qt, k.reshape(b, seq, -1), vt).reshape(t, MLA_W)

    proj3 = proj.reshape(b, seq, -1)
    gbias = jnp.zeros((1, LANES), F32).at[0, GATE_LANE:GATE_LANE + 2 * ML_HEADS].set(b_mlstm_gates.reshape(-1))
    hm = _mlstm(proj3, conv_w, conv_b, gates.reshape(b, seq, LANES), gbias, mlstm_norm_g).reshape(t, ML_W)

    mixed = _merge(out_a, hm, proj, w_proj_a.astype(BF16), w_proj_b.astype(BF16), seq)

    route_pad = LANES - N_GROUPS - N_EXPERTS
    w_route = jnp.concatenate([w_group, w_router, jnp.zeros((d, route_pad), F32)], axis=1)
    b_route = jnp.concatenate([b_group, b_router, jnp.zeros((route_pad,), F32)]).reshape(1, LANES)
    w_route_hi = w_route.astype(BF16)
    w_route2 = jnp.concatenate([w_route_hi, (w_route - w_route_hi.astype(F32)).astype(BF16)], axis=1)
    x1, h2, logits = _outproj(mixed, x2, w_out.astype(BF16), mod4, norm_ffn_g, w_route2, b_route, seq)

    posm, wts, counts = _route(logits)
    pos = posm[:, 0:2]

    tile = MOE_TILE
    i32 = jnp.int32
    cnt = counts[0, N_GROUPS:N_GROUPS + N_EXPERTS].astype(i32)
    padded = ((cnt + tile - 1) // tile) * tile
    ends = jnp.cumsum(padded)
    offs = ends - padded
    nt_max = (2 * t) // tile + N_EXPERTS
    n_tiles = ends[-1] // tile
    tile_idx = jnp.arange(nt_max, dtype=i32)
    tile_start = tile_idx * tile
    live = tile_idx < n_tiles
    texp = jnp.minimum(jnp.sum((ends[None, :] <= tile_start[:, None]).astype(i32), axis=1), N_EXPERTS - 1)
    active = cnt > 0
    order = jnp.cumsum(active.astype(i32)) - 1
    n_active = jnp.sum(active.astype(i32))
    experts = jnp.arange(N_EXPERTS, dtype=i32)
    by_order = jnp.sum(jnp.where(active[None, :] & (order[None, :] == experts[:, None]), experts[None, :], 0), axis=1)
    t_order = order[texp]
    first = (live & (tile_start == offs[texp])).astype(i32)
    nslot = MOE_WEIGHT_SLOTS
    ahead = t_order + nslot - 1
    prefetch = jnp.where(ahead < n_active, by_order[jnp.minimum(ahead, N_EXPERTS - 1)], -1)
    lead = jnp.where(experts[:nslot - 1] < n_active, by_order[:nslot - 1], -1)
    head = jnp.zeros((nt_max,), i32).at[0].set(n_tiles).at[1:nslot].set(lead)
    plan = jnp.stack([first, t_order % nslot, prefetch, head]).astype(i32)

    spare_idx = n_tiles + jnp.arange(N_EXPERTS, dtype=i32)
    zparts = [jnp.where(spare_idx < nt_max, spare_idx * tile, -1)]
    pad = padded - cnt
    for size in _zero_piece_sizes(tile)[1:]:
        if size > 1:
            above = pad & ~(2 * size - 1)
            zparts.append(jnp.where((pad & size) != 0, ends - above - size, -1))
    for u in range(SUBLANES - 1):
        zparts.append(jnp.where(u < (pad & (SUBLANES - 1)), offs + cnt + u, -1))
    zrow = jnp.concatenate(zparts).astype(i32)

    xs = _dispatch(pos, zrow, h2, nt_max * tile, seq)
    ys = _moe(plan, xs, w_gate_e, w_up_e, w_down_e)

    return _combine(pos, ys, x1, mod4, wts, seq)


def kernel(x, c, positions, w_ada, b_ada, norm_mix_g, w_in, q_a_norm_g, w_uq, kv_a_norm_g, w_ukv, q_norm_g, k_norm_g, conv_w, conv_b, b_mlstm_gates, mlstm_norm_g, w_proj_a, w_proj_b, w_out, norm_ffn_g, w_group, b_group, w_router, b_router, w_gate_e, w_up_e, w_down_e):
    b, seq, d = x.shape
    depth = w_ada.shape[0]
    x2 = x.reshape(b * seq, d)
    pos2 = positions.reshape(b * seq, 1)
    c_pad = jnp.zeros((SUBLANES, d), F32).at[:b].set(c)
    for l in range(depth):
        mod = _adaln(c_pad, w_ada[l], b_ada[l])
        mod4 = mod[:b].reshape(b, 6, 1, d)
        x2 = _layer(x2, mod4, pos2, seq, w_in[l], q_a_norm_g[l], w_uq[l], kv_a_norm_g[l], w_ukv[l], q_norm_g[l],
                    k_norm_g[l], conv_w[l], conv_b[l], b_mlstm_gates[l], mlstm_norm_g[l], w_proj_a[l], w_proj_b[l],
                    w_out[l], norm_mix_g[l], norm_ffn_g[l], w_group[l], b_group[l], w_router[l], b_router[l],
                    w_gate_e[l], w_up_e[l], w_down_e[l])
    return x2.reshape(b, seq, d)
```

```python
import functools
import math

import jax
import jax.numpy as jnp
from jax import lax
from jax.experimental import pallas as pl
from jax.experimental.pallas import tpu as pltpu

F32 = jnp.float32
BF16 = jnp.bfloat16

LANES = 128
SUBLANES = 8

D_MODEL = 2048
MLA_HEADS = 8
MLA_NOPE = 128
MLA_ROPE = 64
MLA_QK = MLA_NOPE + MLA_ROPE
MLA_V = 128
Q_LORA = 512
KV_LORA = 256
ROPE_THETA = 10000.0
ML_HEADS = 8
ML_DQK = 128
ML_DV = 128
ML_CONV = 4
MLA_W = MLA_HEADS * MLA_V
ML_W = ML_HEADS * ML_DV
N_GROUPS = 4
EXP_PER_GROUP = 8
N_EXPERTS = N_GROUPS * EXP_PER_GROUP
EPS = 1e-6

QK_PAD = 2 * LANES

SRC_CQ = 0
SRC_KPE = Q_LORA + KV_LORA
SRC_QK = SRC_KPE + MLA_ROPE
SRC_V = SRC_QK + 2 * ML_HEADS * ML_DQK
SRC_O = SRC_V + ML_W
SRC_I = SRC_O + ML_W
SRC_GA = SRC_I + 2 * ML_HEADS
SRC_GB = SRC_GA + D_MODEL
GATE_LANE = SRC_I % LANES

IN_BLOCK = 1024
IN_BLOCK_SRC = (SRC_QK, SRC_QK + IN_BLOCK, SRC_GA, SRC_GA + IN_BLOCK, SRC_GB, SRC_GB + IN_BLOCK, SRC_V, SRC_O, SRC_CQ)
COL_QK = 0
COL_GA = 2048
COL_GB = 4096
COL_V = 6144
COL_O = 7168
COL_LAT = 8192
LAT_W = Q_LORA + KV_LORA + MLA_ROPE

MLSTM_CHUNK = 256
MOE_TILE = 256
MOE_WEIGHT_SLOTS = 3
MOE_X_SLOTS = 3
ROUTE_RANK_ROWS = 256
FLASH_HEADS_PER_STEP = 2
FLASH_Q_TILE = 512
FLASH_K_TILE = 512
OUTPROJ_ROW_GROUP = 256
MERGE_ROW_GROUP = 256


def _cparams(sem, vmem_mb):
    return pltpu.CompilerParams(dimension_semantics=sem, vmem_limit_bytes=vmem_mb * 1024 * 1024)


def _dot(a, b):
    return jnp.dot(a, b, preferred_element_type=F32)


def _dot_nt(a, b):
    return lax.dot_general(a, b, (((1,), (1,)), ((), ())), preferred_element_type=F32)


def _sigmoid(x):
    return 1.0 / (1.0 + jnp.exp(-x))


def _rms_scale(x, width):
    return lax.rsqrt(jnp.sum(x * x, axis=-1, keepdims=True) * (1.0 / width) + EPS)


def _adaln_kernel(c_ref, w_ref, b_ref, o_ref):
    c = c_ref[...]
    cond = (c * _sigmoid(c)).astype(BF16)
    o_ref[...] = _dot(cond, w_ref[...].astype(BF16)) + b_ref[...]


def _adaln(c_pad, w_ada, b_ada):
    rows, d = c_pad.shape
    n = w_ada.shape[1]
    tn = 2048
    return pl.pallas_call(
        _adaln_kernel,
        grid=(n // tn,),
        in_specs=[
            pl.BlockSpec((rows, d), lambda j: (0, 0)),
            pl.BlockSpec((d, tn), lambda j: (0, j)),
            pl.BlockSpec((1, tn), lambda j: (0, j)),
        ],
        out_specs=pl.BlockSpec((rows, tn), lambda j: (0, j)),
        out_shape=jax.ShapeDtypeStruct((rows, n), F32),
        compiler_params=_cparams(("arbitrary",), 48),
        name="adaln",
    )(c_pad, w_ada, b_ada.reshape(1, n))


def _inproj_kernel(off_ref, x_ref, sc_ref, sh_ref, g_ref, w_ref, wgate_ref, proj_ref, gates_ref, h_ref):
    j = pl.program_id(1)

    @pl.when(j == 0)
    def _():
        x = x_ref[...]
        h = x * _rms_scale(x, x.shape[-1]) * g_ref[...]
        h = (h * (1.0 + sc_ref[...]) + sh_ref[...]).astype(BF16)
        h_ref[...] = h
        gates_ref[...] = _dot_nt(h, wgate_ref[...].astype(BF16))

    proj_ref[...] = _dot_nt(h_ref[...], w_ref[...].astype(BF16)).astype(proj_ref.dtype)


def _inproj(x2, mod4, norm_g, w_in_t, seq):
    t, d = x2.shape
    tm = min(1024, seq)
    tn = IN_BLOCK
    bpr = seq // tm
    assert all(o % SUBLANES == 0 for o in IN_BLOCK_SRC)
    offs = jnp.asarray([o // SUBLANES for o in IN_BLOCK_SRC], jnp.int32)
    nblk = len(IN_BLOCK_SRC)
    gate_tile = SRC_I // LANES
    grid_spec = pltpu.PrefetchScalarGridSpec(
        num_scalar_prefetch=1,
        grid=(t // tm, nblk),
        in_specs=[
            pl.BlockSpec((tm, d), lambda i, j, o: (i, 0)),
            pl.BlockSpec((None, None, 1, d), lambda i, j, o: (i // bpr, 1, 0, 0)),
            pl.BlockSpec((None, None, 1, d), lambda i, j, o: (i // bpr, 0, 0, 0)),
            pl.BlockSpec((1, d), lambda i, j, o: (0, 0)),
            pl.BlockSpec((pl.Element(tn), pl.Element(d)), lambda i, j, o: (o[j] * SUBLANES, 0)),
            pl.BlockSpec((LANES, d), lambda i, j, o: (gate_tile, 0)),
        ],
        out_specs=[
            pl.BlockSpec((tm, tn), lambda i, j, o: (i, j)),
            pl.BlockSpec((tm, LANES), lambda i, j, o: (i, 0)),
        ],
        scratch_shapes=[pltpu.VMEM((tm, d), BF16)],
    )
    return pl.pallas_call(
        _inproj_kernel,
        grid_spec=grid_spec,
        out_shape=[
            jax.ShapeDtypeStruct((t, nblk * tn), BF16),
            jax.ShapeDtypeStruct((t, LANES), F32),
        ],
        compiler_params=_cparams(("arbitrary", "arbitrary"), 56),
        name="inproj",
    )(offs, x2, mod4, mod4, norm_g.reshape(1, d), w_in_t, w_in_t)


def _mla_prep_kernel(lat_ref, pos_ref, wuq_ref, wukv_ref, gqa_ref, gkva_ref, gq_ref, gk_ref, inv_ref,
                     qt_ref, k_ref, vt_ref):
    lat = lat_ref[...].astype(F32)
    cq = lat[:, :Q_LORA]
    ckv = lat[:, Q_LORA:Q_LORA + KV_LORA]
    kc = lat[:, SRC_KPE:SRC_KPE + LANES]
    cqn = (cq * _rms_scale(cq, Q_LORA) * gqa_ref[...]).astype(BF16)
    ckvn = (ckv * _rms_scale(ckv, KV_LORA) * gkva_ref[...]).astype(BF16)
    qraw = _dot(cqn, wuq_ref[...])
    kv = _dot(ckvn, wukv_ref[...])

    ang = pos_ref[...].astype(F32) * inv_ref[...]
    lane = lax.broadcasted_iota(jnp.int32, ang.shape, 1)
    lo = lane < MLA_ROPE
    cs = jnp.cos(ang - jnp.where(lo, 0.0, 0.5 * math.pi))

    quarter = MLA_ROPE // 2
    want = jnp.where(lane < 3 * quarter, lane - quarter, lane - 3 * quarter)
    came = pltpu.roll(lane, quarter, 1)
    swapped = jnp.where(came == want, pltpu.roll(kc, quarter, 1), pltpu.roll(kc, 3 * quarter, 1))
    kc = jnp.where(lo, kc, swapped)

    gq = gq_ref[...]
    gk = gk_ref[...]
    gq_n, gq_r = gq[:, :LANES], gq[:, LANES:]
    gk_n, gk_r = gk[:, :LANES], gk[:, LANES:]

    def rope(chunk, g_cs):
        a = chunk * g_cs
        return jnp.where(lo, a + pltpu.roll(a, MLA_ROPE, 1), 0.0)

    gq_cs = gq_r * cs
    kpe_ss = jnp.sum(jnp.where(lo, kc * kc, 0.0), axis=-1, keepdims=True)
    k_rope = rope(kc, gk_r * cs)
    scale = MLA_QK ** -0.5 * math.log2(math.e)
    for h in range(MLA_HEADS):
        kn = kv[:, h * MLA_NOPE:(h + 1) * MLA_NOPE]
        sk = lax.rsqrt((jnp.sum(kn * kn, axis=-1, keepdims=True) + kpe_ss) * (1.0 / MLA_QK) + EPS)
        k_ref[:, h * QK_PAD:h * QK_PAD + LANES] = (kn * sk * gk_n).astype(BF16)
        k_ref[:, h * QK_PAD + LANES:(h + 1) * QK_PAD] = (k_rope * sk).astype(BF16)
        qn = qraw[:, h * QK_PAD:h * QK_PAD + LANES]
        qr = qraw[:, h * QK_PAD + LANES:(h + 1) * QK_PAD]
        ss = jnp.sum(qn * qn, axis=-1, keepdims=True) + jnp.sum(jnp.where(lo, qr * qr, 0.0), axis=-1, keepdims=True)
        sq = lax.rsqrt(ss * (1.0 / MLA_QK) + EPS) * scale
        qt_ref[h * QK_PAD:h * QK_PAD + LANES, :] = (qn * sq * gq_n).astype(BF16).T
        qt_ref[h * QK_PAD + LANES:(h + 1) * QK_PAD, :] = (rope(qr, gq_cs) * sq).astype(BF16).T
        vh = kv[:, MLA_HEADS * MLA_NOPE + h * MLA_V:MLA_HEADS * MLA_NOPE + (h + 1) * MLA_V]
        vt_ref[h * MLA_V:(h + 1) * MLA_V, :] = vh.astype(BF16).T


def _mla_prep(proj, pos2, wuq_p, wukv_p, gqa, gkva, gq, gk, inv_lanes, seq):
    t = proj.shape[0]
    tm = min(512, seq)
    hq = MLA_HEADS * QK_PAD
    lat_blk = COL_LAT // IN_BLOCK
    const = lambda i: (0, 0)
    return pl.pallas_call(
        _mla_prep_kernel,
        grid=(t // tm,),
        in_specs=[
            pl.BlockSpec((tm, IN_BLOCK), lambda i: (i, lat_blk)),
            pl.BlockSpec((tm, 1), lambda i: (i, 0)),
            pl.BlockSpec(wuq_p.shape, const),
            pl.BlockSpec(wukv_p.shape, const),
            pl.BlockSpec(gqa.shape, const),
            pl.BlockSpec(gkva.shape, const),
            pl.BlockSpec(gq.shape, const),
            pl.BlockSpec(gk.shape, const),
            pl.BlockSpec(inv_lanes.shape, const),
        ],
        out_specs=[
            pl.BlockSpec((hq, tm), lambda i: (0, i)),
            pl.BlockSpec((tm, hq), lambda i: (i, 0)),
            pl.BlockSpec((MLA_W, tm), lambda i: (0, i)),
        ],
        out_shape=[
            jax.ShapeDtypeStruct((hq, t), BF16),
            jax.ShapeDtypeStruct((t, hq), BF16),
            jax.ShapeDtypeStruct((MLA_W, t), BF16),
        ],
        compiler_params=_cparams(("arbitrary",), 48),
        name="mla_prep",
    )(proj, pos2, wuq_p, wukv_p, gqa, gkva, gq, gk, inv_lanes)


def _flash_kernel(qt_ref, k_ref, vt_ref, o_ref, *, tq, tk):
    seq = k_ref.shape[0]
    heads = k_ref.shape[1] // QK_PAD

    def scores(h, k0, q0):
        kj = k_ref[k0:k0 + tk, h * QK_PAD:(h + 1) * QK_PAD]
        return _dot(kj, qt_ref[h * QK_PAD:(h + 1) * QK_PAD, q0:q0 + tq])

    def update(h, state, st, k0, q0):
        m, l, acc = state
        if k0 + tk - 1 > q0:
            key = lax.broadcasted_iota(jnp.int32, st.shape, 0) + k0
            qry = lax.broadcasted_iota(jnp.int32, st.shape, 1) + q0
            st = jnp.where(key <= qry, st, -jnp.inf)
        m_new = jnp.maximum(m, jnp.max(st, axis=0, keepdims=True))
        alpha = jnp.exp2(m - m_new)
        p = jnp.exp2(st - m_new)
        l = alpha * l + jnp.sum(p, axis=0, keepdims=True)
        acc = alpha * acc + _dot(vt_ref[h * MLA_V:(h + 1) * MLA_V, k0:k0 + tk], p.astype(BF16))
        return m_new, l, acc

    steps = [(h, qi * tq, j * tk) for qi in range(seq // tq) for j in range((qi + 1) * tq // tk) for h in range(heads)]
    st_next = scores(steps[0][0], steps[0][2], steps[0][1])
    states = {}
    for n, (h, q0, k0) in enumerate(steps):
        st = st_next
        if n + 1 < len(steps):
            hn, qn, kn = steps[n + 1]
            st_next = scores(hn, kn, qn)
        if k0 == 0:
            states[h] = (jnp.full((1, tq), -jnp.inf, F32), jnp.zeros((1, tq), F32), jnp.zeros((MLA_V, tq), F32))
        states[h] = update(h, states[h], st, k0, q0)
        if k0 + tk >= q0 + tq:
            _, l, acc = states[h]
            o_ref[q0:q0 + tq, h * MLA_V:(h + 1) * MLA_V] = (acc / l).T.astype(o_ref.dtype)


def _flash(qt, k3, vt):
    b, seq, _ = k3.shape
    tq = min(FLASH_Q_TILE, seq)
    tk = min(FLASH_K_TILE, seq)
    hp = FLASH_HEADS_PER_STEP
    kern = functools.partial(_flash_kernel, tq=tq, tk=tk)
    return pl.pallas_call(
        kern,
        grid=(b, MLA_HEADS // hp),
        in_specs=[
            pl.BlockSpec((hp * QK_PAD, seq), lambda i, h: (h, i)),
            pl.BlockSpec((None, seq, hp * QK_PAD), lambda i, h: (i, 0, h)),
            pl.BlockSpec((hp * MLA_V, seq), lambda i, h: (h, i)),
        ],
        out_specs=pl.BlockSpec((None, seq, hp * MLA_V), lambda i, h: (i, 0, h)),
        out_shape=jax.ShapeDtypeStruct((b, seq, MLA_W), BF16),
        compiler_params=_cparams(("arbitrary", "arbitrary"), 40),
        name="flash",
    )(qt, k3, vt)


def _log_sigmoid(x):
    return -(jnp.maximum(-x, 0.0) + jnp.log1p(jnp.exp(-jnp.abs(x))))


def _conv_silu_qk(cur_ref, halo_ref, w_ref, b_ref, buf_ref, qt_ref, k_ref, first):
    L = cur_ref.shape[0]
    cols = cur_ref.shape[1]
    half = cols // 2
    buf_ref[0:SUBLANES, :] = jnp.where(first, 0.0, halo_ref[...].astype(F32))
    buf_ref[SUBLANES:SUBLANES + L, :] = cur_ref[...].astype(F32)
    cw = 512
    for c in range(cols // cw):
        sl = slice(c * cw, (c + 1) * cw)
        acc = jnp.zeros((L, cw), F32) + b_ref[:, sl]
        for j in range(ML_CONV):
            off = SUBLANES - (ML_CONV - 1) + j
            acc = acc + buf_ref[off:off + L, sl] * w_ref[j:j + 1, sl]
        y = acc * _sigmoid(acc)
        if c * cw < half:
            qt_ref[sl, :] = y.astype(qt_ref.dtype).T
        else:
            k_ref[:, c * cw - half:(c + 1) * cw - half] = (y * (ML_DQK ** -0.5)).astype(k_ref.dtype)


def _mlstm_kernel(cur_ref, halo_ref, cw_ref, cb_ref, v_ref, o_ref, gates_ref, gbias_ref, ng_ref, out_ref,
                  ct_ref, m_ref, buf_ref, qt_ref, k_ref):
    L = cur_ref.shape[0]
    first = pl.program_id(1) == 0

    @pl.when(first)
    def _():
        ct_ref[...] = jnp.zeros(ct_ref.shape, F32)
        m_ref[...] = jnp.zeros(m_ref.shape, F32)

    _conv_silu_qk(cur_ref, halo_ref, cw_ref, cb_ref, buf_ref, qt_ref, k_ref, first)

    g = gates_ref[...] + gbias_ref[...]
    gt = g.T
    lf = _log_sigmoid(g)
    lft = _log_sigmoid(gt)
    r = lax.broadcasted_iota(jnp.int32, (L, L), 0)
    c = lax.broadcasted_iota(jnp.int32, (L, L), 1)
    src_le_qry = r <= c
    tril = (c <= r).astype(F32)
    triu = src_le_qry.astype(F32)
    hi = lax.Precision.HIGHEST
    bcol_all = jnp.dot(tril, lf, preferred_element_type=F32, precision=hi)
    brow_all = jnp.dot(lft, triu, preferred_element_type=F32, precision=hi)
    row = lax.broadcasted_iota(jnp.int32, (ML_DV, L), 0)
    ones_row = jnp.where(row == 0, 1.0, 0.0)

    def lead_matmuls(h):
        hs = slice(h * ML_DQK, (h + 1) * ML_DQK)
        qt = qt_ref[hs, :]
        return _dot(k_ref[:, hs], qt), _dot(ct_ref[h].astype(BF16), qt)

    lead_next = lead_matmuls(0)
    for h in range(ML_HEADS):
        hs = slice(h * ML_DQK, (h + 1) * ML_DQK)
        qk_t, cq_t = lead_next
        if h + 1 < ML_HEADS:
            lead_next = lead_matmuls(h + 1)
        li, lf_ = GATE_LANE + h, GATE_LANE + ML_HEADS + h
        b_row = brow_all[lf_:lf_ + 1, :]
        i_row = gt[li:li + 1, :]
        u_col = g[:, li:li + 1] - bcol_all[:, lf_:lf_ + 1]
        m_prev = m_ref[h][:, :1]
        logw_t = jnp.where(src_le_qry, b_row + u_col, -jnp.inf)
        log_inter = b_row + m_prev
        m_t = jnp.maximum(jnp.max(logw_t, axis=0, keepdims=True), log_inter)
        w_t = jnp.exp(logw_t - m_t)
        a = jnp.exp(log_inter - m_t)
        kh = k_ref[:, hs]
        vt_aug = jnp.concatenate([v_ref[:, hs].astype(F32).T, ones_row], axis=0)
        s_t = qk_t * w_t
        ct = ct_ref[h]
        nd = _dot(vt_aug.astype(BF16), s_t.astype(BF16)) + a * cq_t
        num = nd[:ML_DV, :]
        den = nd[ML_DV:ML_DV + 1, :]
        hout_t = num * (1.0 / jnp.maximum(jnp.abs(den), jnp.exp(-m_t)))
        hn_t = hout_t * lax.rsqrt(jnp.sum(hout_t * hout_t, axis=0, keepdims=True) * (1.0 / ML_DV) + EPS)
        gate = _sigmoid(o_ref[:, hs].astype(F32))
        out_ref[:, hs] = (hn_t.T * ng_ref[:, hs] * gate).astype(out_ref.dtype)

        b_last = b_row[:, L - 1:L]
        logg = b_last - b_row + i_row
        m_new = jnp.maximum(b_last + m_prev, jnp.max(logg, axis=-1, keepdims=True))
        g_row = jnp.exp(logg - m_new)
        decay = jnp.exp(b_last + m_prev - m_new)
        ct_ref[h] = decay * ct + _dot((vt_aug * g_row).astype(BF16), kh)
        m_ref[h] = jnp.broadcast_to(m_new, m_ref.shape[1:])


def _mlstm(proj3, conv_w, conv_b, gates3, gbias, norm_g):
    b, seq, _ = proj3.shape
    w = ML_W
    qk_cols = 2 * ML_HEADS * ML_DQK
    L = min(MLSTM_CHUNK, seq)
    hb = L // SUBLANES
    return pl.pallas_call(
        _mlstm_kernel,
        grid=(b, seq // L),
        in_specs=[
            pl.BlockSpec((None, L, qk_cols), lambda i, c: (i, c, COL_QK // qk_cols)),
            pl.BlockSpec((None, SUBLANES, qk_cols), lambda i, c: (i, jnp.maximum(c * hb - 1, 0), COL_QK // qk_cols)),
            pl.BlockSpec((ML_CONV, qk_cols), lambda i, c: (0, 0)),
            pl.BlockSpec((1, qk_cols), lambda i, c: (0, 0)),
            pl.BlockSpec((None, L, w), lambda i, c: (i, c, COL_V // w)),
            pl.BlockSpec((None, L, w), lambda i, c: (i, c, COL_O // w)),
            pl.BlockSpec((None, L, LANES), lambda i, c: (i, c, 0)),
            pl.BlockSpec((1, LANES), lambda i, c: (0, 0)),
            pl.BlockSpec((1, w), lambda i, c: (0, 0)),
        ],
        out_specs=pl.BlockSpec((None, L, w), lambda i, c: (i, c, 0)),
        out_shape=jax.ShapeDtypeStruct((b, seq, w), BF16),
        scratch_shapes=[
            pltpu.VMEM((ML_HEADS, 2 * ML_DV, ML_DQK), F32),
            pltpu.VMEM((ML_HEADS, 1, LANES), F32),
            pltpu.VMEM((L + SUBLANES, qk_cols), F32),
            pltpu.VMEM((ML_HEADS * ML_DQK, L), BF16),
            pltpu.VMEM((L, ML_HEADS * ML_DQK), BF16),
        ],
        compiler_params=_cparams(("arbitrary", "arbitrary"), 40),
        name="mlstm",
    )(proj3, proj3, conv_w, conv_b.reshape(1, qk_cols), proj3, proj3, gates3, gbias, norm_g.reshape(1, w))


def _merge_kernel(a_ref, b_ref, ga_ref, gb_ref, wa_ref, wb_ref, o_ref):
    tm = a_ref.shape[0]
    rows = min(MERGE_ROW_GROUP, tm)
    groups = [slice(r0, r0 + rows) for r0 in range(0, tm, rows)]

    def proj(rs):
        return _dot(a_ref[rs, :], wa_ref[...]), _dot(b_ref[rs, :], wb_ref[...])

    nxt = proj(groups[0])
    for n, rs in enumerate(groups):
        pa, pb = nxt
        if n + 1 < len(groups):
            nxt = proj(groups[n + 1])
        mixed = _sigmoid(ga_ref[rs, :].astype(F32)) * pa + _sigmoid(gb_ref[rs, :].astype(F32)) * pb
        o_ref[rs, :] = mixed.astype(o_ref.dtype)


def _merge(out_a, hm, proj, wa, wb, seq):
    t = out_a.shape[0]
    d = wa.shape[1]
    tm = min(512, seq)
    tn = 2048
    return pl.pallas_call(
        _merge_kernel,
        grid=(d // tn, t // tm),
        in_specs=[
            pl.BlockSpec((tm, MLA_W), lambda j, i: (i, 0)),
            pl.BlockSpec((tm, ML_W), lambda j, i: (i, 0)),
            pl.BlockSpec((tm, tn), lambda j, i: (i, COL_GA // tn + j)),
            pl.BlockSpec((tm, tn), lambda j, i: (i, COL_GB // tn + j)),
            pl.BlockSpec((MLA_W, tn), lambda j, i: (0, j)),
            pl.BlockSpec((ML_W, tn), lambda j, i: (0, j)),
        ],
        out_specs=pl.BlockSpec((tm, tn), lambda j, i: (i, j)),
        out_shape=jax.ShapeDtypeStruct((t, d), BF16),
        compiler_params=_cparams(("arbitrary", "arbitrary"), 40),
        name="merge",
    )(out_a, hm, proj, proj, wa, wb)


def _outproj_kernel(mix_ref, x_ref, w_ref, gt_ref, sc_ref, sh_ref, g_ref, wr_ref, br_ref, x1_ref, h2_ref, lg_ref):
    tm = mix_ref.shape[0]
    rows = OUTPROJ_ROW_GROUP
    groups = [slice(r0, r0 + rows) for r0 in range(0, tm, rows)]
    y_next = _dot(mix_ref[groups[0], :], w_ref[...])
    for n, rs in enumerate(groups):
        y = y_next
        if n + 1 < len(groups):
            y_next = _dot(mix_ref[groups[n + 1], :], w_ref[...])
        x1 = x_ref[rs, :] + gt_ref[...] * y
        x1_ref[rs, :] = x1
        h2 = x1 * _rms_scale(x1, x1.shape[-1]) * g_ref[...]
        h2 = h2 * (1.0 + sc_ref[...]) + sh_ref[...]
        h2_ref[rs, :] = h2
        h_hi = h2.astype(BF16)
        h_lo = (h2 - h_hi.astype(F32)).astype(BF16)
        r = _dot(h_hi, wr_ref[...]) + _dot(h_lo, wr_ref[...])
        lg_ref[rs, :] = r[:, :LANES] + r[:, LANES:] + br_ref[...]


def _outproj(mixed, x2, w_out, mod4, norm_g, w_route2, b_route, seq):
    t, d = x2.shape
    tm = min(512, seq)
    bpr = seq // tm
    mod_spec = lambda k: pl.BlockSpec((None, None, 1, d), lambda i: (i // bpr, k, 0, 0))
    const = lambda i: (0, 0)
    return pl.pallas_call(
        _outproj_kernel,
        grid=(t // tm,),
        in_specs=[
            pl.BlockSpec((tm, d), lambda i: (i, 0)),
            pl.BlockSpec((tm, d), lambda i: (i, 0)),
            pl.BlockSpec((d, d), const, pipeline_mode=pl.Buffered(1)),
            mod_spec(2),
            mod_spec(4),
            mod_spec(3),
            pl.BlockSpec((1, d), const),
            pl.BlockSpec((d, 2 * LANES), const, pipeline_mode=pl.Buffered(1)),
            pl.BlockSpec((1, LANES), const),
        ],
        out_specs=[
            pl.BlockSpec((tm, d), lambda i: (i, 0)),
            pl.BlockSpec((tm, d), lambda i: (i, 0)),
            pl.BlockSpec((tm, LANES), lambda i: (i, 0)),
        ],
        out_shape=[
            jax.ShapeDtypeStruct((t, d), F32),
            jax.ShapeDtypeStruct((t, d), F32),
            jax.ShapeDtypeStruct((t, LANES), F32),
        ],
        compiler_params=_cparams(("arbitrary",), 56),
        name="outproj",
    )(mixed, x2, w_out, mod4, mod4, mod4, norm_g.reshape(1, d), w_route2, b_route)


def _route_kernel(lg_ref, pos_ref, wts_ref, cnt_ref, carry_ref, offs_ref, meta_s, wts_s):
    phase = pl.program_id(0)
    i = pl.program_id(1)
    tm = lg_ref.shape[0]
    lane = lax.broadcasted_iota(jnp.int32, (tm, LANES), 1)

    @pl.when((phase == 0) & (i == 0))
    def _():
        carry_ref[...] = jnp.zeros(carry_ref.shape, F32)

    @pl.when(phase == 0)
    def _():
        lg = lg_ref[...]
        big = jnp.int32(LANES)
        ninf = -jnp.inf

        def first_argmax(vals):
            mx = jnp.max(vals, axis=-1, keepdims=True)
            idx = jnp.min(jnp.where(vals == mx, lane, big), axis=-1, keepdims=True)
            return mx, idx

        gl = jnp.where(lane < N_GROUPS, lg, ninf)
        gmax, gsel = first_argmax(gl)
        g_w = 1.0 / jnp.sum(jnp.exp(gl - gmax), axis=-1, keepdims=True)
        lo = N_GROUPS + gsel * EXP_PER_GROUP
        in_grp = (lane >= lo) & (lane < lo + EXP_PER_GROUP)
        el = jnp.where(in_grp, lg, ninf)
        e1, i1 = first_argmax(el)
        e2, i2 = first_argmax(jnp.where(lane == i1, ninf, el))
        p2 = jnp.exp(e2 - e1)
        w1 = g_w / (1.0 + p2)
        w2 = g_w * p2 / (1.0 + p2)

        oh1 = lane == i1
        oh2 = lane == i2
        oh = jnp.where(oh1 | oh2, 1.0, 0.0)
        sub = min(ROUTE_RANK_ROWS, tm)
        r = lax.broadcasted_iota(jnp.int32, (sub, sub), 0)
        c = lax.broadcasted_iota(jnp.int32, (sub, sub), 1)
        strict = jnp.where(c < r, 1.0, 0.0).astype(BF16)
        carry = carry_ref[...]
        parts = []
        for r0 in range(0, tm, sub):
            oh_sub = oh[r0:r0 + sub, :]
            parts.append(_dot(strict, oh_sub.astype(BF16)) + carry)
            carry = carry + jnp.sum(oh_sub, axis=0, keepdims=True)
        before = jnp.concatenate(parts, axis=0)
        rank1 = jnp.sum(jnp.where(oh1, before, 0.0), axis=-1, keepdims=True).astype(jnp.int32)
        rank2 = jnp.sum(jnp.where(oh2, before, 0.0), axis=-1, keepdims=True).astype(jnp.int32)
        carry_ref[...] = carry
        meta_s[i] = jnp.where(lane == 0, i1, jnp.where(lane == 1, i2, jnp.where(lane == 2, rank1, jnp.where(lane == 3, rank2, 0))))
        wts_s[i] = jnp.where(lane == 0, w1, jnp.where(lane == 1, w2, 0.0))

    @pl.when((phase == 1) & (i == 0))
    def _():
        cnt = carry_ref[...]
        cnt_ref[...] = jnp.broadcast_to(cnt, cnt_ref.shape)
        padded = jnp.ceil(cnt * (1.0 / MOE_TILE)) * MOE_TILE
        r = lax.broadcasted_iota(jnp.int32, (LANES, LANES), 0)
        c = lax.broadcasted_iota(jnp.int32, (LANES, LANES), 1)
        upper = jnp.where(r < c, 1.0, 0.0).astype(BF16)
        padded8 = jnp.broadcast_to(padded, (SUBLANES, LANES)).astype(BF16)
        offs_ref[...] = _dot(padded8, upper)[:1, :]

    @pl.when(phase == 1)
    def _():
        meta = meta_s[i]
        offs = offs_ref[...]
        off1 = jnp.sum(jnp.where(lane == meta[:, 0:1], offs, 0.0), axis=-1, keepdims=True).astype(jnp.int32)
        off2 = jnp.sum(jnp.where(lane == meta[:, 1:2], offs, 0.0), axis=-1, keepdims=True).astype(jnp.int32)
        pos1 = off1 + meta[:, 2:3]
        pos2 = off2 + meta[:, 3:4]
        pos_ref[...] = jnp.where(lane == 0, pos1, jnp.where(lane == 1, pos2, 0))
        wts_ref[...] = wts_s[i]


def _route(logits):
    t = logits.shape[0]
    tm = min(1024, t)
    nb = t // tm
    assert (2 * t) // MOE_TILE + N_EXPERTS <= 256, "tile-padded offsets must stay exact in bf16 (8 significant bits)"
    return pl.pallas_call(
        _route_kernel,
        grid=(2, nb),
        in_specs=[pl.BlockSpec((tm, LANES), lambda p, i: (i * (1 - p), 0))],
        out_specs=[
            pl.BlockSpec((tm, LANES), lambda p, i: (i * p, 0)),
            pl.BlockSpec((tm, LANES), lambda p, i: (i * p, 0)),
            pl.BlockSpec((SUBLANES, LANES), lambda p, i: (0, 0)),
        ],
        out_shape=[
            jax.ShapeDtypeStruct((t, LANES), jnp.int32),
            jax.ShapeDtypeStruct((t, LANES), F32),
            jax.ShapeDtypeStruct((SUBLANES, LANES), F32),
        ],
        scratch_shapes=[
            pltpu.VMEM((1, LANES), F32),
            pltpu.VMEM((1, LANES), F32),
            pltpu.VMEM((nb, tm, LANES), jnp.int32),
            pltpu.VMEM((nb, tm, LANES), F32),
        ],
        compiler_params=_cparams(("arbitrary", "arbitrary"), 32),
        name="route",
    )(logits)


def _row_gather_start(idx_at, src_hbm, dst, sem, rows):
    for r in range(rows):
        pltpu.make_async_copy(src_hbm.at[pl.ds(idx_at(r), 1)], dst.at[pl.ds(r, 1)], sem).start(priority=r % 2)


def _row_gather_wait(src_hbm, dst, sem, rows):
    pltpu.make_async_copy(src_hbm.at[pl.ds(0, rows)], dst, sem).wait()


def _zero_piece_sizes(tile):
    sizes = [tile]
    size = tile // 2
    while size >= SUBLANES:
        sizes.append(size)
        size //= 2
    return sizes + [1] * (SUBLANES - 1)


def _zero_pieces(tile):
    return [(n * N_EXPERTS + e, size) for n, size in enumerate(_zero_piece_sizes(tile)) for e in range(N_EXPERTS)]


def _dispatch_kernel(pos_ref, zrow_ref, h_ref, xs_hbm, zbuf, sems):
    i = pl.program_id(0)
    tm = h_ref.shape[0]
    pieces = _zero_pieces(zbuf.shape[0])

    def zero_copy(z, size):
        start = zrow_ref[z] if size == 1 else pl.multiple_of(zrow_ref[z], SUBLANES)
        return pltpu.make_async_copy(zbuf.at[pl.ds(0, size)], xs_hbm.at[pl.ds(start, size)], sems.at[0])

    @pl.when(i == 0)
    def _():
        zbuf[...] = jnp.zeros(zbuf.shape, zbuf.dtype)
        for z, size in pieces:
            @pl.when(zrow_ref[z] >= 0)
            def _():
                zero_copy(z, size).start()

    for k in range(2):
        base = (2 * i + k) * tm
        for r in range(tm):
            pltpu.make_async_copy(h_ref.at[pl.ds(r, 1)], xs_hbm.at[pl.ds(pos_ref[base + r], 1)],
                                  sems.at[1]).start(priority=r % 2)
    for k in range(2):
        pltpu.make_async_copy(h_ref, xs_hbm.at[pl.ds(0, tm)], sems.at[1]).wait()

    @pl.when(i == 0)
    def _():
        for z, size in pieces:
            @pl.when(zrow_ref[z] >= 0)
            def _():
                zero_copy(z, size).wait()


def _dispatch(pos, zrow, h2, n_rows, seq):
    t, d = h2.shape
    tm = min(1024, seq)
    pos_blocks = pos.reshape(t // tm, tm, 2).transpose(0, 2, 1).reshape(-1)
    grid_spec = pltpu.PrefetchScalarGridSpec(
        num_scalar_prefetch=2,
        grid=(t // tm,),
        in_specs=[pl.BlockSpec((tm, d), lambda i, p, z: (i, 0))],
        out_specs=pl.BlockSpec(memory_space=pl.ANY),
        scratch_shapes=[
            pltpu.VMEM((MOE_TILE, d), F32),
            pltpu.SemaphoreType.DMA((2,)),
        ],
    )
    return pl.pallas_call(
        _dispatch_kernel,
        grid_spec=grid_spec,
        out_shape=jax.ShapeDtypeStruct((n_rows, d), F32),
        compiler_params=_cparams(("arbitrary",), 40),
        name="moe_dispatch",
    )(pos_blocks, zrow, h2)


def _moe_kernel(plan_ref, x_hbm, wg_hbm, wu_hbm, wd_hbm, y_ref, stg_g, stg_u, stg_d, wsem, xbuf, xsem):
    j = pl.program_id(0)
    nt = plan_ref[3, 0]
    nslot = stg_g.shape[0]

    def w_copies(e, s):
        return (pltpu.make_async_copy(wg_hbm.at[e], stg_g.at[s], wsem.at[s]),
                pltpu.make_async_copy(wu_hbm.at[e], stg_u.at[s], wsem.at[s]),
                pltpu.make_async_copy(wd_hbm.at[e], stg_d.at[s], wsem.at[s]))

    xslots = xbuf.shape[0]
    tile = xbuf.shape[1]

    def x_copy(t):
        s = lax.rem(t, xslots)
        return pltpu.make_async_copy(x_hbm.at[pl.ds(pl.multiple_of(t * tile, tile), tile)], xbuf.at[s], xsem.at[s])

    @pl.when(j == 0)
    def _():
        for s in range(nslot - 1):
            @pl.when(plan_ref[3, 1 + s] >= 0)
            def _():
                for cp in w_copies(plan_ref[3, 1 + s], s):
                    cp.start()
        for t in range(xslots - 1):
            @pl.when(t < nt)
            def _():
                x_copy(t).start()

    @pl.when(j + xslots - 1 < nt)
    def _():
        x_copy(j + xslots - 1).start()

    @pl.when((j < nt) & (plan_ref[0, j] == 1))
    def _():
        s = plan_ref[1, j]
        for cp in w_copies(0, s):
            cp.wait()

        @pl.when(plan_ref[2, j] >= 0)
        def _():
            free = lax.rem(s + nslot - 1, nslot)
            for cp in w_copies(plan_ref[2, j], free):
                cp.start()

    @pl.when(j < nt)
    def _():
        s = plan_ref[1, j]
        x_copy(j).wait()
        x = xbuf[lax.rem(j, xslots)].astype(BF16)
        hg = _dot(x, stg_g[s].astype(BF16))
        hu = _dot(x, stg_u[s].astype(BF16))
        act = (hg * _sigmoid(hg) * hu).astype(BF16)
        y_ref[...] = _dot(act, stg_d[s].astype(BF16))

    @pl.when(j >= nt)
    def _():
        y_ref[...] = jnp.zeros(y_ref.shape, y_ref.dtype)


def _moe(plan, xs, w_gate_e, w_up_e, w_down_e):
    d = xs.shape[1]
    f = w_gate_e.shape[-1]
    nt_max = plan.shape[1]
    tile = MOE_TILE
    any_spec = pl.BlockSpec(memory_space=pl.ANY)
    grid_spec = pltpu.PrefetchScalarGridSpec(
        num_scalar_prefetch=1,
        grid=(nt_max,),
        in_specs=[any_spec, any_spec, any_spec, any_spec],
        out_specs=pl.BlockSpec((tile, d), lambda j, plan_: (j, 0)),
        scratch_shapes=[
            pltpu.VMEM((MOE_WEIGHT_SLOTS, d, f), F32),
            pltpu.VMEM((MOE_WEIGHT_SLOTS, d, f), F32),
            pltpu.VMEM((MOE_WEIGHT_SLOTS, f, d), F32),
            pltpu.SemaphoreType.DMA((MOE_WEIGHT_SLOTS,)),
            pltpu.VMEM((MOE_X_SLOTS, tile, d), F32),
            pltpu.SemaphoreType.DMA((MOE_X_SLOTS,)),
        ],
    )
    return pl.pallas_call(
        _moe_kernel,
        grid_spec=grid_spec,
        out_shape=jax.ShapeDtypeStruct((nt_max * tile, d), F32),
        compiler_params=_cparams(("arbitrary",), 56),
        name="moe_experts",
    )(plan, xs, w_gate_e, w_up_e, w_down_e)


def _combine_kernel(pos_ref, y_hbm, x1_ref, gt_ref, wts_ref, o_ref, ybuf, sem):
    i = pl.program_id(0)
    n = pl.num_programs(0)
    tm = x1_ref.shape[0]
    slot = lax.rem(i, 2)

    def start(blk, s):
        for k in range(2):
            base = (2 * blk + k) * tm
            _row_gather_start(lambda r, base=base: pos_ref[base + r], y_hbm, ybuf.at[s, k], sem.at[s], tm)

    @pl.when(i == 0)
    def _():
        start(0, 0)

        @pl.when(n > 1)
        def _():
            start(1, 1)

    for k in range(2):
        _row_gather_wait(y_hbm, ybuf.at[slot, k], sem.at[slot], tm)
    w = wts_ref[...]
    moe = w[:, 0:1] * ybuf[slot, 0] + w[:, 1:2] * ybuf[slot, 1]
    o_ref[...] = x1_ref[...] + gt_ref[...] * moe

    @pl.when(i + 2 < n)
    def _():
        start(i + 2, slot)


def _combine(pos, ys, x1, mod4, wts, seq):
    t, d = x1.shape
    tm = min(512, seq)
    bpr = seq // tm
    pos_blocks = pos.reshape(t // tm, tm, 2).transpose(0, 2, 1).reshape(-1)
    grid_spec = pltpu.PrefetchScalarGridSpec(
        num_scalar_prefetch=1,
        grid=(t // tm,),
        in_specs=[
            pl.BlockSpec(memory_space=pl.ANY),
            pl.BlockSpec((tm, d), lambda i, p: (i, 0)),
            pl.BlockSpec((None, None, 1, d), lambda i, p: (i // bpr, 5, 0, 0)),
            pl.BlockSpec((tm, LANES), lambda i, p: (i, 0)),
        ],
        out_specs=pl.BlockSpec((tm, d), lambda i, p: (i, 0)),
        scratch_shapes=[
            pltpu.VMEM((2, 2, tm, d), F32),
            pltpu.SemaphoreType.DMA((2,)),
        ],
    )
    return pl.pallas_call(
        _combine_kernel,
        grid_spec=grid_spec,
        out_shape=jax.ShapeDtypeStruct((t, d), F32),
        compiler_params=_cparams(("arbitrary",), 48),
        name="moe_combine",
    )(pos_blocks, ys, x1, mod4, wts)


def _swap_halves(w):
    half = w.shape[-1] // 2
    return jnp.concatenate([w[..., half:], w[..., :half]], axis=-1)


def _layout_w_uq(w_uq):
    r = w_uq.shape[0]
    w = w_uq.reshape(r, MLA_HEADS, MLA_QK)
    rope = w[..., MLA_NOPE:]
    return jnp.concatenate([w[..., :MLA_NOPE], rope, _swap_halves(rope)], axis=-1).reshape(r, MLA_HEADS * QK_PAD).astype(BF16)


def _layout_w_ukv(w_ukv):
    r = w_ukv.shape[0]
    w = w_ukv.reshape(r, MLA_HEADS, MLA_NOPE + MLA_V)
    return jnp.concatenate([w[..., :MLA_NOPE].reshape(r, -1), w[..., MLA_NOPE:].reshape(r, -1)], axis=-1).astype(BF16)


def _rope_gain(g):
    g1 = g[MLA_NOPE:MLA_NOPE + MLA_ROPE // 2]
    g2 = g[MLA_NOPE + MLA_ROPE // 2:]
    return jnp.concatenate([g[:MLA_NOPE], g1, g2, -g2, g1]).reshape(1, QK_PAD)


def _layer(x2, cond_mod4, pos2, seq, w_in, q_a_norm_g, w_uq, kv_a_norm_g, w_ukv, q_norm_g, k_norm_g, conv_w, conv_b,
           b_mlstm_gates, mlstm_norm_g, w_proj_a, w_proj_b, w_out, norm_mix_g, norm_ffn_g, w_group, b_group,
           w_router, b_router, w_gate_e, w_up_e, w_down_e):
    t, d = x2.shape
    b = t // seq
    mod4 = cond_mod4

    proj, gates = _inproj(x2, mod4, norm_mix_g, w_in.T, seq)

    inv = ROPE_THETA ** (-jnp.arange(0, MLA_ROPE, 2, dtype=F32) / MLA_ROPE)
    inv_lanes = jnp.tile(inv, LANES // (MLA_ROPE // 2)).reshape(1, LANES)
    qt, k, vt = _mla_prep(proj, pos2, _layout_w_uq(w_uq), _layout_w_ukv(w_ukv), q_a_norm_g.reshape(1, -1),
                        kv_a_norm_g.reshape(1, -1), _rope_gain(q_norm_g), _rope_gain(k_norm_g), inv_lanes, seq)
    out_a = _flash(qt, k.reshape(b, seq, -1), vt).reshape(t, MLA_W)

    proj3 = proj.reshape(b, seq, -1)
    gbias = jnp.zeros((1, LANES), F32).at[0, GATE_LANE:GATE_LANE + 2 * ML_HEADS].set(b_mlstm_gates.reshape(-1))
    hm = _mlstm(proj3, conv_w, conv_b, gates.reshape(b, seq, LANES), gbias, mlstm_norm_g).reshape(t, ML_W)

    mixed = _merge(out_a, hm, proj, w_proj_a.astype(BF16), w_proj_b.astype(BF16), seq)

    route_pad = LANES - N_GROUPS - N_EXPERTS
    w_route = jnp.concatenate([w_group, w_router, jnp.zeros((d, route_pad), F32)], axis=1)
    b_route = jnp.concatenate([b_group, b_router, jnp.zeros((route_pad,), F32)]).reshape(1, LANES)
    w_route_hi = w_route.astype(BF16)
    w_route2 = jnp.concatenate([w_route_hi, (w_route - w_route_hi.astype(F32)).astype(BF16)], axis=1)
    x1, h2, logits = _outproj(mixed, x2, w_out.astype(BF16), mod4, norm_ffn_g, w_route2, b_route, seq)

    posm, wts, counts = _route(logits)
    pos = posm[:, 0:2]

    tile = MOE_TILE
    i32 = jnp.int32
    cnt = counts[0, N_GROUPS:N_GROUPS + N_EXPERTS].astype(i32)
    padded = ((cnt + tile - 1) // tile) * tile
    ends = jnp.cumsum(padded)
    offs = ends - padded
    nt_max = (2 * t) // tile + N_EXPERTS
    n_tiles = ends[-1] // tile
    tile_idx = jnp.arange(nt_max, dtype=i32)
    tile_start = tile_idx * tile
    live = tile_idx < n_tiles
    texp = jnp.minimum(jnp.sum((ends[None, :] <= tile_start[:, None]).astype(i32), axis=1), N_EXPERTS - 1)
    active = cnt > 0
    order = jnp.cumsum(active.astype(i32)) - 1
    n_active = jnp.sum(active.astype(i32))
    experts = jnp.arange(N_EXPERTS, dtype=i32)
    by_order = jnp.sum(jnp.where(active[None, :] & (order[None, :] == experts[:, None]), experts[None, :], 0), axis=1)
    t_order = order[texp]
    first = (live & (tile_start == offs[texp])).astype(i32)
    nslot = MOE_WEIGHT_SLOTS
    ahead = t_order + nslot - 1
    prefetch = jnp.where(ahead < n_active, by_order[jnp.minimum(ahead, N_EXPERTS - 1)], -1)
    lead = jnp.where(experts[:nslot - 1] < n_active, by_order[:nslot - 1], -1)
    head = jnp.zeros((nt_max,), i32).at[0].set(n_tiles).at[1:nslot].set(lead)
    plan = jnp.stack([first, t_order % nslot, prefetch, head]).astype(i32)

    spare_idx = n_tiles + jnp.arange(N_EXPERTS, dtype=i32)
    zparts = [jnp.where(spare_idx < nt_max, spare_idx * tile, -1)]
    pad = padded - cnt
    for size in _zero_piece_sizes(tile)[1:]:
        if size > 1:
            above = pad & ~(2 * size - 1)
            zparts.append(jnp.where((pad & size) != 0, ends - above - size, -1))
    for u in range(SUBLANES - 1):
        zparts.append(jnp.where(u < (pad & (SUBLANES - 1)), offs + cnt + u, -1))
    zrow = jnp.concatenate(zparts).astype(i32)

    xs = _dispatch(pos, zrow, h2, nt_max * tile, seq)
    ys = _moe(plan, xs, w_gate_e, w_up_e, w_down_e)

    return _combine(pos, ys, x1, mod4, wts, seq)


def kernel(x, c, positions, w_ada, b_ada, norm_mix_g, w_in, q_a_norm_g, w_uq, kv_a_norm_g, w_ukv, q_norm_g, k_norm_g, conv_w, conv_b, b_mlstm_gates, mlstm_norm_g, w_proj_a, w_proj_b, w_out, norm_ffn_g, w_group, b_group, w_router, b_router, w_gate_e, w_up_e, w_down_e):
    b, seq, d = x.shape
    depth = w_ada.shape[0]
    x2 = x.reshape(b * seq, d)
    pos2 = positions.reshape(b * seq, 1)
    c_pad = jnp.zeros((SUBLANES, d), F32).at[:b].set(c)
    for l in range(depth):
        mod = _adaln(c_pad, w_ada[l], b_ada[l])
        mod4 = mod[:b].reshape(b, 6, 1, d)
        x2 = _layer(x2, mod4, pos2, seq, w_in[l], q_a_norm_g[l], w_uq[l], kv_a_norm_g[l], w_ukv[l], q_norm_g[l],
                    k_norm_g[l], conv_w[l], conv_b[l], b_mlstm_gates[l], mlstm_norm_g[l], w_proj_a[l], w_proj_b[l],
                    w_out[l], norm_mix_g[l], norm_ffn_g[l], w_group[l], b_group[l], w_router[l], b_router[l],
                    w_gate_e[l], w_up_e[l], w_down_e[l])
    return x2.reshape(b, seq, d)
```

```python
import functools
import math

import jax
import jax.numpy as jnp
from jax import lax
from jax.experimental import pallas as pl
from jax.experimental.pallas import tpu as pltpu

F32 = jnp.float32
BF16 = jnp.bfloat16

LANES = 128
SUBLANES = 8

D_MODEL = 2048
MLA_HEADS = 8
MLA_NOPE = 128
MLA_ROPE = 64
MLA_QK = MLA_NOPE + MLA_ROPE
MLA_V = 128
Q_LORA = 512
KV_LORA = 256
ROPE_THETA = 10000.0
ML_HEADS = 8
ML_DQK = 128
ML_DV = 128
ML_CONV = 4
MLA_W = MLA_HEADS * MLA_V
ML_W = ML_HEADS * ML_DV
N_GROUPS = 4
EXP_PER_GROUP = 8
N_EXPERTS = N_GROUPS * EXP_PER_GROUP
EPS = 1e-6

QK_PAD = 2 * LANES

SRC_CQ = 0
SRC_KPE = Q_LORA + KV_LORA
SRC_QK = SRC_KPE + MLA_ROPE
SRC_V = SRC_QK + 2 * ML_HEADS * ML_DQK
SRC_O = SRC_V + ML_W
SRC_I = SRC_O + ML_W
SRC_GA = SRC_I + 2 * ML_HEADS
SRC_GB = SRC_GA + D_MODEL
GATE_LANE = SRC_I % LANES

IN_BLOCK = 1024
IN_BLOCK_SRC = (SRC_QK, SRC_QK + IN_BLOCK, SRC_GA, SRC_GA + IN_BLOCK, SRC_GB, SRC_GB + IN_BLOCK, SRC_V, SRC_O, SRC_CQ)
COL_QK = 0
COL_GA = 2048
COL_GB = 4096
COL_V = 6144
COL_O = 7168
COL_LAT = 8192
LAT_W = Q_LORA + KV_LORA + MLA_ROPE

MLSTM_CHUNK = 256
MOE_TILE = 256
MOE_WEIGHT_SLOTS = 3
MOE_X_SLOTS = 4
ROUTE_RANK_ROWS = 256
FLASH_HEADS_PER_STEP = 2
FLASH_Q_TILE = 512
FLASH_K_TILE = 512
OUTPROJ_ROW_GROUP = 256
MERGE_ROW_GROUP = 256


def _cparams(sem, vmem_mb):
    return pltpu.CompilerParams(dimension_semantics=sem, vmem_limit_bytes=vmem_mb * 1024 * 1024)


def _dot(a, b):
    return jnp.dot(a, b, preferred_element_type=F32)


def _dot_nt(a, b):
    return lax.dot_general(a, b, (((1,), (1,)), ((), ())), preferred_element_type=F32)


def _sigmoid(x):
    return 1.0 / (1.0 + jnp.exp(-x))


def _rms_scale(x, width):
    return lax.rsqrt(jnp.sum(x * x, axis=-1, keepdims=True) * (1.0 / width) + EPS)


def _adaln_kernel(c_ref, w_ref, b_ref, o_ref):
    c = c_ref[...]
    cond = (c * _sigmoid(c)).astype(BF16)
    o_ref[...] = _dot(cond, w_ref[...].astype(BF16)) + b_ref[...]


def _adaln(c_pad, w_ada, b_ada):
    rows, d = c_pad.shape
    n = w_ada.shape[1]
    tn = 2048
    return pl.pallas_call(
        _adaln_kernel,
        grid=(n // tn,),
        in_specs=[
            pl.BlockSpec((rows, d), lambda j: (0, 0)),
            pl.BlockSpec((d, tn), lambda j: (0, j)),
            pl.BlockSpec((1, tn), lambda j: (0, j)),
        ],
        out_specs=pl.BlockSpec((rows, tn), lambda j: (0, j)),
        out_shape=jax.ShapeDtypeStruct((rows, n), F32),
        compiler_params=_cparams(("arbitrary",), 48),
        name="adaln",
    )(c_pad, w_ada, b_ada.reshape(1, n))


def _inproj_kernel(off_ref, x_ref, sc_ref, sh_ref, g_ref, w_ref, wgate_ref, proj_ref, gates_ref, h_ref):
    j = pl.program_id(1)

    @pl.when(j == 0)
    def _():
        x = x_ref[...]
        h = x * _rms_scale(x, x.shape[-1]) * g_ref[...]
        h = (h * (1.0 + sc_ref[...]) + sh_ref[...]).astype(BF16)
        h_ref[...] = h
        gates_ref[...] = _dot_nt(h, wgate_ref[...].astype(BF16))

    proj_ref[...] = _dot_nt(h_ref[...], w_ref[...].astype(BF16)).astype(proj_ref.dtype)


def _inproj(x2, mod4, norm_g, w_in_t, seq):
    t, d = x2.shape
    tm = min(1024, seq)
    tn = IN_BLOCK
    bpr = seq // tm
    assert all(o % SUBLANES == 0 for o in IN_BLOCK_SRC)
    offs = jnp.asarray([o // SUBLANES for o in IN_BLOCK_SRC], jnp.int32)
    nblk = len(IN_BLOCK_SRC)
    gate_tile = SRC_I // LANES
    grid_spec = pltpu.PrefetchScalarGridSpec(
        num_scalar_prefetch=1,
        grid=(t // tm, nblk),
        in_specs=[
            pl.BlockSpec((tm, d), lambda i, j, o: (i, 0)),
            pl.BlockSpec((None, None, 1, d), lambda i, j, o: (i // bpr, 1, 0, 0)),
            pl.BlockSpec((None, None, 1, d), lambda i, j, o: (i // bpr, 0, 0, 0)),
            pl.BlockSpec((1, d), lambda i, j, o: (0, 0)),
            pl.BlockSpec((pl.Element(tn), pl.Element(d)), lambda i, j, o: (o[j] * SUBLANES, 0)),
            pl.BlockSpec((LANES, d), lambda i, j, o: (gate_tile, 0)),
        ],
        out_specs=[
            pl.BlockSpec((tm, tn), lambda i, j, o: (i, j)),
            pl.BlockSpec((tm, LANES), lambda i, j, o: (i, 0)),
        ],
        scratch_shapes=[pltpu.VMEM((tm, d), BF16)],
    )
    return pl.pallas_call(
        _inproj_kernel,
        grid_spec=grid_spec,
        out_shape=[
            jax.ShapeDtypeStruct((t, nblk * tn), BF16),
            jax.ShapeDtypeStruct((t, LANES), F32),
        ],
        compiler_params=_cparams(("arbitrary", "arbitrary"), 56),
        name="inproj",
    )(offs, x2, mod4, mod4, norm_g.reshape(1, d), w_in_t, w_in_t)


def _mla_prep_kernel(lat_ref, pos_ref, wuq_ref, wukv_ref, gqa_ref, gkva_ref, gq_ref, gk_ref, inv_ref,
                     qt_ref, k_ref, vt_ref):
    lat = lat_ref[...].astype(F32)
    cq = lat[:, :Q_LORA]
    ckv = lat[:, Q_LORA:Q_LORA + KV_LORA]
    kc = lat[:, SRC_KPE:SRC_KPE + LANES]
    cqn = (cq * _rms_scale(cq, Q_LORA) * gqa_ref[...]).astype(BF16)
    ckvn = (ckv * _rms_scale(ckv, KV_LORA) * gkva_ref[...]).astype(BF16)
    qraw = _dot(cqn, wuq_ref[...])
    kv = _dot(ckvn, wukv_ref[...])

    ang = pos_ref[...].astype(F32) * inv_ref[...]
    lane = lax.broadcasted_iota(jnp.int32, ang.shape, 1)
    lo = lane < MLA_ROPE
    cs = jnp.cos(ang - jnp.where(lo, 0.0, 0.5 * math.pi))

    quarter = MLA_ROPE // 2
    want = jnp.where(lane < 3 * quarter, lane - quarter, lane - 3 * quarter)
    came = pltpu.roll(lane, quarter, 1)
    swapped = jnp.where(came == want, pltpu.roll(kc, quarter, 1), pltpu.roll(kc, 3 * quarter, 1))
    kc = jnp.where(lo, kc, swapped)

    gq = gq_ref[...]
    gk = gk_ref[...]
    gq_n, gq_r = gq[:, :LANES], gq[:, LANES:]
    gk_n, gk_r = gk[:, :LANES], gk[:, LANES:]

    def rope(chunk, g_cs):
        a = chunk * g_cs
        return jnp.where(lo, a + pltpu.roll(a, MLA_ROPE, 1), 0.0)

    gq_cs = gq_r * cs
    kpe_ss = jnp.sum(jnp.where(lo, kc * kc, 0.0), axis=-1, keepdims=True)
    k_rope = rope(kc, gk_r * cs)
    scale = MLA_QK ** -0.5 * math.log2(math.e)
    for h in range(MLA_HEADS):
        kn = kv[:, h * MLA_NOPE:(h + 1) * MLA_NOPE]
        sk = lax.rsqrt((jnp.sum(kn * kn, axis=-1, keepdims=True) + kpe_ss) * (1.0 / MLA_QK) + EPS)
        k_ref[:, h * QK_PAD:h * QK_PAD + LANES] = (kn * sk * gk_n).astype(BF16)
        k_ref[:, h * QK_PAD + LANES:(h + 1) * QK_PAD] = (k_rope * sk).astype(BF16)
        qn = qraw[:, h * QK_PAD:h * QK_PAD + LANES]
        qr = qraw[:, h * QK_PAD + LANES:(h + 1) * QK_PAD]
        ss = jnp.sum(qn * qn, axis=-1, keepdims=True) + jnp.sum(jnp.where(lo, qr * qr, 0.0), axis=-1, keepdims=True)
        sq = lax.rsqrt(ss * (1.0 / MLA_QK) + EPS) * scale
        qt_ref[h * QK_PAD:h * QK_PAD + LANES, :] = (qn * sq * gq_n).astype(BF16).T
        qt_ref[h * QK_PAD + LANES:(h + 1) * QK_PAD, :] = (rope(qr, gq_cs) * sq).astype(BF16).T
        vh = kv[:, MLA_HEADS * MLA_NOPE + h * MLA_V:MLA_HEADS * MLA_NOPE + (h + 1) * MLA_V]
        vt_ref[h * MLA_V:(h + 1) * MLA_V, :] = vh.astype(BF16).T


def _mla_prep(proj, pos2, wuq_p, wukv_p, gqa, gkva, gq, gk, inv_lanes, seq):
    t = proj.shape[0]
    tm = min(512, seq)
    hq = MLA_HEADS * QK_PAD
    lat_blk = COL_LAT // IN_BLOCK
    const = lambda i: (0, 0)
    return pl.pallas_call(
        _mla_prep_kernel,
        grid=(t // tm,),
        in_specs=[
            pl.BlockSpec((tm, IN_BLOCK), lambda i: (i, lat_blk)),
            pl.BlockSpec((tm, 1), lambda i: (i, 0)),
            pl.BlockSpec(wuq_p.shape, const),
            pl.BlockSpec(wukv_p.shape, const),
            pl.BlockSpec(gqa.shape, const),
            pl.BlockSpec(gkva.shape, const),
            pl.BlockSpec(gq.shape, const),
            pl.BlockSpec(gk.shape, const),
            pl.BlockSpec(inv_lanes.shape, const),
        ],
        out_specs=[
            pl.BlockSpec((hq, tm), lambda i: (0, i)),
            pl.BlockSpec((tm, hq), lambda i: (i, 0)),
            pl.BlockSpec((MLA_W, tm), lambda i: (0, i)),
        ],
        out_shape=[
            jax.ShapeDtypeStruct((hq, t), BF16),
            jax.ShapeDtypeStruct((t, hq), BF16),
            jax.ShapeDtypeStruct((MLA_W, t), BF16),
        ],
        compiler_params=_cparams(("arbitrary",), 48),
        name="mla_prep",
    )(proj, pos2, wuq_p, wukv_p, gqa, gkva, gq, gk, inv_lanes)


def _flash_kernel(qt_ref, k_ref, vt_ref, o_ref, *, tq, tk):
    seq = k_ref.shape[0]
    heads = k_ref.shape[1] // QK_PAD

    def scores(h, k0, q0):
        kj = k_ref[k0:k0 + tk, h * QK_PAD:(h + 1) * QK_PAD]
        return _dot(kj, qt_ref[h * QK_PAD:(h + 1) * QK_PAD, q0:q0 + tq])

    def update(h, state, st, k0, q0):
        m, l, acc = state
        if k0 + tk - 1 > q0:
            key = lax.broadcasted_iota(jnp.int32, st.shape, 0) + k0
            qry = lax.broadcasted_iota(jnp.int32, st.shape, 1) + q0
            st = jnp.where(key <= qry, st, -jnp.inf)
        m_new = jnp.maximum(m, jnp.max(st, axis=0, keepdims=True))
        alpha = jnp.exp2(m - m_new)
        p = jnp.exp2(st - m_new)
        l = alpha * l + jnp.sum(p, axis=0, keepdims=True)
        acc = alpha * acc + _dot(vt_ref[h * MLA_V:(h + 1) * MLA_V, k0:k0 + tk], p.astype(BF16))
        return m_new, l, acc

    steps = [(h, qi * tq, j * tk) for qi in range(seq // tq) for j in range((qi + 1) * tq // tk) for h in range(heads)]
    st_next = scores(steps[0][0], steps[0][2], steps[0][1])
    states = {}
    for n, (h, q0, k0) in enumerate(steps):
        st = st_next
        if n + 1 < len(steps):
            hn, qn, kn = steps[n + 1]
            st_next = scores(hn, kn, qn)
        if k0 == 0:
            states[h] = (jnp.full((1, tq), -jnp.inf, F32), jnp.zeros((1, tq), F32), jnp.zeros((MLA_V, tq), F32))
        states[h] = update(h, states[h], st, k0, q0)
        if k0 + tk >= q0 + tq:
            _, l, acc = states[h]
            o_ref[q0:q0 + tq, h * MLA_V:(h + 1) * MLA_V] = (acc / l).T.astype(o_ref.dtype)


def _flash(qt, k3, vt):
    b, seq, _ = k3.shape
    tq = min(FLASH_Q_TILE, seq)
    tk = min(FLASH_K_TILE, seq)
    hp = FLASH_HEADS_PER_STEP
    kern = functools.partial(_flash_kernel, tq=tq, tk=tk)
    return pl.pallas_call(
        kern,
        grid=(b, MLA_HEADS // hp),
        in_specs=[
            pl.BlockSpec((hp * QK_PAD, seq), lambda i, h: (h, i)),
            pl.BlockSpec((None, seq, hp * QK_PAD), lambda i, h: (i, 0, h)),
            pl.BlockSpec((hp * MLA_V, seq), lambda i, h: (h, i)),
        ],
        out_specs=pl.BlockSpec((None, seq, hp * MLA_V), lambda i, h: (i, 0, h)),
        out_shape=jax.ShapeDtypeStruct((b, seq, MLA_W), BF16),
        compiler_params=_cparams(("arbitrary", "arbitrary"), 40),
        name="flash",
    )(qt, k3, vt)


def _log_sigmoid(x):
    return -(jnp.maximum(-x, 0.0) + jnp.log1p(jnp.exp(-jnp.abs(x))))


def _conv_silu_qk(cur_ref, halo_ref, w_ref, b_ref, buf_ref, qt_ref, k_ref, first):
    L = cur_ref.shape[0]
    cols = cur_ref.shape[1]
    half = cols // 2
    buf_ref[0:SUBLANES, :] = jnp.where(first, 0.0, halo_ref[...].astype(F32))
    buf_ref[SUBLANES:SUBLANES + L, :] = cur_ref[...].astype(F32)
    cw = 512
    for c in range(cols // cw):
        sl = slice(c * cw, (c + 1) * cw)
        acc = jnp.zeros((L, cw), F32) + b_ref[:, sl]
        for j in range(ML_CONV):
            off = SUBLANES - (ML_CONV - 1) + j
            acc = acc + buf_ref[off:off + L, sl] * w_ref[j:j + 1, sl]
        y = acc * _sigmoid(acc)
        if c * cw < half:
            qt_ref[sl, :] = y.astype(qt_ref.dtype).T
        else:
            k_ref[:, c * cw - half:(c + 1) * cw - half] = (y * (ML_DQK ** -0.5)).astype(k_ref.dtype)


def _mlstm_kernel(cur_ref, halo_ref, cw_ref, cb_ref, v_ref, o_ref, gates_ref, gbias_ref, ng_ref, out_ref,
                  ct_ref, m_ref, buf_ref, qt_ref, k_ref):
    L = cur_ref.shape[0]
    first = pl.program_id(1) == 0

    @pl.when(first)
    def _():
        ct_ref[...] = jnp.zeros(ct_ref.shape, F32)
        m_ref[...] = jnp.zeros(m_ref.shape, F32)

    _conv_silu_qk(cur_ref, halo_ref, cw_ref, cb_ref, buf_ref, qt_ref, k_ref, first)

    g = gates_ref[...] + gbias_ref[...]
    gt = g.T
    lf = _log_sigmoid(g)
    lft = _log_sigmoid(gt)
    r = lax.broadcasted_iota(jnp.int32, (L, L), 0)
    c = lax.broadcasted_iota(jnp.int32, (L, L), 1)
    src_le_qry = r <= c
    tril = (c <= r).astype(F32)
    triu = src_le_qry.astype(F32)
    hi = lax.Precision.HIGHEST
    bcol_all = jnp.dot(tril, lf, preferred_element_type=F32, precision=hi)
    brow_all = jnp.dot(lft, triu, preferred_element_type=F32, precision=hi)
    row = lax.broadcasted_iota(jnp.int32, (ML_DV, L), 0)
    ones_row = jnp.where(row == 0, 1.0, 0.0)

    def lead_matmuls(h):
        hs = slice(h * ML_DQK, (h + 1) * ML_DQK)
        qt = qt_ref[hs, :]
        return _dot(k_ref[:, hs], qt), _dot(ct_ref[h].astype(BF16), qt)

    lead_next = lead_matmuls(0)
    for h in range(ML_HEADS):
        hs = slice(h * ML_DQK, (h + 1) * ML_DQK)
        qk_t, cq_t = lead_next
        if h + 1 < ML_HEADS:
            lead_next = lead_matmuls(h + 1)
        li, lf_ = GATE_LANE + h, GATE_LANE + ML_HEADS + h
        b_row = brow_all[lf_:lf_ + 1, :]
        i_row = gt[li:li + 1, :]
        u_col = g[:, li:li + 1] - bcol_all[:, lf_:lf_ + 1]
        m_prev = m_ref[h][:, :1]
        logw_t = jnp.where(src_le_qry, b_row + u_col, -jnp.inf)
        log_inter = b_row + m_prev
        m_t = jnp.maximum(jnp.max(logw_t, axis=0, keepdims=True), log_inter)
        w_t = jnp.exp(logw_t - m_t)
        a = jnp.exp(log_inter - m_t)
        kh = k_ref[:, hs]
        vt_aug = jnp.concatenate([v_ref[:, hs].astype(F32).T, ones_row], axis=0)
        s_t = qk_t * w_t
        ct = ct_ref[h]
        nd = _dot(vt_aug.astype(BF16), s_t.astype(BF16)) + a * cq_t
        num = nd[:ML_DV, :]
        den = nd[ML_DV:ML_DV + 1, :]
        hout_t = num * (1.0 / jnp.maximum(jnp.abs(den), jnp.exp(-m_t)))
        hn_t = hout_t * lax.rsqrt(jnp.sum(hout_t * hout_t, axis=0, keepdims=True) * (1.0 / ML_DV) + EPS)
        gate = _sigmoid(o_ref[:, hs].astype(F32))
        out_ref[:, hs] = (hn_t.T * ng_ref[:, hs] * gate).astype(out_ref.dtype)

        b_last = b_row[:, L - 1:L]
        logg = b_last - b_row + i_row
        m_new = jnp.maximum(b_last + m_prev, jnp.max(logg, axis=-1, keepdims=True))
        g_row = jnp.exp(logg - m_new)
        decay = jnp.exp(b_last + m_prev - m_new)
        ct_ref[h] = decay * ct + _dot((vt_aug * g_row).astype(BF16), kh)
        m_ref[h] = jnp.broadcast_to(m_new, m_ref.shape[1:])


def _mlstm(proj3, conv_w, conv_b, gates3, gbias, norm_g):
    b, seq, _ = proj3.shape
    w = ML_W
    qk_cols = 2 * ML_HEADS * ML_DQK
    L = min(MLSTM_CHUNK, seq)
    hb = L // SUBLANES
    return pl.pallas_call(
        _mlstm_kernel,
        grid=(b, seq // L),
        in_specs=[
            pl.BlockSpec((None, L, qk_cols), lambda i, c: (i, c, COL_QK // qk_cols)),
            pl.BlockSpec((None, SUBLANES, qk_cols), lambda i, c: (i, jnp.maximum(c * hb - 1, 0), COL_QK // qk_cols)),
            pl.BlockSpec((ML_CONV, qk_cols), lambda i, c: (0, 0)),
            pl.BlockSpec((1, qk_cols), lambda i, c: (0, 0)),
            pl.BlockSpec((None, L, w), lambda i, c: (i, c, COL_V // w)),
            pl.BlockSpec((None, L, w), lambda i, c: (i, c, COL_O // w)),
            pl.BlockSpec((None, L, LANES), lambda i, c: (i, c, 0)),
            pl.BlockSpec((1, LANES), lambda i, c: (0, 0)),
            pl.BlockSpec((1, w), lambda i, c: (0, 0)),
        ],
        out_specs=pl.BlockSpec((None, L, w), lambda i, c: (i, c, 0)),
        out_shape=jax.ShapeDtypeStruct((b, seq, w), BF16),
        scratch_shapes=[
            pltpu.VMEM((ML_HEADS, 2 * ML_DV, ML_DQK), F32),
            pltpu.VMEM((ML_HEADS, 1, LANES), F32),
            pltpu.VMEM((L + SUBLANES, qk_cols), F32),
            pltpu.VMEM((ML_HEADS * ML_DQK, L), BF16),
            pltpu.VMEM((L, ML_HEADS * ML_DQK), BF16),
        ],
        compiler_params=_cparams(("arbitrary", "arbitrary"), 40),
        name="mlstm",
    )(proj3, proj3, conv_w, conv_b.reshape(1, qk_cols), proj3, proj3, gates3, gbias, norm_g.reshape(1, w))


def _merge_kernel(a_ref, b_ref, ga_ref, gb_ref, wa_ref, wb_ref, o_ref):
    tm = a_ref.shape[0]
    rows = min(MERGE_ROW_GROUP, tm)
    groups = [slice(r0, r0 + rows) for r0 in range(0, tm, rows)]

    def proj(rs):
        return _dot(a_ref[rs, :], wa_ref[...]), _dot(b_ref[rs, :], wb_ref[...])

    nxt = proj(groups[0])
    for n, rs in enumerate(groups):
        pa, pb = nxt
        if n + 1 < len(groups):
            nxt = proj(groups[n + 1])
        mixed = _sigmoid(ga_ref[rs, :].astype(F32)) * pa + _sigmoid(gb_ref[rs, :].astype(F32)) * pb
        o_ref[rs, :] = mixed.astype(o_ref.dtype)


def _merge(out_a, hm, proj, wa, wb, seq):
    t = out_a.shape[0]
    d = wa.shape[1]
    tm = min(512, seq)
    tn = 2048
    return pl.pallas_call(
        _merge_kernel,
        grid=(d // tn, t // tm),
        in_specs=[
            pl.BlockSpec((tm, MLA_W), lambda j, i: (i, 0)),
            pl.BlockSpec((tm, ML_W), lambda j, i: (i, 0)),
            pl.BlockSpec((tm, tn), lambda j, i: (i, COL_GA // tn + j)),
            pl.BlockSpec((tm, tn), lambda j, i: (i, COL_GB // tn + j)),
            pl.BlockSpec((MLA_W, tn), lambda j, i: (0, j)),
            pl.BlockSpec((ML_W, tn), lambda j, i: (0, j)),
        ],
        out_specs=pl.BlockSpec((tm, tn), lambda j, i: (i, j)),
        out_shape=jax.ShapeDtypeStruct((t, d), BF16),
        compiler_params=_cparams(("arbitrary", "arbitrary"), 40),
        name="merge",
    )(out_a, hm, proj, proj, wa, wb)


def _outproj_kernel(mix_ref, x_ref, w_ref, gt_ref, sc_ref, sh_ref, g_ref, wr_ref, br_ref, x1_ref, h2_ref, lg_ref):
    tm = mix_ref.shape[0]
    rows = OUTPROJ_ROW_GROUP
    groups = [slice(r0, r0 + rows) for r0 in range(0, tm, rows)]
    y_next = _dot(mix_ref[groups[0], :], w_ref[...])
    for n, rs in enumerate(groups):
        y = y_next
        if n + 1 < len(groups):
            y_next = _dot(mix_ref[groups[n + 1], :], w_ref[...])
        x1 = x_ref[rs, :] + gt_ref[...] * y
        x1_ref[rs, :] = x1
        h2 = x1 * _rms_scale(x1, x1.shape[-1]) * g_ref[...]
        h2 = h2 * (1.0 + sc_ref[...]) + sh_ref[...]
        h2_ref[rs, :] = h2
        h_hi = h2.astype(BF16)
        h_lo = (h2 - h_hi.astype(F32)).astype(BF16)
        r = _dot(h_hi, wr_ref[...]) + _dot(h_lo, wr_ref[...])
        lg_ref[rs, :] = r[:, :LANES] + r[:, LANES:] + br_ref[...]


def _outproj(mixed, x2, w_out, mod4, norm_g, w_route2, b_route, seq):
    t, d = x2.shape
    tm = min(512, seq)
    bpr = seq // tm
    mod_spec = lambda k: pl.BlockSpec((None, None, 1, d), lambda i: (i // bpr, k, 0, 0))
    const = lambda i: (0, 0)
    return pl.pallas_call(
        _outproj_kernel,
        grid=(t // tm,),
        in_specs=[
            pl.BlockSpec((tm, d), lambda i: (i, 0)),
            pl.BlockSpec((tm, d), lambda i: (i, 0)),
            pl.BlockSpec((d, d), const, pipeline_mode=pl.Buffered(1)),
            mod_spec(2),
            mod_spec(4),
            mod_spec(3),
            pl.BlockSpec((1, d), const),
            pl.BlockSpec((d, 2 * LANES), const, pipeline_mode=pl.Buffered(1)),
            pl.BlockSpec((1, LANES), const),
        ],
        out_specs=[
            pl.BlockSpec((tm, d), lambda i: (i, 0)),
            pl.BlockSpec((tm, d), lambda i: (i, 0)),
            pl.BlockSpec((tm, LANES), lambda i: (i, 0)),
        ],
        out_shape=[
            jax.ShapeDtypeStruct((t, d), F32),
            jax.ShapeDtypeStruct((t, d), F32),
            jax.ShapeDtypeStruct((t, LANES), F32),
        ],
        compiler_params=_cparams(("arbitrary",), 56),
        name="outproj",
    )(mixed, x2, w_out, mod4, mod4, mod4, norm_g.reshape(1, d), w_route2, b_route)


def _route_kernel(lg_ref, pos_ref, wts_ref, cnt_ref, carry_ref, offs_ref, meta_s, wts_s):
    phase = pl.program_id(0)
    i = pl.program_id(1)
    tm = lg_ref.shape[0]
    lane = lax.broadcasted_iota(jnp.int32, (tm, LANES), 1)

    @pl.when((phase == 0) & (i == 0))
    def _():
        carry_ref[...] = jnp.zeros(carry_ref.shape, F32)

    @pl.when(phase == 0)
    def _():
        lg = lg_ref[...]
        big = jnp.int32(LANES)
        ninf = -jnp.inf

        def first_argmax(vals):
            mx = jnp.max(vals, axis=-1, keepdims=True)
            idx = jnp.min(jnp.where(vals == mx, lane, big), axis=-1, keepdims=True)
            return mx, idx

        gl = jnp.where(lane < N_GROUPS, lg, ninf)
        gmax, gsel = first_argmax(gl)
        g_w = 1.0 / jnp.sum(jnp.exp(gl - gmax), axis=-1, keepdims=True)
        lo = N_GROUPS + gsel * EXP_PER_GROUP
        in_grp = (lane >= lo) & (lane < lo + EXP_PER_GROUP)
        el = jnp.where(in_grp, lg, ninf)
        e1, i1 = first_argmax(el)
        e2, i2 = first_argmax(jnp.where(lane == i1, ninf, el))
        p2 = jnp.exp(e2 - e1)
        w1 = g_w / (1.0 + p2)
        w2 = g_w * p2 / (1.0 + p2)

        oh1 = lane == i1
        oh2 = lane == i2
        oh = jnp.where(oh1 | oh2, 1.0, 0.0)
        sub = min(ROUTE_RANK_ROWS, tm)
        r = lax.broadcasted_iota(jnp.int32, (sub, sub), 0)
        c = lax.broadcasted_iota(jnp.int32, (sub, sub), 1)
        strict = jnp.where(c < r, 1.0, 0.0).astype(BF16)
        carry = carry_ref[...]
        parts = []
        for r0 in range(0, tm, sub):
            oh_sub = oh[r0:r0 + sub, :]
            parts.append(_dot(strict, oh_sub.astype(BF16)) + carry)
            carry = carry + jnp.sum(oh_sub, axis=0, keepdims=True)
        before = jnp.concatenate(parts, axis=0)
        rank1 = jnp.sum(jnp.where(oh1, before, 0.0), axis=-1, keepdims=True).astype(jnp.int32)
        rank2 = jnp.sum(jnp.where(oh2, before, 0.0), axis=-1, keepdims=True).astype(jnp.int32)
        carry_ref[...] = carry
        meta_s[i] = jnp.where(lane == 0, i1, jnp.where(lane == 1, i2, jnp.where(lane == 2, rank1, jnp.where(lane == 3, rank2, 0))))
        wts_s[i] = jnp.where(lane == 0, w1, jnp.where(lane == 1, w2, 0.0))

    @pl.when((phase == 1) & (i == 0))
    def _():
        cnt = carry_ref[...]
        cnt_ref[...] = jnp.broadcast_to(cnt, cnt_ref.shape)
        padded = jnp.ceil(cnt * (1.0 / MOE_TILE)) * MOE_TILE
        r = lax.broadcasted_iota(jnp.int32, (LANES, LANES), 0)
        c = lax.broadcasted_iota(jnp.int32, (LANES, LANES), 1)
        upper = jnp.where(r < c, 1.0, 0.0).astype(BF16)
        padded8 = jnp.broadcast_to(padded, (SUBLANES, LANES)).astype(BF16)
        offs_ref[...] = _dot(padded8, upper)[:1, :]

    @pl.when(phase == 1)
    def _():
        meta = meta_s[i]
        offs = offs_ref[...]
        off1 = jnp.sum(jnp.where(lane == meta[:, 0:1], offs, 0.0), axis=-1, keepdims=True).astype(jnp.int32)
        off2 = jnp.sum(jnp.where(lane == meta[:, 1:2], offs, 0.0), axis=-1, keepdims=True).astype(jnp.int32)
        pos1 = off1 + meta[:, 2:3]
        pos2 = off2 + meta[:, 3:4]
        pos_ref[...] = jnp.where(lane == 0, pos1, jnp.where(lane == 1, pos2, 0))
        wts_ref[...] = wts_s[i]


def _route(logits):
    t = logits.shape[0]
    tm = min(1024, t)
    nb = t // tm
    assert (2 * t) // MOE_TILE + N_EXPERTS <= 256, "tile-padded offsets must stay exact in bf16 (8 significant bits)"
    return pl.pallas_call(
        _route_kernel,
        grid=(2, nb),
        in_specs=[pl.BlockSpec((tm, LANES), lambda p, i: (i * (1 - p), 0))],
        out_specs=[
            pl.BlockSpec((tm, LANES), lambda p, i: (i * p, 0)),
            pl.BlockSpec((tm, LANES), lambda p, i: (i * p, 0)),
            pl.BlockSpec((SUBLANES, LANES), lambda p, i: (0, 0)),
        ],
        out_shape=[
            jax.ShapeDtypeStruct((t, LANES), jnp.int32),
            jax.ShapeDtypeStruct((t, LANES), F32),
            jax.ShapeDtypeStruct((SUBLANES, LANES), F32),
        ],
        scratch_shapes=[
            pltpu.VMEM((1, LANES), F32),
            pltpu.VMEM((1, LANES), F32),
            pltpu.VMEM((nb, tm, LANES), jnp.int32),
            pltpu.VMEM((nb, tm, LANES), F32),
        ],
        compiler_params=_cparams(("arbitrary", "arbitrary"), 32),
        name="route",
    )(logits)


def _row_gather_start(idx_at, src_hbm, dst, sem, rows):
    for r in range(rows):
        pltpu.make_async_copy(src_hbm.at[pl.ds(idx_at(r), 1)], dst.at[pl.ds(r, 1)], sem).start(priority=r % 2)


def _row_gather_wait(src_hbm, dst, sem, rows):
    pltpu.make_async_copy(src_hbm.at[pl.ds(0, rows)], dst, sem).wait()


def _zero_piece_sizes(tile):
    sizes = [tile]
    size = tile // 2
    while size >= SUBLANES:
        sizes.append(size)
        size //= 2
    return sizes + [1] * (SUBLANES - 1)


def _zero_pieces(tile):
    return [(n * N_EXPERTS + e, size) for n, size in enumerate(_zero_piece_sizes(tile)) for e in range(N_EXPERTS)]


def _dispatch_kernel(pos_ref, zrow_ref, h_ref, xs_hbm, zbuf, sems):
    i = pl.program_id(0)
    tm = h_ref.shape[0]
    pieces = _zero_pieces(zbuf.shape[0])

    def zero_copy(z, size):
        start = zrow_ref[z] if size == 1 else pl.multiple_of(zrow_ref[z], SUBLANES)
        return pltpu.make_async_copy(zbuf.at[pl.ds(0, size)], xs_hbm.at[pl.ds(start, size)], sems.at[0])

    @pl.when(i == 0)
    def _():
        zbuf[...] = jnp.zeros(zbuf.shape, zbuf.dtype)
        for z, size in pieces:
            @pl.when(zrow_ref[z] >= 0)
            def _():
                zero_copy(z, size).start()

    for k in range(2):
        base = (2 * i + k) * tm
        for r in range(tm):
            pltpu.make_async_copy(h_ref.at[pl.ds(r, 1)], xs_hbm.at[pl.ds(pos_ref[base + r], 1)],
                                  sems.at[1]).start(priority=r % 2)
    for k in range(2):
        pltpu.make_async_copy(h_ref, xs_hbm.at[pl.ds(0, tm)], sems.at[1]).wait()

    @pl.when(i == 0)
    def _():
        for z, size in pieces:
            @pl.when(zrow_ref[z] >= 0)
            def _():
                zero_copy(z, size).wait()


def _dispatch(pos, zrow, h2, n_rows, seq):
    t, d = h2.shape
    tm = min(1024, seq)
    pos_blocks = pos.reshape(t // tm, tm, 2).transpose(0, 2, 1).reshape(-1)
    grid_spec = pltpu.PrefetchScalarGridSpec(
        num_scalar_prefetch=2,
        grid=(t // tm,),
        in_specs=[pl.BlockSpec((tm, d), lambda i, p, z: (i, 0))],
        out_specs=pl.BlockSpec(memory_space=pl.ANY),
        scratch_shapes=[
            pltpu.VMEM((MOE_TILE, d), F32),
            pltpu.SemaphoreType.DMA((2,)),
        ],
    )
    return pl.pallas_call(
        _dispatch_kernel,
        grid_spec=grid_spec,
        out_shape=jax.ShapeDtypeStruct((n_rows, d), F32),
        compiler_params=_cparams(("arbitrary",), 40),
        name="moe_dispatch",
    )(pos_blocks, zrow, h2)


def _moe_kernel(plan_ref, x_hbm, wg_hbm, wu_hbm, wd_hbm, y_ref, stg_g, stg_u, stg_d, wsem, xbuf, xsem):
    j = pl.program_id(0)
    nt = plan_ref[3, 0]
    nslot = stg_g.shape[0]

    def w_copies(e, s):
        return (pltpu.make_async_copy(wg_hbm.at[e], stg_g.at[s], wsem.at[s]),
                pltpu.make_async_copy(wu_hbm.at[e], stg_u.at[s], wsem.at[s]),
                pltpu.make_async_copy(wd_hbm.at[e], stg_d.at[s], wsem.at[s]))

    xslots = xbuf.shape[0]
    tile = xbuf.shape[1]

    def x_copy(t):
        s = lax.rem(t, xslots)
        return pltpu.make_async_copy(x_hbm.at[pl.ds(pl.multiple_of(t * tile, tile), tile)], xbuf.at[s], xsem.at[s])

    @pl.when(j == 0)
    def _():
        for s in range(nslot - 1):
            @pl.when(plan_ref[3, 1 + s] >= 0)
            def _():
                for cp in w_copies(plan_ref[3, 1 + s], s):
                    cp.start()
        for t in range(xslots - 1):
            @pl.when(t < nt)
            def _():
                x_copy(t).start()

    @pl.when(j + xslots - 1 < nt)
    def _():
        x_copy(j + xslots - 1).start()

    @pl.when((j < nt) & (plan_ref[0, j] == 1))
    def _():
        s = plan_ref[1, j]
        for cp in w_copies(0, s):
            cp.wait()

        @pl.when(plan_ref[2, j] >= 0)
        def _():
            free = lax.rem(s + nslot - 1, nslot)
            for cp in w_copies(plan_ref[2, j], free):
                cp.start()

    @pl.when(j < nt)
    def _():
        s = plan_ref[1, j]
        x_copy(j).wait()
        x = xbuf[lax.rem(j, xslots)].astype(BF16)
        hg = _dot(x, stg_g[s].astype(BF16))
        hu = _dot(x, stg_u[s].astype(BF16))
        act = (hg * _sigmoid(hg) * hu).astype(BF16)
        y_ref[...] = _dot(act, stg_d[s].astype(BF16))

    @pl.when(j >= nt)
    def _():
        y_ref[...] = jnp.zeros(y_ref.shape, y_ref.dtype)


def _moe(plan, xs, w_gate_e, w_up_e, w_down_e):
    d = xs.shape[1]
    f = w_gate_e.shape[-1]
    nt_max = plan.shape[1]
    tile = MOE_TILE
    any_spec = pl.BlockSpec(memory_space=pl.ANY)
    grid_spec = pltpu.PrefetchScalarGridSpec(
        num_scalar_prefetch=1,
        grid=(nt_max,),
        in_specs=[any_spec, any_spec, any_spec, any_spec],
        out_specs=pl.BlockSpec((tile, d), lambda j, plan_: (j, 0)),
        scratch_shapes=[
            pltpu.VMEM((MOE_WEIGHT_SLOTS, d, f), F32),
            pltpu.VMEM((MOE_WEIGHT_SLOTS, d, f), F32),
            pltpu.VMEM((MOE_WEIGHT_SLOTS, f, d), F32),
            pltpu.SemaphoreType.DMA((MOE_WEIGHT_SLOTS,)),
            pltpu.VMEM((MOE_X_SLOTS, tile, d), F32),
            pltpu.SemaphoreType.DMA((MOE_X_SLOTS,)),
        ],
    )
    return pl.pallas_call(
        _moe_kernel,
        grid_spec=grid_spec,
        out_shape=jax.ShapeDtypeStruct((nt_max * tile, d), F32),
        compiler_params=_cparams(("arbitrary",), 56),
        name="moe_experts",
    )(plan, xs, w_gate_e, w_up_e, w_down_e)


def _combine_kernel(pos_ref, y_hbm, x1_ref, gt_ref, wts_ref, o_ref, ybuf, sem):
    i = pl.program_id(0)
    n = pl.num_programs(0)
    tm = x1_ref.shape[0]
    slot = lax.rem(i, 2)

    def start(blk, s):
        for k in range(2):
            base = (2 * blk + k) * tm
            _row_gather_start(lambda r, base=base: pos_ref[base + r], y_hbm, ybuf.at[s, k], sem.at[s], tm)

    @pl.when(i == 0)
    def _():
        start(0, 0)

        @pl.when(n > 1)
        def _():
            start(1, 1)

    for k in range(2):
        _row_gather_wait(y_hbm, ybuf.at[slot, k], sem.at[slot], tm)
    w = wts_ref[...]
    moe = w[:, 0:1] * ybuf[slot, 0] + w[:, 1:2] * ybuf[slot, 1]
    o_ref[...] = x1_ref[...] + gt_ref[...] * moe

    @pl.when(i + 2 < n)
    def _():
        start(i + 2, slot)


def _combine(pos, ys, x1, mod4, wts, seq):
    t, d = x1.shape
    tm = min(512, seq)
    bpr = seq // tm
    pos_blocks = pos.reshape(t // tm, tm, 2).transpose(0, 2, 1).reshape(-1)
    grid_spec = pltpu.PrefetchScalarGridSpec(
        num_scalar_prefetch=1,
        grid=(t // tm,),
        in_specs=[
            pl.BlockSpec(memory_space=pl.ANY),
            pl.BlockSpec((tm, d), lambda i, p: (i, 0)),
            pl.BlockSpec((None, None, 1, d), lambda i, p: (i // bpr, 5, 0, 0)),
            pl.BlockSpec((tm, LANES), lambda i, p: (i, 0)),
        ],
        out_specs=pl.BlockSpec((tm, d), lambda i, p: (i, 0)),
        scratch_shapes=[
            pltpu.VMEM((2, 2, tm, d), F32),
            pltpu.SemaphoreType.DMA((2,)),
        ],
    )
    return pl.pallas_call(
        _combine_kernel,
        grid_spec=grid_spec,
        out_shape=jax.ShapeDtypeStruct((t, d), F32),
        compiler_params=_cparams(("arbitrary",), 48),
        name="moe_combine",
    )(pos_blocks, ys, x1, mod4, wts)


def _swap_halves(w):
    half = w.shape[-1] // 2
    return jnp.concatenate([w[..., half:], w[..., :half]], axis=-1)


def _layout_w_uq(w_uq):
    r = w_uq.shape[0]
    w = w_uq.reshape(r, MLA_HEADS, MLA_QK)
    rope = w[..., MLA_NOPE:]
    return jnp.concatenate([w[..., :MLA_NOPE], rope, _swap_halves(rope)], axis=-1).reshape(r, MLA_HEADS * QK_PAD).astype(BF16)


def _layout_w_ukv(w_ukv):
    r = w_ukv.shape[0]
    w = w_ukv.reshape(r, MLA_HEADS, MLA_NOPE + MLA_V)
    return jnp.concatenate([w[..., :MLA_NOPE].reshape(r, -1), w[..., MLA_NOPE:].reshape(r, -1)], axis=-1).astype(BF16)


def _rope_gain(g):
    g1 = g[MLA_NOPE:MLA_NOPE + MLA_ROPE // 2]
    g2 = g[MLA_NOPE + MLA_ROPE // 2:]
    return jnp.concatenate([g[:MLA_NOPE], g1, g2, -g2, g1]).reshape(1, QK_PAD)


def _layer(x2, cond_mod4, pos2, seq, w_in, q_a_norm_g, w_uq, kv_a_norm_g, w_ukv, q_norm_g, k_norm_g, conv_w, conv_b,
           b_mlstm_gates, mlstm_norm_g, w_proj_a, w_proj_b, w_out, norm_mix_g, norm_ffn_g, w_group, b_group,
           w_router, b_router, w_gate_e, w_up_e, w_down_e):
    t, d = x2.shape
    b = t // seq
    mod4 = cond_mod4

    proj, gates = _inproj(x2, mod4, norm_mix_g, w_in.T, seq)

    inv = ROPE_THETA ** (-jnp.arange(0, MLA_ROPE, 2, dtype=F32) / MLA_ROPE)
    inv_lanes = jnp.tile(inv, LANES // (MLA_ROPE // 2)).reshape(1, LANES)
    qt, k, vt = _mla_prep(proj, pos2, _layout_w_uq(w_uq), _layout_w_ukv(w_ukv), q_a_norm_g.reshape(1, -1),
                        kv_a_norm_g.reshape(1, -1), _rope_gain(q_norm_g), _rope_gain(k_norm_g), inv_lanes, seq)
    out_a = _flash(qt, k.reshape(b, seq, -1), vt).reshape(t, MLA_W)

    proj3 = proj.reshape(b, seq, -1)
    gbias = jnp.zeros((1, LANES), F32).at[0, GATE_LANE:GATE_LANE + 2 * ML_HEADS].set(b_mlstm_gates.reshape(-1))
    hm = _mlstm(proj3, conv_w, conv_b, gates.reshape(b, seq, LANES), gbias, mlstm_norm_g).reshape(t, ML_W)

    mixed = _merge(out_a, hm, proj, w_proj_a.astype(BF16), w_proj_b.astype(BF16), seq)

    route_pad = LANES - N_GROUPS - N_EXPERTS
    w_route = jnp.concatenate([w_group, w_router, jnp.zeros((d, route_pad), F32)], axis=1)
    b_route = jnp.concatenate([b_group, b_router, jnp.zeros((route_pad,), F32)]).reshape(1, LANES)
    w_route_hi = w_route.astype(BF16)
    w_route2 = jnp.concatenate([w_route_hi, (w_route - w_route_hi.astype(F32)).astype(BF16)], axis=1)
    x1, h2, logits = _outproj(mixed, x2, w_out.astype(BF16), mod4, norm_ffn_g, w_route2, b_route, seq)

    posm, wts, counts = _route(logits)
    pos = posm[:, 0:2]

    tile = MOE_TILE
    i32 = jnp.int32
    cnt = counts[0, N_GROUPS:N_GROUPS + N_EXPERTS].astype(i32)
    padded = ((cnt + tile - 1) // tile) * tile
    ends = jnp.cumsum(padded)
    offs = ends - padded
    nt_max = (2 * t) // tile + N_EXPERTS
    n_tiles = ends[-1] // tile
    tile_idx = jnp.arange(nt_max, dtype=i32)
    tile_start = tile_idx * tile
    live = tile_idx < n_tiles
    texp = jnp.minimum(jnp.sum((ends[None, :] <= tile_start[:, None]).astype(i32), axis=1), N_EXPERTS - 1)
    active = cnt > 0
    order = jnp.cumsum(active.astype(i32)) - 1
    n_active = jnp.sum(active.astype(i32))
    experts = jnp.arange(N_EXPERTS, dtype=i32)
    by_order = jnp.sum(jnp.where(active[None, :] & (order[None, :] == experts[:, None]), experts[None, :], 0), axis=1)
    t_order = order[texp]
    first = (live & (tile_start == offs[texp])).astype(i32)
    nslot = MOE_WEIGHT_SLOTS
    ahead = t_order + nslot - 1
    prefetch = jnp.where(ahead < n_active, by_order[jnp.minimum(ahead, N_EXPERTS - 1)], -1)
    lead = jnp.where(experts[:nslot - 1] < n_active, by_order[:nslot - 1], -1)
    head = jnp.zeros((nt_max,), i32).at[0].set(n_tiles).at[1:nslot].set(lead)
    plan = jnp.stack([first, t_order % nslot, prefetch, head]).astype(i32)

    spare_idx = n_tiles + jnp.arange(N_EXPERTS, dtype=i32)
    zparts = [jnp.where(spare_idx < nt_max, spare_idx * tile, -1)]
    pad = padded - cnt
    for size in _zero_piece_sizes(tile)[1:]:
        if size > 1:
            above = pad & ~(2 * size - 1)
            zparts.append(jnp.where((pad & size) != 0, ends - above - size, -1))
    for u in range(SUBLANES - 1):
        zparts.append(jnp.where(u < (pad & (SUBLANES - 1)), offs + cnt + u, -1))
    zrow = jnp.concatenate(zparts).astype(i32)

    xs = _dispatch(pos, zrow, h2, nt_max * tile, seq)
    ys = _moe(plan, xs, w_gate_e, w_up_e, w_down_e)

    return _combine(pos, ys, x1, mod4, wts, seq)


def kernel(x, c, positions, w_ada, b_ada, norm_mix_g, w_in, q_a_norm_g, w_uq, kv_a_norm_g, w_ukv, q_norm_g, k_norm_g, conv_w, conv_b, b_mlstm_gates, mlstm_norm_g, w_proj_a, w_proj_b, w_out, norm_ffn_g, w_group, b_group, w_router, b_router, w_gate_e, w_up_e, w_down_e):
    b, seq, d = x.shape
    depth = w_ada.shape[0]
    x2 = x.reshape(b * seq, d)
    pos2 = positions.reshape(b * seq, 1)
    c_pad = jnp.zeros((SUBLANES, d), F32).at[:b].set(c)
    for l in range(depth):
        mod = _adaln(c_pad, w_ada[l], b_ada[l])
        mod4 = mod[:b].reshape(b, 6, 1, d)
        x2 = _layer(x2, mod4, pos2, seq, w_in[l], q_a_norm_g[l], w_uq[l], kv_a_norm_g[l], w_ukv[l], q_norm_g[l],
                    k_norm_g[l], conv_w[l], conv_b[l], b_mlstm_gates[l], mlstm_norm_g[l], w_proj_a[l], w_proj_b[l],
                    w_out[l], norm_mix_g[l], norm_ffn_g[l], w_group[l], b_group[l], w_router[l], b_router[l],
                    w_gate_e[l], w_up_e[l], w_down_e[l])
    return x2.reshape(b, seq, d)
```
